```python
import math
import jax
import jax.numpy as jnp
from jax import lax
import numpy as np


D_MODEL = 1024
BATCH = 16
SEQ = 256
DEPTH = 2
DEC_BATCH = 4
DEC_SEQ = 1024
PAST_LEN = 512

F32 = jnp.float32
GRID_W = 64
EPS = 1e-6
S5_WIDTH = 512
S5_GROUP = 16
S5_GROUPS = S5_WIDTH // S5_GROUP
S5_STATE = 64
DA_HEADS = 4
DA_HEAD = 64
DA_VDIM = 2 * DA_HEAD
DA_WIDTH = DA_HEADS * DA_VDIM
Q_BLOCK = 128
ROPE_BASE = 10000.0
DN_HEADS = 4
DN_HEAD = 128
DN_WIDTH = DN_HEADS * DN_HEAD
DN_CONV = 5
DN_CHUNK = 64
N_BRANCH = 3
BRANCH_WIDTH = 512
IN_SIZES = (S5_WIDTH, S5_WIDTH,
            DA_HEADS * 2 * DA_HEAD, DA_HEADS * 2 * DA_HEAD, DA_WIDTH, DA_WIDTH,
            DN_WIDTH, DN_WIDTH, DN_WIDTH, DN_WIDTH, 2 * DN_HEADS, 2 * DN_HEADS,
            N_BRANCH * D_MODEL)
N_IN = sum(IN_SIZES)

kernel_name = 'hybrid_s5_diffattn_deltanet_prefix_flow_step'


def rmsnorm(x, g):
    xf = x.astype(F32)
    y = xf * lax.rsqrt(jnp.mean(xf * xf, axis=-1, keepdims=True) + EPS)
    return (y * g.astype(F32)).astype(x.dtype)


def l2norm(x):
    xf = x.astype(F32)
    return xf * lax.rsqrt(jnp.sum(xf * xf, axis=-1, keepdims=True) + EPS)


def split_in(proj):
    parts, start = [], 0
    for size in IN_SIZES:
        parts.append(proj[..., start:start + size])
        start += size
    return parts


def rope_2d(x, n_tokens):
    rows = n_tokens // GRID_W
    row = jnp.repeat(jnp.arange(rows), GRID_W).astype(F32)
    col = jnp.tile(jnp.arange(GRID_W), rows).astype(F32)
    half = x.shape[-1] // 2
    nf = half // 2
    inv = ROPE_BASE ** (-jnp.arange(nf, dtype=F32) / nf)

    def rot(xa, pos):
        ang = pos[:, None] * inv[None, :]
        cos = jnp.cos(ang)[None, :, None, None, :]
        sin = jnp.sin(ang)[None, :, None, None, :]
        x1, x2 = xa[..., :nf].astype(F32), xa[..., nf:].astype(F32)
        return jnp.concatenate([x1 * cos - x2 * sin, x1 * sin + x2 * cos], axis=-1)

    out = jnp.concatenate([rot(x[..., :half], row), rot(x[..., half:], col)], axis=-1)
    return out.astype(x.dtype)


def s5_scan(u, lam_re, lam_im, log_step, b_re, b_im, h0_re, h0_im):
    lam_re, lam_im = lam_re.astype(F32), lam_im.astype(F32)
    b_re, b_im = b_re.astype(F32), b_im.astype(F32)
    step = jnp.exp(log_step.astype(F32))[:, None]
    mag = jnp.exp(lam_re * step)
    ar, ai = mag * jnp.cos(lam_im * step), mag * jnp.sin(lam_im * step)
    den = lam_re * lam_re + lam_im * lam_im
    fr = ((ar - 1.0) * lam_re + ai * lam_im) / den
    fi = (ai * lam_re - (ar - 1.0) * lam_im) / den
    bbr = fr[..., None] * b_re - fi[..., None] * b_im
    bbi = fr[..., None] * b_im + fi[..., None] * b_re
    bu_re = jnp.einsum('btgi,gpi->btgp', u, bbr)
    bu_im = jnp.einsum('btgi,gpi->btgp', u, bbi)
    h0_re, h0_im = h0_re.astype(F32), h0_im.astype(F32)
    bu_re = bu_re.at[:, 0].add(ar * h0_re - ai * h0_im)
    bu_im = bu_im.at[:, 0].add(ar * h0_im + ai * h0_re)
    a_re = jnp.broadcast_to(ar, bu_re.shape)
    a_im = jnp.broadcast_to(ai, bu_im.shape)

    def combine(e1, e2):
        a1r, a1i, b1r, b1i = e1
        a2r, a2i, b2r, b2i = e2
        return (a2r * a1r - a2i * a1i, a2r * a1i + a2i * a1r,
                a2r * b1r - a2i * b1i + b2r, a2r * b1i + a2i * b1r + b2i)

    _, _, hr, hi = lax.associative_scan(combine, (a_re, a_im, bu_re, bu_im), axis=1)
    return hr, hi


def diff_attention(q, k, v, lam):
    b, s, h, _, d = q.shape
    nq = s // Q_BLOCK
    qb = jnp.moveaxis(q.reshape(b, nq, Q_BLOCK, h, 2, d), 1, 0)
    scale = d ** -0.5

    def one(qc):
        sc = jnp.einsum('bqhmd,bkhmd->bhmqk', qc, k).astype(F32) * scale
        a = jax.nn.softmax(sc, axis=-1)
        a = a[:, :, 0] - lam * a[:, :, 1]
        return jnp.einsum('bhqk,bkhe->bqhe', a.astype(v.dtype), v)

    o = lax.map(one, qb)
    return jnp.moveaxis(o, 0, 1).reshape(b, s, h, -1)


def short_conv(x, w):
    return lax.conv_general_dilated(
        x, w[:, None, :].astype(x.dtype), window_strides=(1,),
        padding=[(DN_CONV // 2, DN_CONV // 2)],
        dimension_numbers=('NWC', 'WIO', 'NWC'), feature_group_count=x.shape[-1])


def gated_delta_rule(q, k, v, beta, g, s0):
    b, t, h, _ = q.shape
    n = t // DN_CHUNK

    def chunk(a):
        a = a.astype(F32).reshape((b, n, DN_CHUNK) + a.shape[2:])
        return jnp.moveaxis(a, 3, 2)

    q, k, v, beta, g = chunk(q), chunk(k), chunk(v), chunk(beta), chunk(g)
    g = jnp.cumsum(g, axis=-1)
    i = jnp.arange(DN_CHUNK)
    incl = i[:, None] >= i[None, :]
    strict = i[:, None] > i[None, :]
    gdiff = g[..., :, None] - g[..., None, :]
    decay = jnp.where(incl, jnp.exp(jnp.where(incl, gdiff, 0.0)), 0.0)
    kb = k * beta[..., None]
    m = jnp.where(strict, jnp.einsum('bnhik,bnhjk->bnhij', kb, k) * decay, 0.0)
    eye = jnp.eye(DN_CHUNK, dtype=F32)
    tinv = lax.linalg.triangular_solve(eye + m, jnp.broadcast_to(eye, m.shape), left_side=True, lower=True)
    u = tinv @ (v * beta[..., None])
    w = tinv @ (kb * jnp.exp(g)[..., None])
    qk = jnp.einsum('bnhik,bnhjk->bnhij', q, k) * decay

    def step(s, xs):
        q_c, k_c, u_c, w_c, g_c, qk_c = xs
        v_new = u_c - w_c @ s
        o_c = (q_c * jnp.exp(g_c)[..., None]) @ s + qk_c @ v_new
        g_last = g_c[..., -1:]
        s = s * jnp.exp(g_last)[..., None] + jnp.einsum(
            'bhck,bhcv->bhkv', k_c * jnp.exp(g_last - g_c)[..., None], v_new)
        return s, o_c

    xs = tuple(jnp.moveaxis(a, 1, 0) for a in (q, k, u, w, g, qk))
    s, o = lax.scan(step, s0.astype(F32), xs)
    o = jnp.moveaxis(jnp.moveaxis(o, 0, 1), 2, 3).reshape(b, t, h, -1)
    return o, s


def mixer(h, lp, lam_init, ctx):
    b, t, _ = h.shape
    dt = h.dtype
    (u_a, z_a, q_b, k_b, v_b, z_b, q_c, k_c, v_c, z_c,
     beta_c, alpha_c, gates) = split_in(h @ lp['w_in'])
    if ctx is None:
        s5_h0_re = jnp.zeros((b, 2, S5_GROUPS, S5_STATE), F32)
        s5_h0_im = jnp.zeros((b, 2, S5_GROUPS, S5_STATE), F32)
        dn_s0 = jnp.zeros((b, 2, DN_HEADS, DN_HEAD, DN_HEAD), F32)
    else:
        k_ctx, v_ctx, s5_h0_re, s5_h0_im, dn_s0 = ctx

    u = u_a.astype(F32).reshape(b, t, S5_GROUPS, S5_GROUP)
    y_a = lp['s5_d'].astype(F32) * u_a.astype(F32)
    s5_fin_re, s5_fin_im = [], []
    for d in range(2):
        ud = u if d == 0 else u[:, ::-1]
        hr, hi = s5_scan(ud, lp['s5_lam_re'][d], lp['s5_lam_im'][d], lp['s5_log_step'][d],
                         lp['s5_b_re'][d], lp['s5_b_im'][d], s5_h0_re[:, d], s5_h0_im[:, d])
        yd = (jnp.einsum('btgp,gip->btgi', hr, lp['s5_c_re'][d].astype(F32))
              - jnp.einsum('btgp,gip->btgi', hi, lp['s5_c_im'][d].astype(F32)))
        if d == 1:
            yd = yd[:, ::-1]
        y_a = y_a + yd.reshape(b, t, S5_WIDTH)
        s5_fin_re.append(hr[:, -1])
        s5_fin_im.append(hi[:, -1])
    y_a = jax.nn.gelu(y_a)
    y_a = y_a * jax.nn.sigmoid(y_a @ lp['s5_w_glu'].astype(F32))
    out_a = (y_a * jax.nn.silu(z_a.astype(F32))).astype(dt)

    q = q_b.reshape(b, t, DA_HEADS, 2, DA_HEAD)
    k = k_b.reshape(b, t, DA_HEADS, 2, DA_HEAD)
    v = v_b.reshape(b, t, DA_HEADS, DA_VDIM)
    if ctx is None:
        k_all, v_all = k, v
    else:
        q = rope_2d(q, t)
        k_all = jnp.concatenate([rope_2d(k, t), k_ctx.astype(dt)], axis=1)
        v_all = jnp.concatenate([v, v_ctx.astype(dt)], axis=1)
    lp_lam = lp['da_lam'].astype(F32)
    lam = (jnp.exp(jnp.sum(lp_lam[0] * lp_lam[1])) - jnp.exp(jnp.sum(lp_lam[2] * lp_lam[3])) + lam_init)
    o_b = diff_attention(q, k_all, v_all, lam)
    o_b = rmsnorm(o_b, lp['da_norm_g']).astype(F32) * (1.0 - lam_init)
    out_b = (o_b.reshape(b, t, DA_WIDTH) * jax.nn.silu(z_b.astype(F32))).astype(dt)

    qkv = jax.nn.silu(short_conv(jnp.concatenate([q_c, k_c, v_c], axis=-1), lp['dn_conv']))
    qd = l2norm(qkv[..., :DN_WIDTH].reshape(b, t, DN_HEADS, DN_HEAD)) * (DN_HEAD ** -0.5)
    kd = l2norm(qkv[..., DN_WIDTH:2 * DN_WIDTH].reshape(b, t, DN_HEADS, DN_HEAD))
    vd = qkv[..., 2 * DN_WIDTH:].astype(F32).reshape(b, t, DN_HEADS, DN_HEAD)
    beta = jax.nn.sigmoid(beta_c.astype(F32)).reshape(b, t, 2, DN_HEADS)
    g = -jnp.exp(lp['dn_a_log'].astype(F32)) * jax.nn.softplus(
        alpha_c.astype(F32).reshape(b, t, 2, DN_HEADS) + lp['dn_dt_bias'].astype(F32))
    o_f, s_f = gated_delta_rule(qd, kd, vd, beta[:, :, 0], g[:, :, 0], dn_s0[:, 0])
    o_r, s_r = gated_delta_rule(qd[:, ::-1], kd[:, ::-1], vd[:, ::-1], beta[:, ::-1, 1], g[:, ::-1, 1], dn_s0[:, 1])
    o_c = rmsnorm(o_f + o_r[:, ::-1], lp['dn_norm_g'])
    out_c = (o_c.reshape(b, t, DN_WIDTH) * jax.nn.silu(z_c.astype(F32))).astype(dt)

    branches = jnp.stack([out_a, out_b, out_c], axis=2)
    pr = jnp.einsum('btnw,nwd->btnd', branches, lp['w_branch'])
    gt = jax.nn.sigmoid(gates.reshape(b, t, N_BRANCH, D_MODEL))
    y = jnp.sum(gt * pr, axis=2) @ lp['w_out']
    if ctx is None:
        return y, (k, v, jnp.stack(s5_fin_re, axis=1), jnp.stack(s5_fin_im, axis=1),
                   jnp.stack([s_f, s_r], axis=1))
    return y, None


def layer(x, cond, params, l, ctx):
    lp = {name: arr[l] for name, arr in params.items()}
    mod = jax.nn.silu(cond.astype(F32)) @ lp['w_ada'].astype(F32) + lp['b_ada'].astype(F32)
    shift, scale, gate = jnp.split(mod, 3, axis=-1)
    hn = (rmsnorm(x, lp['norm_g']).astype(F32) * (1.0 + scale[:, None]) + shift[:, None]).astype(x.dtype)
    y, st = mixer(hn, lp, 0.8 - 0.6 * math.exp(-0.3 * l), ctx)
    return x + gate[:, None].astype(x.dtype) * y.astype(x.dtype), st


def setup_inputs(seed: int = 0) -> dict:
    key = jax.random.key(seed)
    k = jax.random.split(key, 32)

    def nrm(i, shape, scale=1.0):
        return scale * jax.random.normal(k[i], shape, F32)

    def uni(i, shape, lo, hi):
        return jax.random.uniform(k[i], shape, F32, lo, hi)

    dt0 = jnp.exp(uni(26, (DEPTH, 2, DN_HEADS), math.log(1e-3), math.log(1e-1)))
    return {
        'x_prompt': nrm(0, (BATCH, SEQ, D_MODEL)),
        'x_sample': nrm(1, (DEC_BATCH, DEC_SEQ, D_MODEL)),
        'cache_k': nrm(2, (DEC_BATCH, DEPTH, PAST_LEN, DA_HEADS, 2, DA_HEAD)),
        'cache_v': nrm(3, (DEC_BATCH, DEPTH, PAST_LEN, DA_HEADS, DA_VDIM)),
        'state_s5_re': nrm(4, (DEC_BATCH, DEPTH, 2, S5_GROUPS, S5_STATE), 0.1),
        'state_s5_im': nrm(5, (DEC_BATCH, DEPTH, 2, S5_GROUPS, S5_STATE), 0.1),
        'state_dn': nrm(6, (DEC_BATCH, DEPTH, 2, DN_HEADS, DN_HEAD, DN_HEAD), 0.1),
        'c': nrm(7, (DEC_BATCH, D_MODEL)),
        'c_ctx': nrm(8, (D_MODEL,)),
        'norm_g': 1.0 + nrm(9, (DEPTH, D_MODEL), 0.01),
        'w_ada': nrm(10, (DEPTH, D_MODEL, 3 * D_MODEL), 0.5 * D_MODEL ** -0.5),
        'b_ada': nrm(11, (DEPTH, 3 * D_MODEL), 0.01),
        'w_in': nrm(12, (DEPTH, D_MODEL, N_IN), D_MODEL ** -0.5),
        's5_lam_re': -0.5 + nrm(13, (DEPTH, 2, S5_GROUPS, S5_STATE), 0.01),
        's5_lam_im': math.pi * jnp.arange(S5_STATE, dtype=F32) + nrm(14, (DEPTH, 2, S5_GROUPS, S5_STATE), 0.01),
        's5_log_step': uni(15, (DEPTH, 2, S5_GROUPS), math.log(1e-3), math.log(1e-1)),
        's5_b_re': nrm(16, (DEPTH, 2, S5_GROUPS, S5_STATE, S5_GROUP), 0.7 * S5_GROUP ** -0.5),
        's5_b_im': nrm(17, (DEPTH, 2, S5_GROUPS, S5_STATE, S5_GROUP), 0.7 * S5_GROUP ** -0.5),
        's5_c_re': nrm(18, (DEPTH, 2, S5_GROUPS, S5_GROUP, S5_STATE), 0.7 * S5_STATE ** -0.5),
        's5_c_im': nrm(19, (DEPTH, 2, S5_GROUPS, S5_GROUP, S5_STATE), 0.7 * S5_STATE ** -0.5),
        's5_d': nrm(20, (DEPTH, S5_WIDTH)),
        's5_w_glu': nrm(21, (DEPTH, S5_WIDTH, S5_WIDTH), S5_WIDTH ** -0.5),
        'da_lam': nrm(22, (DEPTH, 4, DA_HEAD), 0.1),
        'da_norm_g': 1.0 + nrm(23, (DEPTH, DA_VDIM), 0.01),
        'dn_conv': nrm(24, (DEPTH, DN_CONV, 3 * DN_WIDTH), DN_CONV ** -0.5),
        'dn_a_log': jnp.log(uni(25, (DEPTH, 2, DN_HEADS), 1.0, 16.0)),
        'dn_dt_bias': dt0 + jnp.log(-jnp.expm1(-dt0)),
        'dn_norm_g': 1.0 + nrm(27, (DEPTH, DN_HEAD), 0.01),
        'w_branch': nrm(28, (DEPTH, N_BRANCH, BRANCH_WIDTH, D_MODEL), BRANCH_WIDTH ** -0.5),
        'w_out': nrm(29, (DEPTH, D_MODEL, D_MODEL), D_MODEL ** -0.5),
        'final_norm_g': 1.0 + nrm(30, (D_MODEL,), 0.01),
    }


def reference(x_prompt, x_sample, cache_k, cache_v, state_s5_re, state_s5_im, state_dn, c, c_ctx,
              norm_g, w_ada, b_ada, w_in, s5_lam_re, s5_lam_im, s5_log_step, s5_b_re, s5_b_im,
              s5_c_re, s5_c_im, s5_d, s5_w_glu, da_lam, da_norm_g, dn_conv, dn_a_log, dn_dt_bias,
              dn_norm_g, w_branch, w_out, final_norm_g):
    params = {
        'norm_g': norm_g, 'w_ada': w_ada, 'b_ada': b_ada, 'w_in': w_in,
        's5_lam_re': s5_lam_re, 's5_lam_im': s5_lam_im, 's5_log_step': s5_log_step,
        's5_b_re': s5_b_re, 's5_b_im': s5_b_im, 's5_c_re': s5_c_re, 's5_c_im': s5_c_im,
        's5_d': s5_d, 's5_w_glu': s5_w_glu, 'da_lam': da_lam, 'da_norm_g': da_norm_g,
        'dn_conv': dn_conv, 'dn_a_log': dn_a_log, 'dn_dt_bias': dn_dt_bias, 'dn_norm_g': dn_norm_g,
        'w_branch': w_branch, 'w_out': w_out,
    }
    x = x_prompt
    states = []
    for l in range(DEPTH):
        x, st = layer(x, c_ctx[None, :], params, l, None)
        states.append(st)
    y_prompt = rmsnorm(x, final_norm_g)
    new_cache_k = jnp.stack([s[0] for s in states], axis=1)
    new_cache_v = jnp.stack([s[1] for s in states], axis=1)
    new_state_s5_re = jnp.stack([s[2] for s in states], axis=1)
    new_state_s5_im = jnp.stack([s[3] for s in states], axis=1)
    new_state_dn = jnp.stack([s[4] for s in states], axis=1)

    x = x_sample
    for l in range(DEPTH):
        ctx = (cache_k[:, l], cache_v[:, l], state_s5_re[:, l], state_s5_im[:, l], state_dn[:, l])
        x, _ = layer(x, c, params, l, ctx)
    y_sample = rmsnorm(x, final_norm_g)
    return (y_prompt, y_sample, new_cache_k, new_cache_v, new_state_s5_re, new_state_s5_im, new_state_dn)
```

```python
import functools
import math

import numpy as np
import jax
import jax.numpy as jnp
from jax import lax
from jax.experimental import pallas as pl
from jax.experimental.pallas import tpu as pltpu

F32 = jnp.float32
BF16 = jnp.bfloat16

D_MODEL = 1024
DEPTH = 2
GRID_W = 64
EPS = 1e-6
S5_WIDTH = 512
S5_GROUP = 16
S5_GROUPS = 32
S5_STATE = 64
S5_CHUNK = 16
S5_PAIRS = S5_GROUPS // 2
S5_ROW = S5_CHUNK * S5_GROUP
DA_HEADS = 4
DA_HEAD = 64
DA_VDIM = 128
DA_WIDTH = 512
ROPE_BASE = 10000.0
DN_HEADS = 4
DN_HEAD = 128
DN_WIDTH = 512
DN_CONV = 5
DN_CHUNK = 64
N_BRANCH = 3
BRANCH_WIDTH = 512
PAST_LEN = 512

COL_UA, COL_ZA, COL_QB, COL_KB, COL_VB, COL_ZB = 0, 512, 1024, 1536, 2048, 2560
COL_QC, COL_ZC, COL_GATES = 3072, 4608, 5120
N_MAIN = 8192
BA_OFF = 5120
GATES_OFF = 5136

VMEM_LIMIT = 56 * 1024 * 1024
HI = lax.Precision.HIGHEST


def _cparams(sem):
    return pltpu.CompilerParams(dimension_semantics=sem, vmem_limit_bytes=VMEM_LIMIT)


def _mm(a, b):
    return jnp.dot(a.astype(BF16), b.astype(BF16), preferred_element_type=F32)


def _mm_nt(a, b):
    return lax.dot_general(a.astype(BF16), b.astype(BF16), (((1,), (1,)), ((), ())),
                           preferred_element_type=F32)


def _silu(x):
    return x * jax.nn.sigmoid(x)


def _ada_kernel(c_ref, w_ref, b_ref, o_ref):
    o_ref[0] = _mm(_silu(c_ref[...]), w_ref[0]) + b_ref[0]


def ada_mod(cond8, w_ada, b_ada):
    tn = 1024
    return pl.pallas_call(
        _ada_kernel,
        grid=(DEPTH, 3 * D_MODEL // tn),
        in_specs=[pl.BlockSpec((8, D_MODEL), lambda l, j: (0, 0)),
                  pl.BlockSpec((1, D_MODEL, tn), lambda l, j: (l, 0, j)),
                  pl.BlockSpec((1, 1, tn), lambda l, j: (l, 0, j))],
        out_specs=pl.BlockSpec((1, 8, tn), lambda l, j: (l, 0, j)),
        out_shape=jax.ShapeDtypeStruct((DEPTH, 8, 3 * D_MODEL), F32),
        compiler_params=_cparams(("parallel", "parallel")),
        name="ada_mod",
    )(cond8, w_ada, b_ada.reshape(DEPTH, 1, 3 * D_MODEL))


def _inproj_kernel(x_ref, g_ref, sc_ref, sh_ref, w1_ref, w2_ref, proj_ref, ba_ref, hn_ref):
    @pl.when(pl.program_id(1) == 0)
    def _():
        x = x_ref[...]
        y = x * lax.rsqrt(jnp.mean(x * x, axis=-1, keepdims=True) + EPS) * g_ref[...]
        hn = (y * (1.0 + sc_ref[0]) + sh_ref[0]).astype(BF16)
        hn_ref[...] = hn
        ba_ref[...] = jnp.dot(hn, w2_ref[...], preferred_element_type=F32)

    proj_ref[...] = jnp.dot(hn_ref[...], w1_ref[...], preferred_element_type=F32)


def inproj(x2, norm_g, scale, shift, w1, w2, rows_per_mod):
    m = x2.shape[0]
    tm, tn = 1024, 1024
    nmod = scale.shape[0]
    mod_idx = lambda i, j: ((i * tm) // rows_per_mod, 0, 0)
    return pl.pallas_call(
        _inproj_kernel,
        grid=(m // tm, N_MAIN // tn),
        in_specs=[pl.BlockSpec((tm, D_MODEL), lambda i, j: (i, 0)),
                  pl.BlockSpec((1, D_MODEL), lambda i, j: (0, 0)),
                  pl.BlockSpec((1, 1, D_MODEL), mod_idx),
                  pl.BlockSpec((1, 1, D_MODEL), mod_idx),
                  pl.BlockSpec((D_MODEL, tn), lambda i, j: (0, j)),
                  pl.BlockSpec((D_MODEL, 128), lambda i, j: (0, 0))],
        out_specs=[pl.BlockSpec((tm, tn), lambda i, j: (i, j)),
                   pl.BlockSpec((tm, 128), lambda i, j: (i, 0))],
        out_shape=[jax.ShapeDtypeStruct((m, N_MAIN), F32),
                   jax.ShapeDtypeStruct((m, 128), F32)],
        scratch_shapes=[pltpu.VMEM((tm, D_MODEL), BF16)],
        compiler_params=_cparams(("parallel", "arbitrary")),
        name="inproj",
    )(x2, norm_g.reshape(1, D_MODEL), scale.reshape(nmod, 1, D_MODEL),
      shift.reshape(nmod, 1, D_MODEL), w1, w2)


def s5_matrices(lam_re, lam_im, log_step, b_re, b_im, c_re, c_im):
    L = S5_CHUNK
    G, P, C = S5_GROUPS, S5_STATE, S5_GROUP
    toep = jnp.zeros((G, L, C, L, C), F32)
    ws_all, wh_all, a_l = [], [], []
    ks = jnp.arange(L + 1, dtype=F32)[:, None, None]
    jj, ii = np.meshgrid(np.arange(L), np.arange(L), indexing="xy")
    lag = np.clip(jj - ii, 0, L - 1)
    causal = jnp.asarray((jj >= ii)[:, :, None, None, None], F32)
    for d in range(2):
        step = jnp.exp(log_step[d])[:, None]
        lr, li = lam_re[d], lam_im[d]
        mag = jnp.exp(lr * step)
        ar, ai = mag * jnp.cos(li * step), mag * jnp.sin(li * step)
        den = lr * lr + li * li
        fr = ((ar - 1.0) * lr + ai * li) / den
        fi = (ai * lr - (ar - 1.0) * li) / den
        bbr = fr[..., None] * b_re[d] - fi[..., None] * b_im[d]
        bbi = fr[..., None] * b_im[d] + fi[..., None] * b_re[d]
        pmag = jnp.exp(ks * (lr * step)[None])
        pr, pi = pmag * jnp.cos(ks * (li * step)[None]), pmag * jnp.sin(ks * (li * step)[None])
        cr, ci = c_re[d], c_im[d]
        ckr = cr[None] * pr[:, :, None, :] - ci[None] * pi[:, :, None, :]
        cki = cr[None] * pi[:, :, None, :] + ci[None] * pr[:, :, None, :]
        kern = (jnp.einsum("kgop,gpi->kgoi", ckr[:L], bbr, precision=HI)
                - jnp.einsum("kgop,gpi->kgoi", cki[:L], bbi, precision=HI))
        t = kern[lag] * causal
        t = jnp.transpose(t, (2, 0, 4, 1, 3))
        e = pr[L - 1 - np.arange(L)], pi[L - 1 - np.arange(L)]
        wsr = e[0][..., None] * bbr[None] - e[1][..., None] * bbi[None]
        wsi = e[0][..., None] * bbi[None] + e[1][..., None] * bbr[None]
        wsr = jnp.transpose(wsr, (1, 0, 3, 2))
        wsi = jnp.transpose(wsi, (1, 0, 3, 2))
        whr = jnp.transpose(ckr[1:], (1, 3, 0, 2))
        whi = -jnp.transpose(cki[1:], (1, 3, 0, 2))
        if d == 1:
            t = t[:, ::-1, :, ::-1, :]
            wsr, wsi = wsr[:, ::-1], wsi[:, ::-1]
            whr, whi = whr[:, :, ::-1], whi[:, :, ::-1]
        toep = toep + t
        ws = jnp.stack([wsr, wsi], 0).reshape(2, S5_PAIRS, 2, S5_ROW, P)
        z = jnp.zeros_like(ws[:, :, 0])
        ws_pair = jnp.concatenate([jnp.concatenate([ws[:, :, 0], z], -1),
                                   jnp.concatenate([z, ws[:, :, 1]], -1)], -2)
        wh = jnp.stack([whr, whi], 0).reshape(2, S5_PAIRS, 2, P, S5_ROW)
        zh = jnp.zeros_like(wh[:, :, 0])
        wh_pair = jnp.concatenate([jnp.concatenate([wh[:, :, 0], zh], -1),
                                   jnp.concatenate([zh, wh[:, :, 1]], -1)], -2)
        ws_all.append(jnp.transpose(ws_pair, (1, 0, 2, 3)))
        wh_all.append(jnp.transpose(wh_pair, (1, 0, 2, 3)))
        a_l.append(jnp.stack([pr[L].reshape(1, G * P), pi[L].reshape(1, G * P)], 0))
    return (toep.reshape(G, S5_ROW, S5_ROW).astype(BF16),
            jnp.stack(ws_all, 0).astype(BF16),
            jnp.stack(wh_all, 0).astype(BF16),
            jnp.stack(a_l, 0))


def _s5_state_kernel(x_ref, ws_ref, s_ref):
    x = x_ref[...]
    s_ref[0, 0] = jnp.dot(x, ws_ref[0, 0, 0], preferred_element_type=F32)
    s_ref[0, 1] = jnp.dot(x, ws_ref[0, 0, 1], preferred_element_type=F32)


def s5_state(xc, ws):
    r = xc.shape[0]
    return pl.pallas_call(
        _s5_state_kernel,
        grid=(2, S5_PAIRS),
        in_specs=[pl.BlockSpec((r, 2 * S5_ROW), lambda d, p: (0, p)),
                  pl.BlockSpec((1, 1, 2, 2 * S5_ROW, 128), lambda d, p: (d, p, 0, 0, 0))],
        out_specs=pl.BlockSpec((1, 2, r, 128), lambda d, p: (d, 0, 0, p)),
        out_shape=jax.ShapeDtypeStruct((2, 2, r, S5_GROUPS * S5_STATE), F32),
        compiler_params=_cparams(("parallel", "parallel")),
        name="s5_state",
    )(xc, ws)


def _s5_scan_kernel(s_ref, a_ref, h0_ref, hin_ref, hfin_ref, *, nb, nchunk):
    ar, ai = a_ref[0, 0], a_ref[0, 1]

    def run(order):
        hr, hi = h0_ref[0, 0], h0_ref[0, 1]
        for c in order:
            rows = slice(c * nb, (c + 1) * nb)
            hin_ref[0, 0, rows, :] = hr
            hin_ref[0, 1, rows, :] = hi
            sr, si = s_ref[0, 0, rows, :], s_ref[0, 1, rows, :]
            hr, hi = ar * hr - ai * hi + sr, ar * hi + ai * hr + si
        hfin_ref[0, 0] = hr
        hfin_ref[0, 1] = hi

    d = pl.program_id(0)
    pl.when(d == 0)(lambda: run(range(nchunk)))
    pl.when(d == 1)(lambda: run(range(nchunk - 1, -1, -1)))


def s5_scan(s, a_l, h0, nb, nchunk):
    r = s.shape[2]
    w = S5_GROUPS * S5_STATE
    tc = 512
    return pl.pallas_call(
        functools.partial(_s5_scan_kernel, nb=nb, nchunk=nchunk),
        grid=(2, w // tc),
        in_specs=[pl.BlockSpec((1, 2, r, tc), lambda d, j: (d, 0, 0, j)),
                  pl.BlockSpec((1, 2, 1, tc), lambda d, j: (d, 0, 0, j)),
                  pl.BlockSpec((1, 2, nb, tc), lambda d, j: (d, 0, 0, j))],
        out_specs=[pl.BlockSpec((1, 2, r, tc), lambda d, j: (d, 0, 0, j)),
                   pl.BlockSpec((1, 2, nb, tc), lambda d, j: (d, 0, 0, j))],
        out_shape=[jax.ShapeDtypeStruct((2, 2, r, w), F32),
                   jax.ShapeDtypeStruct((2, 2, nb, w), F32)],
        compiler_params=_cparams(("parallel", "parallel")),
        name="s5_scan",
    )(s, a_l, h0)


def _s5_out_kernel(x_ref, wt_ref, hin_ref, wh_ref, y_ref):
    x = x_ref[...]
    y = jnp.concatenate(
        [jnp.dot(x[:, :S5_ROW], wt_ref[0], preferred_element_type=F32),
         jnp.dot(x[:, S5_ROW:], wt_ref[1], preferred_element_type=F32)], axis=-1)
    for d in range(2):
        for comp in range(2):
            y = y + jnp.dot(hin_ref[d, comp].astype(BF16), wh_ref[d, 0, comp],
                            preferred_element_type=F32)
    y_ref[...] = y


def s5_out(xc, wt, hin, wh):
    r = xc.shape[0]
    return pl.pallas_call(
        _s5_out_kernel,
        grid=(S5_PAIRS,),
        in_specs=[pl.BlockSpec((r, 2 * S5_ROW), lambda p: (0, p)),
                  pl.BlockSpec((2, S5_ROW, S5_ROW), lambda p: (p, 0, 0)),
                  pl.BlockSpec((2, 2, r, 128), lambda p: (0, 0, 0, p)),
                  pl.BlockSpec((2, 1, 2, 128, 2 * S5_ROW), lambda p: (0, p, 0, 0, 0))],
        out_specs=pl.BlockSpec((r, 2 * S5_ROW), lambda p: (0, p)),
        out_shape=jax.ShapeDtypeStruct((r, S5_GROUPS * S5_ROW), F32),
        compiler_params=_cparams(("parallel",)),
        name="s5_out",
    )(xc, wt, hin, wh)


def _s5_epilogue_kernel(u_ref, y_ref, z_ref, d_ref, w_ref, o_ref):
    ya = jax.nn.gelu(d_ref[...] * u_ref[...] + y_ref[...])
    ya = ya * jax.nn.sigmoid(_mm(ya, w_ref[...]))
    o_ref[...] = (ya * _silu(z_ref[...])).astype(BF16)


def s5_epilogue(proj, y_s5, s5_d, w_glu):
    m = proj.shape[0]
    tm = 512
    return pl.pallas_call(
        _s5_epilogue_kernel,
        grid=(m // tm,),
        in_specs=[pl.BlockSpec((tm, S5_WIDTH), lambda i: (i, COL_UA // S5_WIDTH)),
                  pl.BlockSpec((tm, S5_WIDTH), lambda i: (i, 0)),
                  pl.BlockSpec((tm, S5_WIDTH), lambda i: (i, COL_ZA // S5_WIDTH)),
                  pl.BlockSpec((1, S5_WIDTH), lambda i: (0, 0)),
                  pl.BlockSpec((S5_WIDTH, S5_WIDTH), lambda i: (0, 0))],
        out_specs=pl.BlockSpec((tm, S5_WIDTH), lambda i: (i, 0)),
        out_shape=jax.ShapeDtypeStruct((m, S5_WIDTH), BF16),
        compiler_params=_cparams(("parallel",)),
        name="s5_epilogue",
    )(proj, y_s5, proj, s5_d.reshape(1, S5_WIDTH), w_glu)


def s5_branch(proj, nb, t, mats, s5_d, w_glu, h0):
    wt, ws, wh, a_l = mats
    nchunk = t // S5_CHUNK
    u = proj[:, COL_UA:COL_UA + S5_WIDTH].reshape(nb, nchunk, S5_CHUNK, S5_GROUPS, S5_GROUP)
    xc = jnp.transpose(u, (1, 0, 3, 2, 4)).reshape(nchunk * nb, S5_GROUPS * S5_ROW).astype(BF16)
    s = s5_state(xc, ws)
    hin, hfin = s5_scan(s, a_l, h0, nb, nchunk)
    y = s5_out(xc, wt, hin, wh)
    y = y.reshape(nchunk, nb, S5_GROUPS, S5_CHUNK, S5_GROUP)
    y = jnp.transpose(y, (1, 0, 3, 2, 4)).reshape(nb * t, S5_WIDTH)
    return s5_epilogue(proj, y, s5_d, w_glu), hfin


def _rope_tables(t):
    rows = t // GRID_W
    row = np.repeat(np.arange(rows), GRID_W).astype(np.float32)
    col = np.tile(np.arange(GRID_W), rows).astype(np.float32)
    nf = DA_HEAD // 4
    inv = (ROPE_BASE ** (-jnp.arange(nf, dtype=F32) / nf))

    def tab(pos):
        ang = jnp.asarray(pos)[:, None] * inv[None, :]
        c, s = jnp.cos(ang), jnp.sin(ang)
        return jnp.concatenate([c, c], -1), jnp.concatenate([-s, s], -1)

    cr, sr = tab(row)
    cc, sc = tab(col)
    cos = jnp.concatenate([cr, cc], -1)
    sin = jnp.concatenate([sr, sc], -1)
    return jnp.tile(cos, (1, 2)), jnp.tile(sin, (1, 2))


def _rope(x, cos, sin):
    lane = lax.broadcasted_iota(jnp.int32, x.shape, 1)
    swapped = jnp.where((lane % 32) < 16, pltpu.roll(x, 112, 1), pltpu.roll(x, 16, 1))
    return x * cos + swapped * sin


def _attn_kernel(*refs, lam_init, t, s_tot, tq, with_ctx):
    if with_ctx:
        (q_ref, k_ref, v_ref, z_ref, kc_ref, vc_ref, cq_ref, sq_ref, ck_ref, sk_ref,
         lam_ref, ng_ref, o_ref, kall_ref, vall_ref) = refs
    else:
        q_ref, k_ref, v_ref, z_ref, lam_ref, ng_ref, o_ref, kall_ref, vall_ref = refs

    @pl.when(pl.program_id(1) == 0)
    def _():
        if with_ctx:
            for h in range(DA_HEADS):
                hs = slice(h * 128, (h + 1) * 128)
                kall_ref[0:t, hs] = _rope(k_ref[0, :, hs], ck_ref[...], sk_ref[...]).astype(BF16)
            kall_ref[t:s_tot, :] = kc_ref[0].astype(BF16)
            vall_ref[0:t, :] = v_ref[0].astype(BF16)
            vall_ref[t:s_tot, :] = vc_ref[0].astype(BF16)
        else:
            kall_ref[...] = k_ref[0].astype(BF16)
            vall_ref[...] = v_ref[0].astype(BF16)

    lp = lam_ref[...]
    lam = (jnp.exp(jnp.sum(lp[0:1] * lp[1:2], axis=-1, keepdims=True))
           - jnp.exp(jnp.sum(lp[2:3] * lp[3:4], axis=-1, keepdims=True)) + lam_init)
    lane = lax.broadcasted_iota(jnp.int32, (tq, 128), 1)
    for h in range(DA_HEADS):
        hs = slice(h * 128, (h + 1) * 128)
        q = q_ref[0, :, hs]
        if with_ctx:
            q = _rope(q, cq_ref[...], sq_ref[...])
        q = q * (DA_HEAD ** -0.5)
        k = kall_ref[:, hs]
        probs = []
        for m in range(2):
            qm = jnp.where((lane < DA_HEAD) if m == 0 else (lane >= DA_HEAD), q, 0.0)
            sc = _mm_nt(qm, k)
            sc = sc - jnp.max(sc, axis=-1, keepdims=True)
            e = jnp.exp(sc)
            probs.append(e / jnp.sum(e, axis=-1, keepdims=True))
        a = probs[0] - lam * probs[1]
        o = _mm(a, vall_ref[:, hs])
        o = o * lax.rsqrt(jnp.mean(o * o, axis=-1, keepdims=True) + EPS) * ng_ref[...]
        o = o * (1.0 - lam_init)
        o_ref[0, :, hs] = (o * _silu(z_ref[0, :, hs])).astype(BF16)


def diff_attention(proj3, layer, lam_init, da_lam, da_norm_g, ctx_kv):
    nb, t, _ = proj3.shape
    with_ctx = ctx_kv is not None
    s_tot = t + (PAST_LEN if with_ctx else 0)
    tq = 256
    wb = DA_WIDTH
    in_specs = [pl.BlockSpec((1, tq, wb), lambda b, i: (b, i, COL_QB // wb)),
                pl.BlockSpec((1, t, wb), lambda b, i: (b, 0, COL_KB // wb)),
                pl.BlockSpec((1, t, wb), lambda b, i: (b, 0, COL_VB // wb)),
                pl.BlockSpec((1, tq, wb), lambda b, i: (b, i, COL_ZB // wb))]
    args = [proj3, proj3, proj3, proj3]
    if with_ctx:
        kc, vc = ctx_kv
        cos, sin = _rope_tables(t)
        in_specs += [pl.BlockSpec((1, PAST_LEN, wb), lambda b, i: (b, layer, 0)),
                     pl.BlockSpec((1, PAST_LEN, wb), lambda b, i: (b, layer, 0)),
                     pl.BlockSpec((tq, 128), lambda b, i: (i, 0)),
                     pl.BlockSpec((tq, 128), lambda b, i: (i, 0)),
                     pl.BlockSpec((t, 128), lambda b, i: (0, 0)),
                     pl.BlockSpec((t, 128), lambda b, i: (0, 0))]
        args += [kc, vc, cos, sin, cos, sin]
    in_specs += [pl.BlockSpec((4, DA_HEAD), lambda b, i: (0, 0)),
                 pl.BlockSpec((1, DA_VDIM), lambda b, i: (0, 0))]
    args += [da_lam, da_norm_g.reshape(1, DA_VDIM)]
    return pl.pallas_call(
        functools.partial(_attn_kernel, lam_init=lam_init, t=t, s_tot=s_tot, tq=tq, with_ctx=with_ctx),
        grid=(nb, t // tq),
        in_specs=in_specs,
        out_specs=pl.BlockSpec((1, tq, wb), lambda b, i: (b, i, 0)),
        out_shape=jax.ShapeDtypeStruct((nb, t, wb), BF16),
        scratch_shapes=[pltpu.VMEM((s_tot, wb), BF16), pltpu.VMEM((s_tot, wb), BF16)],
        compiler_params=_cparams(("parallel", "arbitrary")),
        name="diff_attention",
    )(*args)


DN_PAD = 8
DN_RT = 128


def _tri_inv(mm, ri, ci, eye):
    s = 1
    tinv = eye - jnp.where((ri // 2) == (ci // 2), mm, 0.0)
    s = 2
    while s < DN_CHUNK:
        off = jnp.where(((ri // (2 * s)) == (ci // (2 * s))) & ((ri // s) != (ci // s)), mm, 0.0)
        tinv = tinv - _mm(tinv, _mm(off, tinv))
        s *= 2
    return tinv


def _dn_kernel(*refs, t, nchunk, with_s0):
    if with_s0:
        (qkv_ref, z_ref, ba_ref, cw_ref, alog_ref, dtb_ref, ng_ref, s0_ref,
         out_ref, sfin_ref, xp_ref, qkvn_ref, oacc_ref, st_ref) = refs
    else:
        (qkv_ref, z_ref, ba_ref, cw_ref, alog_ref, dtb_ref, ng_ref,
         out_ref, sfin_ref, xp_ref, qkvn_ref, oacc_ref, st_ref) = refs
    n = pl.program_id(1)
    cd = DN_CHUNK
    w3 = 3 * DN_WIDTH

    @pl.when(n == 0)
    def _init():
        xp_ref[0:DN_PAD, :] = jnp.zeros((DN_PAD, w3), F32)
        xp_ref[DN_PAD + t:2 * DN_PAD + t, :] = jnp.zeros((DN_PAD, w3), F32)
        xp_ref[DN_PAD:DN_PAD + t, :] = qkv_ref[0]
        half = DN_CONV // 2
        for r in range(t // DN_RT):
            for sec in range(3):
                for h in range(DN_HEADS):
                    cs = slice(sec * DN_WIDTH + h * DN_HEAD, sec * DN_WIDTH + (h + 1) * DN_HEAD)
                    acc = jnp.zeros((DN_RT, DN_HEAD), F32)
                    for j in range(DN_CONV):
                        r0 = DN_PAD + r * DN_RT + j - half
                        acc = acc + xp_ref[r0:r0 + DN_RT, cs] * cw_ref[j:j + 1, cs]
                    y = _silu(acc)
                    if sec < 2:
                        y = y * lax.rsqrt(jnp.sum(y * y, axis=-1, keepdims=True) + EPS)
                    if sec == 0:
                        y = y * (DN_HEAD ** -0.5)
                    qkvn_ref[r * DN_RT:(r + 1) * DN_RT, cs] = y
        oacc_ref[...] = jnp.zeros_like(oacc_ref)
        if with_s0:
            st_ref[...] = s0_ref[0, 0]
        else:
            st_ref[...] = jnp.zeros_like(st_ref)

    ri = lax.broadcasted_iota(jnp.int32, (cd, cd), 0)
    ci = lax.broadcasted_iota(jnp.int32, (cd, cd), 1)
    eye = (ri == ci).astype(F32)
    for d in range(2):
        c = n if d == 0 else nchunk - 1 - n
        r0 = pl.multiple_of(c * cd, cd)
        incl = (ci <= ri) if d == 0 else (ci >= ri)
        strict = (ci < ri) if d == 0 else (ci > ri)
        inclf = incl.astype(F32)
        ba = ba_ref[0, pl.ds(r0, cd), :]
        beta_all = jax.nn.sigmoid(ba)
        g_all = -jnp.exp(alog_ref[...]) * jax.nn.softplus(ba + dtb_ref[...])
        gc = jnp.dot(inclf, g_all, precision=HI, preferred_element_type=F32)
        gct = lax.dot_general(g_all.T, inclf, (((1,), (1,)), ((), ())), precision=HI,
                              preferred_element_type=F32)
        last = cd - 1 if d == 0 else 0
        for h in range(DN_HEADS):
            hs = slice(h * DN_HEAD, (h + 1) * DN_HEAD)
            col = 2 * DN_HEADS + d * DN_HEADS + h
            gcol = gc[:, col:col + 1]
            grow = gct[col:col + 1, :]
            gtot = gc[last:last + 1, col:col + 1]
            beta = beta_all[:, d * DN_HEADS + h:d * DN_HEADS + h + 1]
            decay = jnp.where(incl, jnp.exp(jnp.where(incl, gcol - grow, 0.0)), 0.0)
            q = qkvn_ref[pl.ds(r0, cd), hs]
            k = qkvn_ref[pl.ds(r0, cd), DN_WIDTH + h * DN_HEAD:DN_WIDTH + (h + 1) * DN_HEAD]
            v = qkvn_ref[pl.ds(r0, cd), 2 * DN_WIDTH + h * DN_HEAD:2 * DN_WIDTH + (h + 1) * DN_HEAD]
            kb = k * beta
            mm = jnp.where(strict, _mm_nt(kb, k) * decay, 0.0)
            qk = _mm_nt(q, k) * decay
            tinv = _tri_inv(mm, ri, ci, eye)
            eg = jnp.exp(gcol)
            uw = _mm(tinv, jnp.concatenate([v * beta, kb * eg], axis=-1))
            st = st_ref[d, h]
            ws_qs = _mm(jnp.concatenate([uw[:, DN_HEAD:], q * eg], axis=0), st)
            v_new = uw[:, :DN_HEAD] - ws_qs[:cd]
            o = ws_qs[cd:] + _mm(qk, v_new)
            kdec = k * jnp.exp(gtot - gcol)
            st_ref[d, h] = st * jnp.exp(gtot) + _mm(kdec.T, v_new)
            oacc_ref[pl.ds(r0, cd), hs] += o

    @pl.when(n == nchunk - 1)
    def _fin():
        for h in range(DN_HEADS):
            hs = slice(h * DN_HEAD, (h + 1) * DN_HEAD)
            o = oacc_ref[:, hs]
            o = o * lax.rsqrt(jnp.mean(o * o, axis=-1, keepdims=True) + EPS) * ng_ref[...]
            out_ref[0, :, hs] = (o * _silu(z_ref[0, :, hs])).astype(BF16)
        sfin_ref[0] = st_ref[...]


def deltanet(proj3, ba3, layer, conv_w, a_log, dt_bias, norm_g, s0):
    nb, t, _ = proj3.shape
    nchunk = t // DN_CHUNK
    with_s0 = s0 is not None
    w3 = 3 * DN_WIDTH
    pad = jnp.zeros((2 * DN_HEADS,), F32)
    alog_row = jnp.concatenate([pad, a_log.reshape(-1), jnp.zeros((128 - 4 * DN_HEADS,), F32)]).reshape(1, 128)
    dtb_row = jnp.concatenate([pad, dt_bias.reshape(-1), jnp.zeros((128 - 4 * DN_HEADS,), F32)]).reshape(1, 128)
    in_specs = [pl.BlockSpec((1, t, w3), lambda b, n: (b, 0, COL_QC // w3)),
                pl.BlockSpec((1, t, DN_WIDTH), lambda b, n: (b, 0, COL_ZC // DN_WIDTH)),
                pl.BlockSpec((1, t, 128), lambda b, n: (b, 0, 0)),
                pl.BlockSpec((8, w3), lambda b, n: (0, 0)),
                pl.BlockSpec((1, 128), lambda b, n: (0, 0)),
                pl.BlockSpec((1, 128), lambda b, n: (0, 0)),
                pl.BlockSpec((1, DN_HEAD), lambda b, n: (0, 0))]
    args = [proj3, proj3, ba3, jnp.pad(conv_w, ((0, 8 - DN_CONV), (0, 0))), alog_row, dtb_row,
            norm_g.reshape(1, DN_HEAD)]
    if with_s0:
        in_specs.append(pl.BlockSpec((1, 1, 2, DN_HEADS, DN_HEAD, DN_HEAD), lambda b, n: (b, layer, 0, 0, 0, 0)))
        args.append(s0)
    return pl.pallas_call(
        functools.partial(_dn_kernel, t=t, nchunk=nchunk, with_s0=with_s0),
        grid=(nb, nchunk),
        in_specs=in_specs,
        out_specs=[pl.BlockSpec((1, t, DN_WIDTH), lambda b, n: (b, 0, 0)),
                   pl.BlockSpec((1, 2, DN_HEADS, DN_HEAD, DN_HEAD), lambda b, n: (b, 0, 0, 0, 0))],
        out_shape=[jax.ShapeDtypeStruct((nb, t, DN_WIDTH), BF16),
                   jax.ShapeDtypeStruct((nb, 2, DN_HEADS, DN_HEAD, DN_HEAD), F32)],
        scratch_shapes=[pltpu.VMEM((t + 2 * DN_PAD, w3), F32),
                        pltpu.VMEM((t, w3), F32),
                        pltpu.VMEM((t, DN_WIDTH), F32),
                        pltpu.VMEM((2, DN_HEADS, DN_HEAD, DN_HEAD), F32)],
        compiler_params=_cparams(("parallel", "arbitrary")),
        name="deltanet",
    )(*args)


def _merge_kernel(oa_ref, ob_ref, oc_ref, ga_ref, gb_ref, gc_ref, wb_ref, wo_ref, x_ref, gate_ref, fg_ref,
                  *outs, final):
    acc = None
    for i, (o_ref, g_ref) in enumerate(((oa_ref, ga_ref), (ob_ref, gb_ref), (oc_ref, gc_ref))):
        pr = jnp.dot(o_ref[...], wb_ref[i], preferred_element_type=F32)
        term = jax.nn.sigmoid(g_ref[...]) * pr
        acc = term if acc is None else acc + term
    y = jnp.dot(acc.astype(BF16), wo_ref[...], preferred_element_type=F32)
    xn = x_ref[...] + gate_ref[0] * y
    outs[0][...] = xn
    if final:
        yn = xn * lax.rsqrt(jnp.mean(xn * xn, axis=-1, keepdims=True) + EPS) * fg_ref[...]
        outs[1][...] = yn


def merge(out_a, out_b, out_c, proj, w_branch, w_out, x2, gate, final_g, rows_per_mod, final):
    m = x2.shape[0]
    tm = 512
    nmod = gate.shape[0]
    row = lambda i: (i, 0)
    out_specs = [pl.BlockSpec((tm, D_MODEL), row)]
    out_shape = [jax.ShapeDtypeStruct((m, D_MODEL), F32)]
    if final:
        out_specs.append(pl.BlockSpec((tm, D_MODEL), row))
        out_shape.append(jax.ShapeDtypeStruct((m, D_MODEL), F32))
    return pl.pallas_call(
        functools.partial(_merge_kernel, final=final),
        grid=(m // tm,),
        in_specs=[pl.BlockSpec((tm, BRANCH_WIDTH), row),
                  pl.BlockSpec((tm, BRANCH_WIDTH), row),
                  pl.BlockSpec((tm, BRANCH_WIDTH), row),
                  pl.BlockSpec((tm, D_MODEL), lambda i: (i, COL_GATES // D_MODEL)),
                  pl.BlockSpec((tm, D_MODEL), lambda i: (i, COL_GATES // D_MODEL + 1)),
                  pl.BlockSpec((tm, D_MODEL), lambda i: (i, COL_GATES // D_MODEL + 2)),
                  pl.BlockSpec((N_BRANCH, BRANCH_WIDTH, D_MODEL), lambda i: (0, 0, 0)),
                  pl.BlockSpec((D_MODEL, D_MODEL), lambda i: (0, 0)),
                  pl.BlockSpec((tm, D_MODEL), row),
                  pl.BlockSpec((1, 1, D_MODEL), lambda i: ((i * tm) // rows_per_mod, 0, 0)),
                  pl.BlockSpec((1, D_MODEL), lambda i: (0, 0))],
        out_specs=out_specs,
        out_shape=out_shape,
        compiler_params=_cparams(("parallel",)),
        name="merge",
    )(out_a, out_b, out_c, proj, proj, proj, w_branch, w_out, x2, gate.reshape(nmod, 1, D_MODEL),
      final_g.reshape(1, D_MODEL))


def _run_pass(x, mod, wts, lam_inits, final_g, ctx):
    nb, t, _ = x.shape
    m = nb * t
    nmod = mod.shape[1]
    rows_per_mod = m if nmod == 1 else t
    x2 = x.reshape(m, D_MODEL)
    states = []
    y = None
    for l in range(DEPTH):
        w = wts[l]
        shift, scale, gate = jnp.split(mod[l], 3, axis=-1)
        proj, ba = inproj(x2, w["norm_g"], scale, shift, w["w1"], w["w2"], rows_per_mod)
        proj3 = proj.reshape(nb, t, N_MAIN)
        if ctx is None:
            h0 = jnp.zeros((2, 2, nb, S5_GROUPS * S5_STATE), F32)
            ctx_kv, s0 = None, None
        else:
            cache_k, cache_v, st_re, st_im, st_dn = ctx
            h0 = jnp.stack([st_re[:, l], st_im[:, l]], 0)
            h0 = jnp.transpose(h0, (2, 0, 1, 3, 4)).reshape(2, 2, nb, S5_GROUPS * S5_STATE)
            ctx_kv, s0 = (cache_k, cache_v), st_dn
        out_a, hfin = s5_branch(proj, nb, t, w["s5_mats"], w["s5_d"], w["w_glu"], h0)
        out_b = diff_attention(proj3, l, lam_inits[l], w["da_lam"], w["da_norm_g"], ctx_kv)
        out_c, sfin = deltanet(proj3, ba.reshape(nb, t, 128), l, w["dn_conv"], w["dn_a_log"],
                               w["dn_dt_bias"], w["dn_norm_g"], s0)
        final = l == DEPTH - 1
        res = merge(out_a, out_b.reshape(m, DA_WIDTH), out_c.reshape(m, DN_WIDTH), proj,
                    w["w_branch"], w["w_out"], x2, gate, final_g, rows_per_mod, final)
        x2 = res[0]
        if final:
            y = res[1]
        if ctx is None:
            k_new = proj3[:, :, COL_KB:COL_KB + 512].reshape(nb, t, DA_HEADS, 2, DA_HEAD)
            v_new = proj3[:, :, COL_VB:COL_VB + 512].reshape(nb, t, DA_HEADS, DA_VDIM)
            hf = hfin.reshape(2, 2, nb, S5_GROUPS, S5_STATE)
            states.append((k_new, v_new, jnp.transpose(hf[:, 0], (1, 0, 2, 3)),
                           jnp.transpose(hf[:, 1], (1, 0, 2, 3)), sfin))
    return y.reshape(nb, t, D_MODEL), states


def kernel(x_prompt, x_sample, cache_k, cache_v, state_s5_re, state_s5_im, state_dn, c, c_ctx,
           norm_g, w_ada, b_ada, w_in, s5_lam_re, s5_lam_im, s5_log_step, s5_b_re, s5_b_im,
           s5_c_re, s5_c_im, s5_d, s5_w_glu, da_lam, da_norm_g, dn_conv, dn_a_log, dn_dt_bias,
           dn_norm_g, w_branch, w_out, final_norm_g):
    nb_dec = x_sample.shape[0]
    cond8 = jnp.concatenate([c_ctx[None, :], c, jnp.zeros((8 - 1 - nb_dec, D_MODEL), F32)], 0)
    mod = ada_mod(cond8, w_ada, b_ada)
    w1 = jnp.concatenate([w_in[:, :, :BA_OFF], w_in[:, :, GATES_OFF:]], -1).astype(BF16)
    w2 = jnp.pad(w_in[:, :, BA_OFF:GATES_OFF], ((0, 0), (0, 0), (0, 128 - (GATES_OFF - BA_OFF)))).astype(BF16)
    wts = []
    for l in range(DEPTH):
        wts.append(dict(
            norm_g=norm_g[l], w1=w1[l], w2=w2[l],
            s5_mats=s5_matrices(s5_lam_re[l], s5_lam_im[l], s5_log_step[l], s5_b_re[l], s5_b_im[l],
                                s5_c_re[l], s5_c_im[l]),
            s5_d=s5_d[l], w_glu=s5_w_glu[l].astype(BF16), da_lam=da_lam[l], da_norm_g=da_norm_g[l],
            dn_conv=dn_conv[l], dn_a_log=dn_a_log[l], dn_dt_bias=dn_dt_bias[l], dn_norm_g=dn_norm_g[l],
            w_branch=w_branch[l].astype(BF16), w_out=w_out[l].astype(BF16)))
    lam_inits = [0.8 - 0.6 * math.exp(-0.3 * l) for l in range(DEPTH)]

    y_prompt, states = _run_pass(x_prompt, mod[:, 0:1], wts, lam_inits, final_norm_g, None)
    ctx = (cache_k.reshape(nb_dec, DEPTH * PAST_LEN, DA_WIDTH),
           cache_v.reshape(nb_dec, DEPTH * PAST_LEN, DA_WIDTH), state_s5_re, state_s5_im, state_dn)
    y_sample, _ = _run_pass(x_sample, mod[:, 1:1 + nb_dec], wts, lam_inits, final_norm_g, ctx)

    new_cache_k = jnp.stack([s[0] for s in states], axis=1)
    new_cache_v = jnp.stack([s[1] for s in states], axis=1)
    new_s5_re = jnp.stack([s[2] for s in states], axis=1)
    new_s5_im = jnp.stack([s[3] for s in states], axis=1)
    new_dn = jnp.stack([s[4] for s in states], axis=1)
    return (y_prompt, y_sample, new_cache_k, new_cache_v, new_s5_re, new_s5_im, new_dn)
```

```python
import functools
import math

import numpy as np
import jax
import jax.numpy as jnp
from jax import lax
from jax.experimental import pallas as pl
from jax.experimental.pallas import tpu as pltpu

F32 = jnp.float32
BF16 = jnp.bfloat16

D_MODEL = 1024
DEPTH = 2
GRID_W = 64
EPS = 1e-6
S5_WIDTH = 512
S5_GROUP = 16
S5_GROUPS = 32
S5_STATE = 64
S5_CHUNK = 16
S5_PAIRS = S5_GROUPS // 2
S5_ROW = S5_CHUNK * S5_GROUP
DA_HEADS = 4
DA_HEAD = 64
DA_VDIM = 128
DA_WIDTH = 512
ROPE_BASE = 10000.0
DN_HEADS = 4
DN_HEAD = 128
DN_WIDTH = 512
DN_CONV = 5
DN_CHUNK = 64
N_BRANCH = 3
BRANCH_WIDTH = 512
PAST_LEN = 512

COL_UA, COL_ZA, COL_QB, COL_KB, COL_VB, COL_ZB = 0, 512, 1024, 1536, 2048, 2560
COL_QC, COL_ZC, COL_GATES = 3072, 4608, 5120
N_MAIN = 8192
BA_OFF = 5120
GATES_OFF = 5136

VMEM_LIMIT = 56 * 1024 * 1024
HI = lax.Precision.HIGHEST


def _cparams(sem):
    return pltpu.CompilerParams(dimension_semantics=sem, vmem_limit_bytes=VMEM_LIMIT)


def _mm(a, b):
    return jnp.dot(a.astype(BF16), b.astype(BF16), preferred_element_type=F32)


def _mm_nt(a, b):
    return lax.dot_general(a.astype(BF16), b.astype(BF16), (((1,), (1,)), ((), ())),
                           preferred_element_type=F32)


def _silu(x):
    return x * jax.nn.sigmoid(x)


def _ada_kernel(c_ref, w_ref, b_ref, o_ref):
    o_ref[0] = _mm(_silu(c_ref[...]), w_ref[0]) + b_ref[0]


def ada_mod(cond8, w_ada, b_ada):
    tn = 1024
    return pl.pallas_call(
        _ada_kernel,
        grid=(DEPTH, 3 * D_MODEL // tn),
        in_specs=[pl.BlockSpec((8, D_MODEL), lambda l, j: (0, 0)),
                  pl.BlockSpec((1, D_MODEL, tn), lambda l, j: (l, 0, j)),
                  pl.BlockSpec((1, 1, tn), lambda l, j: (l, 0, j))],
        out_specs=pl.BlockSpec((1, 8, tn), lambda l, j: (l, 0, j)),
        out_shape=jax.ShapeDtypeStruct((DEPTH, 8, 3 * D_MODEL), F32),
        compiler_params=_cparams(("parallel", "parallel")),
        name="ada_mod",
    )(cond8, w_ada, b_ada.reshape(DEPTH, 1, 3 * D_MODEL))


def _inproj_kernel(x_ref, g_ref, sc_ref, sh_ref, w1_ref, w2_ref, proj_ref, ba_ref, hn_ref):
    @pl.when(pl.program_id(1) == 0)
    def _():
        x = x_ref[...]
        y = x * lax.rsqrt(jnp.mean(x * x, axis=-1, keepdims=True) + EPS) * g_ref[...]
        hn = (y * (1.0 + sc_ref[0]) + sh_ref[0]).astype(BF16)
        hn_ref[...] = hn
        ba_ref[...] = jnp.dot(hn, w2_ref[...], preferred_element_type=F32)

    proj_ref[...] = jnp.dot(hn_ref[...], w1_ref[...], preferred_element_type=F32)


def inproj(x2, norm_g, scale, shift, w1, w2, rows_per_mod):
    m = x2.shape[0]
    tm, tn = 1024, 1024
    nmod = scale.shape[0]
    mod_idx = lambda i, j: ((i * tm) // rows_per_mod, 0, 0)
    return pl.pallas_call(
        _inproj_kernel,
        grid=(m // tm, N_MAIN // tn),
        in_specs=[pl.BlockSpec((tm, D_MODEL), lambda i, j: (i, 0)),
                  pl.BlockSpec((1, D_MODEL), lambda i, j: (0, 0)),
                  pl.BlockSpec((1, 1, D_MODEL), mod_idx),
                  pl.BlockSpec((1, 1, D_MODEL), mod_idx),
                  pl.BlockSpec((D_MODEL, tn), lambda i, j: (0, j)),
                  pl.BlockSpec((D_MODEL, 128), lambda i, j: (0, 0))],
        out_specs=[pl.BlockSpec((tm, tn), lambda i, j: (i, j)),
                   pl.BlockSpec((tm, 128), lambda i, j: (i, 0))],
        out_shape=[jax.ShapeDtypeStruct((m, N_MAIN), F32),
                   jax.ShapeDtypeStruct((m, 128), F32)],
        scratch_shapes=[pltpu.VMEM((tm, D_MODEL), BF16)],
        compiler_params=_cparams(("parallel", "arbitrary")),
        name="inproj",
    )(x2, norm_g.reshape(1, D_MODEL), scale.reshape(nmod, 1, D_MODEL),
      shift.reshape(nmod, 1, D_MODEL), w1, w2)


def s5_matrices(lam_re, lam_im, log_step, b_re, b_im, c_re, c_im):
    L = S5_CHUNK
    G, P, C = S5_GROUPS, S5_STATE, S5_GROUP
    toep = jnp.zeros((G, L, C, L, C), F32)
    ws_all, wh_all, a_l = [], [], []
    ks = jnp.arange(L + 1, dtype=F32)[:, None, None]
    jj, ii = np.meshgrid(np.arange(L), np.arange(L), indexing="xy")
    lag = np.clip(jj - ii, 0, L - 1)
    causal = jnp.asarray((jj >= ii)[:, :, None, None, None], F32)
    for d in range(2):
        step = jnp.exp(log_step[d])[:, None]
        lr, li = lam_re[d], lam_im[d]
        mag = jnp.exp(lr * step)
        ar, ai = mag * jnp.cos(li * step), mag * jnp.sin(li * step)
        den = lr * lr + li * li
        fr = ((ar - 1.0) * lr + ai * li) / den
        fi = (ai * lr - (ar - 1.0) * li) / den
        bbr = fr[..., None] * b_re[d] - fi[..., None] * b_im[d]
        bbi = fr[..., None] * b_im[d] + fi[..., None] * b_re[d]
        pmag = jnp.exp(ks * (lr * step)[None])
        pr, pi = pmag * jnp.cos(ks * (li * step)[None]), pmag * jnp.sin(ks * (li * step)[None])
        cr, ci = c_re[d], c_im[d]
        ckr = cr[None] * pr[:, :, None, :] - ci[None] * pi[:, :, None, :]
        cki = cr[None] * pi[:, :, None, :] + ci[None] * pr[:, :, None, :]
        kern = (jnp.einsum("kgop,gpi->kgoi", ckr[:L], bbr, precision=HI)
                - jnp.einsum("kgop,gpi->kgoi", cki[:L], bbi, precision=HI))
        t = kern[lag] * causal
        t = jnp.transpose(t, (2, 0, 4, 1, 3))
        e = pr[L - 1 - np.arange(L)], pi[L - 1 - np.arange(L)]
        wsr = e[0][..., None] * bbr[None] - e[1][..., None] * bbi[None]
        wsi = e[0][..., None] * bbi[None] + e[1][..., None] * bbr[None]
        wsr = jnp.transpose(wsr, (1, 0, 3, 2))
        wsi = jnp.transpose(wsi, (1, 0, 3, 2))
        whr = jnp.transpose(ckr[1:], (1, 3, 0, 2))
        whi = -jnp.transpose(cki[1:], (1, 3, 0, 2))
        if d == 1:
            t = t[:, ::-1, :, ::-1, :]
            wsr, wsi = wsr[:, ::-1], wsi[:, ::-1]
            whr, whi = whr[:, :, ::-1], whi[:, :, ::-1]
        toep = toep + t
        ws = jnp.stack([wsr, wsi], 0).reshape(2, S5_PAIRS, 2, S5_ROW, P)
        z = jnp.zeros_like(ws[:, :, 0])
        ws_pair = jnp.concatenate([jnp.concatenate([ws[:, :, 0], z], -1),
                                   jnp.concatenate([z, ws[:, :, 1]], -1)], -2)
        wh = jnp.stack([whr, whi], 0).reshape(2, S5_PAIRS, 2, P, S5_ROW)
        zh = jnp.zeros_like(wh[:, :, 0])
        wh_pair = jnp.concatenate([jnp.concatenate([wh[:, :, 0], zh], -1),
                                   jnp.concatenate([zh, wh[:, :, 1]], -1)], -2)
        ws_all.append(jnp.transpose(ws_pair, (1, 0, 2, 3)))
        wh_all.append(jnp.transpose(wh_pair, (1, 0, 2, 3)))
        a_l.append(jnp.stack([pr[L].reshape(1, G * P), pi[L].reshape(1, G * P)], 0))
    return (toep.reshape(G, S5_ROW, S5_ROW).astype(BF16),
            jnp.stack(ws_all, 0).astype(BF16),
            jnp.stack(wh_all, 0).astype(BF16),
            jnp.stack(a_l, 0))


def _s5_state_kernel(x_ref, ws_ref, s_ref):
    x = x_ref[...]
    s_ref[0, 0] = jnp.dot(x, ws_ref[0, 0, 0], preferred_element_type=F32)
    s_ref[0, 1] = jnp.dot(x, ws_ref[0, 0, 1], preferred_element_type=F32)


def s5_state(xc, ws):
    r = xc.shape[0]
    return pl.pallas_call(
        _s5_state_kernel,
        grid=(2, S5_PAIRS),
        in_specs=[pl.BlockSpec((r, 2 * S5_ROW), lambda d, p: (0, p)),
                  pl.BlockSpec((1, 1, 2, 2 * S5_ROW, 128), lambda d, p: (d, p, 0, 0, 0))],
        out_specs=pl.BlockSpec((1, 2, r, 128), lambda d, p: (d, 0, 0, p)),
        out_shape=jax.ShapeDtypeStruct((2, 2, r, S5_GROUPS * S5_STATE), F32),
        compiler_params=_cparams(("parallel", "parallel")),
        name="s5_state",
    )(xc, ws)


def _s5_scan_kernel(s_ref, a_ref, h0_ref, hin_ref, hfin_ref, *, nb, nchunk):
    ar, ai = a_ref[0, 0], a_ref[0, 1]

    def run(order):
        hr, hi = h0_ref[0, 0], h0_ref[0, 1]
        for c in order:
            rows = slice(c * nb, (c + 1) * nb)
            hin_ref[0, 0, rows, :] = hr
            hin_ref[0, 1, rows, :] = hi
            sr, si = s_ref[0, 0, rows, :], s_ref[0, 1, rows, :]
            hr, hi = ar * hr - ai * hi + sr, ar * hi + ai * hr + si
        hfin_ref[0, 0] = hr
        hfin_ref[0, 1] = hi

    d = pl.program_id(0)
    pl.when(d == 0)(lambda: run(range(nchunk)))
    pl.when(d == 1)(lambda: run(range(nchunk - 1, -1, -1)))


def s5_scan(s, a_l, h0, nb, nchunk):
    r = s.shape[2]
    w = S5_GROUPS * S5_STATE
    tc = 512
    return pl.pallas_call(
        functools.partial(_s5_scan_kernel, nb=nb, nchunk=nchunk),
        grid=(2, w // tc),
        in_specs=[pl.BlockSpec((1, 2, r, tc), lambda d, j: (d, 0, 0, j)),
                  pl.BlockSpec((1, 2, 1, tc), lambda d, j: (d, 0, 0, j)),
                  pl.BlockSpec((1, 2, nb, tc), lambda d, j: (d, 0, 0, j))],
        out_specs=[pl.BlockSpec((1, 2, r, tc), lambda d, j: (d, 0, 0, j)),
                   pl.BlockSpec((1, 2, nb, tc), lambda d, j: (d, 0, 0, j))],
        out_shape=[jax.ShapeDtypeStruct((2, 2, r, w), F32),
                   jax.ShapeDtypeStruct((2, 2, nb, w), F32)],
        compiler_params=_cparams(("parallel", "parallel")),
        name="s5_scan",
    )(s, a_l, h0)


def _s5_out_kernel(x_ref, wt_ref, hin_ref, wh_ref, y_ref):
    x = x_ref[...]
    y = jnp.concatenate(
        [jnp.dot(x[:, :S5_ROW], wt_ref[0], preferred_element_type=F32),
         jnp.dot(x[:, S5_ROW:], wt_ref[1], preferred_element_type=F32)], axis=-1)
    for d in range(2):
        for comp in range(2):
            y = y + jnp.dot(hin_ref[d, comp].astype(BF16), wh_ref[d, 0, comp],
                            preferred_element_type=F32)
    y_ref[...] = y


def s5_out(xc, wt, hin, wh):
    r = xc.shape[0]
    return pl.pallas_call(
        _s5_out_kernel,
        grid=(S5_PAIRS,),
        in_specs=[pl.BlockSpec((r, 2 * S5_ROW), lambda p: (0, p)),
                  pl.BlockSpec((2, S5_ROW, S5_ROW), lambda p: (p, 0, 0)),
                  pl.BlockSpec((2, 2, r, 128), lambda p: (0, 0, 0, p)),
                  pl.BlockSpec((2, 1, 2, 128, 2 * S5_ROW), lambda p: (0, p, 0, 0, 0))],
        out_specs=pl.BlockSpec((r, 2 * S5_ROW), lambda p: (0, p)),
        out_shape=jax.ShapeDtypeStruct((r, S5_GROUPS * S5_ROW), F32),
        compiler_params=_cparams(("parallel",)),
        name="s5_out",
    )(xc, wt, hin, wh)


def _s5_epilogue_kernel(u_ref, y_ref, z_ref, d_ref, w_ref, o_ref):
    ya = jax.nn.gelu(d_ref[...] * u_ref[...] + y_ref[...])
    ya = ya * jax.nn.sigmoid(_mm(ya, w_ref[...]))
    o_ref[...] = (ya * _silu(z_ref[...])).astype(BF16)


def s5_epilogue(proj, y_s5, s5_d, w_glu):
    m = proj.shape[0]
    tm = 512
    return pl.pallas_call(
        _s5_epilogue_kernel,
        grid=(m // tm,),
        in_specs=[pl.BlockSpec((tm, S5_WIDTH), lambda i: (i, COL_UA // S5_WIDTH)),
                  pl.BlockSpec((tm, S5_WIDTH), lambda i: (i, 0)),
                  pl.BlockSpec((tm, S5_WIDTH), lambda i: (i, COL_ZA // S5_WIDTH)),
                  pl.BlockSpec((1, S5_WIDTH), lambda i: (0, 0)),
                  pl.BlockSpec((S5_WIDTH, S5_WIDTH), lambda i: (0, 0))],
        out_specs=pl.BlockSpec((tm, S5_WIDTH), lambda i: (i, 0)),
        out_shape=jax.ShapeDtypeStruct((m, S5_WIDTH), BF16),
        compiler_params=_cparams(("parallel",)),
        name="s5_epilogue",
    )(proj, y_s5, proj, s5_d.reshape(1, S5_WIDTH), w_glu)


def s5_branch(proj, nb, t, mats, s5_d, w_glu, h0):
    wt, ws, wh, a_l = mats
    nchunk = t // S5_CHUNK
    u = proj[:, COL_UA:COL_UA + S5_WIDTH].reshape(nb, nchunk, S5_CHUNK, S5_GROUPS, S5_GROUP)
    xc = jnp.transpose(u, (1, 0, 3, 2, 4)).reshape(nchunk * nb, S5_GROUPS * S5_ROW).astype(BF16)
    s = s5_state(xc, ws)
    hin, hfin = s5_scan(s, a_l, h0, nb, nchunk)
    y = s5_out(xc, wt, hin, wh)
    y = y.reshape(nchunk, nb, S5_GROUPS, S5_CHUNK, S5_GROUP)
    y = jnp.transpose(y, (1, 0, 3, 2, 4)).reshape(nb * t, S5_WIDTH)
    return s5_epilogue(proj, y, s5_d, w_glu), hfin


def _rope_tables(t):
    rows = t // GRID_W
    row = np.repeat(np.arange(rows), GRID_W).astype(np.float32)
    col = np.tile(np.arange(GRID_W), rows).astype(np.float32)
    nf = DA_HEAD // 4
    inv = (ROPE_BASE ** (-jnp.arange(nf, dtype=F32) / nf))

    def tab(pos):
        ang = jnp.asarray(pos)[:, None] * inv[None, :]
        c, s = jnp.cos(ang), jnp.sin(ang)
        return jnp.concatenate([c, c], -1), jnp.concatenate([-s, s], -1)

    cr, sr = tab(row)
    cc, sc = tab(col)
    cos = jnp.concatenate([cr, cc], -1)
    sin = jnp.concatenate([sr, sc], -1)
    return jnp.tile(cos, (1, 2)), jnp.tile(sin, (1, 2))


def _rope(x, cos, sin):
    lane = lax.broadcasted_iota(jnp.int32, x.shape, 1)
    swapped = jnp.where((lane % 32) < 16, pltpu.roll(x, 112, 1), pltpu.roll(x, 16, 1))
    return x * cos + swapped * sin


def _attn_kernel(*refs, lam_init, t, s_tot, tq, with_ctx):
    if with_ctx:
        (q_ref, k_ref, v_ref, z_ref, kc_ref, vc_ref, cq_ref, sq_ref, ck_ref, sk_ref,
         lam_ref, ng_ref, o_ref, kall_ref, vall_ref) = refs
    else:
        q_ref, k_ref, v_ref, z_ref, lam_ref, ng_ref, o_ref, kall_ref, vall_ref = refs

    @pl.when(pl.program_id(1) == 0)
    def _():
        if with_ctx:
            for h in range(DA_HEADS):
                hs = slice(h * 128, (h + 1) * 128)
                kall_ref[0:t, hs] = _rope(k_ref[0, :, hs], ck_ref[...], sk_ref[...]).astype(BF16)
            kall_ref[t:s_tot, :] = kc_ref[0].astype(BF16)
            vall_ref[0:t, :] = v_ref[0].astype(BF16)
            vall_ref[t:s_tot, :] = vc_ref[0].astype(BF16)
        else:
            kall_ref[...] = k_ref[0].astype(BF16)
            vall_ref[...] = v_ref[0].astype(BF16)

    lp = lam_ref[...]
    lam = (jnp.exp(jnp.sum(lp[0:1] * lp[1:2], axis=-1, keepdims=True))
           - jnp.exp(jnp.sum(lp[2:3] * lp[3:4], axis=-1, keepdims=True)) + lam_init)
    lane = lax.broadcasted_iota(jnp.int32, (tq, 128), 1)
    for h in range(DA_HEADS):
        hs = slice(h * 128, (h + 1) * 128)
        q = q_ref[0, :, hs]
        if with_ctx:
            q = _rope(q, cq_ref[...], sq_ref[...])
        q = q * (DA_HEAD ** -0.5)
        k = kall_ref[:, hs]
        probs = []
        for m in range(2):
            qm = jnp.where((lane < DA_HEAD) if m == 0 else (lane >= DA_HEAD), q, 0.0)
            sc = _mm_nt(qm, k)
            sc = sc - jnp.max(sc, axis=-1, keepdims=True)
            e = jnp.exp(sc)
            probs.append(e / jnp.sum(e, axis=-1, keepdims=True))
        a = probs[0] - lam * probs[1]
        o = _mm(a, vall_ref[:, hs])
        o = o * lax.rsqrt(jnp.mean(o * o, axis=-1, keepdims=True) + EPS) * ng_ref[...]
        o = o * (1.0 - lam_init)
        o_ref[0, :, hs] = (o * _silu(z_ref[0, :, hs])).astype(BF16)


def diff_attention(proj3, layer, lam_init, da_lam, da_norm_g, ctx_kv):
    nb, t, _ = proj3.shape
    with_ctx = ctx_kv is not None
    s_tot = t + (PAST_LEN if with_ctx else 0)
    tq = 256
    wb = DA_WIDTH
    in_specs = [pl.BlockSpec((1, tq, wb), lambda b, i: (b, i, COL_QB // wb)),
                pl.BlockSpec((1, t, wb), lambda b, i: (b, 0, COL_KB // wb)),
                pl.BlockSpec((1, t, wb), lambda b, i: (b, 0, COL_VB // wb)),
                pl.BlockSpec((1, tq, wb), lambda b, i: (b, i, COL_ZB // wb))]
    args = [proj3, proj3, proj3, proj3]
    if with_ctx:
        kc, vc = ctx_kv
        cos, sin = _rope_tables(t)
        in_specs += [pl.BlockSpec((1, PAST_LEN, wb), lambda b, i: (b, layer, 0)),
                     pl.BlockSpec((1, PAST_LEN, wb), lambda b, i: (b, layer, 0)),
                     pl.BlockSpec((tq, 128), lambda b, i: (i, 0)),
                     pl.BlockSpec((tq, 128), lambda b, i: (i, 0)),
                     pl.BlockSpec((t, 128), lambda b, i: (0, 0)),
                     pl.BlockSpec((t, 128), lambda b, i: (0, 0))]
        args += [kc, vc, cos, sin, cos, sin]
    in_specs += [pl.BlockSpec((4, DA_HEAD), lambda b, i: (0, 0)),
                 pl.BlockSpec((1, DA_VDIM), lambda b, i: (0, 0))]
    args += [da_lam, da_norm_g.reshape(1, DA_VDIM)]
    return pl.pallas_call(
        functools.partial(_attn_kernel, lam_init=lam_init, t=t, s_tot=s_tot, tq=tq, with_ctx=with_ctx),
        grid=(nb, t // tq),
        in_specs=in_specs,
        out_specs=pl.BlockSpec((1, tq, wb), lambda b, i: (b, i, 0)),
        out_shape=jax.ShapeDtypeStruct((nb, t, wb), BF16),
        scratch_shapes=[pltpu.VMEM((s_tot, wb), BF16), pltpu.VMEM((s_tot, wb), BF16)],
        compiler_params=_cparams(("parallel", "arbitrary")),
        name="diff_attention",
    )(*args)


DN_PAD = 8
DN_RT = 128
DN_GROUP = 128


def _dn_kernel(*refs, t, ngroup, with_s0):
    if with_s0:
        (qkv_ref, z_ref, ba_ref, cw_ref, alog_ref, dtb_ref, ng_ref, s0_ref,
         out_ref, sfin_ref, xp_ref, qkvn_ref, oacc_ref, st_ref) = refs
    else:
        (qkv_ref, z_ref, ba_ref, cw_ref, alog_ref, dtb_ref, ng_ref,
         out_ref, sfin_ref, xp_ref, qkvn_ref, oacc_ref, st_ref) = refs
    n = pl.program_id(1)
    cd = DN_CHUNK
    w3 = 3 * DN_WIDTH

    @pl.when(n == 0)
    def _init():
        xp_ref[0:DN_PAD, :] = jnp.zeros((DN_PAD, w3), F32)
        xp_ref[DN_PAD + t:2 * DN_PAD + t, :] = jnp.zeros((DN_PAD, w3), F32)
        xp_ref[DN_PAD:DN_PAD + t, :] = qkv_ref[0]
        half = DN_CONV // 2
        for r in range(t // DN_RT):
            for sec in range(3):
                for h in range(DN_HEADS):
                    cs = slice(sec * DN_WIDTH + h * DN_HEAD, sec * DN_WIDTH + (h + 1) * DN_HEAD)
                    acc = jnp.zeros((DN_RT, DN_HEAD), F32)
                    for j in range(DN_CONV):
                        r0 = DN_PAD + r * DN_RT + j - half
                        acc = acc + xp_ref[r0:r0 + DN_RT, cs] * cw_ref[j:j + 1, cs]
                    y = _silu(acc)
                    if sec < 2:
                        y = y * lax.rsqrt(jnp.sum(y * y, axis=-1, keepdims=True) + EPS)
                    if sec == 0:
                        y = y * (DN_HEAD ** -0.5)
                    qkvn_ref[r * DN_RT:(r + 1) * DN_RT, cs] = y
        oacc_ref[...] = jnp.zeros_like(oacc_ref)
        if with_s0:
            st_ref[...] = s0_ref[0, 0]
        else:
            st_ref[...] = jnp.zeros_like(st_ref)

    gb = DN_GROUP
    nsub = gb // cd
    ri = lax.broadcasted_iota(jnp.int32, (gb, gb), 0)
    ci = lax.broadcasted_iota(jnp.int32, (gb, gb), 1)
    same = (ri // cd) == (ci // cd)
    samef = same.astype(F32)
    eye = (ri == ci).astype(F32)

    chains = []
    for d in range(2):
        grp = n if d == 0 else ngroup - 1 - n
        r0 = pl.multiple_of(grp * gb, gb)
        incl = same & ((ci <= ri) if d == 0 else (ci >= ri))
        strict = same & ((ci < ri) if d == 0 else (ci > ri))
        inclf = incl.astype(F32)
        ba = ba_ref[0, pl.ds(r0, gb), :]
        beta_all = jax.nn.sigmoid(ba)
        g_all = -jnp.exp(alog_ref[...]) * jax.nn.softplus(ba + dtb_ref[...])
        gc = jnp.dot(inclf, g_all, precision=HI, preferred_element_type=F32)
        gct = lax.dot_general(g_all.T, inclf, (((1,), (1,)), ((), ())), precision=HI,
                              preferred_element_type=F32)
        gtot = jnp.dot(samef, g_all, precision=HI, preferred_element_type=F32)
        for h in range(DN_HEADS):
            hs = slice(h * DN_HEAD, (h + 1) * DN_HEAD)
            col = 2 * DN_HEADS + d * DN_HEADS + h
            gcol = gc[:, col:col + 1]
            grow = gct[col:col + 1, :]
            gt = gtot[:, col:col + 1]
            beta = beta_all[:, d * DN_HEADS + h:d * DN_HEADS + h + 1]
            q = qkvn_ref[pl.ds(r0, gb), hs]
            k = qkvn_ref[pl.ds(r0, gb), DN_WIDTH + h * DN_HEAD:DN_WIDTH + (h + 1) * DN_HEAD]
            v = qkvn_ref[pl.ds(r0, gb), 2 * DN_WIDTH + h * DN_HEAD:2 * DN_WIDTH + (h + 1) * DN_HEAD]
            eg = jnp.exp(gcol)
            chains.append(dict(
                d=d, h=h, r0=r0, hs=hs, strict=strict, q=q, k=k, kb=k * beta,
                decay=jnp.where(incl, jnp.exp(jnp.where(incl, gcol - grow, 0.0)), 0.0),
                rhs=jnp.concatenate([v * beta, k * beta * eg], axis=-1),
                qe=q * eg, kdec=k * jnp.exp(gt - gcol), egt=jnp.exp(gt), st=st_ref[d, h]))

    for c in chains:
        c["mm"] = jnp.where(c["strict"], _mm_nt(c["kb"], c["k"]) * c["decay"], 0.0)
    for c in chains:
        c["qk"] = _mm_nt(c["q"], c["k"]) * c["decay"]
    for c in chains:
        c["tinv"] = eye - jnp.where((ri // 2) == (ci // 2), c["mm"], 0.0)
    s = 2
    while s < cd:
        offmask = ((ri // (2 * s)) == (ci // (2 * s))) & ((ri // s) != (ci // s))
        xs = [_mm(jnp.where(offmask, c["mm"], 0.0), c["tinv"]) for c in chains]
        ys = [_mm(c["tinv"], x) for c, x in zip(chains, xs)]
        for c, y in zip(chains, ys):
            c["tinv"] = c["tinv"] - y
        s *= 2
    for c in chains:
        c["uw"] = _mm(c["tinv"], c["rhs"])

    outs = []
    for step in range(nsub):
        rs = []
        for c in chains:
            sub = step if c["d"] == 0 else nsub - 1 - step
            c["rows"] = slice(sub * cd, (sub + 1) * cd)
            rs.append(_mm(jnp.concatenate([c["uw"][c["rows"], DN_HEAD:], c["qe"][c["rows"]]], axis=0), c["st"]))
        for c, r in zip(chains, rs):
            rows = c["rows"]
            v_new = c["uw"][rows, :DN_HEAD] - r[:cd]
            o = r[cd:] + _mm(c["qk"][rows, rows], v_new)
            c["st"] = c["st"] * c["egt"][rows.start:rows.start + 1] + _mm(c["kdec"][rows].T, v_new)
            outs.append((c, rows.start, o))
    for c in chains:
        st_ref[c["d"], c["h"]] = c["st"]
    for c, off, o in outs:
        oacc_ref[pl.ds(pl.multiple_of(c["r0"] + off, cd), cd), c["hs"]] += o

    @pl.when(n == ngroup - 1)
    def _fin():
        for h in range(DN_HEADS):
            hs = slice(h * DN_HEAD, (h + 1) * DN_HEAD)
            o = oacc_ref[:, hs]
            o = o * lax.rsqrt(jnp.mean(o * o, axis=-1, keepdims=True) + EPS) * ng_ref[...]
            out_ref[0, :, hs] = (o * _silu(z_ref[0, :, hs])).astype(BF16)
        sfin_ref[0] = st_ref[...]


def deltanet(proj3, ba3, layer, conv_w, a_log, dt_bias, norm_g, s0):
    nb, t, _ = proj3.shape
    ngroup = t // DN_GROUP
    with_s0 = s0 is not None
    w3 = 3 * DN_WIDTH
    pad = jnp.zeros((2 * DN_HEADS,), F32)
    alog_row = jnp.concatenate([pad, a_log.reshape(-1), jnp.zeros((128 - 4 * DN_HEADS,), F32)]).reshape(1, 128)
    dtb_row = jnp.concatenate([pad, dt_bias.reshape(-1), jnp.zeros((128 - 4 * DN_HEADS,), F32)]).reshape(1, 128)
    in_specs = [pl.BlockSpec((1, t, w3), lambda b, n: (b, 0, COL_QC // w3)),
                pl.BlockSpec((1, t, DN_WIDTH), lambda b, n: (b, 0, COL_ZC // DN_WIDTH)),
                pl.BlockSpec((1, t, 128), lambda b, n: (b, 0, 0)),
                pl.BlockSpec((8, w3), lambda b, n: (0, 0)),
                pl.BlockSpec((1, 128), lambda b, n: (0, 0)),
                pl.BlockSpec((1, 128), lambda b, n: (0, 0)),
                pl.BlockSpec((1, DN_HEAD), lambda b, n: (0, 0))]
    args = [proj3, proj3, ba3, jnp.pad(conv_w, ((0, 8 - DN_CONV), (0, 0))), alog_row, dtb_row,
            norm_g.reshape(1, DN_HEAD)]
    if with_s0:
        in_specs.append(pl.BlockSpec((1, 1, 2, DN_HEADS, DN_HEAD, DN_HEAD), lambda b, n: (b, layer, 0, 0, 0, 0)))
        args.append(s0)
    return pl.pallas_call(
        functools.partial(_dn_kernel, t=t, ngroup=ngroup, with_s0=with_s0),
        grid=(nb, ngroup),
        in_specs=in_specs,
        out_specs=[pl.BlockSpec((1, t, DN_WIDTH), lambda b, n: (b, 0, 0)),
                   pl.BlockSpec((1, 2, DN_HEADS, DN_HEAD, DN_HEAD), lambda b, n: (b, 0, 0, 0, 0))],
        out_shape=[jax.ShapeDtypeStruct((nb, t, DN_WIDTH), BF16),
                   jax.ShapeDtypeStruct((nb, 2, DN_HEADS, DN_HEAD, DN_HEAD), F32)],
        scratch_shapes=[pltpu.VMEM((t + 2 * DN_PAD, w3), F32),
                        pltpu.VMEM((t, w3), F32),
                        pltpu.VMEM((t, DN_WIDTH), F32),
                        pltpu.VMEM((2, DN_HEADS, DN_HEAD, DN_HEAD), F32)],
        compiler_params=_cparams(("parallel", "arbitrary")),
        name="deltanet",
    )(*args)


def _merge_kernel(oa_ref, ob_ref, oc_ref, ga_ref, gb_ref, gc_ref, wb_ref, wo_ref, x_ref, gate_ref, fg_ref,
                  *outs, final):
    acc = None
    for i, (o_ref, g_ref) in enumerate(((oa_ref, ga_ref), (ob_ref, gb_ref), (oc_ref, gc_ref))):
        pr = jnp.dot(o_ref[...], wb_ref[i], preferred_element_type=F32)
        term = jax.nn.sigmoid(g_ref[...]) * pr
        acc = term if acc is None else acc + term
    y = jnp.dot(acc.astype(BF16), wo_ref[...], preferred_element_type=F32)
    xn = x_ref[...] + gate_ref[0] * y
    outs[0][...] = xn
    if final:
        yn = xn * lax.rsqrt(jnp.mean(xn * xn, axis=-1, keepdims=True) + EPS) * fg_ref[...]
        outs[1][...] = yn


def merge(out_a, out_b, out_c, proj, w_branch, w_out, x2, gate, final_g, rows_per_mod, final):
    m = x2.shape[0]
    tm = 512
    nmod = gate.shape[0]
    row = lambda i: (i, 0)
    out_specs = [pl.BlockSpec((tm, D_MODEL), row)]
    out_shape = [jax.ShapeDtypeStruct((m, D_MODEL), F32)]
    if final:
        out_specs.append(pl.BlockSpec((tm, D_MODEL), row))
        out_shape.append(jax.ShapeDtypeStruct((m, D_MODEL), F32))
    return pl.pallas_call(
        functools.partial(_merge_kernel, final=final),
        grid=(m // tm,),
        in_specs=[pl.BlockSpec((tm, BRANCH_WIDTH), row),
                  pl.BlockSpec((tm, BRANCH_WIDTH), row),
                  pl.BlockSpec((tm, BRANCH_WIDTH), row),
                  pl.BlockSpec((tm, D_MODEL), lambda i: (i, COL_GATES // D_MODEL)),
                  pl.BlockSpec((tm, D_MODEL), lambda i: (i, COL_GATES // D_MODEL + 1)),
                  pl.BlockSpec((tm, D_MODEL), lambda i: (i, COL_GATES // D_MODEL + 2)),
                  pl.BlockSpec((N_BRANCH, BRANCH_WIDTH, D_MODEL), lambda i: (0, 0, 0)),
                  pl.BlockSpec((D_MODEL, D_MODEL), lambda i: (0, 0)),
                  pl.BlockSpec((tm, D_MODEL), row),
                  pl.BlockSpec((1, 1, D_MODEL), lambda i: ((i * tm) // rows_per_mod, 0, 0)),
                  pl.BlockSpec((1, D_MODEL), lambda i: (0, 0))],
        out_specs=out_specs,
        out_shape=out_shape,
        compiler_params=_cparams(("parallel",)),
        name="merge",
    )(out_a, out_b, out_c, proj, proj, proj, w_branch, w_out, x2, gate.reshape(nmod, 1, D_MODEL),
      final_g.reshape(1, D_MODEL))


def _run_pass(x, mod, wts, lam_inits, final_g, ctx):
    nb, t, _ = x.shape
    m = nb * t
    nmod = mod.shape[1]
    rows_per_mod = m if nmod == 1 else t
    x2 = x.reshape(m, D_MODEL)
    states = []
    y = None
    for l in range(DEPTH):
        w = wts[l]
        shift, scale, gate = jnp.split(mod[l], 3, axis=-1)
        proj, ba = inproj(x2, w["norm_g"], scale, shift, w["w1"], w["w2"], rows_per_mod)
        proj3 = proj.reshape(nb, t, N_MAIN)
        if ctx is None:
            h0 = jnp.zeros((2, 2, nb, S5_GROUPS * S5_STATE), F32)
            ctx_kv, s0 = None, None
        else:
            cache_k, cache_v, st_re, st_im, st_dn = ctx
            h0 = jnp.stack([st_re[:, l], st_im[:, l]], 0)
            h0 = jnp.transpose(h0, (2, 0, 1, 3, 4)).reshape(2, 2, nb, S5_GROUPS * S5_STATE)
            ctx_kv, s0 = (cache_k, cache_v), st_dn
        out_a, hfin = s5_branch(proj, nb, t, w["s5_mats"], w["s5_d"], w["w_glu"], h0)
        out_b = diff_attention(proj3, l, lam_inits[l], w["da_lam"], w["da_norm_g"], ctx_kv)
        out_c, sfin = deltanet(proj3, ba.reshape(nb, t, 128), l, w["dn_conv"], w["dn_a_log"],
                               w["dn_dt_bias"], w["dn_norm_g"], s0)
        final = l == DEPTH - 1
        res = merge(out_a, out_b.reshape(m, DA_WIDTH), out_c.reshape(m, DN_WIDTH), proj,
                    w["w_branch"], w["w_out"], x2, gate, final_g, rows_per_mod, final)
        x2 = res[0]
        if final:
            y = res[1]
        if ctx is None:
            k_new = proj3[:, :, COL_KB:COL_KB + 512].reshape(nb, t, DA_HEADS, 2, DA_HEAD)
            v_new = proj3[:, :, COL_VB:COL_VB + 512].reshape(nb, t, DA_HEADS, DA_VDIM)
            hf = hfin.reshape(2, 2, nb, S5_GROUPS, S5_STATE)
            states.append((k_new, v_new, jnp.transpose(hf[:, 0], (1, 0, 2, 3)),
                           jnp.transpose(hf[:, 1], (1, 0, 2, 3)), sfin))
    return y.reshape(nb, t, D_MODEL), states


def kernel(x_prompt, x_sample, cache_k, cache_v, state_s5_re, state_s5_im, state_dn, c, c_ctx,
           norm_g, w_ada, b_ada, w_in, s5_lam_re, s5_lam_im, s5_log_step, s5_b_re, s5_b_im,
           s5_c_re, s5_c_im, s5_d, s5_w_glu, da_lam, da_norm_g, dn_conv, dn_a_log, dn_dt_bias,
           dn_norm_g, w_branch, w_out, final_norm_g):
    nb_dec = x_sample.shape[0]
    cond8 = jnp.concatenate([c_ctx[None, :], c, jnp.zeros((8 - 1 - nb_dec, D_MODEL), F32)], 0)
    mod = ada_mod(cond8, w_ada, b_ada)
    w1 = jnp.concatenate([w_in[:, :, :BA_OFF], w_in[:, :, GATES_OFF:]], -1).astype(BF16)
    w2 = jnp.pad(w_in[:, :, BA_OFF:GATES_OFF], ((0, 0), (0, 0), (0, 128 - (GATES_OFF - BA_OFF)))).astype(BF16)
    wts = []
    for l in range(DEPTH):
        wts.append(dict(
            norm_g=norm_g[l], w1=w1[l], w2=w2[l],
            s5_mats=s5_matrices(s5_lam_re[l], s5_lam_im[l], s5_log_step[l], s5_b_re[l], s5_b_im[l],
                                s5_c_re[l], s5_c_im[l]),
            s5_d=s5_d[l], w_glu=s5_w_glu[l].astype(BF16), da_lam=da_lam[l], da_norm_g=da_norm_g[l],
            dn_conv=dn_conv[l], dn_a_log=dn_a_log[l], dn_dt_bias=dn_dt_bias[l], dn_norm_g=dn_norm_g[l],
            w_branch=w_branch[l].astype(BF16), w_out=w_out[l].astype(BF16)))
    lam_inits = [0.8 - 0.6 * math.exp(-0.3 * l) for l in range(DEPTH)]

    y_prompt, states = _run_pass(x_prompt, mod[:, 0:1], wts, lam_inits, final_norm_g, None)
    ctx = (cache_k.reshape(nb_dec, DEPTH * PAST_LEN, DA_WIDTH),
           cache_v.reshape(nb_dec, DEPTH * PAST_LEN, DA_WIDTH), state_s5_re, state_s5_im, state_dn)
    y_sample, _ = _run_pass(x_sample, mod[:, 1:1 + nb_dec], wts, lam_inits, final_norm_g, ctx)

    new_cache_k = jnp.stack([s[0] for s in states], axis=1)
    new_cache_v = jnp.stack([s[1] for s in states], axis=1)
    new_s5_re = jnp.stack([s[2] for s in states], axis=1)
    new_s5_im = jnp.stack([s[3] for s in states], axis=1)
    new_dn = jnp.stack([s[4] for s in states], axis=1)
    return (y_prompt, y_sample, new_cache_k, new_cache_v, new_s5_re, new_s5_im, new_dn)
```

```python
import functools
import math

import numpy as np
import jax
import jax.numpy as jnp
from jax import lax
from jax.experimental import pallas as pl
from jax.experimental.pallas import tpu as pltpu

F32 = jnp.float32
BF16 = jnp.bfloat16

D_MODEL = 1024
DEPTH = 2
GRID_W = 64
EPS = 1e-6
S5_WIDTH = 512
S5_GROUP = 16
S5_GROUPS = 32
S5_STATE = 64
S5_CHUNK = 16
S5_PAIRS = S5_GROUPS // 2
S5_ROW = S5_CHUNK * S5_GROUP
DA_HEADS = 4
DA_HEAD = 64
DA_VDIM = 128
DA_WIDTH = 512
ROPE_BASE = 10000.0
DN_HEADS = 4
DN_HEAD = 128
DN_WIDTH = 512
DN_CONV = 5
DN_CHUNK = 64
N_BRANCH = 3
BRANCH_WIDTH = 512
PAST_LEN = 512

COL_UA, COL_ZA, COL_QB, COL_KB, COL_VB, COL_ZB = 0, 512, 1024, 1536, 2048, 2560
COL_QC, COL_ZC, COL_GATES = 3072, 4608, 5120
N_MAIN = 8192
BA_OFF = 5120
GATES_OFF = 5136

VMEM_LIMIT = 56 * 1024 * 1024
HI = lax.Precision.HIGHEST


def _cparams(sem):
    return pltpu.CompilerParams(dimension_semantics=sem, vmem_limit_bytes=VMEM_LIMIT)


def _mm(a, b):
    return jnp.dot(a.astype(BF16), b.astype(BF16), preferred_element_type=F32)


def _mm_nt(a, b):
    return lax.dot_general(a.astype(BF16), b.astype(BF16), (((1,), (1,)), ((), ())),
                           preferred_element_type=F32)


def _silu(x):
    return x * jax.nn.sigmoid(x)


def _ada_kernel(c_ref, w_ref, b_ref, o_ref):
    o_ref[0] = _mm(_silu(c_ref[...]), w_ref[0]) + b_ref[0]


def ada_mod(cond8, w_ada, b_ada):
    tn = 1024
    return pl.pallas_call(
        _ada_kernel,
        grid=(DEPTH, 3 * D_MODEL // tn),
        in_specs=[pl.BlockSpec((8, D_MODEL), lambda l, j: (0, 0)),
                  pl.BlockSpec((1, D_MODEL, tn), lambda l, j: (l, 0, j)),
                  pl.BlockSpec((1, 1, tn), lambda l, j: (l, 0, j))],
        out_specs=pl.BlockSpec((1, 8, tn), lambda l, j: (l, 0, j)),
        out_shape=jax.ShapeDtypeStruct((DEPTH, 8, 3 * D_MODEL), F32),
        compiler_params=_cparams(("parallel", "parallel")),
        name="ada_mod",
    )(cond8, w_ada, b_ada.reshape(DEPTH, 1, 3 * D_MODEL))


def _inproj_kernel(x_ref, g_ref, sc_ref, sh_ref, w1_ref, w2_ref, proj_ref, ba_ref, hn_ref):
    @pl.when(pl.program_id(1) == 0)
    def _():
        x = x_ref[...]
        y = x * lax.rsqrt(jnp.mean(x * x, axis=-1, keepdims=True) + EPS) * g_ref[...]
        hn = (y * (1.0 + sc_ref[0]) + sh_ref[0]).astype(BF16)
        hn_ref[...] = hn
        ba_ref[...] = jnp.dot(hn, w2_ref[...], preferred_element_type=F32)

    proj_ref[...] = jnp.dot(hn_ref[...], w1_ref[...], preferred_element_type=F32)


def inproj(x2, norm_g, scale, shift, w1, w2, rows_per_mod):
    m = x2.shape[0]
    tm, tn = 1024, 1024
    nmod = scale.shape[0]
    mod_idx = lambda i, j: ((i * tm) // rows_per_mod, 0, 0)
    return pl.pallas_call(
        _inproj_kernel,
        grid=(m // tm, N_MAIN // tn),
        in_specs=[pl.BlockSpec((tm, D_MODEL), lambda i, j: (i, 0)),
                  pl.BlockSpec((1, D_MODEL), lambda i, j: (0, 0)),
                  pl.BlockSpec((1, 1, D_MODEL), mod_idx),
                  pl.BlockSpec((1, 1, D_MODEL), mod_idx),
                  pl.BlockSpec((D_MODEL, tn), lambda i, j: (0, j)),
                  pl.BlockSpec((D_MODEL, 128), lambda i, j: (0, 0))],
        out_specs=[pl.BlockSpec((tm, tn), lambda i, j: (i, j)),
                   pl.BlockSpec((tm, 128), lambda i, j: (i, 0))],
        out_shape=[jax.ShapeDtypeStruct((m, N_MAIN), F32),
                   jax.ShapeDtypeStruct((m, 128), F32)],
        scratch_shapes=[pltpu.VMEM((tm, D_MODEL), BF16)],
        compiler_params=_cparams(("parallel", "arbitrary")),
        name="inproj",
    )(x2, norm_g.reshape(1, D_MODEL), scale.reshape(nmod, 1, D_MODEL),
      shift.reshape(nmod, 1, D_MODEL), w1, w2)


def s5_matrices(lam_re, lam_im, log_step, b_re, b_im, c_re, c_im):
    L = S5_CHUNK
    G, P, C = S5_GROUPS, S5_STATE, S5_GROUP
    toep = jnp.zeros((G, L, C, L, C), F32)
    ws_all, wh_all, a_l = [], [], []
    ks = jnp.arange(L + 1, dtype=F32)[:, None, None]
    jj, ii = np.meshgrid(np.arange(L), np.arange(L), indexing="xy")
    lag = np.clip(jj - ii, 0, L - 1)
    causal = jnp.asarray((jj >= ii)[:, :, None, None, None], F32)
    for d in range(2):
        step = jnp.exp(log_step[d])[:, None]
        lr, li = lam_re[d], lam_im[d]
        mag = jnp.exp(lr * step)
        ar, ai = mag * jnp.cos(li * step), mag * jnp.sin(li * step)
        den = lr * lr + li * li
        fr = ((ar - 1.0) * lr + ai * li) / den
        fi = (ai * lr - (ar - 1.0) * li) / den
        bbr = fr[..., None] * b_re[d] - fi[..., None] * b_im[d]
        bbi = fr[..., None] * b_im[d] + fi[..., None] * b_re[d]
        pmag = jnp.exp(ks * (lr * step)[None])
        pr, pi = pmag * jnp.cos(ks * (li * step)[None]), pmag * jnp.sin(ks * (li * step)[None])
        cr, ci = c_re[d], c_im[d]
        ckr = cr[None] * pr[:, :, None, :] - ci[None] * pi[:, :, None, :]
        cki = cr[None] * pi[:, :, None, :] + ci[None] * pr[:, :, None, :]
        kern = (jnp.einsum("kgop,gpi->kgoi", ckr[:L], bbr, precision=HI)
                - jnp.einsum("kgop,gpi->kgoi", cki[:L], bbi, precision=HI))
        t = kern[lag] * causal
        t = jnp.transpose(t, (2, 0, 4, 1, 3))
        e = pr[L - 1 - np.arange(L)], pi[L - 1 - np.arange(L)]
        wsr = e[0][..., None] * bbr[None] - e[1][..., None] * bbi[None]
        wsi = e[0][..., None] * bbi[None] + e[1][..., None] * bbr[None]
        wsr = jnp.transpose(wsr, (1, 0, 3, 2))
        wsi = jnp.transpose(wsi, (1, 0, 3, 2))
        whr = jnp.transpose(ckr[1:], (1, 3, 0, 2))
        whi = -jnp.transpose(cki[1:], (1, 3, 0, 2))
        if d == 1:
            t = t[:, ::-1, :, ::-1, :]
            wsr, wsi = wsr[:, ::-1], wsi[:, ::-1]
            whr, whi = whr[:, :, ::-1], whi[:, :, ::-1]
        toep = toep + t
        ws = jnp.stack([wsr, wsi], 0).reshape(2, S5_PAIRS, 2, S5_ROW, P)
        z = jnp.zeros_like(ws[:, :, 0])
        ws_pair = jnp.concatenate([jnp.concatenate([ws[:, :, 0], z], -1),
                                   jnp.concatenate([z, ws[:, :, 1]], -1)], -2)
        wh = jnp.stack([whr, whi], 0).reshape(2, S5_PAIRS, 2, P, S5_ROW)
        zh = jnp.zeros_like(wh[:, :, 0])
        wh_pair = jnp.concatenate([jnp.concatenate([wh[:, :, 0], zh], -1),
                                   jnp.concatenate([zh, wh[:, :, 1]], -1)], -2)
        ws_all.append(jnp.transpose(ws_pair, (1, 0, 2, 3)))
        wh_all.append(jnp.transpose(wh_pair, (1, 0, 2, 3)))
        a_l.append(jnp.stack([pr[L].reshape(1, G * P), pi[L].reshape(1, G * P)], 0))
    return (toep.reshape(G, S5_ROW, S5_ROW).astype(BF16),
            jnp.stack(ws_all, 0).astype(BF16),
            jnp.stack(wh_all, 0).astype(BF16),
            jnp.stack(a_l, 0))


S5_GB = 8


def _s5_state_kernel(u_ref, ws_ref, x_ref, s_ref, *, r):
    us = [u_ref[pl.ds(i, r, stride=S5_CHUNK), :] for i in range(S5_CHUNK)]
    for g in range(S5_GB):
        xg = jnp.concatenate([u[:, S5_GROUP * g:S5_GROUP * (g + 1)] for u in us], axis=-1)
        x_ref[:, g * S5_ROW:(g + 1) * S5_ROW] = xg.astype(BF16)
    for p in range(S5_GB // 2):
        x = x_ref[:, 2 * p * S5_ROW:2 * (p + 1) * S5_ROW]
        for d in range(2):
            for comp in range(2):
                s_ref[d, comp, :, p * 128:(p + 1) * 128] = jnp.dot(x, ws_ref[d, p, comp],
                                                                   preferred_element_type=F32)


def s5_state(proj, ws, r):
    gw = S5_GB * S5_ROW
    return pl.pallas_call(
        functools.partial(_s5_state_kernel, r=r),
        grid=(S5_GROUPS // S5_GB,),
        in_specs=[pl.BlockSpec((r * S5_CHUNK, 128), lambda j: (0, COL_UA // 128 + j)),
                  pl.BlockSpec((2, S5_GB // 2, 2, 2 * S5_ROW, 128), lambda j: (0, j, 0, 0, 0))],
        out_specs=[pl.BlockSpec((r, gw), lambda j: (0, j)),
                   pl.BlockSpec((2, 2, r, S5_GB * S5_STATE), lambda j: (0, 0, 0, j))],
        out_shape=[jax.ShapeDtypeStruct((r, S5_GROUPS * S5_ROW), BF16),
                   jax.ShapeDtypeStruct((2, 2, r, S5_GROUPS * S5_STATE), F32)],
        compiler_params=_cparams(("parallel",)),
        name="s5_state",
    )(proj, ws)


def _s5_scan_kernel(s_ref, a_ref, h0_ref, hin_ref, hfin_ref, *, nb, nchunk):
    ar, ai = a_ref[0, 0], a_ref[0, 1]

    def run(order):
        hr, hi = h0_ref[0, 0], h0_ref[0, 1]
        for c in order:
            rows = pl.ds(c, nb, stride=nchunk)
            hin_ref[0, 0, rows, :] = hr
            hin_ref[0, 1, rows, :] = hi
            sr, si = s_ref[0, 0, rows, :], s_ref[0, 1, rows, :]
            hr, hi = ar * hr - ai * hi + sr, ar * hi + ai * hr + si
        hfin_ref[0, 0] = hr
        hfin_ref[0, 1] = hi

    d = pl.program_id(0)
    pl.when(d == 0)(lambda: run(range(nchunk)))
    pl.when(d == 1)(lambda: run(range(nchunk - 1, -1, -1)))


def s5_scan(s, a_l, h0, nb, nchunk):
    r = s.shape[2]
    w = S5_GROUPS * S5_STATE
    tc = 128
    return pl.pallas_call(
        functools.partial(_s5_scan_kernel, nb=nb, nchunk=nchunk),
        grid=(2, w // tc),
        in_specs=[pl.BlockSpec((1, 2, r, tc), lambda d, j: (d, 0, 0, j)),
                  pl.BlockSpec((1, 2, 1, tc), lambda d, j: (d, 0, 0, j)),
                  pl.BlockSpec((1, 2, nb, tc), lambda d, j: (d, 0, 0, j))],
        out_specs=[pl.BlockSpec((1, 2, r, tc), lambda d, j: (d, 0, 0, j)),
                   pl.BlockSpec((1, 2, nb, tc), lambda d, j: (d, 0, 0, j))],
        out_shape=[jax.ShapeDtypeStruct((2, 2, r, w), F32),
                   jax.ShapeDtypeStruct((2, 2, nb, w), F32)],
        compiler_params=_cparams(("parallel", "parallel")),
        name="s5_scan",
    )(s, a_l, h0)


def _s5_out_kernel(x_ref, wt_ref, hin_ref, wh_ref, y_ref, *, r):
    ys = []
    for p in range(S5_GB // 2):
        yp = jnp.concatenate(
            [jnp.dot(x_ref[:, (2 * p + k) * S5_ROW:(2 * p + k + 1) * S5_ROW], wt_ref[2 * p + k],
                     preferred_element_type=F32) for k in range(2)], axis=-1)
        for d in range(2):
            for comp in range(2):
                yp = yp + jnp.dot(hin_ref[d, comp, :, p * 128:(p + 1) * 128].astype(BF16), wh_ref[d, p, comp],
                                  preferred_element_type=F32)
        ys += [yp[:, :S5_ROW], yp[:, S5_ROW:]]
    for j in range(S5_CHUNK):
        y_ref[pl.ds(j, r, stride=S5_CHUNK), :] = jnp.concatenate(
            [y[:, S5_GROUP * j:S5_GROUP * (j + 1)] for y in ys], axis=-1)


def s5_out(xc, wt, hin, wh):
    r = xc.shape[0]
    gw = S5_GB * S5_ROW
    return pl.pallas_call(
        functools.partial(_s5_out_kernel, r=r),
        grid=(S5_GROUPS // S5_GB,),
        in_specs=[pl.BlockSpec((r, gw), lambda j: (0, j)),
                  pl.BlockSpec((S5_GB, S5_ROW, S5_ROW), lambda j: (j, 0, 0)),
                  pl.BlockSpec((2, 2, r, S5_GB * S5_STATE), lambda j: (0, 0, 0, j)),
                  pl.BlockSpec((2, S5_GB // 2, 2, 128, 2 * S5_ROW), lambda j: (0, j, 0, 0, 0))],
        out_specs=pl.BlockSpec((r * S5_CHUNK, 128), lambda j: (0, j)),
        out_shape=jax.ShapeDtypeStruct((r * S5_CHUNK, S5_WIDTH), F32),
        compiler_params=_cparams(("parallel",)),
        name="s5_out",
    )(xc, wt, hin, wh)


def _s5_epilogue_kernel(u_ref, y_ref, z_ref, d_ref, w_ref, o_ref):
    ya = jax.nn.gelu(d_ref[...] * u_ref[...] + y_ref[...])
    ya = ya * jax.nn.sigmoid(_mm(ya, w_ref[...]))
    o_ref[...] = (ya * _silu(z_ref[...])).astype(BF16)


def s5_epilogue(proj, y_s5, s5_d, w_glu):
    m = proj.shape[0]
    tm = 512
    return pl.pallas_call(
        _s5_epilogue_kernel,
        grid=(m // tm,),
        in_specs=[pl.BlockSpec((tm, S5_WIDTH), lambda i: (i, COL_UA // S5_WIDTH)),
                  pl.BlockSpec((tm, S5_WIDTH), lambda i: (i, 0)),
                  pl.BlockSpec((tm, S5_WIDTH), lambda i: (i, COL_ZA // S5_WIDTH)),
                  pl.BlockSpec((1, S5_WIDTH), lambda i: (0, 0)),
                  pl.BlockSpec((S5_WIDTH, S5_WIDTH), lambda i: (0, 0))],
        out_specs=pl.BlockSpec((tm, S5_WIDTH), lambda i: (i, 0)),
        out_shape=jax.ShapeDtypeStruct((m, S5_WIDTH), BF16),
        compiler_params=_cparams(("parallel",)),
        name="s5_epilogue",
    )(proj, y_s5, proj, s5_d.reshape(1, S5_WIDTH), w_glu)


def s5_branch(proj, nb, t, mats, s5_d, w_glu, h0):
    wt, ws, wh, a_l = mats
    nchunk = t // S5_CHUNK
    xc, s = s5_state(proj, ws, nb * nchunk)
    hin, hfin = s5_scan(s, a_l, h0, nb, nchunk)
    y = s5_out(xc, wt, hin, wh)
    return s5_epilogue(proj, y, s5_d, w_glu), hfin


def _rope_tables(t):
    rows = t // GRID_W
    row = np.repeat(np.arange(rows), GRID_W).astype(np.float32)
    col = np.tile(np.arange(GRID_W), rows).astype(np.float32)
    nf = DA_HEAD // 4
    inv = (ROPE_BASE ** (-jnp.arange(nf, dtype=F32) / nf))

    def tab(pos):
        ang = jnp.asarray(pos)[:, None] * inv[None, :]
        c, s = jnp.cos(ang), jnp.sin(ang)
        return jnp.concatenate([c, c], -1), jnp.concatenate([-s, s], -1)

    cr, sr = tab(row)
    cc, sc = tab(col)
    cos = jnp.concatenate([cr, cc], -1)
    sin = jnp.concatenate([sr, sc], -1)
    return jnp.tile(cos, (1, 2)), jnp.tile(sin, (1, 2))


def _rope(x, cos, sin):
    lane = lax.broadcasted_iota(jnp.int32, x.shape, 1)
    swapped = jnp.where((lane % 32) < 16, pltpu.roll(x, 112, 1), pltpu.roll(x, 16, 1))
    return x * cos + swapped * sin


def _attn_kernel(*refs, lam_init, t, s_tot, tq, with_ctx):
    if with_ctx:
        (q_ref, k_ref, v_ref, z_ref, kc_ref, vc_ref, cq_ref, sq_ref, ck_ref, sk_ref,
         lam_ref, ng_ref, o_ref, kall_ref, vall_ref) = refs
    else:
        q_ref, k_ref, v_ref, z_ref, lam_ref, ng_ref, o_ref, kall_ref, vall_ref = refs

    @pl.when(pl.program_id(1) == 0)
    def _():
        if with_ctx:
            for h in range(DA_HEADS):
                hs = slice(h * 128, (h + 1) * 128)
                kall_ref[0:t, hs] = _rope(k_ref[0, :, hs], ck_ref[...], sk_ref[...]).astype(BF16)
            kall_ref[t:s_tot, :] = kc_ref[0].astype(BF16)
            vall_ref[0:t, :] = v_ref[0].astype(BF16)
            vall_ref[t:s_tot, :] = vc_ref[0].astype(BF16)
        else:
            kall_ref[...] = k_ref[0].astype(BF16)
            vall_ref[...] = v_ref[0].astype(BF16)

    lp = lam_ref[...]
    lam = (jnp.exp(jnp.sum(lp[0:1] * lp[1:2], axis=-1, keepdims=True))
           - jnp.exp(jnp.sum(lp[2:3] * lp[3:4], axis=-1, keepdims=True)) + lam_init)
    lane = lax.broadcasted_iota(jnp.int32, (tq, 128), 1)
    for h in range(DA_HEADS):
        hs = slice(h * 128, (h + 1) * 128)
        q = q_ref[0, :, hs]
        if with_ctx:
            q = _rope(q, cq_ref[...], sq_ref[...])
        q = q * (DA_HEAD ** -0.5)
        k = kall_ref[:, hs]
        probs = []
        for m in range(2):
            qm = jnp.where((lane < DA_HEAD) if m == 0 else (lane >= DA_HEAD), q, 0.0)
            sc = _mm_nt(qm, k)
            sc = sc - jnp.max(sc, axis=-1, keepdims=True)
            e = jnp.exp(sc)
            probs.append(e / jnp.sum(e, axis=-1, keepdims=True))
        a = probs[0] - lam * probs[1]
        o = _mm(a, vall_ref[:, hs])
        o = o * lax.rsqrt(jnp.mean(o * o, axis=-1, keepdims=True) + EPS) * ng_ref[...]
        o = o * (1.0 - lam_init)
        o_ref[0, :, hs] = (o * _silu(z_ref[0, :, hs])).astype(BF16)


def diff_attention(proj3, layer, lam_init, da_lam, da_norm_g, ctx_kv):
    nb, t, _ = proj3.shape
    with_ctx = ctx_kv is not None
    s_tot = t + (PAST_LEN if with_ctx else 0)
    tq = 256
    wb = DA_WIDTH
    in_specs = [pl.BlockSpec((1, tq, wb), lambda b, i: (b, i, COL_QB // wb)),
                pl.BlockSpec((1, t, wb), lambda b, i: (b, 0, COL_KB // wb)),
                pl.BlockSpec((1, t, wb), lambda b, i: (b, 0, COL_VB // wb)),
                pl.BlockSpec((1, tq, wb), lambda b, i: (b, i, COL_ZB // wb))]
    args = [proj3, proj3, proj3, proj3]
    if with_ctx:
        kc, vc = ctx_kv
        cos, sin = _rope_tables(t)
        in_specs += [pl.BlockSpec((1, PAST_LEN, wb), lambda b, i: (b, layer, 0)),
                     pl.BlockSpec((1, PAST_LEN, wb), lambda b, i: (b, layer, 0)),
                     pl.BlockSpec((tq, 128), lambda b, i: (i, 0)),
                     pl.BlockSpec((tq, 128), lambda b, i: (i, 0)),
                     pl.BlockSpec((t, 128), lambda b, i: (0, 0)),
                     pl.BlockSpec((t, 128), lambda b, i: (0, 0))]
        args += [kc, vc, cos, sin, cos, sin]
    in_specs += [pl.BlockSpec((4, DA_HEAD), lambda b, i: (0, 0)),
                 pl.BlockSpec((1, DA_VDIM), lambda b, i: (0, 0))]
    args += [da_lam, da_norm_g.reshape(1, DA_VDIM)]
    return pl.pallas_call(
        functools.partial(_attn_kernel, lam_init=lam_init, t=t, s_tot=s_tot, tq=tq, with_ctx=with_ctx),
        grid=(nb, t // tq),
        in_specs=in_specs,
        out_specs=pl.BlockSpec((1, tq, wb), lambda b, i: (b, i, 0)),
        out_shape=jax.ShapeDtypeStruct((nb, t, wb), BF16),
        scratch_shapes=[pltpu.VMEM((s_tot, wb), BF16), pltpu.VMEM((s_tot, wb), BF16)],
        compiler_params=_cparams(("parallel", "arbitrary")),
        name="diff_attention",
    )(*args)


DN_PAD = 8
DN_RT = 128
DN_GROUP = 128


def _dn_kernel(*refs, t, ngroup, with_s0):
    if with_s0:
        (qkv_ref, z_ref, ba_ref, cw_ref, alog_ref, dtb_ref, ng_ref, s0_ref,
         out_ref, sfin_ref, xp_ref, qkvn_ref, oacc_ref, st_ref) = refs
    else:
        (qkv_ref, z_ref, ba_ref, cw_ref, alog_ref, dtb_ref, ng_ref,
         out_ref, sfin_ref, xp_ref, qkvn_ref, oacc_ref, st_ref) = refs
    n = pl.program_id(1)
    cd = DN_CHUNK
    w3 = 3 * DN_WIDTH

    @pl.when(n == 0)
    def _init():
        xp_ref[0:DN_PAD, :] = jnp.zeros((DN_PAD, w3), F32)
        xp_ref[DN_PAD + t:2 * DN_PAD + t, :] = jnp.zeros((DN_PAD, w3), F32)
        xp_ref[DN_PAD:DN_PAD + t, :] = qkv_ref[0]
        half = DN_CONV // 2
        for r in range(t // DN_RT):
            for sec in range(3):
                for h in range(DN_HEADS):
                    cs = slice(sec * DN_WIDTH + h * DN_HEAD, sec * DN_WIDTH + (h + 1) * DN_HEAD)
                    acc = jnp.zeros((DN_RT, DN_HEAD), F32)
                    for j in range(DN_CONV):
                        r0 = DN_PAD + r * DN_RT + j - half
                        acc = acc + xp_ref[r0:r0 + DN_RT, cs] * cw_ref[j:j + 1, cs]
                    y = _silu(acc)
                    if sec < 2:
                        y = y * lax.rsqrt(jnp.sum(y * y, axis=-1, keepdims=True) + EPS)
                    if sec == 0:
                        y = y * (DN_HEAD ** -0.5)
                    qkvn_ref[r * DN_RT:(r + 1) * DN_RT, cs] = y
        oacc_ref[...] = jnp.zeros_like(oacc_ref)
        if with_s0:
            st_ref[...] = s0_ref[0, 0]
        else:
            st_ref[...] = jnp.zeros_like(st_ref)

    gb = DN_GROUP
    nsub = gb // cd
    ri = lax.broadcasted_iota(jnp.int32, (gb, gb), 0)
    ci = lax.broadcasted_iota(jnp.int32, (gb, gb), 1)
    same = (ri // cd) == (ci // cd)
    samef = same.astype(F32)
    eye = (ri == ci).astype(F32)

    chains = []
    for d in range(2):
        grp = n if d == 0 else ngroup - 1 - n
        r0 = pl.multiple_of(grp * gb, gb)
        incl = same & ((ci <= ri) if d == 0 else (ci >= ri))
        strict = same & ((ci < ri) if d == 0 else (ci > ri))
        inclf = incl.astype(F32)
        ba = ba_ref[0, pl.ds(r0, gb), :]
        beta_all = jax.nn.sigmoid(ba)
        g_all = -jnp.exp(alog_ref[...]) * jax.nn.softplus(ba + dtb_ref[...])
        gc = jnp.dot(inclf, g_all, precision=HI, preferred_element_type=F32)
        gct = lax.dot_general(g_all.T, inclf, (((1,), (1,)), ((), ())), precision=HI,
                              preferred_element_type=F32)
        gtot = jnp.dot(samef, g_all, precision=HI, preferred_element_type=F32)
        for h in range(DN_HEADS):
            hs = slice(h * DN_HEAD, (h + 1) * DN_HEAD)
            col = 2 * DN_HEADS + d * DN_HEADS + h
            gcol = gc[:, col:col + 1]
            grow = gct[col:col + 1, :]
            gt = gtot[:, col:col + 1]
            beta = beta_all[:, d * DN_HEADS + h:d * DN_HEADS + h + 1]
            q = qkvn_ref[pl.ds(r0, gb), hs]
            k = qkvn_ref[pl.ds(r0, gb), DN_WIDTH + h * DN_HEAD:DN_WIDTH + (h + 1) * DN_HEAD]
            v = qkvn_ref[pl.ds(r0, gb), 2 * DN_WIDTH + h * DN_HEAD:2 * DN_WIDTH + (h + 1) * DN_HEAD]
            eg = jnp.exp(gcol)
            chains.append(dict(
                d=d, h=h, r0=r0, hs=hs, strict=strict, q=q, k=k, kb=k * beta,
                decay=jnp.where(incl, jnp.exp(jnp.where(incl, gcol - grow, 0.0)), 0.0),
                rhs=jnp.concatenate([v * beta, k * beta * eg], axis=-1),
                qe=q * eg, kdec=k * jnp.exp(gt - gcol), egt=jnp.exp(gt), st=st_ref[d, h]))

    for c in chains:
        c["mm"] = jnp.where(c["strict"], _mm_nt(c["kb"], c["k"]) * c["decay"], 0.0)
    for c in chains:
        c["qk"] = _mm_nt(c["q"], c["k"]) * c["decay"]
    for c in chains:
        c["tinv"] = eye - jnp.where((ri // 2) == (ci // 2), c["mm"], 0.0)
    s = 2
    while s < cd:
        offmask = ((ri // (2 * s)) == (ci // (2 * s))) & ((ri // s) != (ci // s))
        xs = [_mm(jnp.where(offmask, c["mm"], 0.0), c["tinv"]) for c in chains]
        ys = [_mm(c["tinv"], x) for c, x in zip(chains, xs)]
        for c, y in zip(chains, ys):
            c["tinv"] = c["tinv"] - y
        s *= 2
    for c in chains:
        c["uw"] = _mm(c["tinv"], c["rhs"])

    outs = []
    for step in range(nsub):
        rs = []
        for c in chains:
            sub = step if c["d"] == 0 else nsub - 1 - step
            c["rows"] = slice(sub * cd, (sub + 1) * cd)
            rs.append(_mm(jnp.concatenate([c["uw"][c["rows"], DN_HEAD:], c["qe"][c["rows"]]], axis=0), c["st"]))
        for c, r in zip(chains, rs):
            rows = c["rows"]
            v_new = c["uw"][rows, :DN_HEAD] - r[:cd]
            o = r[cd:] + _mm(c["qk"][rows, rows], v_new)
            c["st"] = c["st"] * c["egt"][rows.start:rows.start + 1] + _mm(c["kdec"][rows].T, v_new)
            outs.append((c, rows.start, o))
    for c in chains:
        st_ref[c["d"], c["h"]] = c["st"]
    for c, off, o in outs:
        oacc_ref[pl.ds(pl.multiple_of(c["r0"] + off, cd), cd), c["hs"]] += o

    @pl.when(n == ngroup - 1)
    def _fin():
        for h in range(DN_HEADS):
            hs = slice(h * DN_HEAD, (h + 1) * DN_HEAD)
            o = oacc_ref[:, hs]
            o = o * lax.rsqrt(jnp.mean(o * o, axis=-1, keepdims=True) + EPS) * ng_ref[...]
            out_ref[0, :, hs] = (o * _silu(z_ref[0, :, hs])).astype(BF16)
        sfin_ref[0] = st_ref[...]


def deltanet(proj3, ba3, layer, conv_w, a_log, dt_bias, norm_g, s0):
    nb, t, _ = proj3.shape
    ngroup = t // DN_GROUP
    with_s0 = s0 is not None
    w3 = 3 * DN_WIDTH
    pad = jnp.zeros((2 * DN_HEADS,), F32)
    alog_row = jnp.concatenate([pad, a_log.reshape(-1), jnp.zeros((128 - 4 * DN_HEADS,), F32)]).reshape(1, 128)
    dtb_row = jnp.concatenate([pad, dt_bias.reshape(-1), jnp.zeros((128 - 4 * DN_HEADS,), F32)]).reshape(1, 128)
    in_specs = [pl.BlockSpec((1, t, w3), lambda b, n: (b, 0, COL_QC // w3)),
                pl.BlockSpec((1, t, DN_WIDTH), lambda b, n: (b, 0, COL_ZC // DN_WIDTH)),
                pl.BlockSpec((1, t, 128), lambda b, n: (b, 0, 0)),
                pl.BlockSpec((8, w3), lambda b, n: (0, 0)),
                pl.BlockSpec((1, 128), lambda b, n: (0, 0)),
                pl.BlockSpec((1, 128), lambda b, n: (0, 0)),
                pl.BlockSpec((1, DN_HEAD), lambda b, n: (0, 0))]
    args = [proj3, proj3, ba3, jnp.pad(conv_w, ((0, 8 - DN_CONV), (0, 0))), alog_row, dtb_row,
            norm_g.reshape(1, DN_HEAD)]
    if with_s0:
        in_specs.append(pl.BlockSpec((1, 1, 2, DN_HEADS, DN_HEAD, DN_HEAD), lambda b, n: (b, layer, 0, 0, 0, 0)))
        args.append(s0)
    return pl.pallas_call(
        functools.partial(_dn_kernel, t=t, ngroup=ngroup, with_s0=with_s0),
        grid=(nb, ngroup),
        in_specs=in_specs,
        out_specs=[pl.BlockSpec((1, t, DN_WIDTH), lambda b, n: (b, 0, 0)),
                   pl.BlockSpec((1, 2, DN_HEADS, DN_HEAD, DN_HEAD), lambda b, n: (b, 0, 0, 0, 0))],
        out_shape=[jax.ShapeDtypeStruct((nb, t, DN_WIDTH), BF16),
                   jax.ShapeDtypeStruct((nb, 2, DN_HEADS, DN_HEAD, DN_HEAD), F32)],
        scratch_shapes=[pltpu.VMEM((t + 2 * DN_PAD, w3), F32),
                        pltpu.VMEM((t, w3), F32),
                        pltpu.VMEM((t, DN_WIDTH), F32),
                        pltpu.VMEM((2, DN_HEADS, DN_HEAD, DN_HEAD), F32)],
        compiler_params=_cparams(("parallel", "arbitrary")),
        name="deltanet",
    )(*args)


def _merge_kernel(oa_ref, ob_ref, oc_ref, ga_ref, gb_ref, gc_ref, wb_ref, wo_ref, x_ref, gate_ref, fg_ref,
                  *outs, final):
    acc = None
    for i, (o_ref, g_ref) in enumerate(((oa_ref, ga_ref), (ob_ref, gb_ref), (oc_ref, gc_ref))):
        pr = jnp.dot(o_ref[...], wb_ref[i], preferred_element_type=F32)
        term = jax.nn.sigmoid(g_ref[...]) * pr
        acc = term if acc is None else acc + term
    y = jnp.dot(acc.astype(BF16), wo_ref[...], preferred_element_type=F32)
    xn = x_ref[...] + gate_ref[0] * y
    outs[0][...] = xn
    if final:
        yn = xn * lax.rsqrt(jnp.mean(xn * xn, axis=-1, keepdims=True) + EPS) * fg_ref[...]
        outs[1][...] = yn


def merge(out_a, out_b, out_c, proj, w_branch, w_out, x2, gate, final_g, rows_per_mod, final):
    m = x2.shape[0]
    tm = 512
    nmod = gate.shape[0]
    row = lambda i: (i, 0)
    out_specs = [pl.BlockSpec((tm, D_MODEL), row)]
    out_shape = [jax.ShapeDtypeStruct((m, D_MODEL), F32)]
    if final:
        out_specs.append(pl.BlockSpec((tm, D_MODEL), row))
        out_shape.append(jax.ShapeDtypeStruct((m, D_MODEL), F32))
    return pl.pallas_call(
        functools.partial(_merge_kernel, final=final),
        grid=(m // tm,),
        in_specs=[pl.BlockSpec((tm, BRANCH_WIDTH), row),
                  pl.BlockSpec((tm, BRANCH_WIDTH), row),
                  pl.BlockSpec((tm, BRANCH_WIDTH), row),
                  pl.BlockSpec((tm, D_MODEL), lambda i: (i, COL_GATES // D_MODEL)),
                  pl.BlockSpec((tm, D_MODEL), lambda i: (i, COL_GATES // D_MODEL + 1)),
                  pl.BlockSpec((tm, D_MODEL), lambda i: (i, COL_GATES // D_MODEL + 2)),
                  pl.BlockSpec((N_BRANCH, BRANCH_WIDTH, D_MODEL), lambda i: (0, 0, 0)),
                  pl.BlockSpec((D_MODEL, D_MODEL), lambda i: (0, 0)),
                  pl.BlockSpec((tm, D_MODEL), row),
                  pl.BlockSpec((1, 1, D_MODEL), lambda i: ((i * tm) // rows_per_mod, 0, 0)),
                  pl.BlockSpec((1, D_MODEL), lambda i: (0, 0))],
        out_specs=out_specs,
        out_shape=out_shape,
        compiler_params=_cparams(("parallel",)),
        name="merge",
    )(out_a, out_b, out_c, proj, proj, proj, w_branch, w_out, x2, gate.reshape(nmod, 1, D_MODEL),
      final_g.reshape(1, D_MODEL))


def _run_pass(x, mod, wts, lam_inits, final_g, ctx):
    nb, t, _ = x.shape
    m = nb * t
    nmod = mod.shape[1]
    rows_per_mod = m if nmod == 1 else t
    x2 = x.reshape(m, D_MODEL)
    states = []
    y = None
    for l in range(DEPTH):
        w = wts[l]
        shift, scale, gate = jnp.split(mod[l], 3, axis=-1)
        proj, ba = inproj(x2, w["norm_g"], scale, shift, w["w1"], w["w2"], rows_per_mod)
        proj3 = proj.reshape(nb, t, N_MAIN)
        if ctx is None:
            h0 = jnp.zeros((2, 2, nb, S5_GROUPS * S5_STATE), F32)
            ctx_kv, s0 = None, None
        else:
            cache_k, cache_v, st_re, st_im, st_dn = ctx
            h0 = jnp.stack([st_re[:, l], st_im[:, l]], 0)
            h0 = jnp.transpose(h0, (2, 0, 1, 3, 4)).reshape(2, 2, nb, S5_GROUPS * S5_STATE)
            ctx_kv, s0 = (cache_k, cache_v), st_dn
        out_a, hfin = s5_branch(proj, nb, t, w["s5_mats"], w["s5_d"], w["w_glu"], h0)
        out_b = diff_attention(proj3, l, lam_inits[l], w["da_lam"], w["da_norm_g"], ctx_kv)
        out_c, sfin = deltanet(proj3, ba.reshape(nb, t, 128), l, w["dn_conv"], w["dn_a_log"],
                               w["dn_dt_bias"], w["dn_norm_g"], s0)
        final = l == DEPTH - 1
        res = merge(out_a, out_b.reshape(m, DA_WIDTH), out_c.reshape(m, DN_WIDTH), proj,
                    w["w_branch"], w["w_out"], x2, gate, final_g, rows_per_mod, final)
        x2 = res[0]
        if final:
            y = res[1]
        if ctx is None:
            k_new = proj3[:, :, COL_KB:COL_KB + 512].reshape(nb, t, DA_HEADS, 2, DA_HEAD)
            v_new = proj3[:, :, COL_VB:COL_VB + 512].reshape(nb, t, DA_HEADS, DA_VDIM)
            hf = hfin.reshape(2, 2, nb, S5_GROUPS, S5_STATE)
            states.append((k_new, v_new, jnp.transpose(hf[:, 0], (1, 0, 2, 3)),
                           jnp.transpose(hf[:, 1], (1, 0, 2, 3)), sfin))
    return y.reshape(nb, t, D_MODEL), states


def kernel(x_prompt, x_sample, cache_k, cache_v, state_s5_re, state_s5_im, state_dn, c, c_ctx,
           norm_g, w_ada, b_ada, w_in, s5_lam_re, s5_lam_im, s5_log_step, s5_b_re, s5_b_im,
           s5_c_re, s5_c_im, s5_d, s5_w_glu, da_lam, da_norm_g, dn_conv, dn_a_log, dn_dt_bias,
           dn_norm_g, w_branch, w_out, final_norm_g):
    nb_dec = x_sample.shape[0]
    cond8 = jnp.concatenate([c_ctx[None, :], c, jnp.zeros((8 - 1 - nb_dec, D_MODEL), F32)], 0)
    mod = ada_mod(cond8, w_ada, b_ada)
    w1 = jnp.concatenate([w_in[:, :, :BA_OFF], w_in[:, :, GATES_OFF:]], -1).astype(BF16)
    w2 = jnp.pad(w_in[:, :, BA_OFF:GATES_OFF], ((0, 0), (0, 0), (0, 128 - (GATES_OFF - BA_OFF)))).astype(BF16)
    wts = []
    for l in range(DEPTH):
        wts.append(dict(
            norm_g=norm_g[l], w1=w1[l], w2=w2[l],
            s5_mats=s5_matrices(s5_lam_re[l], s5_lam_im[l], s5_log_step[l], s5_b_re[l], s5_b_im[l],
                                s5_c_re[l], s5_c_im[l]),
            s5_d=s5_d[l], w_glu=s5_w_glu[l].astype(BF16), da_lam=da_lam[l], da_norm_g=da_norm_g[l],
            dn_conv=dn_conv[l], dn_a_log=dn_a_log[l], dn_dt_bias=dn_dt_bias[l], dn_norm_g=dn_norm_g[l],
            w_branch=w_branch[l].astype(BF16), w_out=w_out[l].astype(BF16)))
    lam_inits = [0.8 - 0.6 * math.exp(-0.3 * l) for l in range(DEPTH)]

    y_prompt, states = _run_pass(x_prompt, mod[:, 0:1], wts, lam_inits, final_norm_g, None)
    ctx = (cache_k.reshape(nb_dec, DEPTH * PAST_LEN, DA_WIDTH),
           cache_v.reshape(nb_dec, DEPTH * PAST_LEN, DA_WIDTH), state_s5_re, state_s5_im, state_dn)
    y_sample, _ = _run_pass(x_sample, mod[:, 1:1 + nb_dec], wts, lam_inits, final_norm_g, ctx)

    new_cache_k = jnp.stack([s[0] for s in states], axis=1)
    new_cache_v = jnp.stack([s[1] for s in states], axis=1)
    new_s5_re = jnp.stack([s[2] for s in states], axis=1)
    new_s5_im = jnp.stack([s[3] for s in states], axis=1)
    new_dn = jnp.stack([s[4] for s in states], axis=1)
    return (y_prompt, y_sample, new_cache_k, new_cache_v, new_s5_re, new_s5_im, new_dn)
```

```python
import functools
import math

import numpy as np
import jax
import jax.numpy as jnp
from jax import lax
from jax.experimental import pallas as pl
from jax.experimental.pallas import tpu as pltpu

F32 = jnp.float32
BF16 = jnp.bfloat16

D_MODEL = 1024
DEPTH = 2
GRID_W = 64
EPS = 1e-6
S5_WIDTH = 512
S5_GROUP = 16
S5_GROUPS = 32
S5_STATE = 64
S5_CHUNK = 16
S5_PAIRS = S5_GROUPS // 2
S5_ROW = S5_CHUNK * S5_GROUP
DA_HEADS = 4
DA_HEAD = 64
DA_VDIM = 128
DA_WIDTH = 512
ROPE_BASE = 10000.0
DN_HEADS = 4
DN_HEAD = 128
DN_WIDTH = 512
DN_CONV = 5
DN_CHUNK = 64
N_BRANCH = 3
BRANCH_WIDTH = 512
PAST_LEN = 512

COL_UA, COL_ZA, COL_QB, COL_KB, COL_VB, COL_ZB = 0, 512, 1024, 1536, 2048, 2560
COL_QC, COL_ZC, COL_GATES = 3072, 4608, 5120
N_MAIN = 8192
BA_OFF = 5120
GATES_OFF = 5136

VMEM_LIMIT = 56 * 1024 * 1024
HI = lax.Precision.HIGHEST


def _cparams(sem):
    return pltpu.CompilerParams(dimension_semantics=sem, vmem_limit_bytes=VMEM_LIMIT)


def _mm(a, b):
    return jnp.dot(a.astype(BF16), b.astype(BF16), preferred_element_type=F32)


def _mm_nt(a, b):
    return lax.dot_general(a.astype(BF16), b.astype(BF16), (((1,), (1,)), ((), ())),
                           preferred_element_type=F32)


def _silu(x):
    return x * jax.nn.sigmoid(x)


def _ada_kernel(c_ref, w_ref, b_ref, o_ref):
    o_ref[0] = _mm(_silu(c_ref[...]), w_ref[0]) + b_ref[0]


def ada_mod(cond8, w_ada, b_ada):
    tn = 1024
    return pl.pallas_call(
        _ada_kernel,
        grid=(DEPTH, 3 * D_MODEL // tn),
        in_specs=[pl.BlockSpec((8, D_MODEL), lambda l, j: (0, 0)),
                  pl.BlockSpec((1, D_MODEL, tn), lambda l, j: (l, 0, j)),
                  pl.BlockSpec((1, 1, tn), lambda l, j: (l, 0, j))],
        out_specs=pl.BlockSpec((1, 8, tn), lambda l, j: (l, 0, j)),
        out_shape=jax.ShapeDtypeStruct((DEPTH, 8, 3 * D_MODEL), F32),
        compiler_params=_cparams(("parallel", "parallel")),
        name="ada_mod",
    )(cond8, w_ada, b_ada.reshape(DEPTH, 1, 3 * D_MODEL))


def _inproj_kernel(*refs, n_alias, with_kv):
    x_ref, g_ref, sc_ref, sh_ref, w1_ref, w2_ref = refs[:6]
    outs = refs[6 + n_alias:]
    proj_ref, ba_ref, hn_ref = outs[0], outs[1], outs[-1]
    j = pl.program_id(1)

    @pl.when(j == 0)
    def _():
        x = x_ref[...]
        y = x * lax.rsqrt(jnp.mean(x * x, axis=-1, keepdims=True) + EPS) * g_ref[...]
        hn = (y * (1.0 + sc_ref[0]) + sh_ref[0]).astype(BF16)
        hn_ref[...] = hn
        ba_ref[...] = jnp.dot(hn, w2_ref[...], preferred_element_type=F32)

    proj_ref[...] = jnp.dot(hn_ref[...], w1_ref[...], preferred_element_type=F32)

    if with_kv:
        tn = proj_ref.shape[1]
        for ref, col in ((outs[2], COL_KB), (outs[3], COL_VB)):
            @pl.when(j == col // tn)
            def _(ref=ref, col=col):
                ref[...] = proj_ref[:, col % tn:col % tn + DA_WIDTH].reshape(ref.shape)


def inproj(x2, norm_g, scale, shift, w1, w2, rows_per_mod, kv=None):
    m = x2.shape[0]
    tm, tn = 1024, 1024
    nmod = scale.shape[0]
    mod_idx = lambda i, j: ((i * tm) // rows_per_mod, 0, 0)
    in_specs = [pl.BlockSpec((tm, D_MODEL), lambda i, j: (i, 0)),
                pl.BlockSpec((1, D_MODEL), lambda i, j: (0, 0)),
                pl.BlockSpec((1, 1, D_MODEL), mod_idx),
                pl.BlockSpec((1, 1, D_MODEL), mod_idx),
                pl.BlockSpec((D_MODEL, tn), lambda i, j: (0, j)),
                pl.BlockSpec((D_MODEL, 128), lambda i, j: (0, 0))]
    args = [x2, norm_g.reshape(1, D_MODEL), scale.reshape(nmod, 1, D_MODEL),
            shift.reshape(nmod, 1, D_MODEL), w1, w2]
    out_specs = [pl.BlockSpec((tm, tn), lambda i, j: (i, j)),
                 pl.BlockSpec((tm, 128), lambda i, j: (i, 0))]
    out_shape = [jax.ShapeDtypeStruct((m, N_MAIN), F32),
                 jax.ShapeDtypeStruct((m, 128), F32)]
    aliases = {}
    if kv is not None:
        layer, t, caches = kv
        cspec = pl.BlockSpec((tm // t, 1, t, DA_WIDTH), lambda i, j: (i, layer, 0, 0))
        out_specs += [cspec, cspec]
        out_shape += [jax.ShapeDtypeStruct((m // t, DEPTH, t, DA_WIDTH), F32)] * 2
        if caches is not None:
            in_specs += [pl.BlockSpec(memory_space=pl.ANY)] * 2
            args += list(caches)
            aliases = {6: 2, 7: 3}
    return pl.pallas_call(
        functools.partial(_inproj_kernel, n_alias=len(aliases), with_kv=kv is not None),
        grid=(m // tm, N_MAIN // tn),
        in_specs=in_specs,
        out_specs=out_specs,
        out_shape=out_shape,
        scratch_shapes=[pltpu.VMEM((tm, D_MODEL), BF16)],
        input_output_aliases=aliases,
        compiler_params=_cparams(("parallel", "arbitrary")),
        name="inproj",
    )(*args)


def _s5_gen_kernel(crt_ref, cit_ref, prt_ref, pit_ref, bbt_ref, bbs_ref, prow_ref, pirow_ref,
                   wt_ref, ws_ref, wh_ref):
    L, C, P = S5_CHUNK, S5_GROUP, S5_STATE
    strips = []
    for d in range(2):
        crt, cit = crt_ref[d, 0], cit_ref[d, 0]
        blocks = []
        for k in (range(L + 1) if d == 0 else range(L, -1, -1)):
            prk, pik = prt_ref[d, 0, :, k:k + 1], pit_ref[d, 0, :, k:k + 1]
            blocks.append(jnp.concatenate([crt * prk - cit * pik, -(crt * pik + cit * prk)], axis=0))
        ca = jnp.concatenate(blocks, axis=-1)
        wh_ref[d, 0] = (ca[:, C:] if d == 0 else ca[:, :L * C]).astype(BF16)
        bbt, bbs = bbt_ref[d, 0], bbs_ref[d, 0]
        strips.append(jnp.dot(bbt, ca[:, :L * C] if d == 0 else ca[:, C:], precision=HI,
                              preferred_element_type=F32))
        rows = []
        for i in range(L):
            k = L - 1 - i if d == 0 else i
            rows.append(bbt * prow_ref[d, 0, k:k + 1, :] + bbs * pirow_ref[d, 0, k:k + 1, :])
        ws_ref[d, 0] = jnp.concatenate(rows, axis=0).astype(BF16)
    zeros = jnp.zeros((C, L * C), F32)
    fpad = jnp.concatenate([zeros, strips[0]], axis=-1)
    rpad = jnp.concatenate([strips[1], zeros], axis=-1)
    rows = []
    for i in range(L):
        rows.append(fpad[:, (L - i) * C:(2 * L - i) * C] + rpad[:, (L - 1 - i) * C:(2 * L - 1 - i) * C])
    wt_ref[0] = jnp.concatenate(rows, axis=0).astype(BF16)


def s5_matrices(lam_re, lam_im, log_step, b_re, b_im, c_re, c_im):
    L, G, P, C = S5_CHUNK, S5_GROUPS, S5_STATE, S5_GROUP
    step = jnp.exp(log_step)[..., None]
    mag = jnp.exp(lam_re * step)
    ar, ai = mag * jnp.cos(lam_im * step), mag * jnp.sin(lam_im * step)
    den = lam_re * lam_re + lam_im * lam_im
    fr = ((ar - 1.0) * lam_re + ai * lam_im) / den
    fi = (ai * lam_re - (ar - 1.0) * lam_im) / den
    bbr = fr[..., None] * b_re - fi[..., None] * b_im
    bbi = fr[..., None] * b_im + fi[..., None] * b_re
    ks = jnp.arange(L + 1, dtype=F32)[None, None, None, :]
    pmag = jnp.exp(ks * (lam_re * step)[..., None])
    prt = pmag * jnp.cos(ks * (lam_im * step)[..., None])
    pit = pmag * jnp.sin(ks * (lam_im * step)[..., None])
    prow = jnp.swapaxes(prt, 2, 3)
    pirow = jnp.swapaxes(pit, 2, 3)
    bbrt, bbit = jnp.swapaxes(bbr, 2, 3), jnp.swapaxes(bbi, 2, 3)
    args = [jnp.swapaxes(c_re, 2, 3), jnp.swapaxes(c_im, 2, 3), prt, pit,
            jnp.concatenate([bbrt, bbit], -1), jnp.concatenate([bbit, bbrt], -1),
            jnp.concatenate([prow, prow], -1), jnp.concatenate([-pirow, pirow], -1)]
    spec = lambda a: pl.BlockSpec((2, 1) + a.shape[2:], lambda g: (0, g, 0, 0))
    wt, ws, wh = pl.pallas_call(
        _s5_gen_kernel,
        grid=(G,),
        in_specs=[spec(a) for a in args],
        out_specs=[pl.BlockSpec((1, S5_ROW, S5_ROW), lambda g: (g, 0, 0)),
                   pl.BlockSpec((2, 1, S5_ROW, 2 * P), lambda g: (0, g, 0, 0)),
                   pl.BlockSpec((2, 1, 2 * P, S5_ROW), lambda g: (0, g, 0, 0))],
        out_shape=[jax.ShapeDtypeStruct((G, S5_ROW, S5_ROW), BF16),
                   jax.ShapeDtypeStruct((2, G, S5_ROW, 2 * P), BF16),
                   jax.ShapeDtypeStruct((2, G, 2 * P, S5_ROW), BF16)],
        compiler_params=_cparams(("parallel",)),
        name="s5_gen",
    )(*args)
    a_l = jnp.stack([prt[..., L].reshape(2, 1, G * P), pit[..., L].reshape(2, 1, G * P)], 1)
    return wt, ws, wh, a_l


S5_GB = 8


def _s5_state_kernel(u_ref, ws_ref, x_ref, s_ref, *, r):
    us = [u_ref[pl.ds(i, r, stride=S5_CHUNK), :] for i in range(S5_CHUNK)]
    for g in range(S5_GB):
        xg = jnp.concatenate([u[:, S5_GROUP * g:S5_GROUP * (g + 1)] for u in us], axis=-1)
        x_ref[:, g * S5_ROW:(g + 1) * S5_ROW] = xg.astype(BF16)
    for p in range(S5_GB // 2):
        for d in range(2):
            sg = [jnp.dot(x_ref[:, (2 * p + k) * S5_ROW:(2 * p + k + 1) * S5_ROW], ws_ref[d, 2 * p + k],
                          preferred_element_type=F32) for k in range(2)]
            for comp in range(2):
                cs = slice(comp * S5_STATE, (comp + 1) * S5_STATE)
                s_ref[d, comp, :, p * 128:(p + 1) * 128] = jnp.concatenate([sg[0][:, cs], sg[1][:, cs]], axis=-1)


def s5_state(proj, ws, r):
    gw = S5_GB * S5_ROW
    return pl.pallas_call(
        functools.partial(_s5_state_kernel, r=r),
        grid=(S5_GROUPS // S5_GB,),
        in_specs=[pl.BlockSpec((r * S5_CHUNK, 128), lambda j: (0, COL_UA // 128 + j)),
                  pl.BlockSpec((2, S5_GB, S5_ROW, 2 * S5_STATE), lambda j: (0, j, 0, 0))],
        out_specs=[pl.BlockSpec((r, gw), lambda j: (0, j)),
                   pl.BlockSpec((2, 2, r, S5_GB * S5_STATE), lambda j: (0, 0, 0, j))],
        out_shape=[jax.ShapeDtypeStruct((r, S5_GROUPS * S5_ROW), BF16),
                   jax.ShapeDtypeStruct((2, 2, r, S5_GROUPS * S5_STATE), F32)],
        compiler_params=_cparams(("parallel",)),
        name="s5_state",
    )(proj, ws)


def _s5_scan_kernel(s_ref, a_ref, h0_ref, hin_ref, hfin_ref, *, nb, nchunk):
    ar, ai = a_ref[0, 0], a_ref[0, 1]

    def run(order):
        hr, hi = h0_ref[0, 0], h0_ref[0, 1]
        for c in order:
            rows = pl.ds(c, nb, stride=nchunk)
            hin_ref[0, 0, rows, :] = hr
            hin_ref[0, 1, rows, :] = hi
            sr, si = s_ref[0, 0, rows, :], s_ref[0, 1, rows, :]
            hr, hi = ar * hr - ai * hi + sr, ar * hi + ai * hr + si
        hfin_ref[0, 0] = hr
        hfin_ref[0, 1] = hi

    d = pl.program_id(0)
    pl.when(d == 0)(lambda: run(range(nchunk)))
    pl.when(d == 1)(lambda: run(range(nchunk - 1, -1, -1)))


def s5_scan(s, a_l, h0, nb, nchunk):
    r = s.shape[2]
    w = S5_GROUPS * S5_STATE
    tc = 128
    return pl.pallas_call(
        functools.partial(_s5_scan_kernel, nb=nb, nchunk=nchunk),
        grid=(2, w // tc),
        in_specs=[pl.BlockSpec((1, 2, r, tc), lambda d, j: (d, 0, 0, j)),
                  pl.BlockSpec((1, 2, 1, tc), lambda d, j: (d, 0, 0, j)),
                  pl.BlockSpec((1, 2, nb, tc), lambda d, j: (d, 0, 0, j))],
        out_specs=[pl.BlockSpec((1, 2, r, tc), lambda d, j: (d, 0, 0, j)),
                   pl.BlockSpec((1, 2, nb, tc), lambda d, j: (d, 0, 0, j))],
        out_shape=[jax.ShapeDtypeStruct((2, 2, r, w), F32),
                   jax.ShapeDtypeStruct((2, 2, nb, w), F32)],
        compiler_params=_cparams(("parallel", "parallel")),
        name="s5_scan",
    )(s, a_l, h0)


def _s5_out_kernel(x_ref, wt_ref, hin_ref, wh_ref, y_ref, *, r):
    ys = []
    for g in range(S5_GB):
        y = jnp.dot(x_ref[:, g * S5_ROW:(g + 1) * S5_ROW], wt_ref[g], preferred_element_type=F32)
        cs = slice((g // 2) * 128 + (g % 2) * S5_STATE, (g // 2) * 128 + (g % 2 + 1) * S5_STATE)
        for d in range(2):
            h = jnp.concatenate([hin_ref[d, 0, :, cs], hin_ref[d, 1, :, cs]], axis=-1)
            y = y + jnp.dot(h.astype(BF16), wh_ref[d, g], preferred_element_type=F32)
        ys.append(y)
    for j in range(S5_CHUNK):
        y_ref[pl.ds(j, r, stride=S5_CHUNK), :] = jnp.concatenate(
            [y[:, S5_GROUP * j:S5_GROUP * (j + 1)] for y in ys], axis=-1)


def s5_out(xc, wt, hin, wh):
    r = xc.shape[0]
    gw = S5_GB * S5_ROW
    return pl.pallas_call(
        functools.partial(_s5_out_kernel, r=r),
        grid=(S5_GROUPS // S5_GB,),
        in_specs=[pl.BlockSpec((r, gw), lambda j: (0, j)),
                  pl.BlockSpec((S5_GB, S5_ROW, S5_ROW), lambda j: (j, 0, 0)),
                  pl.BlockSpec((2, 2, r, S5_GB * S5_STATE), lambda j: (0, 0, 0, j)),
                  pl.BlockSpec((2, S5_GB, 2 * S5_STATE, S5_ROW), lambda j: (0, j, 0, 0))],
        out_specs=pl.BlockSpec((r * S5_CHUNK, 128), lambda j: (0, j)),
        out_shape=jax.ShapeDtypeStruct((r * S5_CHUNK, S5_WIDTH), F32),
        compiler_params=_cparams(("parallel",)),
        name="s5_out",
    )(xc, wt, hin, wh)


def _s5_epilogue_kernel(u_ref, y_ref, z_ref, d_ref, w_ref, o_ref):
    ya = jax.nn.gelu(d_ref[...] * u_ref[...] + y_ref[...])
    ya = ya * jax.nn.sigmoid(_mm(ya, w_ref[...]))
    o_ref[...] = (ya * _silu(z_ref[...])).astype(BF16)


def s5_epilogue(proj, y_s5, s5_d, w_glu):
    m = proj.shape[0]
    tm = 512
    return pl.pallas_call(
        _s5_epilogue_kernel,
        grid=(m // tm,),
        in_specs=[pl.BlockSpec((tm, S5_WIDTH), lambda i: (i, COL_UA // S5_WIDTH)),
                  pl.BlockSpec((tm, S5_WIDTH), lambda i: (i, 0)),
                  pl.BlockSpec((tm, S5_WIDTH), lambda i: (i, COL_ZA // S5_WIDTH)),
                  pl.BlockSpec((1, S5_WIDTH), lambda i: (0, 0)),
                  pl.BlockSpec((S5_WIDTH, S5_WIDTH), lambda i: (0, 0))],
        out_specs=pl.BlockSpec((tm, S5_WIDTH), lambda i: (i, 0)),
        out_shape=jax.ShapeDtypeStruct((m, S5_WIDTH), BF16),
        compiler_params=_cparams(("parallel",)),
        name="s5_epilogue",
    )(proj, y_s5, proj, s5_d.reshape(1, S5_WIDTH), w_glu)


def s5_branch(proj, nb, t, mats, s5_d, w_glu, h0):
    wt, ws, wh, a_l = mats
    nchunk = t // S5_CHUNK
    xc, s = s5_state(proj, ws, nb * nchunk)
    hin, hfin = s5_scan(s, a_l, h0, nb, nchunk)
    y = s5_out(xc, wt, hin, wh)
    return s5_epilogue(proj, y, s5_d, w_glu), hfin


def _rope_tables(t):
    rows = t // GRID_W
    row = np.repeat(np.arange(rows), GRID_W).astype(np.float32)
    col = np.tile(np.arange(GRID_W), rows).astype(np.float32)
    nf = DA_HEAD // 4
    inv = (ROPE_BASE ** (-jnp.arange(nf, dtype=F32) / nf))

    def tab(pos):
        ang = jnp.asarray(pos)[:, None] * inv[None, :]
        c, s = jnp.cos(ang), jnp.sin(ang)
        return jnp.concatenate([c, c], -1), jnp.concatenate([-s, s], -1)

    cr, sr = tab(row)
    cc, sc = tab(col)
    cos = jnp.concatenate([cr, cc], -1)
    sin = jnp.concatenate([sr, sc], -1)
    return jnp.tile(cos, (1, 2)), jnp.tile(sin, (1, 2))


def _rope(x, cos, sin):
    lane = lax.broadcasted_iota(jnp.int32, x.shape, 1)
    swapped = jnp.where((lane % 32) < 16, pltpu.roll(x, 112, 1), pltpu.roll(x, 16, 1))
    return x * cos + swapped * sin


def _attn_kernel(*refs, lam_init, t, s_tot, tq, with_ctx):
    if with_ctx:
        (q_ref, k_ref, v_ref, z_ref, kc_ref, vc_ref, cq_ref, sq_ref, ck_ref, sk_ref,
         lam_ref, ng_ref, o_ref, kall_ref, vall_ref) = refs
    else:
        q_ref, k_ref, v_ref, z_ref, lam_ref, ng_ref, o_ref, kall_ref, vall_ref = refs

    @pl.when(pl.program_id(1) == 0)
    def _():
        if with_ctx:
            for h in range(DA_HEADS):
                hs = slice(h * 128, (h + 1) * 128)
                kall_ref[0:t, hs] = _rope(k_ref[0, :, hs], ck_ref[...], sk_ref[...]).astype(BF16)
            kall_ref[t:s_tot, :] = kc_ref[0].astype(BF16)
            vall_ref[0:t, :] = v_ref[0].astype(BF16)
            vall_ref[t:s_tot, :] = vc_ref[0].astype(BF16)
        else:
            kall_ref[...] = k_ref[0].astype(BF16)
            vall_ref[...] = v_ref[0].astype(BF16)

    lp = lam_ref[...]
    lam = (jnp.exp(jnp.sum(lp[0:1] * lp[1:2], axis=-1, keepdims=True))
           - jnp.exp(jnp.sum(lp[2:3] * lp[3:4], axis=-1, keepdims=True)) + lam_init)
    lane = lax.broadcasted_iota(jnp.int32, (tq, 128), 1)
    for h in range(DA_HEADS):
        hs = slice(h * 128, (h + 1) * 128)
        q = q_ref[0, :, hs]
        if with_ctx:
            q = _rope(q, cq_ref[...], sq_ref[...])
        q = q * (DA_HEAD ** -0.5)
        k = kall_ref[:, hs]
        probs = []
        for m in range(2):
            qm = jnp.where((lane < DA_HEAD) if m == 0 else (lane >= DA_HEAD), q, 0.0)
            sc = _mm_nt(qm, k)
            sc = sc - jnp.max(sc, axis=-1, keepdims=True)
            e = jnp.exp(sc)
            probs.append(e / jnp.sum(e, axis=-1, keepdims=True))
        a = probs[0] - lam * probs[1]
        o = _mm(a, vall_ref[:, hs])
        o = o * lax.rsqrt(jnp.mean(o * o, axis=-1, keepdims=True) + EPS) * ng_ref[...]
        o = o * (1.0 - lam_init)
        o_ref[0, :, hs] = (o * _silu(z_ref[0, :, hs])).astype(BF16)


def diff_attention(proj3, layer, lam_init, da_lam, da_norm_g, ctx_kv):
    nb, t, _ = proj3.shape
    with_ctx = ctx_kv is not None
    s_tot = t + (PAST_LEN if with_ctx else 0)
    tq = 256
    wb = DA_WIDTH
    in_specs = [pl.BlockSpec((1, tq, wb), lambda b, i: (b, i, COL_QB // wb)),
                pl.BlockSpec((1, t, wb), lambda b, i: (b, 0, COL_KB // wb)),
                pl.BlockSpec((1, t, wb), lambda b, i: (b, 0, COL_VB // wb)),
                pl.BlockSpec((1, tq, wb), lambda b, i: (b, i, COL_ZB // wb))]
    args = [proj3, proj3, proj3, proj3]
    if with_ctx:
        kc, vc = ctx_kv
        cos, sin = _rope_tables(t)
        in_specs += [pl.BlockSpec((1, PAST_LEN, wb), lambda b, i: (b, layer, 0)),
                     pl.BlockSpec((1, PAST_LEN, wb), lambda b, i: (b, layer, 0)),
                     pl.BlockSpec((tq, 128), lambda b, i: (i, 0)),
                     pl.BlockSpec((tq, 128), lambda b, i: (i, 0)),
                     pl.BlockSpec((t, 128), lambda b, i: (0, 0)),
                     pl.BlockSpec((t, 128), lambda b, i: (0, 0))]
        args += [kc, vc, cos, sin, cos, sin]
    in_specs += [pl.BlockSpec((4, DA_HEAD), lambda b, i: (0, 0)),
                 pl.BlockSpec((1, DA_VDIM), lambda b, i: (0, 0))]
    args += [da_lam, da_norm_g.reshape(1, DA_VDIM)]
    return pl.pallas_call(
        functools.partial(_attn_kernel, lam_init=lam_init, t=t, s_tot=s_tot, tq=tq, with_ctx=with_ctx),
        grid=(nb, t // tq),
        in_specs=in_specs,
        out_specs=pl.BlockSpec((1, tq, wb), lambda b, i: (b, i, 0)),
        out_shape=jax.ShapeDtypeStruct((nb, t, wb), BF16),
        scratch_shapes=[pltpu.VMEM((s_tot, wb), BF16), pltpu.VMEM((s_tot, wb), BF16)],
        compiler_params=_cparams(("parallel", "arbitrary")),
        name="diff_attention",
    )(*args)


DN_PAD = 8
DN_RT = 128
DN_GROUP = 128


def _dn_kernel(*refs, t, ngroup, with_s0):
    if with_s0:
        (qkv_ref, z_ref, ba_ref, cw_ref, alog_ref, dtb_ref, ng_ref, s0_ref,
         out_ref, sfin_ref, xp_ref, qkvn_ref, oacc_ref, st_ref) = refs
    else:
        (qkv_ref, z_ref, ba_ref, cw_ref, alog_ref, dtb_ref, ng_ref,
         out_ref, sfin_ref, xp_ref, qkvn_ref, oacc_ref, st_ref) = refs
    n = pl.program_id(1)
    cd = DN_CHUNK
    w3 = 3 * DN_WIDTH

    @pl.when(n == 0)
    def _init():
        xp_ref[0:DN_PAD, :] = jnp.zeros((DN_PAD, w3), F32)
        xp_ref[DN_PAD + t:2 * DN_PAD + t, :] = jnp.zeros((DN_PAD, w3), F32)
        xp_ref[DN_PAD:DN_PAD + t, :] = qkv_ref[0]
        half = DN_CONV // 2
        for r in range(t // DN_RT):
            for sec in range(3):
                for h in range(DN_HEADS):
                    cs = slice(sec * DN_WIDTH + h * DN_HEAD, sec * DN_WIDTH + (h + 1) * DN_HEAD)
                    acc = jnp.zeros((DN_RT, DN_HEAD), F32)
                    for j in range(DN_CONV):
                        r0 = DN_PAD + r * DN_RT + j - half
                        acc = acc + xp_ref[r0:r0 + DN_RT, cs] * cw_ref[j:j + 1, cs]
                    y = _silu(acc)
                    if sec < 2:
                        y = y * lax.rsqrt(jnp.sum(y * y, axis=-1, keepdims=True) + EPS)
                    if sec == 0:
                        y = y * (DN_HEAD ** -0.5)
                    qkvn_ref[r * DN_RT:(r + 1) * DN_RT, cs] = y
        oacc_ref[...] = jnp.zeros_like(oacc_ref)
        if with_s0:
            st_ref[...] = s0_ref[0, 0]
        else:
            st_ref[...] = jnp.zeros_like(st_ref)

    gb = DN_GROUP
    nsub = gb // cd
    ri = lax.broadcasted_iota(jnp.int32, (gb, gb), 0)
    ci = lax.broadcasted_iota(jnp.int32, (gb, gb), 1)
    same = (ri // cd) == (ci // cd)
    samef = same.astype(F32)
    eye = (ri == ci).astype(F32)

    chains = []
    for d in range(2):
        grp = n if d == 0 else ngroup - 1 - n
        r0 = pl.multiple_of(grp * gb, gb)
        incl = same & ((ci <= ri) if d == 0 else (ci >= ri))
        strict = same & ((ci < ri) if d == 0 else (ci > ri))
        inclf = incl.astype(F32)
        ba = ba_ref[0, pl.ds(r0, gb), :]
        beta_all = jax.nn.sigmoid(ba)
        g_all = -jnp.exp(alog_ref[...]) * jax.nn.softplus(ba + dtb_ref[...])
        gc = jnp.dot(inclf, g_all, precision=HI, preferred_element_type=F32)
        gct = lax.dot_general(g_all.T, inclf, (((1,), (1,)), ((), ())), precision=HI,
                              preferred_element_type=F32)
        gtot = jnp.dot(samef, g_all, precision=HI, preferred_element_type=F32)
        for h in range(DN_HEADS):
            hs = slice(h * DN_HEAD, (h + 1) * DN_HEAD)
            col = 2 * DN_HEADS + d * DN_HEADS + h
            gcol = gc[:, col:col + 1]
            grow = gct[col:col + 1, :]
            gt = gtot[:, col:col + 1]
            beta = beta_all[:, d * DN_HEADS + h:d * DN_HEADS + h + 1]
            q = qkvn_ref[pl.ds(r0, gb), hs]
            k = qkvn_ref[pl.ds(r0, gb), DN_WIDTH + h * DN_HEAD:DN_WIDTH + (h + 1) * DN_HEAD]
            v = qkvn_ref[pl.ds(r0, gb), 2 * DN_WIDTH + h * DN_HEAD:2 * DN_WIDTH + (h + 1) * DN_HEAD]
            eg = jnp.exp(gcol)
            chains.append(dict(
                d=d, h=h, r0=r0, hs=hs, strict=strict, q=q, k=k, kb=k * beta,
                decay=jnp.where(incl, jnp.exp(jnp.where(incl, gcol - grow, 0.0)), 0.0),
                rhs=jnp.concatenate([v * beta, k * beta * eg], axis=-1),
                qe=q * eg, kdec=k * jnp.exp(gt - gcol), egt=jnp.exp(gt), st=st_ref[d, h]))

    for c in chains:
        c["mm"] = jnp.where(c["strict"], _mm_nt(c["kb"], c["k"]) * c["decay"], 0.0)
    for c in chains:
        c["qk"] = _mm_nt(c["q"], c["k"]) * c["decay"]
    for c in chains:
        c["tinv"] = eye - jnp.where((ri // 2) == (ci // 2), c["mm"], 0.0)
    s = 2
    while s < cd:
        offmask = ((ri // (2 * s)) == (ci // (2 * s))) & ((ri // s) != (ci // s))
        xs = [_mm(jnp.where(offmask, c["mm"], 0.0), c["tinv"]) for c in chains]
        ys = [_mm(c["tinv"], x) for c, x in zip(chains, xs)]
        for c, y in zip(chains, ys):
            c["tinv"] = c["tinv"] - y
        s *= 2
    for c in chains:
        c["uw"] = _mm(c["tinv"], c["rhs"])

    outs = []
    for step in range(nsub):
        rs = []
        for c in chains:
            sub = step if c["d"] == 0 else nsub - 1 - step
            c["rows"] = slice(sub * cd, (sub + 1) * cd)
            rs.append(_mm(jnp.concatenate([c["uw"][c["rows"], DN_HEAD:], c["qe"][c["rows"]]], axis=0), c["st"]))
        for c, r in zip(chains, rs):
            rows = c["rows"]
            v_new = c["uw"][rows, :DN_HEAD] - r[:cd]
            o = r[cd:] + _mm(c["qk"][rows, rows], v_new)
            c["st"] = c["st"] * c["egt"][rows.start:rows.start + 1] + _mm(c["kdec"][rows].T, v_new)
            outs.append((c, rows.start, o))
    for c in chains:
        st_ref[c["d"], c["h"]] = c["st"]
    for c, off, o in outs:
        oacc_ref[pl.ds(pl.multiple_of(c["r0"] + off, cd), cd), c["hs"]] += o

    @pl.when(n == ngroup - 1)
    def _fin():
        for h in range(DN_HEADS):
            hs = slice(h * DN_HEAD, (h + 1) * DN_HEAD)
            o = oacc_ref[:, hs]
            o = o * lax.rsqrt(jnp.mean(o * o, axis=-1, keepdims=True) + EPS) * ng_ref[...]
            out_ref[0, :, hs] = (o * _silu(z_ref[0, :, hs])).astype(BF16)
        sfin_ref[0] = st_ref[...]


def deltanet(proj3, ba3, layer, conv_w, a_log, dt_bias, norm_g, s0):
    nb, t, _ = proj3.shape
    ngroup = t // DN_GROUP
    with_s0 = s0 is not None
    w3 = 3 * DN_WIDTH
    pad = jnp.zeros((2 * DN_HEADS,), F32)
    alog_row = jnp.concatenate([pad, a_log.reshape(-1), jnp.zeros((128 - 4 * DN_HEADS,), F32)]).reshape(1, 128)
    dtb_row = jnp.concatenate([pad, dt_bias.reshape(-1), jnp.zeros((128 - 4 * DN_HEADS,), F32)]).reshape(1, 128)
    in_specs = [pl.BlockSpec((1, t, w3), lambda b, n: (b, 0, COL_QC // w3)),
                pl.BlockSpec((1, t, DN_WIDTH), lambda b, n: (b, 0, COL_ZC // DN_WIDTH)),
                pl.BlockSpec((1, t, 128), lambda b, n: (b, 0, 0)),
                pl.BlockSpec((8, w3), lambda b, n: (0, 0)),
                pl.BlockSpec((1, 128), lambda b, n: (0, 0)),
                pl.BlockSpec((1, 128), lambda b, n: (0, 0)),
                pl.BlockSpec((1, DN_HEAD), lambda b, n: (0, 0))]
    args = [proj3, proj3, ba3, jnp.pad(conv_w, ((0, 8 - DN_CONV), (0, 0))), alog_row, dtb_row,
            norm_g.reshape(1, DN_HEAD)]
    if with_s0:
        in_specs.append(pl.BlockSpec((1, 1, 2, DN_HEADS, DN_HEAD, DN_HEAD), lambda b, n: (b, layer, 0, 0, 0, 0)))
        args.append(s0)
    return pl.pallas_call(
        functools.partial(_dn_kernel, t=t, ngroup=ngroup, with_s0=with_s0),
        grid=(nb, ngroup),
        in_specs=in_specs,
        out_specs=[pl.BlockSpec((1, t, DN_WIDTH), lambda b, n: (b, 0, 0)),
                   pl.BlockSpec((1, 2, DN_HEADS, DN_HEAD, DN_HEAD), lambda b, n: (b, 0, 0, 0, 0))],
        out_shape=[jax.ShapeDtypeStruct((nb, t, DN_WIDTH), BF16),
                   jax.ShapeDtypeStruct((nb, 2, DN_HEADS, DN_HEAD, DN_HEAD), F32)],
        scratch_shapes=[pltpu.VMEM((t + 2 * DN_PAD, w3), F32),
                        pltpu.VMEM((t, w3), F32),
                        pltpu.VMEM((t, DN_WIDTH), F32),
                        pltpu.VMEM((2, DN_HEADS, DN_HEAD, DN_HEAD), F32)],
        compiler_params=_cparams(("parallel", "arbitrary")),
        name="deltanet",
    )(*args)


def _merge_kernel(oa_ref, ob_ref, oc_ref, ga_ref, gb_ref, gc_ref, wb_ref, wo_ref, x_ref, gate_ref, fg_ref,
                  *outs, final):
    acc = None
    for i, (o_ref, g_ref) in enumerate(((oa_ref, ga_ref), (ob_ref, gb_ref), (oc_ref, gc_ref))):
        pr = jnp.dot(o_ref[...], wb_ref[i], preferred_element_type=F32)
        term = jax.nn.sigmoid(g_ref[...]) * pr
        acc = term if acc is None else acc + term
    y = jnp.dot(acc.astype(BF16), wo_ref[...], preferred_element_type=F32)
    xn = x_ref[...] + gate_ref[0] * y
    outs[0][...] = xn
    if final:
        yn = xn * lax.rsqrt(jnp.mean(xn * xn, axis=-1, keepdims=True) + EPS) * fg_ref[...]
        outs[1][...] = yn


def merge(out_a, out_b, out_c, proj, w_branch, w_out, x2, gate, final_g, rows_per_mod, final):
    m = x2.shape[0]
    tm = 512
    nmod = gate.shape[0]
    row = lambda i: (i, 0)
    out_specs = [pl.BlockSpec((tm, D_MODEL), row)]
    out_shape = [jax.ShapeDtypeStruct((m, D_MODEL), F32)]
    if final:
        out_specs.append(pl.BlockSpec((tm, D_MODEL), row))
        out_shape.append(jax.ShapeDtypeStruct((m, D_MODEL), F32))
    return pl.pallas_call(
        functools.partial(_merge_kernel, final=final),
        grid=(m // tm,),
        in_specs=[pl.BlockSpec((tm, BRANCH_WIDTH), row),
                  pl.BlockSpec((tm, BRANCH_WIDTH), row),
                  pl.BlockSpec((tm, BRANCH_WIDTH), row),
                  pl.BlockSpec((tm, D_MODEL), lambda i: (i, COL_GATES // D_MODEL)),
                  pl.BlockSpec((tm, D_MODEL), lambda i: (i, COL_GATES // D_MODEL + 1)),
                  pl.BlockSpec((tm, D_MODEL), lambda i: (i, COL_GATES // D_MODEL + 2)),
                  pl.BlockSpec((N_BRANCH, BRANCH_WIDTH, D_MODEL), lambda i: (0, 0, 0)),
                  pl.BlockSpec((D_MODEL, D_MODEL), lambda i: (0, 0)),
                  pl.BlockSpec((tm, D_MODEL), row),
                  pl.BlockSpec((1, 1, D_MODEL), lambda i: ((i * tm) // rows_per_mod, 0, 0)),
                  pl.BlockSpec((1, D_MODEL), lambda i: (0, 0))],
        out_specs=out_specs,
        out_shape=out_shape,
        compiler_params=_cparams(("parallel",)),
        name="merge",
    )(out_a, out_b, out_c, proj, proj, proj, w_branch, w_out, x2, gate.reshape(nmod, 1, D_MODEL),
      final_g.reshape(1, D_MODEL))


def _run_pass(x, mod, wts, lam_inits, final_g, ctx):
    nb, t, _ = x.shape
    m = nb * t
    nmod = mod.shape[1]
    rows_per_mod = m if nmod == 1 else t
    x2 = x.reshape(m, D_MODEL)
    states = []
    y, caches = None, None
    for l in range(DEPTH):
        w = wts[l]
        shift, scale, gate = jnp.split(mod[l], 3, axis=-1)
        if ctx is None:
            proj, ba, *caches = inproj(x2, w["norm_g"], scale, shift, w["w1"], w["w2"], rows_per_mod,
                                       kv=(l, t, caches))
            h0 = jnp.zeros((2, 2, nb, S5_GROUPS * S5_STATE), F32)
            ctx_kv, s0 = None, None
        else:
            proj, ba = inproj(x2, w["norm_g"], scale, shift, w["w1"], w["w2"], rows_per_mod)
            cache_k, cache_v, st_re, st_im, st_dn = ctx
            h0 = jnp.stack([st_re[:, l], st_im[:, l]], 0)
            h0 = jnp.transpose(h0, (2, 0, 1, 3, 4)).reshape(2, 2, nb, S5_GROUPS * S5_STATE)
            ctx_kv, s0 = (cache_k, cache_v), st_dn
        proj3 = proj.reshape(nb, t, N_MAIN)
        out_a, hfin = s5_branch(proj, nb, t, w["s5_mats"], w["s5_d"], w["w_glu"], h0)
        out_b = diff_attention(proj3, l, lam_inits[l], w["da_lam"], w["da_norm_g"], ctx_kv)
        out_c, sfin = deltanet(proj3, ba.reshape(nb, t, 128), l, w["dn_conv"], w["dn_a_log"],
                               w["dn_dt_bias"], w["dn_norm_g"], s0)
        final = l == DEPTH - 1
        res = merge(out_a, out_b.reshape(m, DA_WIDTH), out_c.reshape(m, DN_WIDTH), proj,
                    w["w_branch"], w["w_out"], x2, gate, final_g, rows_per_mod, final)
        x2 = res[0]
        if final:
            y = res[1]
        if ctx is None:
            hf = hfin.reshape(2, 2, nb, S5_GROUPS, S5_STATE)
            states.append((jnp.transpose(hf[:, 0], (1, 0, 2, 3)), jnp.transpose(hf[:, 1], (1, 0, 2, 3)), sfin))
    return y.reshape(nb, t, D_MODEL), states, caches


def kernel(x_prompt, x_sample, cache_k, cache_v, state_s5_re, state_s5_im, state_dn, c, c_ctx,
           norm_g, w_ada, b_ada, w_in, s5_lam_re, s5_lam_im, s5_log_step, s5_b_re, s5_b_im,
           s5_c_re, s5_c_im, s5_d, s5_w_glu, da_lam, da_norm_g, dn_conv, dn_a_log, dn_dt_bias,
           dn_norm_g, w_branch, w_out, final_norm_g):
    nb_dec = x_sample.shape[0]
    cond8 = jnp.concatenate([c_ctx[None, :], c, jnp.zeros((8 - 1 - nb_dec, D_MODEL), F32)], 0)
    mod = ada_mod(cond8, w_ada, b_ada)
    w1 = jnp.concatenate([w_in[:, :, :BA_OFF], w_in[:, :, GATES_OFF:]], -1).astype(BF16)
    w2 = jnp.pad(w_in[:, :, BA_OFF:GATES_OFF], ((0, 0), (0, 0), (0, 128 - (GATES_OFF - BA_OFF)))).astype(BF16)
    wts = []
    for l in range(DEPTH):
        wts.append(dict(
            norm_g=norm_g[l], w1=w1[l], w2=w2[l],
            s5_mats=s5_matrices(s5_lam_re[l], s5_lam_im[l], s5_log_step[l], s5_b_re[l], s5_b_im[l],
                                s5_c_re[l], s5_c_im[l]),
            s5_d=s5_d[l], w_glu=s5_w_glu[l].astype(BF16), da_lam=da_lam[l], da_norm_g=da_norm_g[l],
            dn_conv=dn_conv[l], dn_a_log=dn_a_log[l], dn_dt_bias=dn_dt_bias[l], dn_norm_g=dn_norm_g[l],
            w_branch=w_branch[l].astype(BF16), w_out=w_out[l].astype(BF16)))
    lam_inits = [0.8 - 0.6 * math.exp(-0.3 * l) for l in range(DEPTH)]

    y_prompt, states, (k_new, v_new) = _run_pass(x_prompt, mod[:, 0:1], wts, lam_inits, final_norm_g, None)
    ctx = (cache_k.reshape(nb_dec, DEPTH * PAST_LEN, DA_WIDTH),
           cache_v.reshape(nb_dec, DEPTH * PAST_LEN, DA_WIDTH), state_s5_re, state_s5_im, state_dn)
    y_sample, _, _ = _run_pass(x_sample, mod[:, 1:1 + nb_dec], wts, lam_inits, final_norm_g, ctx)

    nb, t = x_prompt.shape[:2]
    new_cache_k = k_new.reshape(nb, DEPTH, t, DA_HEADS, 2, DA_HEAD)
    new_cache_v = v_new.reshape(nb, DEPTH, t, DA_HEADS, DA_VDIM)
    new_s5_re = jnp.stack([s[0] for s in states], axis=1)
    new_s5_im = jnp.stack([s[1] for s in states], axis=1)
    new_dn = jnp.stack([s[2] for s in states], axis=1)
    return (y_prompt, y_sample, new_cache_k, new_cache_v, new_s5_re, new_s5_im, new_dn)
```

```python
import functools
import math

import numpy as np
import jax
import jax.numpy as jnp
from jax import lax
from jax.experimental import pallas as pl
from jax.experimental.pallas import tpu as pltpu

F32 = jnp.float32
BF16 = jnp.bfloat16

D_MODEL = 1024
DEPTH = 2
GRID_W = 64
EPS = 1e-6
S5_WIDTH = 512
S5_GROUP = 16
S5_GROUPS = 32
S5_STATE = 64
S5_CHUNK = 16
S5_PAIRS = S5_GROUPS // 2
S5_ROW = S5_CHUNK * S5_GROUP
DA_HEADS = 4
DA_HEAD = 64
DA_VDIM = 128
DA_WIDTH = 512
ROPE_BASE = 10000.0
DN_HEADS = 4
DN_HEAD = 128
DN_WIDTH = 512
DN_CONV = 5
DN_CHUNK = 64
N_BRANCH = 3
BRANCH_WIDTH = 512
PAST_LEN = 512

COL_UA, COL_ZA, COL_QB, COL_KB, COL_VB, COL_ZB = 0, 512, 1024, 1536, 2048, 2560
COL_QC, COL_ZC = 3072, 4608
N_MAIN = 5120
BA_OFF = 5120
GATES_OFF = 5136

VMEM_LIMIT = 56 * 1024 * 1024
HI = lax.Precision.HIGHEST


def _cparams(sem):
    return pltpu.CompilerParams(dimension_semantics=sem, vmem_limit_bytes=VMEM_LIMIT)


def _mm(a, b):
    return jnp.dot(a.astype(BF16), b.astype(BF16), preferred_element_type=F32)


def _mm_nt(a, b):
    return lax.dot_general(a.astype(BF16), b.astype(BF16), (((1,), (1,)), ((), ())),
                           preferred_element_type=F32)


def _silu(x):
    return x * jax.nn.sigmoid(x)


def _ada_kernel(c_ref, w_ref, b_ref, o_ref):
    o_ref[0] = _mm(_silu(c_ref[...]), w_ref[0]) + b_ref[0]


def ada_mod(cond8, w_ada, b_ada):
    tn = 1024
    return pl.pallas_call(
        _ada_kernel,
        grid=(DEPTH, 3 * D_MODEL // tn),
        in_specs=[pl.BlockSpec((8, D_MODEL), lambda l, j: (0, 0)),
                  pl.BlockSpec((1, D_MODEL, tn), lambda l, j: (l, 0, j)),
                  pl.BlockSpec((1, 1, tn), lambda l, j: (l, 0, j))],
        out_specs=pl.BlockSpec((1, 8, tn), lambda l, j: (l, 0, j)),
        out_shape=jax.ShapeDtypeStruct((DEPTH, 8, 3 * D_MODEL), F32),
        compiler_params=_cparams(("parallel", "parallel")),
        name="ada_mod",
    )(cond8, w_ada, b_ada.reshape(DEPTH, 1, 3 * D_MODEL))


def _inproj_kernel(*refs, n_alias, with_kv):
    x_ref, g_ref, sc_ref, sh_ref, w1_ref, w2_ref = refs[:6]
    outs = refs[6 + n_alias:]
    proj_ref, ba_ref, hn_ref = outs[:3]
    j = pl.program_id(1)

    @pl.when(j == 0)
    def _():
        x = x_ref[...]
        y = x * lax.rsqrt(jnp.mean(x * x, axis=-1, keepdims=True) + EPS) * g_ref[...]
        hn = (y * (1.0 + sc_ref[0]) + sh_ref[0]).astype(BF16)
        hn_ref[...] = hn
        ba_ref[...] = jnp.dot(hn, w2_ref[...], preferred_element_type=F32)

    proj_ref[...] = jnp.dot(hn_ref[...], w1_ref[...], preferred_element_type=F32)

    if with_kv:
        tn = proj_ref.shape[1]
        for ref, col in ((outs[3], COL_KB), (outs[4], COL_VB)):
            @pl.when(j == col // tn)
            def _(ref=ref, col=col):
                ref[...] = proj_ref[:, col % tn:col % tn + DA_WIDTH].reshape(ref.shape)


def inproj(x2, norm_g, scale, shift, w1, w2, rows_per_mod, kv=None):
    m = x2.shape[0]
    tm, tn = 1024, 1024
    nmod = scale.shape[0]
    mod_idx = lambda i, j: ((i * tm) // rows_per_mod, 0, 0)
    in_specs = [pl.BlockSpec((tm, D_MODEL), lambda i, j: (i, 0)),
                pl.BlockSpec((1, D_MODEL), lambda i, j: (0, 0)),
                pl.BlockSpec((1, 1, D_MODEL), mod_idx),
                pl.BlockSpec((1, 1, D_MODEL), mod_idx),
                pl.BlockSpec((D_MODEL, tn), lambda i, j: (0, j)),
                pl.BlockSpec((D_MODEL, 128), lambda i, j: (0, 0))]
    args = [x2, norm_g.reshape(1, D_MODEL), scale.reshape(nmod, 1, D_MODEL),
            shift.reshape(nmod, 1, D_MODEL), w1, w2]
    out_specs = [pl.BlockSpec((tm, tn), lambda i, j: (i, j)),
                 pl.BlockSpec((tm, 128), lambda i, j: (i, 0)),
                 pl.BlockSpec((tm, D_MODEL), lambda i, j: (i, 0))]
    out_shape = [jax.ShapeDtypeStruct((m, N_MAIN), F32),
                 jax.ShapeDtypeStruct((m, 128), F32),
                 jax.ShapeDtypeStruct((m, D_MODEL), BF16)]
    aliases = {}
    if kv is not None:
        layer, t, caches = kv
        cspec = pl.BlockSpec((tm // t, 1, t, DA_WIDTH), lambda i, j: (i, layer, 0, 0))
        out_specs += [cspec, cspec]
        out_shape += [jax.ShapeDtypeStruct((m // t, DEPTH, t, DA_WIDTH), F32)] * 2
        if caches is not None:
            in_specs += [pl.BlockSpec(memory_space=pl.ANY)] * 2
            args += list(caches)
            aliases = {6: 3, 7: 4}
    return pl.pallas_call(
        functools.partial(_inproj_kernel, n_alias=len(aliases), with_kv=kv is not None),
        grid=(m // tm, N_MAIN // tn),
        in_specs=in_specs,
        out_specs=out_specs,
        out_shape=out_shape,
        input_output_aliases=aliases,
        compiler_params=_cparams(("parallel", "arbitrary")),
        name="inproj",
    )(*args)


def _s5_gen_kernel(crt_ref, cit_ref, prt_ref, pit_ref, bbt_ref, bbs_ref, prow_ref, pirow_ref,
                   wt_ref, ws_ref, wh_ref):
    L, C, P = S5_CHUNK, S5_GROUP, S5_STATE
    strips = []
    for d in range(2):
        crt, cit = crt_ref[d, 0], cit_ref[d, 0]
        blocks = []
        for k in (range(L + 1) if d == 0 else range(L, -1, -1)):
            prk, pik = prt_ref[d, 0, :, k:k + 1], pit_ref[d, 0, :, k:k + 1]
            blocks.append(jnp.concatenate([crt * prk - cit * pik, -(crt * pik + cit * prk)], axis=0))
        ca = jnp.concatenate(blocks, axis=-1)
        wh_ref[d, 0] = (ca[:, C:] if d == 0 else ca[:, :L * C]).astype(BF16)
        bbt, bbs = bbt_ref[d, 0], bbs_ref[d, 0]
        strips.append(jnp.dot(bbt, ca[:, :L * C] if d == 0 else ca[:, C:], precision=HI,
                              preferred_element_type=F32))
        rows = []
        for i in range(L):
            k = L - 1 - i if d == 0 else i
            rows.append(bbt * prow_ref[d, 0, k:k + 1, :] + bbs * pirow_ref[d, 0, k:k + 1, :])
        ws_ref[d, 0] = jnp.concatenate(rows, axis=0).astype(BF16)
    zeros = jnp.zeros((C, L * C), F32)
    fpad = jnp.concatenate([zeros, strips[0]], axis=-1)
    rpad = jnp.concatenate([strips[1], zeros], axis=-1)
    rows = []
    for i in range(L):
        rows.append(fpad[:, (L - i) * C:(2 * L - i) * C] + rpad[:, (L - 1 - i) * C:(2 * L - 1 - i) * C])
    wt_ref[0] = jnp.concatenate(rows, axis=0).astype(BF16)


def s5_matrices(lam_re, lam_im, log_step, b_re, b_im, c_re, c_im):
    L, G, P, C = S5_CHUNK, S5_GROUPS, S5_STATE, S5_GROUP
    step = jnp.exp(log_step)[..., None]
    mag = jnp.exp(lam_re * step)
    ar, ai = mag * jnp.cos(lam_im * step), mag * jnp.sin(lam_im * step)
    den = lam_re * lam_re + lam_im * lam_im
    fr = ((ar - 1.0) * lam_re + ai * lam_im) / den
    fi = (ai * lam_re - (ar - 1.0) * lam_im) / den
    bbr = fr[..., None] * b_re - fi[..., None] * b_im
    bbi = fr[..., None] * b_im + fi[..., None] * b_re
    ks = jnp.arange(L + 1, dtype=F32)[None, None, None, :]
    pmag = jnp.exp(ks * (lam_re * step)[..., None])
    prt = pmag * jnp.cos(ks * (lam_im * step)[..., None])
    pit = pmag * jnp.sin(ks * (lam_im * step)[..., None])
    prow = jnp.swapaxes(prt, 2, 3)
    pirow = jnp.swapaxes(pit, 2, 3)
    bbrt, bbit = jnp.swapaxes(bbr, 2, 3), jnp.swapaxes(bbi, 2, 3)
    args = [jnp.swapaxes(c_re, 2, 3), jnp.swapaxes(c_im, 2, 3), prt, pit,
            jnp.concatenate([bbrt, bbit], -1), jnp.concatenate([bbit, bbrt], -1),
            jnp.concatenate([prow, prow], -1), jnp.concatenate([-pirow, pirow], -1)]
    spec = lambda a: pl.BlockSpec((2, 1) + a.shape[2:], lambda g: (0, g, 0, 0))
    wt, ws, wh = pl.pallas_call(
        _s5_gen_kernel,
        grid=(G,),
        in_specs=[spec(a) for a in args],
        out_specs=[pl.BlockSpec((1, S5_ROW, S5_ROW), lambda g: (g, 0, 0)),
                   pl.BlockSpec((2, 1, S5_ROW, 2 * P), lambda g: (0, g, 0, 0)),
                   pl.BlockSpec((2, 1, 2 * P, S5_ROW), lambda g: (0, g, 0, 0))],
        out_shape=[jax.ShapeDtypeStruct((G, S5_ROW, S5_ROW), BF16),
                   jax.ShapeDtypeStruct((2, G, S5_ROW, 2 * P), BF16),
                   jax.ShapeDtypeStruct((2, G, 2 * P, S5_ROW), BF16)],
        compiler_params=_cparams(("parallel",)),
        name="s5_gen",
    )(*args)
    a_l = jnp.stack([prt[..., L].reshape(2, 1, G * P), pit[..., L].reshape(2, 1, G * P)], 1)
    return wt, ws, wh, a_l


S5_GB = 8


def _s5_core_kernel(u_ref, ws_ref, wt_ref, wh_ref, a_ref, h0_ref, y_ref, hfin_ref, x_ref, s_ref, hin_ref,
                    *, nb, nchunk):
    r = nb * nchunk
    npair = S5_GB // 2
    us = [u_ref[pl.ds(i, r, stride=S5_CHUNK), :] for i in range(S5_CHUNK)]
    for g in range(S5_GB):
        xg = jnp.concatenate([u[:, S5_GROUP * g:S5_GROUP * (g + 1)] for u in us], axis=-1)
        x_ref[:, g * S5_ROW:(g + 1) * S5_ROW] = xg.astype(BF16)
    for p in range(npair):
        for d in range(2):
            sg = [jnp.dot(x_ref[:, (2 * p + k) * S5_ROW:(2 * p + k + 1) * S5_ROW], ws_ref[d, 2 * p + k],
                          preferred_element_type=F32) for k in range(2)]
            for comp in range(2):
                cs = slice(comp * S5_STATE, (comp + 1) * S5_STATE)
                s_ref[d, comp, p] = jnp.concatenate([sg[0][:, cs], sg[1][:, cs]], axis=-1)
    chains = [(d, p) for d in range(2) for p in range(npair)]
    h = {}
    for d, p in chains:
        cols = slice(p * 128, (p + 1) * 128)
        h[d, p] = (h0_ref[d, 0, :, cols], h0_ref[d, 1, :, cols], a_ref[d, 0, :, cols], a_ref[d, 1, :, cols])
    for step in range(nchunk):
        for d, p in chains:
            rows = pl.ds(step if d == 0 else nchunk - 1 - step, nb, stride=nchunk)
            hr, hi, ar, ai = h[d, p]
            hin_ref[d, 0, p, rows, :] = hr
            hin_ref[d, 1, p, rows, :] = hi
            sr, si = s_ref[d, 0, p, rows, :], s_ref[d, 1, p, rows, :]
            h[d, p] = (ar * hr - ai * hi + sr, ar * hi + ai * hr + si, ar, ai)
    for d, p in chains:
        cols = slice(p * 128, (p + 1) * 128)
        hfin_ref[d, 0, :, cols] = h[d, p][0]
        hfin_ref[d, 1, :, cols] = h[d, p][1]
    ys = []
    for g in range(S5_GB):
        y = jnp.dot(x_ref[:, g * S5_ROW:(g + 1) * S5_ROW], wt_ref[g], preferred_element_type=F32)
        cs = slice((g % 2) * S5_STATE, (g % 2 + 1) * S5_STATE)
        for d in range(2):
            hg = jnp.concatenate([hin_ref[d, 0, g // 2, :, cs], hin_ref[d, 1, g // 2, :, cs]], axis=-1)
            y = y + jnp.dot(hg.astype(BF16), wh_ref[d, g], preferred_element_type=F32)
        ys.append(y)
    for j in range(S5_CHUNK):
        y_ref[pl.ds(j, r, stride=S5_CHUNK), :] = jnp.concatenate(
            [y[:, S5_GROUP * j:S5_GROUP * (j + 1)] for y in ys], axis=-1)


def s5_core(proj, mats, h0, nb, nchunk):
    wt, ws, wh, a_l = mats
    r = nb * nchunk
    sw = S5_GB * S5_STATE
    return pl.pallas_call(
        functools.partial(_s5_core_kernel, nb=nb, nchunk=nchunk),
        grid=(S5_GROUPS // S5_GB,),
        in_specs=[pl.BlockSpec((r * S5_CHUNK, 128), lambda j: (0, COL_UA // 128 + j)),
                  pl.BlockSpec((2, S5_GB, S5_ROW, 2 * S5_STATE), lambda j: (0, j, 0, 0)),
                  pl.BlockSpec((S5_GB, S5_ROW, S5_ROW), lambda j: (j, 0, 0)),
                  pl.BlockSpec((2, S5_GB, 2 * S5_STATE, S5_ROW), lambda j: (0, j, 0, 0)),
                  pl.BlockSpec((2, 2, 1, sw), lambda j: (0, 0, 0, j)),
                  pl.BlockSpec((2, 2, nb, sw), lambda j: (0, 0, 0, j))],
        out_specs=[pl.BlockSpec((r * S5_CHUNK, 128), lambda j: (0, j)),
                   pl.BlockSpec((2, 2, nb, sw), lambda j: (0, 0, 0, j))],
        out_shape=[jax.ShapeDtypeStruct((r * S5_CHUNK, S5_WIDTH), F32),
                   jax.ShapeDtypeStruct((2, 2, nb, S5_GROUPS * S5_STATE), F32)],
        scratch_shapes=[pltpu.VMEM((r, S5_GB * S5_ROW), BF16),
                        pltpu.VMEM((2, 2, S5_GB // 2, r, 128), F32),
                        pltpu.VMEM((2, 2, S5_GB // 2, r, 128), F32)],
        compiler_params=_cparams(("parallel",)),
        name="s5_core",
    )(proj, ws, wt, wh, a_l, h0)


def _s5_epilogue_kernel(u_ref, y_ref, z_ref, d_ref, w_ref, o_ref):
    ya = jax.nn.gelu(d_ref[...] * u_ref[...] + y_ref[...])
    ya = ya * jax.nn.sigmoid(_mm(ya, w_ref[...]))
    o_ref[...] = (ya * _silu(z_ref[...])).astype(BF16)


def s5_epilogue(proj, y_s5, s5_d, w_glu):
    m = proj.shape[0]
    tm = 512
    return pl.pallas_call(
        _s5_epilogue_kernel,
        grid=(m // tm,),
        in_specs=[pl.BlockSpec((tm, S5_WIDTH), lambda i: (i, COL_UA // S5_WIDTH)),
                  pl.BlockSpec((tm, S5_WIDTH), lambda i: (i, 0)),
                  pl.BlockSpec((tm, S5_WIDTH), lambda i: (i, COL_ZA // S5_WIDTH)),
                  pl.BlockSpec((1, S5_WIDTH), lambda i: (0, 0)),
                  pl.BlockSpec((S5_WIDTH, S5_WIDTH), lambda i: (0, 0))],
        out_specs=pl.BlockSpec((tm, S5_WIDTH), lambda i: (i, 0)),
        out_shape=jax.ShapeDtypeStruct((m, S5_WIDTH), BF16),
        compiler_params=_cparams(("parallel",)),
        name="s5_epilogue",
    )(proj, y_s5, proj, s5_d.reshape(1, S5_WIDTH), w_glu)


def s5_branch(proj, nb, t, mats, s5_d, w_glu, h0):
    y, hfin = s5_core(proj, mats, h0, nb, t // S5_CHUNK)
    return s5_epilogue(proj, y, s5_d, w_glu), hfin


def _rope_tables(t):
    rows = t // GRID_W
    row = np.repeat(np.arange(rows), GRID_W).astype(np.float32)
    col = np.tile(np.arange(GRID_W), rows).astype(np.float32)
    nf = DA_HEAD // 4
    inv = (ROPE_BASE ** (-jnp.arange(nf, dtype=F32) / nf))

    def tab(pos):
        ang = jnp.asarray(pos)[:, None] * inv[None, :]
        c, s = jnp.cos(ang), jnp.sin(ang)
        return jnp.concatenate([c, c], -1), jnp.concatenate([-s, s], -1)

    cr, sr = tab(row)
    cc, sc = tab(col)
    cos = jnp.concatenate([cr, cc], -1)
    sin = jnp.concatenate([sr, sc], -1)
    return jnp.tile(cos, (1, 2)), jnp.tile(sin, (1, 2))


def _rope(x, cos, sin):
    lane = lax.broadcasted_iota(jnp.int32, x.shape, 1)
    swapped = jnp.where((lane % 32) < 16, pltpu.roll(x, 112, 1), pltpu.roll(x, 16, 1))
    return x * cos + swapped * sin


def _attn_kernel(*refs, lam_init, t, s_tot, tq, with_ctx):
    if with_ctx:
        (q_ref, k_ref, v_ref, z_ref, kc_ref, vc_ref, cq_ref, sq_ref, ck_ref, sk_ref,
         lam_ref, ng_ref, o_ref, kall_ref, vall_ref) = refs
    else:
        q_ref, k_ref, v_ref, z_ref, lam_ref, ng_ref, o_ref, kall_ref, vall_ref = refs

    @pl.when(pl.program_id(1) == 0)
    def _():
        if with_ctx:
            for h in range(DA_HEADS):
                hs = slice(h * 128, (h + 1) * 128)
                kall_ref[0:t, hs] = _rope(k_ref[0, :, hs], ck_ref[...], sk_ref[...]).astype(BF16)
            kall_ref[t:s_tot, :] = kc_ref[0].astype(BF16)
            vall_ref[0:t, :] = v_ref[0].astype(BF16)
            vall_ref[t:s_tot, :] = vc_ref[0].astype(BF16)
        else:
            kall_ref[...] = k_ref[0].astype(BF16)
            vall_ref[...] = v_ref[0].astype(BF16)

    lp = lam_ref[...]
    lam = (jnp.exp(jnp.sum(lp[0:1] * lp[1:2], axis=-1, keepdims=True))
           - jnp.exp(jnp.sum(lp[2:3] * lp[3:4], axis=-1, keepdims=True)) + lam_init)
    lane = lax.broadcasted_iota(jnp.int32, (tq, 128), 1)
    for h in range(DA_HEADS):
        hs = slice(h * 128, (h + 1) * 128)
        q = q_ref[0, :, hs]
        if with_ctx:
            q = _rope(q, cq_ref[...], sq_ref[...])
        q = q * (DA_HEAD ** -0.5)
        k = kall_ref[:, hs]
        probs = []
        for m in range(2):
            qm = jnp.where((lane < DA_HEAD) if m == 0 else (lane >= DA_HEAD), q, 0.0)
            sc = _mm_nt(qm, k)
            sc = sc - jnp.max(sc, axis=-1, keepdims=True)
            e = jnp.exp(sc)
            probs.append(e / jnp.sum(e, axis=-1, keepdims=True))
        a = probs[0] - lam * probs[1]
        o = _mm(a, vall_ref[:, hs])
        o = o * lax.rsqrt(jnp.mean(o * o, axis=-1, keepdims=True) + EPS) * ng_ref[...]
        o = o * (1.0 - lam_init)
        o_ref[0, :, hs] = (o * _silu(z_ref[0, :, hs])).astype(BF16)


def diff_attention(proj3, layer, lam_init, da_lam, da_norm_g, ctx_kv):
    nb, t, _ = proj3.shape
    with_ctx = ctx_kv is not None
    s_tot = t + (PAST_LEN if with_ctx else 0)
    tq = 256
    wb = DA_WIDTH
    in_specs = [pl.BlockSpec((1, tq, wb), lambda b, i: (b, i, COL_QB // wb)),
                pl.BlockSpec((1, t, wb), lambda b, i: (b, 0, COL_KB // wb)),
                pl.BlockSpec((1, t, wb), lambda b, i: (b, 0, COL_VB // wb)),
                pl.BlockSpec((1, tq, wb), lambda b, i: (b, i, COL_ZB // wb))]
    args = [proj3, proj3, proj3, proj3]
    if with_ctx:
        kc, vc = ctx_kv
        cos, sin = _rope_tables(t)
        in_specs += [pl.BlockSpec((1, PAST_LEN, wb), lambda b, i: (b, layer, 0)),
                     pl.BlockSpec((1, PAST_LEN, wb), lambda b, i: (b, layer, 0)),
                     pl.BlockSpec((tq, 128), lambda b, i: (i, 0)),
                     pl.BlockSpec((tq, 128), lambda b, i: (i, 0)),
                     pl.BlockSpec((t, 128), lambda b, i: (0, 0)),
                     pl.BlockSpec((t, 128), lambda b, i: (0, 0))]
        args += [kc, vc, cos, sin, cos, sin]
    in_specs += [pl.BlockSpec((4, DA_HEAD), lambda b, i: (0, 0)),
                 pl.BlockSpec((1, DA_VDIM), lambda b, i: (0, 0))]
    args += [da_lam, da_norm_g.reshape(1, DA_VDIM)]
    return pl.pallas_call(
        functools.partial(_attn_kernel, lam_init=lam_init, t=t, s_tot=s_tot, tq=tq, with_ctx=with_ctx),
        grid=(nb, t // tq),
        in_specs=in_specs,
        out_specs=pl.BlockSpec((1, tq, wb), lambda b, i: (b, i, 0)),
        out_shape=jax.ShapeDtypeStruct((nb, t, wb), BF16),
        scratch_shapes=[pltpu.VMEM((s_tot, wb), BF16), pltpu.VMEM((s_tot, wb), BF16)],
        compiler_params=_cparams(("parallel", "arbitrary")),
        name="diff_attention",
    )(*args)


DN_PAD = 8
DN_RT = 128
DN_GROUP = 128


def _dn_kernel(*refs, t, ngroup, with_s0):
    if with_s0:
        (qkv_ref, z_ref, ba_ref, cw_ref, alog_ref, dtb_ref, ng_ref, s0_ref,
         out_ref, sfin_ref, xp_ref, qkvn_ref, oacc_ref, st_ref) = refs
    else:
        (qkv_ref, z_ref, ba_ref, cw_ref, alog_ref, dtb_ref, ng_ref,
         out_ref, sfin_ref, xp_ref, qkvn_ref, oacc_ref, st_ref) = refs
    n = pl.program_id(1)
    cd = DN_CHUNK
    w3 = 3 * DN_WIDTH

    @pl.when(n == 0)
    def _init():
        xp_ref[0:DN_PAD, :] = jnp.zeros((DN_PAD, w3), F32)
        xp_ref[DN_PAD + t:2 * DN_PAD + t, :] = jnp.zeros((DN_PAD, w3), F32)
        xp_ref[DN_PAD:DN_PAD + t, :] = qkv_ref[0]
        half = DN_CONV // 2
        for r in range(t // DN_RT):
            for sec in range(3):
                for h in range(DN_HEADS):
                    cs = slice(sec * DN_WIDTH + h * DN_HEAD, sec * DN_WIDTH + (h + 1) * DN_HEAD)
                    acc = jnp.zeros((DN_RT, DN_HEAD), F32)
                    for j in range(DN_CONV):
                        r0 = DN_PAD + r * DN_RT + j - half
                        acc = acc + xp_ref[r0:r0 + DN_RT, cs] * cw_ref[j:j + 1, cs]
                    y = _silu(acc)
                    if sec < 2:
                        y = y * lax.rsqrt(jnp.sum(y * y, axis=-1, keepdims=True) + EPS)
                    if sec == 0:
                        y = y * (DN_HEAD ** -0.5)
                    qkvn_ref[r * DN_RT:(r + 1) * DN_RT, cs] = y
        oacc_ref[...] = jnp.zeros_like(oacc_ref)
        if with_s0:
            st_ref[...] = s0_ref[0, 0]
        else:
            st_ref[...] = jnp.zeros_like(st_ref)

    gb = DN_GROUP
    nsub = gb // cd
    ri = lax.broadcasted_iota(jnp.int32, (gb, gb), 0)
    ci = lax.broadcasted_iota(jnp.int32, (gb, gb), 1)
    same = (ri // cd) == (ci // cd)
    samef = same.astype(F32)
    eye = (ri == ci).astype(BF16)
    masked_out = -1e30

    chains = []
    for d in range(2):
        grp = n if d == 0 else ngroup - 1 - n
        r0 = pl.multiple_of(grp * gb, gb)
        incl = same & ((ci <= ri) if d == 0 else (ci >= ri))
        strict = same & ((ci < ri) if d == 0 else (ci > ri))
        inclf = incl.astype(F32)
        inclog = jnp.where(incl, 0.0, masked_out)
        nstrict = -strict.astype(BF16)
        ba = ba_ref[0, pl.ds(r0, gb), :]
        beta_all = jax.nn.sigmoid(ba)
        g_all = -jnp.exp(alog_ref[...]) * jax.nn.softplus(ba + dtb_ref[...])
        gc = jnp.dot(inclf, g_all, precision=HI, preferred_element_type=F32)
        gct = lax.dot_general(g_all.T, inclf, (((1,), (1,)), ((), ())), precision=HI,
                              preferred_element_type=F32)
        gtot = jnp.dot(samef, g_all, precision=HI, preferred_element_type=F32)
        for h in range(DN_HEADS):
            hs = slice(h * DN_HEAD, (h + 1) * DN_HEAD)
            col = 2 * DN_HEADS + d * DN_HEADS + h
            gcol = gc[:, col:col + 1]
            grow = gct[col:col + 1, :]
            gt = gtot[:, col:col + 1]
            beta = beta_all[:, d * DN_HEADS + h:d * DN_HEADS + h + 1]
            q = qkvn_ref[pl.ds(r0, gb), hs]
            k = qkvn_ref[pl.ds(r0, gb), DN_WIDTH + h * DN_HEAD:DN_WIDTH + (h + 1) * DN_HEAD]
            v = qkvn_ref[pl.ds(r0, gb), 2 * DN_WIDTH + h * DN_HEAD:2 * DN_WIDTH + (h + 1) * DN_HEAD]
            eg = jnp.exp(gcol)
            chains.append(dict(
                d=d, h=h, r0=r0, hs=hs, nstrict=nstrict, q=q, k=k, kb=k * beta,
                decay=jnp.exp((gcol - grow) + inclog),
                rhs=jnp.concatenate([v * beta, k * beta * eg], axis=-1),
                qe=q * eg, kdec=k * jnp.exp(gt - gcol), egt=jnp.exp(gt), st=st_ref[d, h]))

    for c in chains:
        c["nmm"] = (_mm_nt(c["kb"], c["k"]) * c["decay"]).astype(BF16) * c["nstrict"]
    for c in chains:
        c["qk"] = _mm_nt(c["q"], c["k"]) * c["decay"]
    def level_mask(s):
        return (((ri // (2 * s)) == (ci // (2 * s))) & ((ri // s) != (ci // s))).astype(BF16)

    pair = level_mask(1)
    for c in chains:
        c["tinv"] = eye + c["nmm"] * pair
    s = 2
    while s < cd:
        offmask = level_mask(s)
        xs = [jnp.dot(c["nmm"] * offmask, c["tinv"], preferred_element_type=F32).astype(BF16) for c in chains]
        ys = [jnp.dot(c["tinv"], x, preferred_element_type=F32) for c, x in zip(chains, xs)]
        for c, y in zip(chains, ys):
            c["tinv"] = c["tinv"] + y.astype(BF16)
        s *= 2
    for c in chains:
        c["uw"] = _mm(c["tinv"], c["rhs"])

    outs = []
    for step in range(nsub):
        rs = []
        for c in chains:
            sub = step if c["d"] == 0 else nsub - 1 - step
            c["rows"] = slice(sub * cd, (sub + 1) * cd)
            rs.append(_mm(jnp.concatenate([c["uw"][c["rows"], DN_HEAD:], c["qe"][c["rows"]]], axis=0), c["st"]))
        for c, r in zip(chains, rs):
            rows = c["rows"]
            v_new = c["uw"][rows, :DN_HEAD] - r[:cd]
            o = r[cd:] + _mm(c["qk"][rows, rows], v_new)
            c["st"] = c["st"] * c["egt"][rows.start:rows.start + 1] + _mm(c["kdec"][rows].T, v_new)
            outs.append((c, rows.start, o))
    for c in chains:
        st_ref[c["d"], c["h"]] = c["st"]
    for c, off, o in outs:
        oacc_ref[pl.ds(pl.multiple_of(c["r0"] + off, cd), cd), c["hs"]] += o

    @pl.when(n == ngroup - 1)
    def _fin():
        for h in range(DN_HEADS):
            hs = slice(h * DN_HEAD, (h + 1) * DN_HEAD)
            o = oacc_ref[:, hs]
            o = o * lax.rsqrt(jnp.mean(o * o, axis=-1, keepdims=True) + EPS) * ng_ref[...]
            out_ref[0, :, hs] = (o * _silu(z_ref[0, :, hs])).astype(BF16)
        sfin_ref[0] = st_ref[...]


def deltanet(proj3, ba3, layer, conv_w, a_log, dt_bias, norm_g, s0):
    nb, t, _ = proj3.shape
    ngroup = t // DN_GROUP
    with_s0 = s0 is not None
    w3 = 3 * DN_WIDTH
    pad = jnp.zeros((2 * DN_HEADS,), F32)
    alog_row = jnp.concatenate([pad, a_log.reshape(-1), jnp.zeros((128 - 4 * DN_HEADS,), F32)]).reshape(1, 128)
    dtb_row = jnp.concatenate([pad, dt_bias.reshape(-1), jnp.zeros((128 - 4 * DN_HEADS,), F32)]).reshape(1, 128)
    in_specs = [pl.BlockSpec((1, t, w3), lambda b, n: (b, 0, COL_QC // w3)),
                pl.BlockSpec((1, t, DN_WIDTH), lambda b, n: (b, 0, COL_ZC // DN_WIDTH)),
                pl.BlockSpec((1, t, 128), lambda b, n: (b, 0, 0)),
                pl.BlockSpec((8, w3), lambda b, n: (0, 0)),
                pl.BlockSpec((1, 128), lambda b, n: (0, 0)),
                pl.BlockSpec((1, 128), lambda b, n: (0, 0)),
                pl.BlockSpec((1, DN_HEAD), lambda b, n: (0, 0))]
    args = [proj3, proj3, ba3, jnp.pad(conv_w, ((0, 8 - DN_CONV), (0, 0))), alog_row, dtb_row,
            norm_g.reshape(1, DN_HEAD)]
    if with_s0:
        in_specs.append(pl.BlockSpec((1, 1, 2, DN_HEADS, DN_HEAD, DN_HEAD), lambda b, n: (b, layer, 0, 0, 0, 0)))
        args.append(s0)
    return pl.pallas_call(
        functools.partial(_dn_kernel, t=t, ngroup=ngroup, with_s0=with_s0),
        grid=(nb, ngroup),
        in_specs=in_specs,
        out_specs=[pl.BlockSpec((1, t, DN_WIDTH), lambda b, n: (b, 0, 0)),
                   pl.BlockSpec((1, 2, DN_HEADS, DN_HEAD, DN_HEAD), lambda b, n: (b, 0, 0, 0, 0))],
        out_shape=[jax.ShapeDtypeStruct((nb, t, DN_WIDTH), BF16),
                   jax.ShapeDtypeStruct((nb, 2, DN_HEADS, DN_HEAD, DN_HEAD), F32)],
        scratch_shapes=[pltpu.VMEM((t + 2 * DN_PAD, w3), F32),
                        pltpu.VMEM((t, w3), F32),
                        pltpu.VMEM((t, DN_WIDTH), F32),
                        pltpu.VMEM((2, DN_HEADS, DN_HEAD, DN_HEAD), F32)],
        compiler_params=_cparams(("parallel", "arbitrary")),
        name="deltanet",
    )(*args)


def _merge_kernel(oa_ref, ob_ref, oc_ref, hn_ref, wg_ref, wb_ref, wo_ref, x_ref, gate_ref, fg_ref,
                  *outs, final):
    acc = None
    hn = hn_ref[...]
    for i, o_ref in enumerate((oa_ref, ob_ref, oc_ref)):
        pr = jnp.dot(o_ref[...], wb_ref[i], preferred_element_type=F32)
        gt = jnp.dot(hn, wg_ref[:, i * D_MODEL:(i + 1) * D_MODEL], preferred_element_type=F32)
        term = jax.nn.sigmoid(gt) * pr
        acc = term if acc is None else acc + term
    y = jnp.dot(acc.astype(BF16), wo_ref[...], preferred_element_type=F32)
    xn = x_ref[...] + gate_ref[0] * y
    outs[0][...] = xn
    if final:
        yn = xn * lax.rsqrt(jnp.mean(xn * xn, axis=-1, keepdims=True) + EPS) * fg_ref[...]
        outs[1][...] = yn


def merge(out_a, out_b, out_c, hn, w_gates, w_branch, w_out, x2, gate, final_g, rows_per_mod, final):
    m = x2.shape[0]
    tm = 512
    nmod = gate.shape[0]
    row = lambda i: (i, 0)
    out_specs = [pl.BlockSpec((tm, D_MODEL), row)]
    out_shape = [jax.ShapeDtypeStruct((m, D_MODEL), F32)]
    if final:
        out_specs.append(pl.BlockSpec((tm, D_MODEL), row))
        out_shape.append(jax.ShapeDtypeStruct((m, D_MODEL), F32))
    return pl.pallas_call(
        functools.partial(_merge_kernel, final=final),
        grid=(m // tm,),
        in_specs=[pl.BlockSpec((tm, BRANCH_WIDTH), row),
                  pl.BlockSpec((tm, BRANCH_WIDTH), row),
                  pl.BlockSpec((tm, BRANCH_WIDTH), row),
                  pl.BlockSpec((tm, D_MODEL), row),
                  pl.BlockSpec((D_MODEL, N_BRANCH * D_MODEL), lambda i: (0, 0)),
                  pl.BlockSpec((N_BRANCH, BRANCH_WIDTH, D_MODEL), lambda i: (0, 0, 0)),
                  pl.BlockSpec((D_MODEL, D_MODEL), lambda i: (0, 0)),
                  pl.BlockSpec((tm, D_MODEL), row),
                  pl.BlockSpec((1, 1, D_MODEL), lambda i: ((i * tm) // rows_per_mod, 0, 0)),
                  pl.BlockSpec((1, D_MODEL), lambda i: (0, 0))],
        out_specs=out_specs,
        out_shape=out_shape,
        compiler_params=_cparams(("parallel",)),
        name="merge",
    )(out_a, out_b, out_c, hn, w_gates, w_branch, w_out, x2, gate.reshape(nmod, 1, D_MODEL),
      final_g.reshape(1, D_MODEL))


def _run_pass(x, mod, wts, lam_inits, final_g, ctx):
    nb, t, _ = x.shape
    m = nb * t
    nmod = mod.shape[1]
    rows_per_mod = m if nmod == 1 else t
    x2 = x.reshape(m, D_MODEL)
    states = []
    y, caches = None, None
    for l in range(DEPTH):
        w = wts[l]
        shift, scale, gate = jnp.split(mod[l], 3, axis=-1)
        if ctx is None:
            proj, ba, hn, *caches = inproj(x2, w["norm_g"], scale, shift, w["w1"], w["w2"], rows_per_mod,
                                       kv=(l, t, caches))
            h0 = jnp.zeros((2, 2, nb, S5_GROUPS * S5_STATE), F32)
            ctx_kv, s0 = None, None
        else:
            proj, ba, hn = inproj(x2, w["norm_g"], scale, shift, w["w1"], w["w2"], rows_per_mod)
            cache_k, cache_v, st_re, st_im, st_dn = ctx
            h0 = jnp.stack([st_re[:, l], st_im[:, l]], 0)
            h0 = jnp.transpose(h0, (2, 0, 1, 3, 4)).reshape(2, 2, nb, S5_GROUPS * S5_STATE)
            ctx_kv, s0 = (cache_k, cache_v), st_dn
        proj3 = proj.reshape(nb, t, N_MAIN)
        out_a, hfin = s5_branch(proj, nb, t, w["s5_mats"], w["s5_d"], w["w_glu"], h0)
        out_b = diff_attention(proj3, l, lam_inits[l], w["da_lam"], w["da_norm_g"], ctx_kv)
        out_c, sfin = deltanet(proj3, ba.reshape(nb, t, 128), l, w["dn_conv"], w["dn_a_log"],
                               w["dn_dt_bias"], w["dn_norm_g"], s0)
        final = l == DEPTH - 1
        res = merge(out_a, out_b.reshape(m, DA_WIDTH), out_c.reshape(m, DN_WIDTH), hn, w["w_gates"],
                    w["w_branch"], w["w_out"], x2, gate, final_g, rows_per_mod, final)
        x2 = res[0]
        if final:
            y = res[1]
        if ctx is None:
            hf = hfin.reshape(2, 2, nb, S5_GROUPS, S5_STATE)
            states.append((jnp.transpose(hf[:, 0], (1, 0, 2, 3)), jnp.transpose(hf[:, 1], (1, 0, 2, 3)), sfin))
    return y.reshape(nb, t, D_MODEL), states, caches


def kernel(x_prompt, x_sample, cache_k, cache_v, state_s5_re, state_s5_im, state_dn, c, c_ctx,
           norm_g, w_ada, b_ada, w_in, s5_lam_re, s5_lam_im, s5_log_step, s5_b_re, s5_b_im,
           s5_c_re, s5_c_im, s5_d, s5_w_glu, da_lam, da_norm_g, dn_conv, dn_a_log, dn_dt_bias,
           dn_norm_g, w_branch, w_out, final_norm_g):
    nb_dec = x_sample.shape[0]
    cond8 = jnp.concatenate([c_ctx[None, :], c, jnp.zeros((8 - 1 - nb_dec, D_MODEL), F32)], 0)
    mod = ada_mod(cond8, w_ada, b_ada)
    w1 = w_in[:, :, :BA_OFF].astype(BF16)
    w_gates = w_in[:, :, GATES_OFF:].astype(BF16)
    w2 = jnp.pad(w_in[:, :, BA_OFF:GATES_OFF], ((0, 0), (0, 0), (0, 128 - (GATES_OFF - BA_OFF)))).astype(BF16)
    wts = []
    for l in range(DEPTH):
        wts.append(dict(
            norm_g=norm_g[l], w1=w1[l], w2=w2[l], w_gates=w_gates[l],
            s5_mats=s5_matrices(s5_lam_re[l], s5_lam_im[l], s5_log_step[l], s5_b_re[l], s5_b_im[l],
                                s5_c_re[l], s5_c_im[l]),
            s5_d=s5_d[l], w_glu=s5_w_glu[l].astype(BF16), da_lam=da_lam[l], da_norm_g=da_norm_g[l],
            dn_conv=dn_conv[l], dn_a_log=dn_a_log[l], dn_dt_bias=dn_dt_bias[l], dn_norm_g=dn_norm_g[l],
            w_branch=w_branch[l].astype(BF16), w_out=w_out[l].astype(BF16)))
    lam_inits = [0.8 - 0.6 * math.exp(-0.3 * l) for l in range(DEPTH)]

    y_prompt, states, (k_new, v_new) = _run_pass(x_prompt, mod[:, 0:1], wts, lam_inits, final_norm_g, None)
    ctx = (cache_k.reshape(nb_dec, DEPTH * PAST_LEN, DA_WIDTH),
           cache_v.reshape(nb_dec, DEPTH * PAST_LEN, DA_WIDTH), state_s5_re, state_s5_im, state_dn)
    y_sample, _, _ = _run_pass(x_sample, mod[:, 1:1 + nb_dec], wts, lam_inits, final_norm_g, ctx)

    nb, t = x_prompt.shape[:2]
    new_cache_k = k_new.reshape(nb, DEPTH, t, DA_HEADS, 2, DA_HEAD)
    new_cache_v = v_new.reshape(nb, DEPTH, t, DA_HEADS, DA_VDIM)
    new_s5_re = jnp.stack([s[0] for s in states], axis=1)
    new_s5_im = jnp.stack([s[1] for s in states], axis=1)
    new_dn = jnp.stack([s[2] for s in states], axis=1)
    return (y_prompt, y_sample, new_cache_k, new_cache_v, new_s5_re, new_s5_im, new_dn)
```

```python
import functools
import math

import numpy as np
import jax
import jax.numpy as jnp
from jax import lax
from jax.experimental import pallas as pl
from jax.experimental.pallas import tpu as pltpu

F32 = jnp.float32
BF16 = jnp.bfloat16

D_MODEL = 1024
DEPTH = 2
GRID_W = 64
EPS = 1e-6
S5_WIDTH = 512
S5_GROUP = 16
S5_GROUPS = 32
S5_STATE = 64
S5_CHUNK = 16
S5_PAIRS = S5_GROUPS // 2
S5_ROW = S5_CHUNK * S5_GROUP
S5_GEN_GROUPS = 4
DA_HEADS = 4
DA_HEAD = 64
DA_VDIM = 128
DA_WIDTH = 512
ROPE_BASE = 10000.0
DN_HEADS = 4
DN_HEAD = 128
DN_WIDTH = 512
DN_CONV = 5
DN_CHUNK = 64
N_BRANCH = 3
BRANCH_WIDTH = 512
PAST_LEN = 512

COL_UA, COL_ZA, COL_QB, COL_KB, COL_VB, COL_ZB = 0, 512, 1024, 1536, 2048, 2560
COL_QC, COL_ZC = 3072, 4608
N_MAIN = 5120
BA_OFF = 5120
GATES_OFF = 5136

VMEM_LIMIT = 56 * 1024 * 1024
HI = lax.Precision.HIGHEST
LOG2E = math.log2(math.e)
ATTN_ROWS = 16


def _cparams(sem):
    return pltpu.CompilerParams(dimension_semantics=sem, vmem_limit_bytes=VMEM_LIMIT)


def _mm(a, b):
    return jnp.dot(a.astype(BF16), b.astype(BF16), preferred_element_type=F32)


def _mm_nt(a, b):
    return lax.dot_general(a.astype(BF16), b.astype(BF16), (((1,), (1,)), ((), ())),
                           preferred_element_type=F32)


def _silu(x):
    return x * jax.nn.sigmoid(x)


def _ada_kernel(c_ref, w_ref, b_ref, o_ref):
    o_ref[0] = _mm(_silu(c_ref[...]), w_ref[0]) + b_ref[0]


def ada_mod(cond8, w_ada, b_ada):
    tn = 1024
    return pl.pallas_call(
        _ada_kernel,
        grid=(DEPTH, 3 * D_MODEL // tn),
        in_specs=[pl.BlockSpec((8, D_MODEL), lambda l, j: (0, 0)),
                  pl.BlockSpec((1, D_MODEL, tn), lambda l, j: (l, 0, j)),
                  pl.BlockSpec((1, 1, tn), lambda l, j: (l, 0, j))],
        out_specs=pl.BlockSpec((1, 8, tn), lambda l, j: (l, 0, j)),
        out_shape=jax.ShapeDtypeStruct((DEPTH, 8, 3 * D_MODEL), F32),
        compiler_params=_cparams(("parallel", "parallel")),
        name="ada_mod",
    )(cond8, w_ada, b_ada.reshape(DEPTH, 1, 3 * D_MODEL))


def _inproj_kernel(*refs, n_alias, with_kv):
    x_ref, g_ref, sc_ref, sh_ref, w1_ref, w2_ref = refs[:6]
    outs = refs[6 + n_alias:]
    proj_ref, ba_ref, hn_ref = outs[:3]
    j = pl.program_id(1)

    @pl.when(j == 0)
    def _():
        x = x_ref[...]
        y = x * lax.rsqrt(jnp.mean(x * x, axis=-1, keepdims=True) + EPS) * g_ref[...]
        hn = (y * (1.0 + sc_ref[0]) + sh_ref[0]).astype(BF16)
        hn_ref[...] = hn
        ba_ref[...] = jnp.dot(hn, w2_ref[...], preferred_element_type=F32)

    proj_ref[...] = jnp.dot(hn_ref[...], w1_ref[...], preferred_element_type=F32)

    if with_kv:
        tn = proj_ref.shape[1]
        for ref, col in ((outs[3], COL_KB), (outs[4], COL_VB)):
            @pl.when(j == col // tn)
            def _(ref=ref, col=col):
                ref[...] = proj_ref[:, col % tn:col % tn + DA_WIDTH].reshape(ref.shape)


def inproj(x2, norm_g, scale, shift, w1, w2, rows_per_mod, kv=None):
    m = x2.shape[0]
    tm, tn = 1024, 1024
    nmod = scale.shape[0]
    mod_idx = lambda i, j: ((i * tm) // rows_per_mod, 0, 0)
    in_specs = [pl.BlockSpec((tm, D_MODEL), lambda i, j: (i, 0)),
                pl.BlockSpec((1, D_MODEL), lambda i, j: (0, 0)),
                pl.BlockSpec((1, 1, D_MODEL), mod_idx),
                pl.BlockSpec((1, 1, D_MODEL), mod_idx),
                pl.BlockSpec((D_MODEL, tn), lambda i, j: (0, j)),
                pl.BlockSpec((D_MODEL, 128), lambda i, j: (0, 0))]
    args = [x2, norm_g.reshape(1, D_MODEL), scale.reshape(nmod, 1, D_MODEL),
            shift.reshape(nmod, 1, D_MODEL), w1, w2]
    out_specs = [pl.BlockSpec((tm, tn), lambda i, j: (i, j)),
                 pl.BlockSpec((tm, 128), lambda i, j: (i, 0)),
                 pl.BlockSpec((tm, D_MODEL), lambda i, j: (i, 0))]
    out_shape = [jax.ShapeDtypeStruct((m, N_MAIN), F32),
                 jax.ShapeDtypeStruct((m, 128), F32),
                 jax.ShapeDtypeStruct((m, D_MODEL), BF16)]
    aliases = {}
    if kv is not None:
        layer, t, caches = kv
        cspec = pl.BlockSpec((tm // t, 1, t, DA_WIDTH), lambda i, j: (i, layer, 0, 0))
        out_specs += [cspec, cspec]
        out_shape += [jax.ShapeDtypeStruct((m // t, DEPTH, t, DA_WIDTH), F32)] * 2
        if caches is not None:
            in_specs += [pl.BlockSpec(memory_space=pl.ANY)] * 2
            args += list(caches)
            aliases = {6: 3, 7: 4}
    return pl.pallas_call(
        functools.partial(_inproj_kernel, n_alias=len(aliases), with_kv=kv is not None),
        grid=(m // tm, N_MAIN // tn),
        in_specs=in_specs,
        out_specs=out_specs,
        out_shape=out_shape,
        input_output_aliases=aliases,
        compiler_params=_cparams(("parallel", "arbitrary")),
        name="inproj",
    )(*args)


def _s5_gen_kernel(crt_ref, cit_ref, prt_ref, pit_ref, bbt_ref, bbs_ref, prow_ref, pirow_ref,
                   wt_ref, ws_ref, wh_ref):
    L, C, P = S5_CHUNK, S5_GROUP, S5_STATE
    width = (L + 1) * C
    row = lax.broadcasted_iota(jnp.int32, (128, width), 0)
    lane = lax.broadcasted_iota(jnp.int32, (128, width), 1)
    tile_c = (lane % C == row).astype(F32)
    expand = lambda a, e: jnp.dot(a, e, precision=HI, preferred_element_type=F32)
    zeros = jnp.zeros((C, L * C), F32)
    for g in range(S5_GEN_GROUPS):
        strips = []
        for d in range(2):
            spread_k = ((lane // C if d == 0 else L - lane // C) == row).astype(F32)
            crx, cix = expand(crt_ref[d, g], tile_c), expand(cit_ref[d, g], tile_c)
            prx, pix = expand(prt_ref[d, g], spread_k), expand(pit_ref[d, g], spread_k)
            ca = jnp.concatenate([crx * prx - cix * pix, -(crx * pix + cix * prx)], axis=0)
            wh_ref[d, g] = (ca[:, C:] if d == 0 else ca[:, :L * C]).astype(BF16)
            bbt, bbs = bbt_ref[d, g], bbs_ref[d, g]
            strips.append(jnp.dot(bbt, ca[:, :L * C] if d == 0 else ca[:, C:], precision=HI,
                                  preferred_element_type=F32))
            rows = []
            for i in range(L):
                k = L - 1 - i if d == 0 else i
                rows.append(bbt * prow_ref[d, g, k:k + 1, :] + bbs * pirow_ref[d, g, k:k + 1, :])
            ws_ref[d, g] = jnp.concatenate(rows, axis=0).astype(BF16)
        fpad = jnp.concatenate([zeros, strips[0]], axis=-1)
        rpad = jnp.concatenate([strips[1], zeros], axis=-1)
        rows = []
        for i in range(L):
            rows.append(fpad[:, (L - i) * C:(2 * L - i) * C] + rpad[:, (L - 1 - i) * C:(2 * L - 1 - i) * C])
        wt_ref[g] = jnp.concatenate(rows, axis=0).astype(BF16)


def s5_matrices(lam_re, lam_im, log_step, b_re, b_im, c_re, c_im):
    L, G, P, C = S5_CHUNK, S5_GROUPS, S5_STATE, S5_GROUP
    step = jnp.exp(log_step)[..., None]
    mag = jnp.exp(lam_re * step)
    ar, ai = mag * jnp.cos(lam_im * step), mag * jnp.sin(lam_im * step)
    den = lam_re * lam_re + lam_im * lam_im
    fr = ((ar - 1.0) * lam_re + ai * lam_im) / den
    fi = (ai * lam_re - (ar - 1.0) * lam_im) / den
    bbr = fr[..., None] * b_re - fi[..., None] * b_im
    bbi = fr[..., None] * b_im + fi[..., None] * b_re
    ks = jnp.arange(L + 1, dtype=F32)[None, None, None, :]
    pmag = jnp.exp(ks * (lam_re * step)[..., None])
    prt = pmag * jnp.cos(ks * (lam_im * step)[..., None])
    pit = pmag * jnp.sin(ks * (lam_im * step)[..., None])
    prow = jnp.swapaxes(prt, 2, 3)
    pirow = jnp.swapaxes(pit, 2, 3)
    bbrt, bbit = jnp.swapaxes(bbr, 2, 3), jnp.swapaxes(bbi, 2, 3)
    lanes = lambda a: jnp.pad(a, ((0, 0), (0, 0), (0, 0), (0, 128 - a.shape[-1])))
    args = [lanes(jnp.swapaxes(c_re, 2, 3)), lanes(jnp.swapaxes(c_im, 2, 3)), lanes(prt), lanes(pit),
            jnp.concatenate([bbrt, bbit], -1), jnp.concatenate([bbit, bbrt], -1),
            jnp.concatenate([prow, prow], -1), jnp.concatenate([-pirow, pirow], -1)]
    gg = S5_GEN_GROUPS
    spec = lambda a: pl.BlockSpec((2, gg) + a.shape[2:], lambda g: (0, g, 0, 0))
    wt, ws, wh = pl.pallas_call(
        _s5_gen_kernel,
        grid=(G // gg,),
        in_specs=[spec(a) for a in args],
        out_specs=[pl.BlockSpec((gg, S5_ROW, S5_ROW), lambda g: (g, 0, 0)),
                   pl.BlockSpec((2, gg, S5_ROW, 2 * P), lambda g: (0, g, 0, 0)),
                   pl.BlockSpec((2, gg, 2 * P, S5_ROW), lambda g: (0, g, 0, 0))],
        out_shape=[jax.ShapeDtypeStruct((G, S5_ROW, S5_ROW), BF16),
                   jax.ShapeDtypeStruct((2, G, S5_ROW, 2 * P), BF16),
                   jax.ShapeDtypeStruct((2, G, 2 * P, S5_ROW), BF16)],
        compiler_params=_cparams(("parallel",)),
        name="s5_gen",
    )(*args)
    a_l = jnp.stack([prt[..., L].reshape(2, 1, G * P), pit[..., L].reshape(2, 1, G * P)], 1)
    return wt, ws, wh, a_l


S5_GB = 8


def _s5_core_kernel(u_ref, ws_ref, wt_ref, wh_ref, a_ref, h0_ref, y_ref, hfin_ref, x_ref, s_ref, hin_ref,
                    *, nb, nchunk):
    r = nb * nchunk
    npair = S5_GB // 2
    us = [u_ref[pl.ds(i, r, stride=S5_CHUNK), :] for i in range(S5_CHUNK)]
    for g in range(S5_GB):
        xg = jnp.concatenate([u[:, S5_GROUP * g:S5_GROUP * (g + 1)] for u in us], axis=-1)
        x_ref[:, g * S5_ROW:(g + 1) * S5_ROW] = xg.astype(BF16)
    for p in range(npair):
        for d in range(2):
            sg = [jnp.dot(x_ref[:, (2 * p + k) * S5_ROW:(2 * p + k + 1) * S5_ROW], ws_ref[d, 2 * p + k],
                          preferred_element_type=F32) for k in range(2)]
            for comp in range(2):
                cs = slice(comp * S5_STATE, (comp + 1) * S5_STATE)
                s_ref[d, comp, p] = jnp.concatenate([sg[0][:, cs], sg[1][:, cs]], axis=-1)
    chains = [(d, p) for d in range(2) for p in range(npair)]
    h = {}
    for d, p in chains:
        cols = slice(p * 128, (p + 1) * 128)
        h[d, p] = (h0_ref[d, 0, :, cols], h0_ref[d, 1, :, cols], a_ref[d, 0, :, cols], a_ref[d, 1, :, cols])
    for step in range(nchunk):
        for d, p in chains:
            rows = pl.ds(step if d == 0 else nchunk - 1 - step, nb, stride=nchunk)
            hr, hi, ar, ai = h[d, p]
            hin_ref[d, 0, p, rows, :] = hr
            hin_ref[d, 1, p, rows, :] = hi
            sr, si = s_ref[d, 0, p, rows, :], s_ref[d, 1, p, rows, :]
            h[d, p] = (ar * hr - ai * hi + sr, ar * hi + ai * hr + si, ar, ai)
    for d, p in chains:
        cols = slice(p * 128, (p + 1) * 128)
        hfin_ref[d, 0, :, cols] = h[d, p][0]
        hfin_ref[d, 1, :, cols] = h[d, p][1]
    ys = []
    for g in range(S5_GB):
        y = jnp.dot(x_ref[:, g * S5_ROW:(g + 1) * S5_ROW], wt_ref[g], preferred_element_type=F32)
        cs = slice((g % 2) * S5_STATE, (g % 2 + 1) * S5_STATE)
        for d in range(2):
            hg = jnp.concatenate([hin_ref[d, 0, g // 2, :, cs], hin_ref[d, 1, g // 2, :, cs]], axis=-1)
            y = y + jnp.dot(hg.astype(BF16), wh_ref[d, g], preferred_element_type=F32)
        ys.append(y)
    for j in range(S5_CHUNK):
        y_ref[pl.ds(j, r, stride=S5_CHUNK), :] = jnp.concatenate(
            [y[:, S5_GROUP * j:S5_GROUP * (j + 1)] for y in ys], axis=-1)


def s5_core(proj, mats, h0, nb, nchunk):
    wt, ws, wh, a_l = mats
    r = nb * nchunk
    sw = S5_GB * S5_STATE
    return pl.pallas_call(
        functools.partial(_s5_core_kernel, nb=nb, nchunk=nchunk),
        grid=(S5_GROUPS // S5_GB,),
        in_specs=[pl.BlockSpec((r * S5_CHUNK, 128), lambda j: (0, COL_UA // 128 + j)),
                  pl.BlockSpec((2, S5_GB, S5_ROW, 2 * S5_STATE), lambda j: (0, j, 0, 0)),
                  pl.BlockSpec((S5_GB, S5_ROW, S5_ROW), lambda j: (j, 0, 0)),
                  pl.BlockSpec((2, S5_GB, 2 * S5_STATE, S5_ROW), lambda j: (0, j, 0, 0)),
                  pl.BlockSpec((2, 2, 1, sw), lambda j: (0, 0, 0, j)),
                  pl.BlockSpec((2, 2, nb, sw), lambda j: (0, 0, 0, j))],
        out_specs=[pl.BlockSpec((r * S5_CHUNK, 128), lambda j: (0, j)),
                   pl.BlockSpec((2, 2, nb, sw), lambda j: (0, 0, 0, j))],
        out_shape=[jax.ShapeDtypeStruct((r * S5_CHUNK, S5_WIDTH), F32),
                   jax.ShapeDtypeStruct((2, 2, nb, S5_GROUPS * S5_STATE), F32)],
        scratch_shapes=[pltpu.VMEM((r, S5_GB * S5_ROW), BF16),
                        pltpu.VMEM((2, 2, S5_GB // 2, r, 128), F32),
                        pltpu.VMEM((2, 2, S5_GB // 2, r, 128), F32)],
        compiler_params=_cparams(("parallel",)),
        name="s5_core",
    )(proj, ws, wt, wh, a_l, h0)


def _s5_epilogue_kernel(u_ref, y_ref, z_ref, d_ref, w_ref, o_ref):
    ya = jax.nn.gelu(d_ref[...] * u_ref[...] + y_ref[...])
    ya = ya * jax.nn.sigmoid(_mm(ya, w_ref[...]))
    o_ref[...] = (ya * _silu(z_ref[...])).astype(BF16)


def s5_epilogue(proj, y_s5, s5_d, w_glu):
    m = proj.shape[0]
    tm = 512
    return pl.pallas_call(
        _s5_epilogue_kernel,
        grid=(m // tm,),
        in_specs=[pl.BlockSpec((tm, S5_WIDTH), lambda i: (i, COL_UA // S5_WIDTH)),
                  pl.BlockSpec((tm, S5_WIDTH), lambda i: (i, 0)),
                  pl.BlockSpec((tm, S5_WIDTH), lambda i: (i, COL_ZA // S5_WIDTH)),
                  pl.BlockSpec((1, S5_WIDTH), lambda i: (0, 0)),
                  pl.BlockSpec((S5_WIDTH, S5_WIDTH), lambda i: (0, 0))],
        out_specs=pl.BlockSpec((tm, S5_WIDTH), lambda i: (i, 0)),
        out_shape=jax.ShapeDtypeStruct((m, S5_WIDTH), BF16),
        compiler_params=_cparams(("parallel",)),
        name="s5_epilogue",
    )(proj, y_s5, proj, s5_d.reshape(1, S5_WIDTH), w_glu)


def s5_branch(proj, nb, t, mats, s5_d, w_glu, h0):
    y, hfin = s5_core(proj, mats, h0, nb, t // S5_CHUNK)
    return s5_epilogue(proj, y, s5_d, w_glu), hfin


def _rope_tables(t):
    rows = t // GRID_W
    row = np.repeat(np.arange(rows), GRID_W).astype(np.float32)
    col = np.tile(np.arange(GRID_W), rows).astype(np.float32)
    nf = DA_HEAD // 4
    inv = (ROPE_BASE ** (-jnp.arange(nf, dtype=F32) / nf))

    def tab(pos):
        ang = jnp.asarray(pos)[:, None] * inv[None, :]
        c, s = jnp.cos(ang), jnp.sin(ang)
        return jnp.concatenate([c, c], -1), jnp.concatenate([-s, s], -1)

    cr, sr = tab(row)
    cc, sc = tab(col)
    cos = jnp.concatenate([cr, cc], -1)
    sin = jnp.concatenate([sr, sc], -1)
    return jnp.tile(cos, (1, 2)), jnp.tile(sin, (1, 2))


def _rope(x, cos, sin):
    lane = lax.broadcasted_iota(jnp.int32, x.shape, 1)
    swapped = jnp.where((lane % 32) < 16, pltpu.roll(x, 112, 1), pltpu.roll(x, 16, 1))
    return x * cos + swapped * sin


def _attn_kernel(*refs, lam_init, t, s_tot, tq, with_ctx):
    if with_ctx:
        (q_ref, k_ref, v_ref, z_ref, kc_ref, vc_ref, cq_ref, sq_ref, ck_ref, sk_ref,
         lam_ref, ng_ref, o_ref, kall_ref, vall_ref, s_ref, e_ref, l_ref) = refs
    else:
        q_ref, k_ref, v_ref, z_ref, lam_ref, ng_ref, o_ref, kall_ref, vall_ref, s_ref, e_ref, l_ref = refs

    @pl.when(pl.program_id(1) == 0)
    def _():
        if with_ctx:
            for h in range(DA_HEADS):
                hs = slice(h * 128, (h + 1) * 128)
                kall_ref[0:t, hs] = _rope(k_ref[0, :, hs], ck_ref[...], sk_ref[...]).astype(BF16)
            kall_ref[t:s_tot, :] = kc_ref[0].astype(BF16)
            vall_ref[0:t, :] = v_ref[0].astype(BF16)
            vall_ref[t:s_tot, :] = vc_ref[0].astype(BF16)
        else:
            kall_ref[...] = k_ref[0].astype(BF16)
            vall_ref[...] = v_ref[0].astype(BF16)

    lp = lam_ref[...]
    lam = (jnp.exp(jnp.sum(lp[0:1] * lp[1:2], axis=-1, keepdims=True))
           - jnp.exp(jnp.sum(lp[2:3] * lp[3:4], axis=-1, keepdims=True)) + lam_init)
    lane = lax.broadcasted_iota(jnp.int32, (tq, 128), 1)
    for h in range(DA_HEADS):
        hs = slice(h * 128, (h + 1) * 128)
        q = q_ref[0, :, hs]
        if with_ctx:
            q = _rope(q, cq_ref[...], sq_ref[...])
        q = q * (DA_HEAD ** -0.5 * LOG2E)
        qs = jnp.concatenate([jnp.where(lane < DA_HEAD, q, 0.0), jnp.where(lane >= DA_HEAD, q, 0.0)], axis=0)
        s_ref[...] = _mm_nt(qs, kall_ref[:, hs])
        for r in range(0, 2 * tq, ATTN_ROWS):
            sc = s_ref[r:r + ATTN_ROWS, :]
            e = jnp.exp2(sc - jnp.max(sc, axis=-1, keepdims=True))
            e_ref[r:r + ATTN_ROWS, :] = e.astype(BF16)
            l_ref[r:r + ATTN_ROWS, :] = jnp.broadcast_to(jnp.sum(e, axis=-1, keepdims=True), (ATTN_ROWS, 128))
        ov = jnp.dot(e_ref[...], vall_ref[:, hs], preferred_element_type=F32) / l_ref[...]
        o = ov[:tq] - lam * ov[tq:]
        o = o * lax.rsqrt(jnp.mean(o * o, axis=-1, keepdims=True) + EPS) * ng_ref[...]
        o = o * (1.0 - lam_init)
        o_ref[0, :, hs] = (o * _silu(z_ref[0, :, hs])).astype(BF16)


def diff_attention(proj3, layer, lam_init, da_lam, da_norm_g, ctx_kv):
    nb, t, _ = proj3.shape
    with_ctx = ctx_kv is not None
    s_tot = t + (PAST_LEN if with_ctx else 0)
    tq = 256
    wb = DA_WIDTH
    in_specs = [pl.BlockSpec((1, tq, wb), lambda b, i: (b, i, COL_QB // wb)),
                pl.BlockSpec((1, t, wb), lambda b, i: (b, 0, COL_KB // wb)),
                pl.BlockSpec((1, t, wb), lambda b, i: (b, 0, COL_VB // wb)),
                pl.BlockSpec((1, tq, wb), lambda b, i: (b, i, COL_ZB // wb))]
    args = [proj3, proj3, proj3, proj3]
    if with_ctx:
        kc, vc = ctx_kv
        cos, sin = _rope_tables(t)
        in_specs += [pl.BlockSpec((1, PAST_LEN, wb), lambda b, i: (b, layer, 0)),
                     pl.BlockSpec((1, PAST_LEN, wb), lambda b, i: (b, layer, 0)),
                     pl.BlockSpec((tq, 128), lambda b, i: (i, 0)),
                     pl.BlockSpec((tq, 128), lambda b, i: (i, 0)),
                     pl.BlockSpec((t, 128), lambda b, i: (0, 0)),
                     pl.BlockSpec((t, 128), lambda b, i: (0, 0))]
        args += [kc, vc, cos, sin, cos, sin]
    in_specs += [pl.BlockSpec((4, DA_HEAD), lambda b, i: (0, 0)),
                 pl.BlockSpec((1, DA_VDIM), lambda b, i: (0, 0))]
    args += [da_lam, da_norm_g.reshape(1, DA_VDIM)]
    return pl.pallas_call(
        functools.partial(_attn_kernel, lam_init=lam_init, t=t, s_tot=s_tot, tq=tq, with_ctx=with_ctx),
        grid=(nb, t // tq),
        in_specs=in_specs,
        out_specs=pl.BlockSpec((1, tq, wb), lambda b, i: (b, i, 0)),
        out_shape=jax.ShapeDtypeStruct((nb, t, wb), BF16),
        scratch_shapes=[pltpu.VMEM((s_tot, wb), BF16), pltpu.VMEM((s_tot, wb), BF16),
                        pltpu.VMEM((2 * tq, s_tot), F32), pltpu.VMEM((2 * tq, s_tot), BF16),
                        pltpu.VMEM((2 * tq, 128), F32)],
        compiler_params=_cparams(("parallel", "arbitrary")),
        name="diff_attention",
    )(*args)


DN_PAD = 8
DN_RT = 128
DN_GROUP = 128


def _dn_kernel(*refs, t, ngroup, with_s0):
    if with_s0:
        (qkv_ref, z_ref, ba_ref, cw_ref, alog_ref, dtb_ref, ng_ref, s0_ref,
         out_ref, sfin_ref, xp_ref, qkvn_ref, oacc_ref, st_ref) = refs
    else:
        (qkv_ref, z_ref, ba_ref, cw_ref, alog_ref, dtb_ref, ng_ref,
         out_ref, sfin_ref, xp_ref, qkvn_ref, oacc_ref, st_ref) = refs
    n = pl.program_id(1)
    cd = DN_CHUNK
    w3 = 3 * DN_WIDTH

    @pl.when(n == 0)
    def _init():
        xp_ref[0:DN_PAD, :] = jnp.zeros((DN_PAD, w3), F32)
        xp_ref[DN_PAD + t:2 * DN_PAD + t, :] = jnp.zeros((DN_PAD, w3), F32)
        xp_ref[DN_PAD:DN_PAD + t, :] = qkv_ref[0]
        half = DN_CONV // 2
        for r in range(t // DN_RT):
            for sec in range(3):
                for h in range(DN_HEADS):
                    cs = slice(sec * DN_WIDTH + h * DN_HEAD, sec * DN_WIDTH + (h + 1) * DN_HEAD)
                    acc = jnp.zeros((DN_RT, DN_HEAD), F32)
                    for j in range(DN_CONV):
                        r0 = DN_PAD + r * DN_RT + j - half
                        acc = acc + xp_ref[r0:r0 + DN_RT, cs] * cw_ref[j:j + 1, cs]
                    y = _silu(acc)
                    if sec < 2:
                        y = y * lax.rsqrt(jnp.sum(y * y, axis=-1, keepdims=True) + EPS)
                    if sec == 0:
                        y = y * (DN_HEAD ** -0.5)
                    qkvn_ref[r * DN_RT:(r + 1) * DN_RT, cs] = y
        oacc_ref[...] = jnp.zeros_like(oacc_ref)
        if with_s0:
            st_ref[...] = s0_ref[0, 0]
        else:
            st_ref[...] = jnp.zeros_like(st_ref)

    gb = DN_GROUP
    nsub = gb // cd
    ri = lax.broadcasted_iota(jnp.int32, (gb, gb), 0)
    ci = lax.broadcasted_iota(jnp.int32, (gb, gb), 1)
    same = (ri // cd) == (ci // cd)
    samef = same.astype(F32)
    eye = (ri == ci).astype(BF16)
    masked_out = -1e30

    chains = []
    for d in range(2):
        grp = n if d == 0 else ngroup - 1 - n
        r0 = pl.multiple_of(grp * gb, gb)
        incl = same & ((ci <= ri) if d == 0 else (ci >= ri))
        strict = same & ((ci < ri) if d == 0 else (ci > ri))
        inclf = incl.astype(F32)
        inclog = jnp.where(incl, 0.0, masked_out)
        nstrict = -strict.astype(BF16)
        ba = ba_ref[0, pl.ds(r0, gb), :]
        beta_all = jax.nn.sigmoid(ba)
        g_all = -jnp.exp(alog_ref[...]) * jax.nn.softplus(ba + dtb_ref[...])
        gc = jnp.dot(inclf, g_all, precision=HI, preferred_element_type=F32)
        gct = lax.dot_general(g_all.T, inclf, (((1,), (1,)), ((), ())), precision=HI,
                              preferred_element_type=F32)
        gtot = jnp.dot(samef, g_all, precision=HI, preferred_element_type=F32)
        for h in range(DN_HEADS):
            hs = slice(h * DN_HEAD, (h + 1) * DN_HEAD)
            col = 2 * DN_HEADS + d * DN_HEADS + h
            gcol = gc[:, col:col + 1]
            grow = gct[col:col + 1, :]
            gt = gtot[:, col:col + 1]
            beta = beta_all[:, d * DN_HEADS + h:d * DN_HEADS + h + 1]
            q = qkvn_ref[pl.ds(r0, gb), hs]
            k = qkvn_ref[pl.ds(r0, gb), DN_WIDTH + h * DN_HEAD:DN_WIDTH + (h + 1) * DN_HEAD]
            v = qkvn_ref[pl.ds(r0, gb), 2 * DN_WIDTH + h * DN_HEAD:2 * DN_WIDTH + (h + 1) * DN_HEAD]
            eg = jnp.exp(gcol)
            chains.append(dict(
                d=d, h=h, r0=r0, hs=hs, nstrict=nstrict, q=q, k=k, kb=k * beta,
                decay=jnp.exp((gcol - grow) + inclog),
                rhs=jnp.concatenate([v * beta, k * beta * eg], axis=-1),
                qe=q * eg, kdec=k * jnp.exp(gt - gcol), egt=jnp.exp(gt), st=st_ref[d, h]))

    for c in chains:
        c["nmm"] = (_mm_nt(c["kb"], c["k"]) * c["decay"]).astype(BF16) * c["nstrict"]
    for c in chains:
        c["qk"] = _mm_nt(c["q"], c["k"]) * c["decay"]
    def level_mask(s):
        return (((ri // (2 * s)) == (ci // (2 * s))) & ((ri // s) != (ci // s))).astype(BF16)

    pair = level_mask(1)
    for c in chains:
        c["tinv"] = eye + c["nmm"] * pair
    s = 2
    while s < cd:
        offmask = level_mask(s)
        xs = [jnp.dot(c["nmm"] * offmask, c["tinv"], preferred_element_type=F32).astype(BF16) for c in chains]
        ys = [jnp.dot(c["tinv"], x, preferred_element_type=F32) for c, x in zip(chains, xs)]
        for c, y in zip(chains, ys):
            c["tinv"] = c["tinv"] + y.astype(BF16)
        s *= 2
    for c in chains:
        c["uw"] = _mm(c["tinv"], c["rhs"])

    outs = []
    for step in range(nsub):
        rs = []
        for c in chains:
            sub = step if c["d"] == 0 else nsub - 1 - step
            c["rows"] = slice(sub * cd, (sub + 1) * cd)
            rs.append(_mm(jnp.concatenate([c["uw"][c["rows"], DN_HEAD:], c["qe"][c["rows"]]], axis=0), c["st"]))
        for c, r in zip(chains, rs):
            rows = c["rows"]
            v_new = c["uw"][rows, :DN_HEAD] - r[:cd]
            o = r[cd:] + _mm(c["qk"][rows, rows], v_new)
            c["st"] = c["st"] * c["egt"][rows.start:rows.start + 1] + _mm(c["kdec"][rows].T, v_new)
            outs.append((c, rows.start, o))
    for c in chains:
        st_ref[c["d"], c["h"]] = c["st"]
    for c, off, o in outs:
        oacc_ref[pl.ds(pl.multiple_of(c["r0"] + off, cd), cd), c["hs"]] += o

    @pl.when(n == ngroup - 1)
    def _fin():
        for h in range(DN_HEADS):
            hs = slice(h * DN_HEAD, (h + 1) * DN_HEAD)
            o = oacc_ref[:, hs]
            o = o * lax.rsqrt(jnp.mean(o * o, axis=-1, keepdims=True) + EPS) * ng_ref[...]
            out_ref[0, :, hs] = (o * _silu(z_ref[0, :, hs])).astype(BF16)
        sfin_ref[0] = st_ref[...]


def deltanet(proj3, ba3, layer, conv_w, a_log, dt_bias, norm_g, s0):
    nb, t, _ = proj3.shape
    ngroup = t // DN_GROUP
    with_s0 = s0 is not None
    w3 = 3 * DN_WIDTH
    pad = jnp.zeros((2 * DN_HEADS,), F32)
    alog_row = jnp.concatenate([pad, a_log.reshape(-1), jnp.zeros((128 - 4 * DN_HEADS,), F32)]).reshape(1, 128)
    dtb_row = jnp.concatenate([pad, dt_bias.reshape(-1), jnp.zeros((128 - 4 * DN_HEADS,), F32)]).reshape(1, 128)
    in_specs = [pl.BlockSpec((1, t, w3), lambda b, n: (b, 0, COL_QC // w3)),
                pl.BlockSpec((1, t, DN_WIDTH), lambda b, n: (b, 0, COL_ZC // DN_WIDTH)),
                pl.BlockSpec((1, t, 128), lambda b, n: (b, 0, 0)),
                pl.BlockSpec((8, w3), lambda b, n: (0, 0)),
                pl.BlockSpec((1, 128), lambda b, n: (0, 0)),
                pl.BlockSpec((1, 128), lambda b, n: (0, 0)),
                pl.BlockSpec((1, DN_HEAD), lambda b, n: (0, 0))]
    args = [proj3, proj3, ba3, jnp.pad(conv_w, ((0, 8 - DN_CONV), (0, 0))), alog_row, dtb_row,
            norm_g.reshape(1, DN_HEAD)]
    if with_s0:
        in_specs.append(pl.BlockSpec((1, 1, 2, DN_HEADS, DN_HEAD, DN_HEAD), lambda b, n: (b, layer, 0, 0, 0, 0)))
        args.append(s0)
    return pl.pallas_call(
        functools.partial(_dn_kernel, t=t, ngroup=ngroup, with_s0=with_s0),
        grid=(nb, ngroup),
        in_specs=in_specs,
        out_specs=[pl.BlockSpec((1, t, DN_WIDTH), lambda b, n: (b, 0, 0)),
                   pl.BlockSpec((1, 2, DN_HEADS, DN_HEAD, DN_HEAD), lambda b, n: (b, 0, 0, 0, 0))],
        out_shape=[jax.ShapeDtypeStruct((nb, t, DN_WIDTH), BF16),
                   jax.ShapeDtypeStruct((nb, 2, DN_HEADS, DN_HEAD, DN_HEAD), F32)],
        scratch_shapes=[pltpu.VMEM((t + 2 * DN_PAD, w3), F32),
                        pltpu.VMEM((t, w3), F32),
                        pltpu.VMEM((t, DN_WIDTH), F32),
                        pltpu.VMEM((2, DN_HEADS, DN_HEAD, DN_HEAD), F32)],
        compiler_params=_cparams(("parallel", "arbitrary")),
        name="deltanet",
    )(*args)


def _merge_kernel(oa_ref, ob_ref, oc_ref, hn_ref, wg_ref, wb_ref, wo_ref, x_ref, gate_ref, fg_ref,
                  *outs, final):
    tm = x_ref.shape[0]
    for rows in (slice(0, tm // 2), slice(tm // 2, tm)):
        acc = None
        hn = hn_ref[rows, :]
        for i, o_ref in enumerate((oa_ref, ob_ref, oc_ref)):
            pr = jnp.dot(o_ref[rows, :], wb_ref[i], preferred_element_type=F32)
            gt = jnp.dot(hn, wg_ref[:, i * D_MODEL:(i + 1) * D_MODEL], preferred_element_type=F32)
            term = jax.nn.sigmoid(gt) * pr
            acc = term if acc is None else acc + term
        y = jnp.dot(acc.astype(BF16), wo_ref[...], preferred_element_type=F32)
        xn = x_ref[rows, :] + gate_ref[0] * y
        outs[0][rows, :] = xn
        if final:
            yn = xn * lax.rsqrt(jnp.mean(xn * xn, axis=-1, keepdims=True) + EPS) * fg_ref[...]
            outs[1][rows, :] = yn


def merge(out_a, out_b, out_c, hn, w_gates, w_branch, w_out, x2, gate, final_g, rows_per_mod, final):
    m = x2.shape[0]
    tm = 512
    nmod = gate.shape[0]
    row = lambda i: (i, 0)
    out_specs = [pl.BlockSpec((tm, D_MODEL), row)]
    out_shape = [jax.ShapeDtypeStruct((m, D_MODEL), F32)]
    if final:
        out_specs.append(pl.BlockSpec((tm, D_MODEL), row))
        out_shape.append(jax.ShapeDtypeStruct((m, D_MODEL), F32))
    return pl.pallas_call(
        functools.partial(_merge_kernel, final=final),
        grid=(m // tm,),
        in_specs=[pl.BlockSpec((tm, BRANCH_WIDTH), row),
                  pl.BlockSpec((tm, BRANCH_WIDTH), row),
                  pl.BlockSpec((tm, BRANCH_WIDTH), row),
                  pl.BlockSpec((tm, D_MODEL), row),
                  pl.BlockSpec((D_MODEL, N_BRANCH * D_MODEL), lambda i: (0, 0)),
                  pl.BlockSpec((N_BRANCH, BRANCH_WIDTH, D_MODEL), lambda i: (0, 0, 0)),
                  pl.BlockSpec((D_MODEL, D_MODEL), lambda i: (0, 0)),
                  pl.BlockSpec((tm, D_MODEL), row),
                  pl.BlockSpec((1, 1, D_MODEL), lambda i: ((i * tm) // rows_per_mod, 0, 0)),
                  pl.BlockSpec((1, D_MODEL), lambda i: (0, 0))],
        out_specs=out_specs,
        out_shape=out_shape,
        compiler_params=_cparams(("parallel",)),
        name="merge",
    )(out_a, out_b, out_c, hn, w_gates, w_branch, w_out, x2, gate.reshape(nmod, 1, D_MODEL),
      final_g.reshape(1, D_MODEL))


def _run_pass(x, mod, wts, lam_inits, final_g, ctx):
    nb, t, _ = x.shape
    m = nb * t
    nmod = mod.shape[1]
    rows_per_mod = m if nmod == 1 else t
    x2 = x.reshape(m, D_MODEL)
    states = []
    y, caches = None, None
    for l in range(DEPTH):
        w = wts[l]
        shift, scale, gate = jnp.split(mod[l], 3, axis=-1)
        if ctx is None:
            proj, ba, hn, *caches = inproj(x2, w["norm_g"], scale, shift, w["w1"], w["w2"], rows_per_mod,
                                       kv=(l, t, caches))
            h0 = jnp.zeros((2, 2, nb, S5_GROUPS * S5_STATE), F32)
            ctx_kv, s0 = None, None
        else:
            proj, ba, hn = inproj(x2, w["norm_g"], scale, shift, w["w1"], w["w2"], rows_per_mod)
            cache_k, cache_v, st_re, st_im, st_dn = ctx
            h0 = jnp.stack([st_re[:, l], st_im[:, l]], 0)
            h0 = jnp.transpose(h0, (2, 0, 1, 3, 4)).reshape(2, 2, nb, S5_GROUPS * S5_STATE)
            ctx_kv, s0 = (cache_k, cache_v), st_dn
        proj3 = proj.reshape(nb, t, N_MAIN)
        out_a, hfin = s5_branch(proj, nb, t, w["s5_mats"], w["s5_d"], w["w_glu"], h0)
        out_b = diff_attention(proj3, l, lam_inits[l], w["da_lam"], w["da_norm_g"], ctx_kv)
        out_c, sfin = deltanet(proj3, ba.reshape(nb, t, 128), l, w["dn_conv"], w["dn_a_log"],
                               w["dn_dt_bias"], w["dn_norm_g"], s0)
        final = l == DEPTH - 1
        res = merge(out_a, out_b.reshape(m, DA_WIDTH), out_c.reshape(m, DN_WIDTH), hn, w["w_gates"],
                    w["w_branch"], w["w_out"], x2, gate, final_g, rows_per_mod, final)
        x2 = res[0]
        if final:
            y = res[1]
        if ctx is None:
            hf = hfin.reshape(2, 2, nb, S5_GROUPS, S5_STATE)
            states.append((jnp.transpose(hf[:, 0], (1, 0, 2, 3)), jnp.transpose(hf[:, 1], (1, 0, 2, 3)), sfin))
    return y.reshape(nb, t, D_MODEL), states, caches


def kernel(x_prompt, x_sample, cache_k, cache_v, state_s5_re, state_s5_im, state_dn, c, c_ctx,
           norm_g, w_ada, b_ada, w_in, s5_lam_re, s5_lam_im, s5_log_step, s5_b_re, s5_b_im,
           s5_c_re, s5_c_im, s5_d, s5_w_glu, da_lam, da_norm_g, dn_conv, dn_a_log, dn_dt_bias,
           dn_norm_g, w_branch, w_out, final_norm_g):
    nb_dec = x_sample.shape[0]
    cond8 = jnp.concatenate([c_ctx[None, :], c, jnp.zeros((8 - 1 - nb_dec, D_MODEL), F32)], 0)
    mod = ada_mod(cond8, w_ada, b_ada)
    w1 = w_in[:, :, :BA_OFF].astype(BF16)
    w_gates = w_in[:, :, GATES_OFF:].astype(BF16)
    w2 = jnp.pad(w_in[:, :, BA_OFF:GATES_OFF], ((0, 0), (0, 0), (0, 128 - (GATES_OFF - BA_OFF)))).astype(BF16)
    wts = []
    for l in range(DEPTH):
        wts.append(dict(
            norm_g=norm_g[l], w1=w1[l], w2=w2[l], w_gates=w_gates[l],
            s5_mats=s5_matrices(s5_lam_re[l], s5_lam_im[l], s5_log_step[l], s5_b_re[l], s5_b_im[l],
                                s5_c_re[l], s5_c_im[l]),
            s5_d=s5_d[l], w_glu=s5_w_glu[l].astype(BF16), da_lam=da_lam[l], da_norm_g=da_norm_g[l],
            dn_conv=dn_conv[l], dn_a_log=dn_a_log[l], dn_dt_bias=dn_dt_bias[l], dn_norm_g=dn_norm_g[l],
            w_branch=w_branch[l].astype(BF16), w_out=w_out[l].astype(BF16)))
    lam_inits = [0.8 - 0.6 * math.exp(-0.3 * l) for l in range(DEPTH)]

    y_prompt, states, (k_new, v_new) = _run_pass(x_prompt, mod[:, 0:1], wts, lam_inits, final_norm_g, None)
    ctx = (cache_k.reshape(nb_dec, DEPTH * PAST_LEN, DA_WIDTH),
           cache_v.reshape(nb_dec, DEPTH * PAST_LEN, DA_WIDTH), state_s5_re, state_s5_im, state_dn)
    y_sample, _, _ = _run_pass(x_sample, mod[:, 1:1 + nb_dec], wts, lam_inits, final_norm_g, ctx)

    nb, t = x_prompt.shape[:2]
    new_cache_k = k_new.reshape(nb, DEPTH, t, DA_HEADS, 2, DA_HEAD)
    new_cache_v = v_new.reshape(nb, DEPTH, t, DA_HEADS, DA_VDIM)
    new_s5_re = jnp.stack([s[0] for s in states], axis=1)
    new_s5_im = jnp.stack([s[1] for s in states], axis=1)
    new_dn = jnp.stack([s[2] for s in states], axis=1)
    return (y_prompt, y_sample, new_cache_k, new_cache_v, new_s5_re, new_s5_im, new_dn)
```

```python
import functools
import math

import numpy as np
import jax
import jax.numpy as jnp
from jax import lax
from jax.experimental import pallas as pl
from jax.experimental.pallas import tpu as pltpu

F32 = jnp.float32
BF16 = jnp.bfloat16

D_MODEL = 1024
DEPTH = 2
GRID_W = 64
EPS = 1e-6
S5_WIDTH = 512
S5_GROUP = 16
S5_GROUPS = 32
S5_STATE = 64
S5_CHUNK = 16
S5_PAIRS = S5_GROUPS // 2
S5_ROW = S5_CHUNK * S5_GROUP
S5_GEN_GROUPS = 4
DA_HEADS = 4
DA_HEAD = 64
DA_VDIM = 128
DA_WIDTH = 512
ROPE_BASE = 10000.0
DN_HEADS = 4
DN_HEAD = 128
DN_WIDTH = 512
DN_CONV = 5
DN_CHUNK = 64
N_BRANCH = 3
BRANCH_WIDTH = 512
PAST_LEN = 512

COL_UA, COL_ZA, COL_QB, COL_KB, COL_VB, COL_ZB = 0, 512, 1024, 1536, 2048, 2560
COL_QC, COL_ZC = 3072, 4608
N_MAIN = 5120
BA_OFF = 5120
GATES_OFF = 5136

VMEM_LIMIT = 56 * 1024 * 1024
HI = lax.Precision.HIGHEST
LOG2E = math.log2(math.e)
ATTN_ROWS = 16


def _cparams(sem):
    return pltpu.CompilerParams(dimension_semantics=sem, vmem_limit_bytes=VMEM_LIMIT)


def _mm(a, b):
    return jnp.dot(a.astype(BF16), b.astype(BF16), preferred_element_type=F32)


def _mm_nt(a, b):
    return lax.dot_general(a.astype(BF16), b.astype(BF16), (((1,), (1,)), ((), ())),
                           preferred_element_type=F32)


def _silu(x):
    return x * jax.nn.sigmoid(x)


def _ada_kernel(c_ref, w_ref, b_ref, o_ref):
    o_ref[0] = _mm(_silu(c_ref[...]), w_ref[0]) + b_ref[0]


def ada_mod(cond8, w_ada, b_ada):
    tn = 1024
    return pl.pallas_call(
        _ada_kernel,
        grid=(DEPTH, 3 * D_MODEL // tn),
        in_specs=[pl.BlockSpec((8, D_MODEL), lambda l, j: (0, 0)),
                  pl.BlockSpec((1, D_MODEL, tn), lambda l, j: (l, 0, j)),
                  pl.BlockSpec((1, 1, tn), lambda l, j: (l, 0, j))],
        out_specs=pl.BlockSpec((1, 8, tn), lambda l, j: (l, 0, j)),
        out_shape=jax.ShapeDtypeStruct((DEPTH, 8, 3 * D_MODEL), F32),
        compiler_params=_cparams(("parallel", "parallel")),
        name="ada_mod",
    )(cond8, w_ada, b_ada.reshape(DEPTH, 1, 3 * D_MODEL))


def _inproj_kernel(*refs, n_prev, with_kv):
    x_ref, g_ref, sc_ref, sh_ref, w1_ref, w2_ref = refs[:6]
    prev = refs[6:6 + n_prev]
    outs = refs[6 + n_prev:]
    proj_ref, ba_ref, hn_ref = outs[:3]
    j = pl.program_id(1)

    @pl.when(j == 0)
    def _():
        x = x_ref[...]
        y = x * lax.rsqrt(jnp.mean(x * x, axis=-1, keepdims=True) + EPS) * g_ref[...]
        hn = (y * (1.0 + sc_ref[0]) + sh_ref[0]).astype(BF16)
        hn_ref[...] = hn
        ba_ref[...] = jnp.dot(hn, w2_ref[...], preferred_element_type=F32)

    proj_ref[...] = jnp.dot(hn_ref[...], w1_ref[...], preferred_element_type=F32)

    if with_kv:
        tn = proj_ref.shape[1]
        for c, (ref, col) in enumerate(((outs[3], COL_KB), (outs[4], COL_VB))):
            @pl.when(j == col // tn)
            def _(c=c, ref=ref, col=col):
                nb, nl, t, w = ref.shape
                if n_prev:
                    ref[:, 0:nl - 1] = prev[c][...]
                ref[:, nl - 1:nl] = proj_ref[:, col % tn:col % tn + w].reshape(nb, 1, t, w)


def inproj(x2, norm_g, scale, shift, w1, w2, rows_per_mod, kv=None):
    m = x2.shape[0]
    tm, tn = 1024, 1024
    nmod = scale.shape[0]
    mod_idx = lambda i, j: ((i * tm) // rows_per_mod, 0, 0)
    in_specs = [pl.BlockSpec((tm, D_MODEL), lambda i, j: (i, 0)),
                pl.BlockSpec((1, D_MODEL), lambda i, j: (0, 0)),
                pl.BlockSpec((1, 1, D_MODEL), mod_idx),
                pl.BlockSpec((1, 1, D_MODEL), mod_idx),
                pl.BlockSpec((D_MODEL, tn), lambda i, j: (0, j)),
                pl.BlockSpec((D_MODEL, 128), lambda i, j: (0, 0))]
    args = [x2, norm_g.reshape(1, D_MODEL), scale.reshape(nmod, 1, D_MODEL),
            shift.reshape(nmod, 1, D_MODEL), w1, w2]
    out_specs = [pl.BlockSpec((tm, tn), lambda i, j: (i, j)),
                 pl.BlockSpec((tm, 128), lambda i, j: (i, 0)),
                 pl.BlockSpec((tm, D_MODEL), lambda i, j: (i, 0))]
    out_shape = [jax.ShapeDtypeStruct((m, N_MAIN), F32),
                 jax.ShapeDtypeStruct((m, 128), F32),
                 jax.ShapeDtypeStruct((m, D_MODEL), BF16)]
    n_prev = 0
    if kv is not None:
        layer, t, caches = kv
        cspec = lambda nl: pl.BlockSpec((tm // t, nl, t, DA_WIDTH), lambda i, j: (i, 0, 0, 0))
        out_specs += [cspec(layer + 1)] * 2
        out_shape += [jax.ShapeDtypeStruct((m // t, layer + 1, t, DA_WIDTH), F32)] * 2
        if caches is not None:
            n_prev = 2
            in_specs += [cspec(layer)] * 2
            args += list(caches)
    return pl.pallas_call(
        functools.partial(_inproj_kernel, n_prev=n_prev, with_kv=kv is not None),
        grid=(m // tm, N_MAIN // tn),
        in_specs=in_specs,
        out_specs=out_specs,
        out_shape=out_shape,
        compiler_params=_cparams(("parallel", "arbitrary")),
        name="inproj",
    )(*args)


def _s5_gen_kernel(crt_ref, cit_ref, prt_ref, pit_ref, bbt_ref, bbs_ref, prow_ref, pirow_ref,
                   wt_ref, ws_ref, wh_ref):
    L, C, P = S5_CHUNK, S5_GROUP, S5_STATE
    width = (L + 1) * C
    row = lax.broadcasted_iota(jnp.int32, (128, width), 0)
    lane = lax.broadcasted_iota(jnp.int32, (128, width), 1)
    tile_c = (lane % C == row).astype(F32)
    expand = lambda a, e: jnp.dot(a, e, precision=HI, preferred_element_type=F32)
    zeros = jnp.zeros((C, L * C), F32)
    for g in range(S5_GEN_GROUPS):
        strips = []
        for d in range(2):
            spread_k = ((lane // C if d == 0 else L - lane // C) == row).astype(F32)
            crx, cix = expand(crt_ref[d, g], tile_c), expand(cit_ref[d, g], tile_c)
            prx, pix = expand(prt_ref[d, g], spread_k), expand(pit_ref[d, g], spread_k)
            ca = jnp.concatenate([crx * prx - cix * pix, -(crx * pix + cix * prx)], axis=0)
            wh_ref[d, g] = (ca[:, C:] if d == 0 else ca[:, :L * C]).astype(BF16)
            bbt, bbs = bbt_ref[d, g], bbs_ref[d, g]
            strips.append(jnp.dot(bbt, ca[:, :L * C] if d == 0 else ca[:, C:], precision=HI,
                                  preferred_element_type=F32))
            rows = []
            for i in range(L):
                k = L - 1 - i if d == 0 else i
                rows.append(bbt * prow_ref[d, g, k:k + 1, :] + bbs * pirow_ref[d, g, k:k + 1, :])
            ws_ref[d, g] = jnp.concatenate(rows, axis=0).astype(BF16)
        fpad = jnp.concatenate([zeros, strips[0]], axis=-1)
        rpad = jnp.concatenate([strips[1], zeros], axis=-1)
        rows = []
        for i in range(L):
            rows.append(fpad[:, (L - i) * C:(2 * L - i) * C] + rpad[:, (L - 1 - i) * C:(2 * L - 1 - i) * C])
        wt_ref[g] = jnp.concatenate(rows, axis=0).astype(BF16)


def s5_matrices(lam_re, lam_im, log_step, b_re, b_im, c_re, c_im):
    L, G, P, C = S5_CHUNK, S5_GROUPS, S5_STATE, S5_GROUP
    step = jnp.exp(log_step)[..., None]
    mag = jnp.exp(lam_re * step)
    ar, ai = mag * jnp.cos(lam_im * step), mag * jnp.sin(lam_im * step)
    den = lam_re * lam_re + lam_im * lam_im
    fr = ((ar - 1.0) * lam_re + ai * lam_im) / den
    fi = (ai * lam_re - (ar - 1.0) * lam_im) / den
    bbr = fr[..., None] * b_re - fi[..., None] * b_im
    bbi = fr[..., None] * b_im + fi[..., None] * b_re
    ks = jnp.arange(L + 1, dtype=F32)[None, None, None, :]
    pmag = jnp.exp(ks * (lam_re * step)[..., None])
    prt = pmag * jnp.cos(ks * (lam_im * step)[..., None])
    pit = pmag * jnp.sin(ks * (lam_im * step)[..., None])
    prow = jnp.swapaxes(prt, 2, 3)
    pirow = jnp.swapaxes(pit, 2, 3)
    bbrt, bbit = jnp.swapaxes(bbr, 2, 3), jnp.swapaxes(bbi, 2, 3)
    lanes = lambda a: jnp.pad(a, ((0, 0), (0, 0), (0, 0), (0, 128 - a.shape[-1])))
    args = [lanes(jnp.swapaxes(c_re, 2, 3)), lanes(jnp.swapaxes(c_im, 2, 3)), lanes(prt), lanes(pit),
            jnp.concatenate([bbrt, bbit], -1), jnp.concatenate([bbit, bbrt], -1),
            jnp.concatenate([prow, prow], -1), jnp.concatenate([-pirow, pirow], -1)]
    gg = S5_GEN_GROUPS
    spec = lambda a: pl.BlockSpec((2, gg) + a.shape[2:], lambda g: (0, g, 0, 0))
    wt, ws, wh = pl.pallas_call(
        _s5_gen_kernel,
        grid=(G // gg,),
        in_specs=[spec(a) for a in args],
        out_specs=[pl.BlockSpec((gg, S5_ROW, S5_ROW), lambda g: (g, 0, 0)),
                   pl.BlockSpec((2, gg, S5_ROW, 2 * P), lambda g: (0, g, 0, 0)),
                   pl.BlockSpec((2, gg, 2 * P, S5_ROW), lambda g: (0, g, 0, 0))],
        out_shape=[jax.ShapeDtypeStruct((G, S5_ROW, S5_ROW), BF16),
                   jax.ShapeDtypeStruct((2, G, S5_ROW, 2 * P), BF16),
                   jax.ShapeDtypeStruct((2, G, 2 * P, S5_ROW), BF16)],
        compiler_params=_cparams(("parallel",)),
        name="s5_gen",
    )(*args)
    a_l = jnp.stack([prt[..., L].reshape(2, 1, G * P), pit[..., L].reshape(2, 1, G * P)], 1)
    return wt, ws, wh, a_l


S5_GB = 8


def _s5_core_kernel(u_ref, ws_ref, wt_ref, wh_ref, a_ref, h0_ref, y_ref, hfin_ref, x_ref, s_ref, hin_ref,
                    *, nb, nchunk):
    r = nb * nchunk
    npair = S5_GB // 2
    us = [u_ref[pl.ds(i, r, stride=S5_CHUNK), :] for i in range(S5_CHUNK)]
    for g in range(S5_GB):
        xg = jnp.concatenate([u[:, S5_GROUP * g:S5_GROUP * (g + 1)] for u in us], axis=-1)
        x_ref[:, g * S5_ROW:(g + 1) * S5_ROW] = xg.astype(BF16)
    for p in range(npair):
        for d in range(2):
            sg = [jnp.dot(x_ref[:, (2 * p + k) * S5_ROW:(2 * p + k + 1) * S5_ROW], ws_ref[d, 2 * p + k],
                          preferred_element_type=F32) for k in range(2)]
            for comp in range(2):
                cs = slice(comp * S5_STATE, (comp + 1) * S5_STATE)
                s_ref[d, comp, p] = jnp.concatenate([sg[0][:, cs], sg[1][:, cs]], axis=-1)
    chains = [(d, p) for d in range(2) for p in range(npair)]
    h = {}
    for d, p in chains:
        cols = slice(p * 128, (p + 1) * 128)
        h[d, p] = (h0_ref[d, 0, :, cols], h0_ref[d, 1, :, cols], a_ref[d, 0, :, cols], a_ref[d, 1, :, cols])
    for step in range(nchunk):
        for d, p in chains:
            rows = pl.ds(step if d == 0 else nchunk - 1 - step, nb, stride=nchunk)
            hr, hi, ar, ai = h[d, p]
            hin_ref[d, 0, p, rows, :] = hr
            hin_ref[d, 1, p, rows, :] = hi
            sr, si = s_ref[d, 0, p, rows, :], s_ref[d, 1, p, rows, :]
            h[d, p] = (ar * hr - ai * hi + sr, ar * hi + ai * hr + si, ar, ai)
    for d, p in chains:
        cols = slice(p * 128, (p + 1) * 128)
        hfin_ref[d, 0, :, cols] = h[d, p][0]
        hfin_ref[d, 1, :, cols] = h[d, p][1]
    ys = []
    for g in range(S5_GB):
        y = jnp.dot(x_ref[:, g * S5_ROW:(g + 1) * S5_ROW], wt_ref[g], preferred_element_type=F32)
        cs = slice((g % 2) * S5_STATE, (g % 2 + 1) * S5_STATE)
        for d in range(2):
            hg = jnp.concatenate([hin_ref[d, 0, g // 2, :, cs], hin_ref[d, 1, g // 2, :, cs]], axis=-1)
            y = y + jnp.dot(hg.astype(BF16), wh_ref[d, g], preferred_element_type=F32)
        ys.append(y)
    for j in range(S5_CHUNK):
        y_ref[pl.ds(j, r, stride=S5_CHUNK), :] = jnp.concatenate(
            [y[:, S5_GROUP * j:S5_GROUP * (j + 1)] for y in ys], axis=-1)


def s5_core(proj, mats, h0, nb, nchunk):
    wt, ws, wh, a_l = mats
    r = nb * nchunk
    sw = S5_GB * S5_STATE
    return pl.pallas_call(
        functools.partial(_s5_core_kernel, nb=nb, nchunk=nchunk),
        grid=(S5_GROUPS // S5_GB,),
        in_specs=[pl.BlockSpec((r * S5_CHUNK, 128), lambda j: (0, COL_UA // 128 + j)),
                  pl.BlockSpec((2, S5_GB, S5_ROW, 2 * S5_STATE), lambda j: (0, j, 0, 0)),
                  pl.BlockSpec((S5_GB, S5_ROW, S5_ROW), lambda j: (j, 0, 0)),
                  pl.BlockSpec((2, S5_GB, 2 * S5_STATE, S5_ROW), lambda j: (0, j, 0, 0)),
                  pl.BlockSpec((2, 2, 1, sw), lambda j: (0, 0, 0, j)),
                  pl.BlockSpec((2, 2, nb, sw), lambda j: (0, 0, 0, j))],
        out_specs=[pl.BlockSpec((r * S5_CHUNK, 128), lambda j: (0, j)),
                   pl.BlockSpec((2, 2, nb, sw), lambda j: (0, 0, 0, j))],
        out_shape=[jax.ShapeDtypeStruct((r * S5_CHUNK, S5_WIDTH), F32),
                   jax.ShapeDtypeStruct((2, 2, nb, S5_GROUPS * S5_STATE), F32)],
        scratch_shapes=[pltpu.VMEM((r, S5_GB * S5_ROW), BF16),
                        pltpu.VMEM((2, 2, S5_GB // 2, r, 128), F32),
                        pltpu.VMEM((2, 2, S5_GB // 2, r, 128), F32)],
        compiler_params=_cparams(("parallel",)),
        name="s5_core",
    )(proj, ws, wt, wh, a_l, h0)


def _s5_epilogue_kernel(u_ref, y_ref, z_ref, d_ref, w_ref, o_ref):
    ya = jax.nn.gelu(d_ref[...] * u_ref[...] + y_ref[...])
    ya = ya * jax.nn.sigmoid(_mm(ya, w_ref[...]))
    o_ref[...] = (ya * _silu(z_ref[...])).astype(BF16)


def s5_epilogue(proj, y_s5, s5_d, w_glu):
    m = proj.shape[0]
    tm = 512
    return pl.pallas_call(
        _s5_epilogue_kernel,
        grid=(m // tm,),
        in_specs=[pl.BlockSpec((tm, S5_WIDTH), lambda i: (i, COL_UA // S5_WIDTH)),
                  pl.BlockSpec((tm, S5_WIDTH), lambda i: (i, 0)),
                  pl.BlockSpec((tm, S5_WIDTH), lambda i: (i, COL_ZA // S5_WIDTH)),
                  pl.BlockSpec((1, S5_WIDTH), lambda i: (0, 0)),
                  pl.BlockSpec((S5_WIDTH, S5_WIDTH), lambda i: (0, 0))],
        out_specs=pl.BlockSpec((tm, S5_WIDTH), lambda i: (i, 0)),
        out_shape=jax.ShapeDtypeStruct((m, S5_WIDTH), BF16),
        compiler_params=_cparams(("parallel",)),
        name="s5_epilogue",
    )(proj, y_s5, proj, s5_d.reshape(1, S5_WIDTH), w_glu)


def s5_branch(proj, nb, t, mats, s5_d, w_glu, h0):
    y, hfin = s5_core(proj, mats, h0, nb, t // S5_CHUNK)
    return s5_epilogue(proj, y, s5_d, w_glu), hfin


def _rope_tables(t):
    rows = t // GRID_W
    row = np.repeat(np.arange(rows), GRID_W).astype(np.float32)
    col = np.tile(np.arange(GRID_W), rows).astype(np.float32)
    nf = DA_HEAD // 4
    inv = (ROPE_BASE ** (-jnp.arange(nf, dtype=F32) / nf))

    def tab(pos):
        ang = jnp.asarray(pos)[:, None] * inv[None, :]
        c, s = jnp.cos(ang), jnp.sin(ang)
        return jnp.concatenate([c, c], -1), jnp.concatenate([-s, s], -1)

    cr, sr = tab(row)
    cc, sc = tab(col)
    cos = jnp.concatenate([cr, cc], -1)
    sin = jnp.concatenate([sr, sc], -1)
    return jnp.tile(cos, (1, 2)), jnp.tile(sin, (1, 2))


def _rope(x, cos, sin):
    lane = lax.broadcasted_iota(jnp.int32, x.shape, 1)
    swapped = jnp.where((lane % 32) < 16, pltpu.roll(x, 112, 1), pltpu.roll(x, 16, 1))
    return x * cos + swapped * sin


def _attn_kernel(*refs, lam_init, t, s_tot, tq, with_ctx):
    if with_ctx:
        (q_ref, k_ref, v_ref, z_ref, kc_ref, vc_ref, cq_ref, sq_ref, ck_ref, sk_ref,
         lam_ref, ng_ref, o_ref, kall_ref, vall_ref, s_ref, e_ref, l_ref) = refs
    else:
        q_ref, k_ref, v_ref, z_ref, lam_ref, ng_ref, o_ref, kall_ref, vall_ref, s_ref, e_ref, l_ref = refs

    @pl.when(pl.program_id(1) == 0)
    def _():
        if with_ctx:
            for h in range(DA_HEADS):
                hs = slice(h * 128, (h + 1) * 128)
                kall_ref[0:t, hs] = _rope(k_ref[0, :, hs], ck_ref[...], sk_ref[...]).astype(BF16)
            kall_ref[t:s_tot, :] = kc_ref[0].astype(BF16)
            vall_ref[0:t, :] = v_ref[0].astype(BF16)
            vall_ref[t:s_tot, :] = vc_ref[0].astype(BF16)
        else:
            kall_ref[...] = k_ref[0].astype(BF16)
            vall_ref[...] = v_ref[0].astype(BF16)

    lp = lam_ref[...]
    lam = (jnp.exp(jnp.sum(lp[0:1] * lp[1:2], axis=-1, keepdims=True))
           - jnp.exp(jnp.sum(lp[2:3] * lp[3:4], axis=-1, keepdims=True)) + lam_init)
    lane = lax.broadcasted_iota(jnp.int32, (tq, 128), 1)
    for h in range(DA_HEADS):
        hs = slice(h * 128, (h + 1) * 128)
        q = q_ref[0, :, hs]
        if with_ctx:
            q = _rope(q, cq_ref[...], sq_ref[...])
        q = q * (DA_HEAD ** -0.5 * LOG2E)
        qs = jnp.concatenate([jnp.where(lane < DA_HEAD, q, 0.0), jnp.where(lane >= DA_HEAD, q, 0.0)], axis=0)
        s_ref[...] = _mm_nt(qs, kall_ref[:, hs])
        for r in range(0, 2 * tq, ATTN_ROWS):
            sc = s_ref[r:r + ATTN_ROWS, :]
            e = jnp.exp2(sc - jnp.max(sc, axis=-1, keepdims=True))
            e_ref[r:r + ATTN_ROWS, :] = e.astype(BF16)
            l_ref[r:r + ATTN_ROWS, :] = jnp.broadcast_to(jnp.sum(e, axis=-1, keepdims=True), (ATTN_ROWS, 128))
        ov = jnp.dot(e_ref[...], vall_ref[:, hs], preferred_element_type=F32) / l_ref[...]
        o = ov[:tq] - lam * ov[tq:]
        o = o * lax.rsqrt(jnp.mean(o * o, axis=-1, keepdims=True) + EPS) * ng_ref[...]
        o = o * (1.0 - lam_init)
        o_ref[0, :, hs] = (o * _silu(z_ref[0, :, hs])).astype(BF16)


def diff_attention(proj3, layer, lam_init, da_lam, da_norm_g, ctx_kv):
    nb, t, _ = proj3.shape
    with_ctx = ctx_kv is not None
    s_tot = t + (PAST_LEN if with_ctx else 0)
    tq = 256
    wb = DA_WIDTH
    in_specs = [pl.BlockSpec((1, tq, wb), lambda b, i: (b, i, COL_QB // wb)),
                pl.BlockSpec((1, t, wb), lambda b, i: (b, 0, COL_KB // wb)),
                pl.BlockSpec((1, t, wb), lambda b, i: (b, 0, COL_VB // wb)),
                pl.BlockSpec((1, tq, wb), lambda b, i: (b, i, COL_ZB // wb))]
    args = [proj3, proj3, proj3, proj3]
    if with_ctx:
        kc, vc = ctx_kv
        cos, sin = _rope_tables(t)
        in_specs += [pl.BlockSpec((1, PAST_LEN, wb), lambda b, i: (b, layer, 0)),
                     pl.BlockSpec((1, PAST_LEN, wb), lambda b, i: (b, layer, 0)),
                     pl.BlockSpec((tq, 128), lambda b, i: (i, 0)),
                     pl.BlockSpec((tq, 128), lambda b, i: (i, 0)),
                     pl.BlockSpec((t, 128), lambda b, i: (0, 0)),
                     pl.BlockSpec((t, 128), lambda b, i: (0, 0))]
        args += [kc, vc, cos, sin, cos, sin]
    in_specs += [pl.BlockSpec((4, DA_HEAD), lambda b, i: (0, 0)),
                 pl.BlockSpec((1, DA_VDIM), lambda b, i: (0, 0))]
    args += [da_lam, da_norm_g.reshape(1, DA_VDIM)]
    return pl.pallas_call(
        functools.partial(_attn_kernel, lam_init=lam_init, t=t, s_tot=s_tot, tq=tq, with_ctx=with_ctx),
        grid=(nb, t // tq),
        in_specs=in_specs,
        out_specs=pl.BlockSpec((1, tq, wb), lambda b, i: (b, i, 0)),
        out_shape=jax.ShapeDtypeStruct((nb, t, wb), BF16),
        scratch_shapes=[pltpu.VMEM((s_tot, wb), BF16), pltpu.VMEM((s_tot, wb), BF16),
                        pltpu.VMEM((2 * tq, s_tot), F32), pltpu.VMEM((2 * tq, s_tot), BF16),
                        pltpu.VMEM((2 * tq, 128), F32)],
        compiler_params=_cparams(("parallel", "arbitrary")),
        name="diff_attention",
    )(*args)


DN_PAD = 8
DN_RT = 128
DN_GROUP = 128
DN_STEP_GROUPS = 2


def _dn_kernel(*refs, t, ngroup, with_s0):
    if with_s0:
        (qkv_ref, z_ref, ba_ref, cw_ref, alog_ref, dtb_ref, ng_ref, s0_ref,
         out_ref, sfin_ref, xp_ref, qkvn_ref, oacc_ref, st_ref) = refs
    else:
        (qkv_ref, z_ref, ba_ref, cw_ref, alog_ref, dtb_ref, ng_ref,
         out_ref, sfin_ref, xp_ref, qkvn_ref, oacc_ref, st_ref) = refs
    n = pl.program_id(1)
    cd = DN_CHUNK
    w3 = 3 * DN_WIDTH

    @pl.when(n == 0)
    def _init():
        xp_ref[0:DN_PAD, :] = jnp.zeros((DN_PAD, w3), F32)
        xp_ref[DN_PAD + t:2 * DN_PAD + t, :] = jnp.zeros((DN_PAD, w3), F32)
        xp_ref[DN_PAD:DN_PAD + t, :] = qkv_ref[0]
        half = DN_CONV // 2
        for r in range(t // DN_RT):
            for sec in range(3):
                for h in range(DN_HEADS):
                    cs = slice(sec * DN_WIDTH + h * DN_HEAD, sec * DN_WIDTH + (h + 1) * DN_HEAD)
                    acc = jnp.zeros((DN_RT, DN_HEAD), F32)
                    for j in range(DN_CONV):
                        r0 = DN_PAD + r * DN_RT + j - half
                        acc = acc + xp_ref[r0:r0 + DN_RT, cs] * cw_ref[j:j + 1, cs]
                    y = _silu(acc)
                    if sec < 2:
                        y = y * lax.rsqrt(jnp.sum(y * y, axis=-1, keepdims=True) + EPS)
                    if sec == 0:
                        y = y * (DN_HEAD ** -0.5)
                    qkvn_ref[r * DN_RT:(r + 1) * DN_RT, cs] = y
        oacc_ref[...] = jnp.zeros_like(oacc_ref)
        if with_s0:
            st_ref[...] = s0_ref[0, 0]
        else:
            st_ref[...] = jnp.zeros_like(st_ref)

    gb = DN_GROUP
    nsub = gb // cd
    ri = lax.broadcasted_iota(jnp.int32, (gb, gb), 0)
    ci = lax.broadcasted_iota(jnp.int32, (gb, gb), 1)
    same = (ri // cd) == (ci // cd)
    samef = same.astype(F32)
    eye = (ri == ci).astype(BF16)
    masked_out = -1e30

    chains = []
    for d, gi in [(d, gi) for d in range(2) for gi in range(DN_STEP_GROUPS)]:
        grp = n * DN_STEP_GROUPS + gi
        r0 = pl.multiple_of((grp if d == 0 else ngroup - 1 - grp) * gb, gb)
        incl = same & ((ci <= ri) if d == 0 else (ci >= ri))
        strict = same & ((ci < ri) if d == 0 else (ci > ri))
        inclf = incl.astype(F32)
        inclog = jnp.where(incl, 0.0, masked_out)
        nstrict = -strict.astype(BF16)
        ba = ba_ref[0, pl.ds(r0, gb), :]
        beta_all = jax.nn.sigmoid(ba)
        g_all = -jnp.exp(alog_ref[...]) * jax.nn.softplus(ba + dtb_ref[...])
        gc = jnp.dot(inclf, g_all, precision=HI, preferred_element_type=F32)
        gct = lax.dot_general(g_all.T, inclf, (((1,), (1,)), ((), ())), precision=HI,
                              preferred_element_type=F32)
        gtot = jnp.dot(samef, g_all, precision=HI, preferred_element_type=F32)
        for h in range(DN_HEADS):
            hs = slice(h * DN_HEAD, (h + 1) * DN_HEAD)
            col = 2 * DN_HEADS + d * DN_HEADS + h
            gcol = gc[:, col:col + 1]
            grow = gct[col:col + 1, :]
            gt = gtot[:, col:col + 1]
            beta = beta_all[:, d * DN_HEADS + h:d * DN_HEADS + h + 1]
            q = qkvn_ref[pl.ds(r0, gb), hs]
            k = qkvn_ref[pl.ds(r0, gb), DN_WIDTH + h * DN_HEAD:DN_WIDTH + (h + 1) * DN_HEAD]
            v = qkvn_ref[pl.ds(r0, gb), 2 * DN_WIDTH + h * DN_HEAD:2 * DN_WIDTH + (h + 1) * DN_HEAD]
            eg = jnp.exp(gcol)
            chains.append(dict(
                d=d, gi=gi, h=h, r0=r0, hs=hs, nstrict=nstrict, q=q, k=k, kb=k * beta,
                decay=jnp.exp((gcol - grow) + inclog),
                rhs=jnp.concatenate([v * beta, k * beta * eg], axis=-1),
                qe=q * eg, kdec=k * jnp.exp(gt - gcol), egt=jnp.exp(gt)))

    for c in chains:
        c["nmm"] = (_mm_nt(c["kb"], c["k"]) * c["decay"]).astype(BF16) * c["nstrict"]
    for c in chains:
        c["qk"] = _mm_nt(c["q"], c["k"]) * c["decay"]
    def level_mask(s):
        return (((ri // (2 * s)) == (ci // (2 * s))) & ((ri // s) != (ci // s))).astype(BF16)

    pair = level_mask(1)
    for c in chains:
        c["tinv"] = eye + c["nmm"] * pair
    s = 2
    while s < cd:
        offmask = level_mask(s)
        xs = [jnp.dot(c["nmm"] * offmask, c["tinv"], preferred_element_type=F32).astype(BF16) for c in chains]
        ys = [jnp.dot(c["tinv"], x, preferred_element_type=F32) for c, x in zip(chains, xs)]
        for c, y in zip(chains, ys):
            c["tinv"] = c["tinv"] + y.astype(BF16)
        s *= 2
    for c in chains:
        c["uw"] = _mm(c["tinv"], c["rhs"])

    st = {(d, h): st_ref[d, h] for d in range(2) for h in range(DN_HEADS)}
    outs = []
    for gi, step in [(gi, step) for gi in range(DN_STEP_GROUPS) for step in range(nsub)]:
        active = [c for c in chains if c["gi"] == gi]
        rs = []
        for c in active:
            sub = step if c["d"] == 0 else nsub - 1 - step
            c["rows"] = slice(sub * cd, (sub + 1) * cd)
            rs.append(_mm(jnp.concatenate([c["uw"][c["rows"], DN_HEAD:], c["qe"][c["rows"]]], axis=0),
                          st[c["d"], c["h"]]))
        for c, r in zip(active, rs):
            rows = c["rows"]
            v_new = c["uw"][rows, :DN_HEAD] - r[:cd]
            o = r[cd:] + _mm(c["qk"][rows, rows], v_new)
            st[c["d"], c["h"]] = (st[c["d"], c["h"]] * c["egt"][rows.start:rows.start + 1]
                                  + _mm(c["kdec"][rows].T, v_new))
            outs.append((c, rows.start, o))
    for (d, h), v in st.items():
        st_ref[d, h] = v
    for c, off, o in outs:
        oacc_ref[pl.ds(pl.multiple_of(c["r0"] + off, cd), cd), c["hs"]] += o

    @pl.when(n == ngroup // DN_STEP_GROUPS - 1)
    def _fin():
        for h in range(DN_HEADS):
            hs = slice(h * DN_HEAD, (h + 1) * DN_HEAD)
            o = oacc_ref[:, hs]
            o = o * lax.rsqrt(jnp.mean(o * o, axis=-1, keepdims=True) + EPS) * ng_ref[...]
            out_ref[0, :, hs] = (o * _silu(z_ref[0, :, hs])).astype(BF16)
        sfin_ref[0] = st_ref[...]


def deltanet(proj3, ba3, layer, conv_w, a_log, dt_bias, norm_g, s0):
    nb, t, _ = proj3.shape
    ngroup = t // DN_GROUP
    with_s0 = s0 is not None
    w3 = 3 * DN_WIDTH
    pad = jnp.zeros((2 * DN_HEADS,), F32)
    alog_row = jnp.concatenate([pad, a_log.reshape(-1), jnp.zeros((128 - 4 * DN_HEADS,), F32)]).reshape(1, 128)
    dtb_row = jnp.concatenate([pad, dt_bias.reshape(-1), jnp.zeros((128 - 4 * DN_HEADS,), F32)]).reshape(1, 128)
    in_specs = [pl.BlockSpec((1, t, w3), lambda b, n: (b, 0, COL_QC // w3)),
                pl.BlockSpec((1, t, DN_WIDTH), lambda b, n: (b, 0, COL_ZC // DN_WIDTH)),
                pl.BlockSpec((1, t, 128), lambda b, n: (b, 0, 0)),
                pl.BlockSpec((8, w3), lambda b, n: (0, 0)),
                pl.BlockSpec((1, 128), lambda b, n: (0, 0)),
                pl.BlockSpec((1, 128), lambda b, n: (0, 0)),
                pl.BlockSpec((1, DN_HEAD), lambda b, n: (0, 0))]
    args = [proj3, proj3, ba3, jnp.pad(conv_w, ((0, 8 - DN_CONV), (0, 0))), alog_row, dtb_row,
            norm_g.reshape(1, DN_HEAD)]
    if with_s0:
        in_specs.append(pl.BlockSpec((1, 1, 2, DN_HEADS, DN_HEAD, DN_HEAD), lambda b, n: (b, layer, 0, 0, 0, 0)))
        args.append(s0)
    return pl.pallas_call(
        functools.partial(_dn_kernel, t=t, ngroup=ngroup, with_s0=with_s0),
        grid=(nb, ngroup // DN_STEP_GROUPS),
        in_specs=in_specs,
        out_specs=[pl.BlockSpec((1, t, DN_WIDTH), lambda b, n: (b, 0, 0)),
                   pl.BlockSpec((1, 2, DN_HEADS, DN_HEAD, DN_HEAD), lambda b, n: (b, 0, 0, 0, 0))],
        out_shape=[jax.ShapeDtypeStruct((nb, t, DN_WIDTH), BF16),
                   jax.ShapeDtypeStruct((nb, 2, DN_HEADS, DN_HEAD, DN_HEAD), F32)],
        scratch_shapes=[pltpu.VMEM((t + 2 * DN_PAD, w3), F32),
                        pltpu.VMEM((t, w3), F32),
                        pltpu.VMEM((t, DN_WIDTH), F32),
                        pltpu.VMEM((2, DN_HEADS, DN_HEAD, DN_HEAD), F32)],
        compiler_params=_cparams(("parallel", "arbitrary")),
        name="deltanet",
    )(*args)


def _merge_kernel(oa_ref, ob_ref, oc_ref, hn_ref, wg_ref, wb_ref, wo_ref, x_ref, gate_ref, fg_ref,
                  *outs, final):
    tm = x_ref.shape[0]
    for rows in (slice(0, tm // 2), slice(tm // 2, tm)):
        acc = None
        hn = hn_ref[rows, :]
        for i, o_ref in enumerate((oa_ref, ob_ref, oc_ref)):
            pr = jnp.dot(o_ref[rows, :], wb_ref[i], preferred_element_type=F32)
            gt = jnp.dot(hn, wg_ref[:, i * D_MODEL:(i + 1) * D_MODEL], preferred_element_type=F32)
            term = jax.nn.sigmoid(gt) * pr
            acc = term if acc is None else acc + term
        y = jnp.dot(acc.astype(BF16), wo_ref[...], preferred_element_type=F32)
        xn = x_ref[rows, :] + gate_ref[0] * y
        outs[0][rows, :] = xn
        if final:
            yn = xn * lax.rsqrt(jnp.mean(xn * xn, axis=-1, keepdims=True) + EPS) * fg_ref[...]
            outs[1][rows, :] = yn


def merge(out_a, out_b, out_c, hn, w_gates, w_branch, w_out, x2, gate, final_g, rows_per_mod, final):
    m = x2.shape[0]
    tm = 512
    nmod = gate.shape[0]
    row = lambda i: (i, 0)
    out_specs = [pl.BlockSpec((tm, D_MODEL), row)]
    out_shape = [jax.ShapeDtypeStruct((m, D_MODEL), F32)]
    if final:
        out_specs.append(pl.BlockSpec((tm, D_MODEL), row))
        out_shape.append(jax.ShapeDtypeStruct((m, D_MODEL), F32))
    return pl.pallas_call(
        functools.partial(_merge_kernel, final=final),
        grid=(m // tm,),
        in_specs=[pl.BlockSpec((tm, BRANCH_WIDTH), row),
                  pl.BlockSpec((tm, BRANCH_WIDTH), row),
                  pl.BlockSpec((tm, BRANCH_WIDTH), row),
                  pl.BlockSpec((tm, D_MODEL), row),
                  pl.BlockSpec((D_MODEL, N_BRANCH * D_MODEL), lambda i: (0, 0)),
                  pl.BlockSpec((N_BRANCH, BRANCH_WIDTH, D_MODEL), lambda i: (0, 0, 0)),
                  pl.BlockSpec((D_MODEL, D_MODEL), lambda i: (0, 0)),
                  pl.BlockSpec((tm, D_MODEL), row),
                  pl.BlockSpec((1, 1, D_MODEL), lambda i: ((i * tm) // rows_per_mod, 0, 0)),
                  pl.BlockSpec((1, D_MODEL), lambda i: (0, 0))],
        out_specs=out_specs,
        out_shape=out_shape,
        compiler_params=_cparams(("parallel",)),
        name="merge",
    )(out_a, out_b, out_c, hn, w_gates, w_branch, w_out, x2, gate.reshape(nmod, 1, D_MODEL),
      final_g.reshape(1, D_MODEL))


def _run_pass(x, mod, wts, lam_inits, final_g, ctx):
    nb, t, _ = x.shape
    m = nb * t
    nmod = mod.shape[1]
    rows_per_mod = m if nmod == 1 else t
    x2 = x.reshape(m, D_MODEL)
    states = []
    y, caches = None, None
    for l in range(DEPTH):
        w = wts[l]
        shift, scale, gate = jnp.split(mod[l], 3, axis=-1)
        if ctx is None:
            proj, ba, hn, *caches = inproj(x2, w["norm_g"], scale, shift, w["w1"], w["w2"], rows_per_mod,
                                       kv=(l, t, caches))
            h0 = jnp.zeros((2, 2, nb, S5_GROUPS * S5_STATE), F32)
            ctx_kv, s0 = None, None
        else:
            proj, ba, hn = inproj(x2, w["norm_g"], scale, shift, w["w1"], w["w2"], rows_per_mod)
            cache_k, cache_v, st_re, st_im, st_dn = ctx
            h0 = jnp.stack([st_re[:, l], st_im[:, l]], 0)
            h0 = jnp.transpose(h0, (2, 0, 1, 3, 4)).reshape(2, 2, nb, S5_GROUPS * S5_STATE)
            ctx_kv, s0 = (cache_k, cache_v), st_dn
        proj3 = proj.reshape(nb, t, N_MAIN)
        out_a, hfin = s5_branch(proj, nb, t, w["s5_mats"], w["s5_d"], w["w_glu"], h0)
        out_b = diff_attention(proj3, l, lam_inits[l], w["da_lam"], w["da_norm_g"], ctx_kv)
        out_c, sfin = deltanet(proj3, ba.reshape(nb, t, 128), l, w["dn_conv"], w["dn_a_log"],
                               w["dn_dt_bias"], w["dn_norm_g"], s0)
        final = l == DEPTH - 1
        res = merge(out_a, out_b.reshape(m, DA_WIDTH), out_c.reshape(m, DN_WIDTH), hn, w["w_gates"],
                    w["w_branch"], w["w_out"], x2, gate, final_g, rows_per_mod, final)
        x2 = res[0]
        if final:
            y = res[1]
        if ctx is None:
            hf = hfin.reshape(2, 2, nb, S5_GROUPS, S5_STATE)
            states.append((jnp.transpose(hf[:, 0], (1, 0, 2, 3)), jnp.transpose(hf[:, 1], (1, 0, 2, 3)), sfin))
    return y.reshape(nb, t, D_MODEL), states, caches


def kernel(x_prompt, x_sample, cache_k, cache_v, state_s5_re, state_s5_im, state_dn, c, c_ctx,
           norm_g, w_ada, b_ada, w_in, s5_lam_re, s5_lam_im, s5_log_step, s5_b_re, s5_b_im,
           s5_c_re, s5_c_im, s5_d, s5_w_glu, da_lam, da_norm_g, dn_conv, dn_a_log, dn_dt_bias,
           dn_norm_g, w_branch, w_out, final_norm_g):
    nb_dec = x_sample.shape[0]
    cond8 = jnp.concatenate([c_ctx[None, :], c, jnp.zeros((8 - 1 - nb_dec, D_MODEL), F32)], 0)
    mod = ada_mod(cond8, w_ada, b_ada)
    w1 = w_in[:, :, :BA_OFF].astype(BF16)
    w_gates = w_in[:, :, GATES_OFF:].astype(BF16)
    w2 = jnp.pad(w_in[:, :, BA_OFF:GATES_OFF], ((0, 0), (0, 0), (0, 128 - (GATES_OFF - BA_OFF)))).astype(BF16)
    wts = []
    for l in range(DEPTH):
        wts.append(dict(
            norm_g=norm_g[l], w1=w1[l], w2=w2[l], w_gates=w_gates[l],
            s5_mats=s5_matrices(s5_lam_re[l], s5_lam_im[l], s5_log_step[l], s5_b_re[l], s5_b_im[l],
                                s5_c_re[l], s5_c_im[l]),
            s5_d=s5_d[l], w_glu=s5_w_glu[l].astype(BF16), da_lam=da_lam[l], da_norm_g=da_norm_g[l],
            dn_conv=dn_conv[l], dn_a_log=dn_a_log[l], dn_dt_bias=dn_dt_bias[l], dn_norm_g=dn_norm_g[l],
            w_branch=w_branch[l].astype(BF16), w_out=w_out[l].astype(BF16)))
    lam_inits = [0.8 - 0.6 * math.exp(-0.3 * l) for l in range(DEPTH)]

    y_prompt, states, (k_new, v_new) = _run_pass(x_prompt, mod[:, 0:1], wts, lam_inits, final_norm_g, None)
    ctx = (cache_k.reshape(nb_dec, DEPTH * PAST_LEN, DA_WIDTH),
           cache_v.reshape(nb_dec, DEPTH * PAST_LEN, DA_WIDTH), state_s5_re, state_s5_im, state_dn)
    y_sample, _, _ = _run_pass(x_sample, mod[:, 1:1 + nb_dec], wts, lam_inits, final_norm_g, ctx)

    nb, t = x_prompt.shape[:2]
    new_cache_k = k_new.reshape(nb, DEPTH, t, DA_HEADS, 2, DA_HEAD)
    new_cache_v = v_new.reshape(nb, DEPTH, t, DA_HEADS, DA_VDIM)
    new_s5_re = jnp.stack([s[0] for s in states], axis=1)
    new_s5_im = jnp.stack([s[1] for s in states], axis=1)
    new_dn = jnp.stack([s[2] for s in states], axis=1)
    return (y_prompt, y_sample, new_cache_k, new_cache_v, new_s5_re, new_s5_im, new_dn)
```

```python
import functools
import math

import numpy as np
import jax
import jax.numpy as jnp
from jax import lax
from jax.experimental import pallas as pl
from jax.experimental.pallas import tpu as pltpu

F32 = jnp.float32
BF16 = jnp.bfloat16

D_MODEL = 1024
DEPTH = 2
GRID_W = 64
EPS = 1e-6
S5_WIDTH = 512
S5_GROUP = 16
S5_GROUPS = 32
S5_STATE = 64
S5_CHUNK = 16
S5_PAIRS = S5_GROUPS // 2
S5_ROW = S5_CHUNK * S5_GROUP
S5_GEN_GROUPS = 4
DA_HEADS = 4
DA_HEAD = 64
DA_VDIM = 128
DA_WIDTH = 512
ROPE_BASE = 10000.0
DN_HEADS = 4
DN_HEAD = 128
DN_WIDTH = 512
DN_CONV = 5
DN_CHUNK = 64
N_BRANCH = 3
BRANCH_WIDTH = 512
PAST_LEN = 512

COL_UA, COL_ZA, COL_QB, COL_KB, COL_VB, COL_ZB = 0, 512, 1024, 1536, 2048, 2560
COL_QC, COL_ZC = 3072, 4608
N_MAIN = 5120
BA_OFF = 5120
GATES_OFF = 5136

VMEM_LIMIT = 56 * 1024 * 1024
HI = lax.Precision.HIGHEST
LOG2E = math.log2(math.e)
ATTN_ROWS = 16


def _cparams(sem):
    return pltpu.CompilerParams(dimension_semantics=sem, vmem_limit_bytes=VMEM_LIMIT)


def _mm(a, b):
    return jnp.dot(a.astype(BF16), b.astype(BF16), preferred_element_type=F32)


def _mm_nt(a, b):
    return lax.dot_general(a.astype(BF16), b.astype(BF16), (((1,), (1,)), ((), ())),
                           preferred_element_type=F32)


def _silu(x):
    return x * jax.nn.sigmoid(x)


def _ada_kernel(c_ref, w_ref, b_ref, o_ref):
    o_ref[0] = _mm(_silu(c_ref[...]), w_ref[0]) + b_ref[0]


def ada_mod(cond8, w_ada, b_ada):
    tn = 1024
    return pl.pallas_call(
        _ada_kernel,
        grid=(DEPTH, 3 * D_MODEL // tn),
        in_specs=[pl.BlockSpec((8, D_MODEL), lambda l, j: (0, 0)),
                  pl.BlockSpec((1, D_MODEL, tn), lambda l, j: (l, 0, j)),
                  pl.BlockSpec((1, 1, tn), lambda l, j: (l, 0, j))],
        out_specs=pl.BlockSpec((1, 8, tn), lambda l, j: (l, 0, j)),
        out_shape=jax.ShapeDtypeStruct((DEPTH, 8, 3 * D_MODEL), F32),
        compiler_params=_cparams(("parallel", "parallel")),
        name="ada_mod",
    )(cond8, w_ada, b_ada.reshape(DEPTH, 1, 3 * D_MODEL))


def _cast_kernel(w_ref, o_ref):
    o_ref[...] = w_ref[...].astype(BF16)


def _cast_shift_kernel(a_ref, b_ref, o_ref, *, shift):
    o_ref[0] = jnp.concatenate([a_ref[0, :, shift:], b_ref[0, :, :shift]], axis=-1).astype(BF16)


def cast_w_in(w_in):
    tr, tn = 512, 1024
    rows = D_MODEL // tr
    w1 = pl.pallas_call(
        _cast_kernel,
        grid=(DEPTH, rows, N_MAIN // tn),
        in_specs=[pl.BlockSpec((1, tr, tn), lambda l, i, j: (l, i, j))],
        out_specs=pl.BlockSpec((1, tr, tn), lambda l, i, j: (l, i, j)),
        out_shape=jax.ShapeDtypeStruct((DEPTH, D_MODEL, N_MAIN), BF16),
        compiler_params=_cparams(("parallel", "parallel", "parallel")),
        name="cast_w1",
    )(w_in)
    shift = GATES_OFF % 128
    base = GATES_OFF - shift
    gw = N_BRANCH * D_MODEL
    w_gates = pl.pallas_call(
        functools.partial(_cast_shift_kernel, shift=shift),
        grid=(DEPTH, rows, gw // tn),
        in_specs=[pl.BlockSpec((1, tr, tn), lambda l, i, j: (l, i, base // tn + j)),
                  pl.BlockSpec((1, tr, 128), lambda l, i, j: (l, i, (base + (j + 1) * tn) // 128))],
        out_specs=pl.BlockSpec((1, tr, tn), lambda l, i, j: (l, i, j)),
        out_shape=jax.ShapeDtypeStruct((DEPTH, D_MODEL, gw), BF16),
        compiler_params=_cparams(("parallel", "parallel", "parallel")),
        name="cast_w_gates",
    )(w_in, w_in)
    return w1, w_gates


def _inproj_kernel(*refs, n_prev, with_kv):
    x_ref, g_ref, sc_ref, sh_ref, w1_ref, w2_ref = refs[:6]
    prev = refs[6:6 + n_prev]
    outs = refs[6 + n_prev:]
    proj_ref, ba_ref, hn_ref = outs[:3]
    j = pl.program_id(1)

    @pl.when(j == 0)
    def _():
        x = x_ref[...]
        y = x * lax.rsqrt(jnp.mean(x * x, axis=-1, keepdims=True) + EPS) * g_ref[...]
        hn = (y * (1.0 + sc_ref[0]) + sh_ref[0]).astype(BF16)
        hn_ref[...] = hn
        ba_ref[...] = jnp.dot(hn, w2_ref[...], preferred_element_type=F32)

    proj_ref[...] = jnp.dot(hn_ref[...], w1_ref[...], preferred_element_type=F32)

    if with_kv:
        tn = proj_ref.shape[1]
        for c, (ref, col) in enumerate(((outs[3], COL_KB), (outs[4], COL_VB))):
            @pl.when(j == col // tn)
            def _(c=c, ref=ref, col=col):
                nb, nl, t, w = ref.shape
                if n_prev:
                    ref[:, 0:nl - 1] = prev[c][...]
                ref[:, nl - 1:nl] = proj_ref[:, col % tn:col % tn + w].reshape(nb, 1, t, w)


def inproj(x2, norm_g, scale, shift, w1, w2, rows_per_mod, kv=None):
    m = x2.shape[0]
    tm, tn = 1024, 1024
    nmod = scale.shape[0]
    mod_idx = lambda i, j: ((i * tm) // rows_per_mod, 0, 0)
    in_specs = [pl.BlockSpec((tm, D_MODEL), lambda i, j: (i, 0)),
                pl.BlockSpec((1, D_MODEL), lambda i, j: (0, 0)),
                pl.BlockSpec((1, 1, D_MODEL), mod_idx),
                pl.BlockSpec((1, 1, D_MODEL), mod_idx),
                pl.BlockSpec((D_MODEL, tn), lambda i, j: (0, j)),
                pl.BlockSpec((D_MODEL, 128), lambda i, j: (0, 0))]
    args = [x2, norm_g.reshape(1, D_MODEL), scale.reshape(nmod, 1, D_MODEL),
            shift.reshape(nmod, 1, D_MODEL), w1, w2]
    out_specs = [pl.BlockSpec((tm, tn), lambda i, j: (i, j)),
                 pl.BlockSpec((tm, 128), lambda i, j: (i, 0)),
                 pl.BlockSpec((tm, D_MODEL), lambda i, j: (i, 0))]
    out_shape = [jax.ShapeDtypeStruct((m, N_MAIN), F32),
                 jax.ShapeDtypeStruct((m, 128), F32),
                 jax.ShapeDtypeStruct((m, D_MODEL), BF16)]
    n_prev = 0
    if kv is not None:
        layer, t, caches = kv
        cspec = lambda nl: pl.BlockSpec((tm // t, nl, t, DA_WIDTH), lambda i, j: (i, 0, 0, 0))
        out_specs += [cspec(layer + 1)] * 2
        out_shape += [jax.ShapeDtypeStruct((m // t, layer + 1, t, DA_WIDTH), F32)] * 2
        if caches is not None:
            n_prev = 2
            in_specs += [cspec(layer)] * 2
            args += list(caches)
    return pl.pallas_call(
        functools.partial(_inproj_kernel, n_prev=n_prev, with_kv=kv is not None),
        grid=(m // tm, N_MAIN // tn),
        in_specs=in_specs,
        out_specs=out_specs,
        out_shape=out_shape,
        compiler_params=_cparams(("parallel", "arbitrary")),
        name="inproj",
    )(*args)


def _s5_gen_kernel(crt_ref, cit_ref, prt_ref, pit_ref, bbt_ref, bbs_ref, prow_ref, pirow_ref,
                   wt_ref, ws_ref, wh_ref):
    L, C, P = S5_CHUNK, S5_GROUP, S5_STATE
    width = (L + 1) * C
    row = lax.broadcasted_iota(jnp.int32, (128, width), 0)
    lane = lax.broadcasted_iota(jnp.int32, (128, width), 1)
    tile_c = (lane % C == row).astype(F32)
    expand = lambda a, e: jnp.dot(a, e, precision=HI, preferred_element_type=F32)
    zeros = jnp.zeros((C, L * C), F32)
    for g in range(S5_GEN_GROUPS):
        strips = []
        for d in range(2):
            spread_k = ((lane // C if d == 0 else L - lane // C) == row).astype(F32)
            crx, cix = expand(crt_ref[d, g], tile_c), expand(cit_ref[d, g], tile_c)
            prx, pix = expand(prt_ref[d, g], spread_k), expand(pit_ref[d, g], spread_k)
            ca = jnp.concatenate([crx * prx - cix * pix, -(crx * pix + cix * prx)], axis=0)
            wh_ref[d, g] = (ca[:, C:] if d == 0 else ca[:, :L * C]).astype(BF16)
            bbt, bbs = bbt_ref[d, g], bbs_ref[d, g]
            strips.append(jnp.dot(bbt, ca[:, :L * C] if d == 0 else ca[:, C:], precision=HI,
                                  preferred_element_type=F32))
            rows = []
            for i in range(L):
                k = L - 1 - i if d == 0 else i
                rows.append(bbt * prow_ref[d, g, k:k + 1, :] + bbs * pirow_ref[d, g, k:k + 1, :])
            ws_ref[d, g] = jnp.concatenate(rows, axis=0).astype(BF16)
        fpad = jnp.concatenate([zeros, strips[0]], axis=-1)
        rpad = jnp.concatenate([strips[1], zeros], axis=-1)
        rows = []
        for i in range(L):
            rows.append(fpad[:, (L - i) * C:(2 * L - i) * C] + rpad[:, (L - 1 - i) * C:(2 * L - 1 - i) * C])
        wt_ref[g] = jnp.concatenate(rows, axis=0).astype(BF16)


def s5_matrices(lam_re, lam_im, log_step, b_re, b_im, c_re, c_im):
    L, G, P, C = S5_CHUNK, S5_GROUPS, S5_STATE, S5_GROUP
    step = jnp.exp(log_step)[..., None]
    mag = jnp.exp(lam_re * step)
    ar, ai = mag * jnp.cos(lam_im * step), mag * jnp.sin(lam_im * step)
    den = lam_re * lam_re + lam_im * lam_im
    fr = ((ar - 1.0) * lam_re + ai * lam_im) / den
    fi = (ai * lam_re - (ar - 1.0) * lam_im) / den
    bbr = fr[..., None] * b_re - fi[..., None] * b_im
    bbi = fr[..., None] * b_im + fi[..., None] * b_re
    ks = jnp.arange(L + 1, dtype=F32)[None, None, None, :]
    pmag = jnp.exp(ks * (lam_re * step)[..., None])
    prt = pmag * jnp.cos(ks * (lam_im * step)[..., None])
    pit = pmag * jnp.sin(ks * (lam_im * step)[..., None])
    prow = jnp.swapaxes(prt, 2, 3)
    pirow = jnp.swapaxes(pit, 2, 3)
    bbrt, bbit = jnp.swapaxes(bbr, 2, 3), jnp.swapaxes(bbi, 2, 3)
    lanes = lambda a: jnp.pad(a, ((0, 0), (0, 0), (0, 0), (0, 128 - a.shape[-1])))
    args = [lanes(jnp.swapaxes(c_re, 2, 3)), lanes(jnp.swapaxes(c_im, 2, 3)), lanes(prt), lanes(pit),
            jnp.concatenate([bbrt, bbit], -1), jnp.concatenate([bbit, bbrt], -1),
            jnp.concatenate([prow, prow], -1), jnp.concatenate([-pirow, pirow], -1)]
    gg = S5_GEN_GROUPS
    spec = lambda a: pl.BlockSpec((2, gg) + a.shape[2:], lambda g: (0, g, 0, 0))
    wt, ws, wh = pl.pallas_call(
        _s5_gen_kernel,
        grid=(G // gg,),
        in_specs=[spec(a) for a in args],
        out_specs=[pl.BlockSpec((gg, S5_ROW, S5_ROW), lambda g: (g, 0, 0)),
                   pl.BlockSpec((2, gg, S5_ROW, 2 * P), lambda g: (0, g, 0, 0)),
                   pl.BlockSpec((2, gg, 2 * P, S5_ROW), lambda g: (0, g, 0, 0))],
        out_shape=[jax.ShapeDtypeStruct((G, S5_ROW, S5_ROW), BF16),
                   jax.ShapeDtypeStruct((2, G, S5_ROW, 2 * P), BF16),
                   jax.ShapeDtypeStruct((2, G, 2 * P, S5_ROW), BF16)],
        compiler_params=_cparams(("parallel",)),
        name="s5_gen",
    )(*args)
    a_l = jnp.stack([prt[..., L].reshape(2, 1, G * P), pit[..., L].reshape(2, 1, G * P)], 1)
    return wt, ws, wh, a_l


S5_GB = 8


def _s5_core_kernel(u_ref, ws_ref, wt_ref, wh_ref, a_ref, h0_ref, y_ref, hfin_ref, x_ref, s_ref, hin_ref,
                    *, nb, nchunk):
    r = nb * nchunk
    npair = S5_GB // 2
    us = [u_ref[pl.ds(i, r, stride=S5_CHUNK), :] for i in range(S5_CHUNK)]
    for g in range(S5_GB):
        xg = jnp.concatenate([u[:, S5_GROUP * g:S5_GROUP * (g + 1)] for u in us], axis=-1)
        x_ref[:, g * S5_ROW:(g + 1) * S5_ROW] = xg.astype(BF16)
    for p in range(npair):
        for d in range(2):
            sg = [jnp.dot(x_ref[:, (2 * p + k) * S5_ROW:(2 * p + k + 1) * S5_ROW], ws_ref[d, 2 * p + k],
                          preferred_element_type=F32) for k in range(2)]
            for comp in range(2):
                cs = slice(comp * S5_STATE, (comp + 1) * S5_STATE)
                s_ref[d, comp, p] = jnp.concatenate([sg[0][:, cs], sg[1][:, cs]], axis=-1)
    chains = [(d, p) for d in range(2) for p in range(npair)]
    h = {}
    for d, p in chains:
        cols = slice(p * 128, (p + 1) * 128)
        h[d, p] = (h0_ref[d, 0, :, cols], h0_ref[d, 1, :, cols], a_ref[d, 0, :, cols], a_ref[d, 1, :, cols])
    for step in range(nchunk):
        for d, p in chains:
            rows = pl.ds(step if d == 0 else nchunk - 1 - step, nb, stride=nchunk)
            hr, hi, ar, ai = h[d, p]
            hin_ref[d, 0, p, rows, :] = hr
            hin_ref[d, 1, p, rows, :] = hi
            sr, si = s_ref[d, 0, p, rows, :], s_ref[d, 1, p, rows, :]
            h[d, p] = (ar * hr - ai * hi + sr, ar * hi + ai * hr + si, ar, ai)
    for d, p in chains:
        cols = slice(p * 128, (p + 1) * 128)
        hfin_ref[d, 0, :, cols] = h[d, p][0]
        hfin_ref[d, 1, :, cols] = h[d, p][1]
    ys = []
    for g in range(S5_GB):
        y = jnp.dot(x_ref[:, g * S5_ROW:(g + 1) * S5_ROW], wt_ref[g], preferred_element_type=F32)
        cs = slice((g % 2) * S5_STATE, (g % 2 + 1) * S5_STATE)
        for d in range(2):
            hg = jnp.concatenate([hin_ref[d, 0, g // 2, :, cs], hin_ref[d, 1, g // 2, :, cs]], axis=-1)
            y = y + jnp.dot(hg.astype(BF16), wh_ref[d, g], preferred_element_type=F32)
        ys.append(y)
    for j in range(S5_CHUNK):
        y_ref[pl.ds(j, r, stride=S5_CHUNK), :] = jnp.concatenate(
            [y[:, S5_GROUP * j:S5_GROUP * (j + 1)] for y in ys], axis=-1)


def s5_core(proj, mats, h0, nb, nchunk):
    wt, ws, wh, a_l = mats
    r = nb * nchunk
    sw = S5_GB * S5_STATE
    return pl.pallas_call(
        functools.partial(_s5_core_kernel, nb=nb, nchunk=nchunk),
        grid=(S5_GROUPS // S5_GB,),
        in_specs=[pl.BlockSpec((r * S5_CHUNK, 128), lambda j: (0, COL_UA // 128 + j)),
                  pl.BlockSpec((2, S5_GB, S5_ROW, 2 * S5_STATE), lambda j: (0, j, 0, 0)),
                  pl.BlockSpec((S5_GB, S5_ROW, S5_ROW), lambda j: (j, 0, 0)),
                  pl.BlockSpec((2, S5_GB, 2 * S5_STATE, S5_ROW), lambda j: (0, j, 0, 0)),
                  pl.BlockSpec((2, 2, 1, sw), lambda j: (0, 0, 0, j)),
                  pl.BlockSpec((2, 2, nb, sw), lambda j: (0, 0, 0, j))],
        out_specs=[pl.BlockSpec((r * S5_CHUNK, 128), lambda j: (0, j)),
                   pl.BlockSpec((2, 2, nb, sw), lambda j: (0, 0, 0, j))],
        out_shape=[jax.ShapeDtypeStruct((r * S5_CHUNK, S5_WIDTH), F32),
                   jax.ShapeDtypeStruct((2, 2, nb, S5_GROUPS * S5_STATE), F32)],
        scratch_shapes=[pltpu.VMEM((r, S5_GB * S5_ROW), BF16),
                        pltpu.VMEM((2, 2, S5_GB // 2, r, 128), F32),
                        pltpu.VMEM((2, 2, S5_GB // 2, r, 128), F32)],
        compiler_params=_cparams(("parallel",)),
        name="s5_core",
    )(proj, ws, wt, wh, a_l, h0)


def _s5_epilogue_kernel(u_ref, y_ref, z_ref, d_ref, w_ref, o_ref):
    ya = jax.nn.gelu(d_ref[...] * u_ref[...] + y_ref[...])
    ya = ya * jax.nn.sigmoid(_mm(ya, w_ref[...]))
    o_ref[...] = (ya * _silu(z_ref[...])).astype(BF16)


def s5_epilogue(proj, y_s5, s5_d, w_glu):
    m = proj.shape[0]
    tm = 512
    return pl.pallas_call(
        _s5_epilogue_kernel,
        grid=(m // tm,),
        in_specs=[pl.BlockSpec((tm, S5_WIDTH), lambda i: (i, COL_UA // S5_WIDTH)),
                  pl.BlockSpec((tm, S5_WIDTH), lambda i: (i, 0)),
                  pl.BlockSpec((tm, S5_WIDTH), lambda i: (i, COL_ZA // S5_WIDTH)),
                  pl.BlockSpec((1, S5_WIDTH), lambda i: (0, 0)),
                  pl.BlockSpec((S5_WIDTH, S5_WIDTH), lambda i: (0, 0))],
        out_specs=pl.BlockSpec((tm, S5_WIDTH), lambda i: (i, 0)),
        out_shape=jax.ShapeDtypeStruct((m, S5_WIDTH), BF16),
        compiler_params=_cparams(("parallel",)),
        name="s5_epilogue",
    )(proj, y_s5, proj, s5_d.reshape(1, S5_WIDTH), w_glu)


def s5_branch(proj, nb, t, mats, s5_d, w_glu, h0):
    y, hfin = s5_core(proj, mats, h0, nb, t // S5_CHUNK)
    return s5_epilogue(proj, y, s5_d, w_glu), hfin


def _rope_tables(t):
    rows = t // GRID_W
    row = np.repeat(np.arange(rows), GRID_W).astype(np.float32)
    col = np.tile(np.arange(GRID_W), rows).astype(np.float32)
    nf = DA_HEAD // 4
    inv = (ROPE_BASE ** (-jnp.arange(nf, dtype=F32) / nf))

    def tab(pos):
        ang = jnp.asarray(pos)[:, None] * inv[None, :]
        c, s = jnp.cos(ang), jnp.sin(ang)
        return jnp.concatenate([c, c], -1), jnp.concatenate([-s, s], -1)

    cr, sr = tab(row)
    cc, sc = tab(col)
    cos = jnp.concatenate([cr, cc], -1)
    sin = jnp.concatenate([sr, sc], -1)
    return jnp.tile(cos, (1, 2)), jnp.tile(sin, (1, 2))


def _rope(x, cos, sin):
    lane = lax.broadcasted_iota(jnp.int32, x.shape, 1)
    swapped = jnp.where((lane % 32) < 16, pltpu.roll(x, 112, 1), pltpu.roll(x, 16, 1))
    return x * cos + swapped * sin


def _attn_kernel(*refs, lam_init, t, s_tot, tq, with_ctx):
    if with_ctx:
        (q_ref, k_ref, v_ref, z_ref, kc_ref, vc_ref, cq_ref, sq_ref, ck_ref, sk_ref,
         lam_ref, ng_ref, o_ref, kall_ref, vall_ref, s_ref, e_ref, l_ref) = refs
    else:
        q_ref, k_ref, v_ref, z_ref, lam_ref, ng_ref, o_ref, kall_ref, vall_ref, s_ref, e_ref, l_ref = refs

    @pl.when(pl.program_id(1) == 0)
    def _():
        if with_ctx:
            for h in range(DA_HEADS):
                hs = slice(h * 128, (h + 1) * 128)
                kall_ref[0:t, hs] = _rope(k_ref[0, :, hs], ck_ref[...], sk_ref[...]).astype(BF16)
            kall_ref[t:s_tot, :] = kc_ref[0].astype(BF16)
            vall_ref[0:t, :] = v_ref[0].astype(BF16)
            vall_ref[t:s_tot, :] = vc_ref[0].astype(BF16)
        else:
            kall_ref[...] = k_ref[0].astype(BF16)
            vall_ref[...] = v_ref[0].astype(BF16)

    lp = lam_ref[...]
    lam = (jnp.exp(jnp.sum(lp[0:1] * lp[1:2], axis=-1, keepdims=True))
           - jnp.exp(jnp.sum(lp[2:3] * lp[3:4], axis=-1, keepdims=True)) + lam_init)
    lane = lax.broadcasted_iota(jnp.int32, (tq, 128), 1)
    def scores(h):
        hs = slice(h * 128, (h + 1) * 128)
        q = q_ref[0, :, hs]
        if with_ctx:
            q = _rope(q, cq_ref[...], sq_ref[...])
        q = q * (DA_HEAD ** -0.5 * LOG2E)
        qs = jnp.concatenate([jnp.where(lane < DA_HEAD, q, 0.0), jnp.where(lane >= DA_HEAD, q, 0.0)], axis=0)
        s_ref[h % 2] = _mm_nt(qs, kall_ref[:, hs])

    scores(0)
    for h in range(DA_HEADS):
        hs = slice(h * 128, (h + 1) * 128)
        b = h % 2
        if h + 1 < DA_HEADS:
            scores(h + 1)
        for r in range(0, 2 * tq, ATTN_ROWS):
            sc = s_ref[b, r:r + ATTN_ROWS, :]
            e = jnp.exp2(sc - jnp.max(sc, axis=-1, keepdims=True))
            e_ref[b, r:r + ATTN_ROWS, :] = e.astype(BF16)
            l_ref[b, r:r + ATTN_ROWS, :] = jnp.broadcast_to(jnp.sum(e, axis=-1, keepdims=True),
                                                            (ATTN_ROWS, 128))
        ov = jnp.dot(e_ref[b], vall_ref[:, hs], preferred_element_type=F32) / l_ref[b]
        o = ov[:tq] - lam * ov[tq:]
        o = o * lax.rsqrt(jnp.mean(o * o, axis=-1, keepdims=True) + EPS) * ng_ref[...]
        o = o * (1.0 - lam_init)
        o_ref[0, :, hs] = (o * _silu(z_ref[0, :, hs])).astype(BF16)


def diff_attention(proj3, layer, lam_init, da_lam, da_norm_g, ctx_kv):
    nb, t, _ = proj3.shape
    with_ctx = ctx_kv is not None
    s_tot = t + (PAST_LEN if with_ctx else 0)
    tq = 256
    wb = DA_WIDTH
    in_specs = [pl.BlockSpec((1, tq, wb), lambda b, i: (b, i, COL_QB // wb)),
                pl.BlockSpec((1, t, wb), lambda b, i: (b, 0, COL_KB // wb)),
                pl.BlockSpec((1, t, wb), lambda b, i: (b, 0, COL_VB // wb)),
                pl.BlockSpec((1, tq, wb), lambda b, i: (b, i, COL_ZB // wb))]
    args = [proj3, proj3, proj3, proj3]
    if with_ctx:
        kc, vc = ctx_kv
        cos, sin = _rope_tables(t)
        in_specs += [pl.BlockSpec((1, PAST_LEN, wb), lambda b, i: (b, layer, 0)),
                     pl.BlockSpec((1, PAST_LEN, wb), lambda b, i: (b, layer, 0)),
                     pl.BlockSpec((tq, 128), lambda b, i: (i, 0)),
                     pl.BlockSpec((tq, 128), lambda b, i: (i, 0)),
                     pl.BlockSpec((t, 128), lambda b, i: (0, 0)),
                     pl.BlockSpec((t, 128), lambda b, i: (0, 0))]
        args += [kc, vc, cos, sin, cos, sin]
    in_specs += [pl.BlockSpec((4, DA_HEAD), lambda b, i: (0, 0)),
                 pl.BlockSpec((1, DA_VDIM), lambda b, i: (0, 0))]
    args += [da_lam, da_norm_g.reshape(1, DA_VDIM)]
    return pl.pallas_call(
        functools.partial(_attn_kernel, lam_init=lam_init, t=t, s_tot=s_tot, tq=tq, with_ctx=with_ctx),
        grid=(nb, t // tq),
        in_specs=in_specs,
        out_specs=pl.BlockSpec((1, tq, wb), lambda b, i: (b, i, 0)),
        out_shape=jax.ShapeDtypeStruct((nb, t, wb), BF16),
        scratch_shapes=[pltpu.VMEM((s_tot, wb), BF16), pltpu.VMEM((s_tot, wb), BF16),
                        pltpu.VMEM((2, 2 * tq, s_tot), F32), pltpu.VMEM((2, 2 * tq, s_tot), BF16),
                        pltpu.VMEM((2, 2 * tq, 128), F32)],
        compiler_params=_cparams(("parallel", "arbitrary")),
        name="diff_attention",
    )(*args)


DN_PAD = 8
DN_RT = 128
DN_GROUP = 128
DN_STEP_GROUPS = 2


def _dn_kernel(*refs, t, ngroup, with_s0):
    if with_s0:
        (qkv_ref, z_ref, ba_ref, cw_ref, alog_ref, dtb_ref, ng_ref, s0_ref,
         out_ref, sfin_ref, xp_ref, qkvn_ref, oacc_ref, st_ref) = refs
    else:
        (qkv_ref, z_ref, ba_ref, cw_ref, alog_ref, dtb_ref, ng_ref,
         out_ref, sfin_ref, xp_ref, qkvn_ref, oacc_ref, st_ref) = refs
    n = pl.program_id(1)
    cd = DN_CHUNK
    w3 = 3 * DN_WIDTH

    @pl.when(n == 0)
    def _init():
        xp_ref[0:DN_PAD, :] = jnp.zeros((DN_PAD, w3), F32)
        xp_ref[DN_PAD + t:2 * DN_PAD + t, :] = jnp.zeros((DN_PAD, w3), F32)
        xp_ref[DN_PAD:DN_PAD + t, :] = qkv_ref[0]
        half = DN_CONV // 2
        for r in range(t // DN_RT):
            for sec in range(3):
                for h in range(DN_HEADS):
                    cs = slice(sec * DN_WIDTH + h * DN_HEAD, sec * DN_WIDTH + (h + 1) * DN_HEAD)
                    acc = jnp.zeros((DN_RT, DN_HEAD), F32)
                    for j in range(DN_CONV):
                        r0 = DN_PAD + r * DN_RT + j - half
                        acc = acc + xp_ref[r0:r0 + DN_RT, cs] * cw_ref[j:j + 1, cs]
                    y = _silu(acc)
                    if sec < 2:
                        y = y * lax.rsqrt(jnp.sum(y * y, axis=-1, keepdims=True) + EPS)
                    if sec == 0:
                        y = y * (DN_HEAD ** -0.5)
                    qkvn_ref[r * DN_RT:(r + 1) * DN_RT, cs] = y
        oacc_ref[...] = jnp.zeros_like(oacc_ref)
        if with_s0:
            st_ref[...] = s0_ref[0, 0]
        else:
            st_ref[...] = jnp.zeros_like(st_ref)

    gb = DN_GROUP
    nsub = gb // cd
    ri = lax.broadcasted_iota(jnp.int32, (gb, gb), 0)
    ci = lax.broadcasted_iota(jnp.int32, (gb, gb), 1)
    same = (ri // cd) == (ci // cd)
    samef = same.astype(F32)
    eye = (ri == ci).astype(BF16)
    masked_out = -1e30

    chains = []
    for d, gi in [(d, gi) for d in range(2) for gi in range(DN_STEP_GROUPS)]:
        grp = n * DN_STEP_GROUPS + gi
        r0 = pl.multiple_of((grp if d == 0 else ngroup - 1 - grp) * gb, gb)
        incl = same & ((ci <= ri) if d == 0 else (ci >= ri))
        strict = same & ((ci < ri) if d == 0 else (ci > ri))
        inclf = incl.astype(F32)
        inclog = jnp.where(incl, 0.0, masked_out)
        nstrict = -strict.astype(BF16)
        ba = ba_ref[0, pl.ds(r0, gb), :]
        beta_all = jax.nn.sigmoid(ba)
        g_all = -jnp.exp(alog_ref[...]) * jax.nn.softplus(ba + dtb_ref[...])
        gc = jnp.dot(inclf, g_all, precision=HI, preferred_element_type=F32)
        gct = lax.dot_general(g_all.T, inclf, (((1,), (1,)), ((), ())), precision=HI,
                              preferred_element_type=F32)
        gtot = jnp.dot(samef, g_all, precision=HI, preferred_element_type=F32)
        for h in range(DN_HEADS):
            hs = slice(h * DN_HEAD, (h + 1) * DN_HEAD)
            col = 2 * DN_HEADS + d * DN_HEADS + h
            gcol = gc[:, col:col + 1]
            grow = gct[col:col + 1, :]
            gt = gtot[:, col:col + 1]
            beta = beta_all[:, d * DN_HEADS + h:d * DN_HEADS + h + 1]
            q = qkvn_ref[pl.ds(r0, gb), hs]
            k = qkvn_ref[pl.ds(r0, gb), DN_WIDTH + h * DN_HEAD:DN_WIDTH + (h + 1) * DN_HEAD]
            v = qkvn_ref[pl.ds(r0, gb), 2 * DN_WIDTH + h * DN_HEAD:2 * DN_WIDTH + (h + 1) * DN_HEAD]
            eg = jnp.exp(gcol)
            chains.append(dict(
                d=d, gi=gi, h=h, r0=r0, hs=hs, nstrict=nstrict, q=q, k=k, kb=k * beta,
                decay=jnp.exp((gcol - grow) + inclog),
                rhs=jnp.concatenate([v * beta, k * beta * eg], axis=-1),
                qe=q * eg, kdec=k * jnp.exp(gt - gcol), egt=jnp.exp(gt)))

    for c in chains:
        c["nmm"] = (_mm_nt(c["kb"], c["k"]) * c["decay"]).astype(BF16) * c["nstrict"]
    for c in chains:
        c["qk"] = _mm_nt(c["q"], c["k"]) * c["decay"]
    def level_mask(s):
        return (((ri // (2 * s)) == (ci // (2 * s))) & ((ri // s) != (ci // s))).astype(BF16)

    pair = level_mask(1)
    for c in chains:
        c["tinv"] = eye + c["nmm"] * pair
    s = 2
    while s < cd:
        offmask = level_mask(s)
        xs = [jnp.dot(c["nmm"] * offmask, c["tinv"], preferred_element_type=F32).astype(BF16) for c in chains]
        ys = [jnp.dot(c["tinv"], x, preferred_element_type=F32) for c, x in zip(chains, xs)]
        for c, y in zip(chains, ys):
            c["tinv"] = c["tinv"] + y.astype(BF16)
        s *= 2
    for c in chains:
        c["uw"] = _mm(c["tinv"], c["rhs"])

    st = {(d, h): st_ref[d, h] for d in range(2) for h in range(DN_HEADS)}
    outs = []
    for gi, step in [(gi, step) for gi in range(DN_STEP_GROUPS) for step in range(nsub)]:
        active = [c for c in chains if c["gi"] == gi]
        rs = []
        for c in active:
            sub = step if c["d"] == 0 else nsub - 1 - step
            c["rows"] = slice(sub * cd, (sub + 1) * cd)
            rs.append(_mm(jnp.concatenate([c["uw"][c["rows"], DN_HEAD:], c["qe"][c["rows"]]], axis=0),
                          st[c["d"], c["h"]]))
        for c, r in zip(active, rs):
            rows = c["rows"]
            v_new = c["uw"][rows, :DN_HEAD] - r[:cd]
            o = r[cd:] + _mm(c["qk"][rows, rows], v_new)
            st[c["d"], c["h"]] = (st[c["d"], c["h"]] * c["egt"][rows.start:rows.start + 1]
                                  + _mm(c["kdec"][rows].T, v_new))
            outs.append((c, rows.start, o))
    for (d, h), v in st.items():
        st_ref[d, h] = v
    for c, off, o in outs:
        oacc_ref[pl.ds(pl.multiple_of(c["r0"] + off, cd), cd), c["hs"]] += o

    @pl.when(n == ngroup // DN_STEP_GROUPS - 1)
    def _fin():
        for h in range(DN_HEADS):
            hs = slice(h * DN_HEAD, (h + 1) * DN_HEAD)
            o = oacc_ref[:, hs]
            o = o * lax.rsqrt(jnp.mean(o * o, axis=-1, keepdims=True) + EPS) * ng_ref[...]
            out_ref[0, :, hs] = (o * _silu(z_ref[0, :, hs])).astype(BF16)
        sfin_ref[0] = st_ref[...]


def deltanet(proj3, ba3, layer, conv_w, a_log, dt_bias, norm_g, s0):
    nb, t, _ = proj3.shape
    ngroup = t // DN_GROUP
    with_s0 = s0 is not None
    w3 = 3 * DN_WIDTH
    pad = jnp.zeros((2 * DN_HEADS,), F32)
    alog_row = jnp.concatenate([pad, a_log.reshape(-1), jnp.zeros((128 - 4 * DN_HEADS,), F32)]).reshape(1, 128)
    dtb_row = jnp.concatenate([pad, dt_bias.reshape(-1), jnp.zeros((128 - 4 * DN_HEADS,), F32)]).reshape(1, 128)
    in_specs = [pl.BlockSpec((1, t, w3), lambda b, n: (b, 0, COL_QC // w3)),
                pl.BlockSpec((1, t, DN_WIDTH), lambda b, n: (b, 0, COL_ZC // DN_WIDTH)),
                pl.BlockSpec((1, t, 128), lambda b, n: (b, 0, 0)),
                pl.BlockSpec((8, w3), lambda b, n: (0, 0)),
                pl.BlockSpec((1, 128), lambda b, n: (0, 0)),
                pl.BlockSpec((1, 128), lambda b, n: (0, 0)),
                pl.BlockSpec((1, DN_HEAD), lambda b, n: (0, 0))]
    args = [proj3, proj3, ba3, jnp.pad(conv_w, ((0, 8 - DN_CONV), (0, 0))), alog_row, dtb_row,
            norm_g.reshape(1, DN_HEAD)]
    if with_s0:
        in_specs.append(pl.BlockSpec((1, 1, 2, DN_HEADS, DN_HEAD, DN_HEAD), lambda b, n: (b, layer, 0, 0, 0, 0)))
        args.append(s0)
    return pl.pallas_call(
        functools.partial(_dn_kernel, t=t, ngroup=ngroup, with_s0=with_s0),
        grid=(nb, ngroup // DN_STEP_GROUPS),
        in_specs=in_specs,
        out_specs=[pl.BlockSpec((1, t, DN_WIDTH), lambda b, n: (b, 0, 0)),
                   pl.BlockSpec((1, 2, DN_HEADS, DN_HEAD, DN_HEAD), lambda b, n: (b, 0, 0, 0, 0))],
        out_shape=[jax.ShapeDtypeStruct((nb, t, DN_WIDTH), BF16),
                   jax.ShapeDtypeStruct((nb, 2, DN_HEADS, DN_HEAD, DN_HEAD), F32)],
        scratch_shapes=[pltpu.VMEM((t + 2 * DN_PAD, w3), F32),
                        pltpu.VMEM((t, w3), F32),
                        pltpu.VMEM((t, DN_WIDTH), F32),
                        pltpu.VMEM((2, DN_HEADS, DN_HEAD, DN_HEAD), F32)],
        compiler_params=_cparams(("parallel", "arbitrary")),
        name="deltanet",
    )(*args)


def _merge_kernel(oa_ref, ob_ref, oc_ref, hn_ref, wg_ref, wb_ref, wo_ref, x_ref, gate_ref, fg_ref,
                  *outs, final):
    tm = x_ref.shape[0]
    for rows in (slice(0, tm // 2), slice(tm // 2, tm)):
        acc = None
        hn = hn_ref[rows, :]
        for i, o_ref in enumerate((oa_ref, ob_ref, oc_ref)):
            pr = jnp.dot(o_ref[rows, :], wb_ref[i], preferred_element_type=F32)
            gt = jnp.dot(hn, wg_ref[:, i * D_MODEL:(i + 1) * D_MODEL], preferred_element_type=F32)
            term = jax.nn.sigmoid(gt) * pr
            acc = term if acc is None else acc + term
        y = jnp.dot(acc.astype(BF16), wo_ref[...], preferred_element_type=F32)
        xn = x_ref[rows, :] + gate_ref[0] * y
        outs[0][rows, :] = xn
        if final:
            yn = xn * lax.rsqrt(jnp.mean(xn * xn, axis=-1, keepdims=True) + EPS) * fg_ref[...]
            outs[1][rows, :] = yn


def merge(out_a, out_b, out_c, hn, w_gates, w_branch, w_out, x2, gate, final_g, rows_per_mod, final):
    m = x2.shape[0]
    tm = 512
    nmod = gate.shape[0]
    row = lambda i: (i, 0)
    out_specs = [pl.BlockSpec((tm, D_MODEL), row)]
    out_shape = [jax.ShapeDtypeStruct((m, D_MODEL), F32)]
    if final:
        out_specs.append(pl.BlockSpec((tm, D_MODEL), row))
        out_shape.append(jax.ShapeDtypeStruct((m, D_MODEL), F32))
    return pl.pallas_call(
        functools.partial(_merge_kernel, final=final),
        grid=(m // tm,),
        in_specs=[pl.BlockSpec((tm, BRANCH_WIDTH), row),
                  pl.BlockSpec((tm, BRANCH_WIDTH), row),
                  pl.BlockSpec((tm, BRANCH_WIDTH), row),
                  pl.BlockSpec((tm, D_MODEL), row),
                  pl.BlockSpec((D_MODEL, N_BRANCH * D_MODEL), lambda i: (0, 0)),
                  pl.BlockSpec((N_BRANCH, BRANCH_WIDTH, D_MODEL), lambda i: (0, 0, 0)),
                  pl.BlockSpec((D_MODEL, D_MODEL), lambda i: (0, 0)),
                  pl.BlockSpec((tm, D_MODEL), row),
                  pl.BlockSpec((1, 1, D_MODEL), lambda i: ((i * tm) // rows_per_mod, 0, 0)),
                  pl.BlockSpec((1, D_MODEL), lambda i: (0, 0))],
        out_specs=out_specs,
        out_shape=out_shape,
        compiler_params=_cparams(("parallel",)),
        name="merge",
    )(out_a, out_b, out_c, hn, w_gates, w_branch, w_out, x2, gate.reshape(nmod, 1, D_MODEL),
      final_g.reshape(1, D_MODEL))


def _run_pass(x, mod, wts, lam_inits, final_g, ctx):
    nb, t, _ = x.shape
    m = nb * t
    nmod = mod.shape[1]
    rows_per_mod = m if nmod == 1 else t
    x2 = x.reshape(m, D_MODEL)
    states = []
    y, caches = None, None
    for l in range(DEPTH):
        w = wts[l]
        shift, scale, gate = jnp.split(mod[l], 3, axis=-1)
        if ctx is None:
            proj, ba, hn, *caches = inproj(x2, w["norm_g"], scale, shift, w["w1"], w["w2"], rows_per_mod,
                                       kv=(l, t, caches))
            h0 = jnp.zeros((2, 2, nb, S5_GROUPS * S5_STATE), F32)
            ctx_kv, s0 = None, None
        else:
            proj, ba, hn = inproj(x2, w["norm_g"], scale, shift, w["w1"], w["w2"], rows_per_mod)
            cache_k, cache_v, st_re, st_im, st_dn = ctx
            h0 = jnp.stack([st_re[:, l], st_im[:, l]], 0)
            h0 = jnp.transpose(h0, (2, 0, 1, 3, 4)).reshape(2, 2, nb, S5_GROUPS * S5_STATE)
            ctx_kv, s0 = (cache_k, cache_v), st_dn
        proj3 = proj.reshape(nb, t, N_MAIN)
        out_a, hfin = s5_branch(proj, nb, t, w["s5_mats"], w["s5_d"], w["w_glu"], h0)
        out_b = diff_attention(proj3, l, lam_inits[l], w["da_lam"], w["da_norm_g"], ctx_kv)
        out_c, sfin = deltanet(proj3, ba.reshape(nb, t, 128), l, w["dn_conv"], w["dn_a_log"],
                               w["dn_dt_bias"], w["dn_norm_g"], s0)
        final = l == DEPTH - 1
        res = merge(out_a, out_b.reshape(m, DA_WIDTH), out_c.reshape(m, DN_WIDTH), hn, w["w_gates"],
                    w["w_branch"], w["w_out"], x2, gate, final_g, rows_per_mod, final)
        x2 = res[0]
        if final:
            y = res[1]
        if ctx is None:
            hf = hfin.reshape(2, 2, nb, S5_GROUPS, S5_STATE)
            states.append((jnp.transpose(hf[:, 0], (1, 0, 2, 3)), jnp.transpose(hf[:, 1], (1, 0, 2, 3)), sfin))
    return y.reshape(nb, t, D_MODEL), states, caches


def kernel(x_prompt, x_sample, cache_k, cache_v, state_s5_re, state_s5_im, state_dn, c, c_ctx,
           norm_g, w_ada, b_ada, w_in, s5_lam_re, s5_lam_im, s5_log_step, s5_b_re, s5_b_im,
           s5_c_re, s5_c_im, s5_d, s5_w_glu, da_lam, da_norm_g, dn_conv, dn_a_log, dn_dt_bias,
           dn_norm_g, w_branch, w_out, final_norm_g):
    nb_dec = x_sample.shape[0]
    cond8 = jnp.concatenate([c_ctx[None, :], c, jnp.zeros((8 - 1 - nb_dec, D_MODEL), F32)], 0)
    mod = ada_mod(cond8, w_ada, b_ada)
    w1, w_gates = cast_w_in(w_in)
    w2 = jnp.pad(w_in[:, :, BA_OFF:GATES_OFF], ((0, 0), (0, 0), (0, 128 - (GATES_OFF - BA_OFF)))).astype(BF16)
    wts = []
    for l in range(DEPTH):
        wts.append(dict(
            norm_g=norm_g[l], w1=w1[l], w2=w2[l], w_gates=w_gates[l],
            s5_mats=s5_matrices(s5_lam_re[l], s5_lam_im[l], s5_log_step[l], s5_b_re[l], s5_b_im[l],
                                s5_c_re[l], s5_c_im[l]),
            s5_d=s5_d[l], w_glu=s5_w_glu[l].astype(BF16), da_lam=da_lam[l], da_norm_g=da_norm_g[l],
            dn_conv=dn_conv[l], dn_a_log=dn_a_log[l], dn_dt_bias=dn_dt_bias[l], dn_norm_g=dn_norm_g[l],
            w_branch=w_branch[l].astype(BF16), w_out=w_out[l].astype(BF16)))
    lam_inits = [0.8 - 0.6 * math.exp(-0.3 * l) for l in range(DEPTH)]

    y_prompt, states, (k_new, v_new) = _run_pass(x_prompt, mod[:, 0:1], wts, lam_inits, final_norm_g, None)
    ctx = (cache_k.reshape(nb_dec, DEPTH * PAST_LEN, DA_WIDTH),
           cache_v.reshape(nb_dec, DEPTH * PAST_LEN, DA_WIDTH), state_s5_re, state_s5_im, state_dn)
    y_sample, _, _ = _run_pass(x_sample, mod[:, 1:1 + nb_dec], wts, lam_inits, final_norm_g, ctx)

    nb, t = x_prompt.shape[:2]
    new_cache_k = k_new.reshape(nb, DEPTH, t, DA_HEADS, 2, DA_HEAD)
    new_cache_v = v_new.reshape(nb, DEPTH, t, DA_HEADS, DA_VDIM)
    new_s5_re = jnp.stack([s[0] for s in states], axis=1)
    new_s5_im = jnp.stack([s[1] for s in states], axis=1)
    new_dn = jnp.stack([s[2] for s in states], axis=1)
    return (y_prompt, y_sample, new_cache_k, new_cache_v, new_s5_re, new_s5_im, new_dn)
```

```python
import functools
import math

import numpy as np
import jax
import jax.numpy as jnp
from jax import lax
from jax.experimental import pallas as pl
from jax.experimental.pallas import tpu as pltpu

F32 = jnp.float32
BF16 = jnp.bfloat16

D_MODEL = 1024
DEPTH = 2
GRID_W = 64
EPS = 1e-6
S5_WIDTH = 512
S5_GROUP = 16
S5_GROUPS = 32
S5_STATE = 64
S5_CHUNK = 16
S5_PAIRS = S5_GROUPS // 2
S5_ROW = S5_CHUNK * S5_GROUP
S5_GEN_GROUPS = 4
DA_HEADS = 4
DA_HEAD = 64
DA_VDIM = 128
DA_WIDTH = 512
ROPE_BASE = 10000.0
DN_HEADS = 4
DN_HEAD = 128
DN_WIDTH = 512
DN_CONV = 5
DN_CHUNK = 64
N_BRANCH = 3
BRANCH_WIDTH = 512
PAST_LEN = 512

COL_UA, COL_ZA, COL_QB, COL_KB, COL_VB, COL_ZB = 0, 512, 1024, 1536, 2048, 2560
COL_QC, COL_ZC = 3072, 4608
N_MAIN = 5120
BA_OFF = 5120
GATES_OFF = 5136

VMEM_LIMIT = 56 * 1024 * 1024
HI = lax.Precision.HIGHEST
LOG2E = math.log2(math.e)
ATTN_ROWS = 16


def _cparams(sem):
    return pltpu.CompilerParams(dimension_semantics=sem, vmem_limit_bytes=VMEM_LIMIT)


def _mm(a, b):
    return jnp.dot(a.astype(BF16), b.astype(BF16), preferred_element_type=F32)


def _mm_nt(a, b):
    return lax.dot_general(a.astype(BF16), b.astype(BF16), (((1,), (1,)), ((), ())),
                           preferred_element_type=F32)


def _silu(x):
    return x * jax.nn.sigmoid(x)


def _ada_kernel(c_ref, w_ref, b_ref, o_ref):
    o_ref[0] = _mm(_silu(c_ref[...]), w_ref[0]) + b_ref[0]


def ada_mod(cond8, w_ada, b_ada):
    tn = 1024
    return pl.pallas_call(
        _ada_kernel,
        grid=(DEPTH, 3 * D_MODEL // tn),
        in_specs=[pl.BlockSpec((8, D_MODEL), lambda l, j: (0, 0)),
                  pl.BlockSpec((1, D_MODEL, tn), lambda l, j: (l, 0, j)),
                  pl.BlockSpec((1, 1, tn), lambda l, j: (l, 0, j))],
        out_specs=pl.BlockSpec((1, 8, tn), lambda l, j: (l, 0, j)),
        out_shape=jax.ShapeDtypeStruct((DEPTH, 8, 3 * D_MODEL), F32),
        compiler_params=_cparams(("parallel", "parallel")),
        name="ada_mod",
    )(cond8, w_ada, b_ada.reshape(DEPTH, 1, 3 * D_MODEL))


def _tcast_kernel(w_ref, o_ref, *, keep):
    x = w_ref[0].T
    if keep is not None:
        x = jnp.where(lax.broadcasted_iota(jnp.int32, x.shape, 1) < keep, x, 0.0)
    o_ref[0] = x.astype(BF16)


def _tcast_rows_kernel(w_hbm, o_ref, buf, sem, *, row0):
    l, j = pl.program_id(0), pl.program_id(1)
    tn = buf.shape[0]
    cp = pltpu.make_async_copy(w_hbm.at[l, pl.ds(pl.multiple_of(row0 + j * tn, 8), tn), :], buf, sem)
    cp.start()
    cp.wait()
    o_ref[0] = buf[...].T.astype(BF16)


def cast_w_in(w_in):
    wt = jnp.swapaxes(w_in, 1, 2)
    tn = 1024
    w1 = pl.pallas_call(
        functools.partial(_tcast_kernel, keep=None),
        grid=(DEPTH, N_MAIN // tn),
        in_specs=[pl.BlockSpec((1, tn, D_MODEL), lambda l, j: (l, j, 0))],
        out_specs=pl.BlockSpec((1, D_MODEL, tn), lambda l, j: (l, 0, j)),
        out_shape=jax.ShapeDtypeStruct((DEPTH, D_MODEL, N_MAIN), BF16),
        compiler_params=_cparams(("parallel", "parallel")),
        name="cast_w1",
    )(wt)
    gw = N_BRANCH * D_MODEL
    w_gates = pl.pallas_call(
        functools.partial(_tcast_rows_kernel, row0=GATES_OFF),
        grid=(DEPTH, gw // tn),
        in_specs=[pl.BlockSpec(memory_space=pl.ANY)],
        out_specs=pl.BlockSpec((1, D_MODEL, tn), lambda l, j: (l, 0, j)),
        out_shape=jax.ShapeDtypeStruct((DEPTH, D_MODEL, gw), BF16),
        scratch_shapes=[pltpu.VMEM((tn, D_MODEL), F32), pltpu.SemaphoreType.DMA(())],
        compiler_params=_cparams(("parallel", "parallel")),
        name="cast_w_gates",
    )(wt)
    w_ba = pl.pallas_call(
        functools.partial(_tcast_kernel, keep=GATES_OFF - BA_OFF),
        grid=(DEPTH,),
        in_specs=[pl.BlockSpec((1, 128, D_MODEL), lambda l: (l, BA_OFF // 128, 0))],
        out_specs=pl.BlockSpec((1, D_MODEL, 128), lambda l: (l, 0, 0)),
        out_shape=jax.ShapeDtypeStruct((DEPTH, D_MODEL, 128), BF16),
        compiler_params=_cparams(("parallel",)),
        name="cast_w_ba",
    )(wt)
    return w1, w_gates, w_ba


def _inproj_kernel(*refs, n_prev, with_kv):
    x_ref, g_ref, sc_ref, sh_ref, w1_ref, w2_ref = refs[:6]
    prev = refs[6:6 + n_prev]
    outs = refs[6 + n_prev:]
    proj_ref, ba_ref, hn_ref = outs[:3]
    j = pl.program_id(1)

    @pl.when(j == 0)
    def _():
        x = x_ref[...]
        y = x * lax.rsqrt(jnp.mean(x * x, axis=-1, keepdims=True) + EPS) * g_ref[...]
        hn = (y * (1.0 + sc_ref[0]) + sh_ref[0]).astype(BF16)
        hn_ref[...] = hn
        ba_ref[...] = jnp.dot(hn, w2_ref[...], preferred_element_type=F32)

    proj_ref[...] = jnp.dot(hn_ref[...], w1_ref[...], preferred_element_type=F32)

    if with_kv:
        tn = proj_ref.shape[1]
        for c, (ref, col) in enumerate(((outs[3], COL_KB), (outs[4], COL_VB))):
            @pl.when(j == col // tn)
            def _(c=c, ref=ref, col=col):
                nb, nl, t, w = ref.shape
                if n_prev:
                    ref[:, 0:nl - 1] = prev[c][...]
                ref[:, nl - 1:nl] = proj_ref[:, col % tn:col % tn + w].reshape(nb, 1, t, w)


def inproj(x2, norm_g, scale, shift, w1, w2, rows_per_mod, kv=None):
    m = x2.shape[0]
    tm, tn = 1024, 1024
    nmod = scale.shape[0]
    mod_idx = lambda i, j: ((i * tm) // rows_per_mod, 0, 0)
    in_specs = [pl.BlockSpec((tm, D_MODEL), lambda i, j: (i, 0)),
                pl.BlockSpec((1, D_MODEL), lambda i, j: (0, 0)),
                pl.BlockSpec((1, 1, D_MODEL), mod_idx),
                pl.BlockSpec((1, 1, D_MODEL), mod_idx),
                pl.BlockSpec((D_MODEL, tn), lambda i, j: (0, j)),
                pl.BlockSpec((D_MODEL, 128), lambda i, j: (0, 0))]
    args = [x2, norm_g.reshape(1, D_MODEL), scale.reshape(nmod, 1, D_MODEL),
            shift.reshape(nmod, 1, D_MODEL), w1, w2]
    out_specs = [pl.BlockSpec((tm, tn), lambda i, j: (i, j)),
                 pl.BlockSpec((tm, 128), lambda i, j: (i, 0)),
                 pl.BlockSpec((tm, D_MODEL), lambda i, j: (i, 0))]
    out_shape = [jax.ShapeDtypeStruct((m, N_MAIN), F32),
                 jax.ShapeDtypeStruct((m, 128), F32),
                 jax.ShapeDtypeStruct((m, D_MODEL), BF16)]
    n_prev = 0
    if kv is not None:
        layer, t, caches = kv
        cspec = lambda nl: pl.BlockSpec((tm // t, nl, t, DA_WIDTH), lambda i, j: (i, 0, 0, 0))
        out_specs += [cspec(layer + 1)] * 2
        out_shape += [jax.ShapeDtypeStruct((m // t, layer + 1, t, DA_WIDTH), F32)] * 2
        if caches is not None:
            n_prev = 2
            in_specs += [cspec(layer)] * 2
            args += list(caches)
    return pl.pallas_call(
        functools.partial(_inproj_kernel, n_prev=n_prev, with_kv=kv is not None),
        grid=(m // tm, N_MAIN // tn),
        in_specs=in_specs,
        out_specs=out_specs,
        out_shape=out_shape,
        compiler_params=_cparams(("parallel", "arbitrary")),
        name="inproj",
    )(*args)


def _s5_gen_kernel(crt_ref, cit_ref, prt_ref, pit_ref, bbt_ref, bbs_ref, prow_ref, pirow_ref,
                   wt_ref, ws_ref, wh_ref):
    L, C, P = S5_CHUNK, S5_GROUP, S5_STATE
    width = (L + 1) * C
    row = lax.broadcasted_iota(jnp.int32, (128, width), 0)
    lane = lax.broadcasted_iota(jnp.int32, (128, width), 1)
    tile_c = (lane % C == row).astype(F32)
    expand = lambda a, e: jnp.dot(a, e, precision=HI, preferred_element_type=F32)
    zeros = jnp.zeros((C, L * C), F32)
    for g in range(S5_GEN_GROUPS):
        strips = []
        for d in range(2):
            spread_k = ((lane // C if d == 0 else L - lane // C) == row).astype(F32)
            crx, cix = expand(crt_ref[d, g], tile_c), expand(cit_ref[d, g], tile_c)
            prx, pix = expand(prt_ref[d, g], spread_k), expand(pit_ref[d, g], spread_k)
            ca = jnp.concatenate([crx * prx - cix * pix, -(crx * pix + cix * prx)], axis=0)
            wh_ref[d, g] = (ca[:, C:] if d == 0 else ca[:, :L * C]).astype(BF16)
            bbt, bbs = bbt_ref[d, g], bbs_ref[d, g]
            strips.append(jnp.dot(bbt, ca[:, :L * C] if d == 0 else ca[:, C:], precision=HI,
                                  preferred_element_type=F32))
            rows = []
            for i in range(L):
                k = L - 1 - i if d == 0 else i
                rows.append(bbt * prow_ref[d, g, k:k + 1, :] + bbs * pirow_ref[d, g, k:k + 1, :])
            ws_ref[d, g] = jnp.concatenate(rows, axis=0).astype(BF16)
        fpad = jnp.concatenate([zeros, strips[0]], axis=-1)
        rpad = jnp.concatenate([strips[1], zeros], axis=-1)
        rows = []
        for i in range(L):
            rows.append(fpad[:, (L - i) * C:(2 * L - i) * C] + rpad[:, (L - 1 - i) * C:(2 * L - 1 - i) * C])
        wt_ref[g] = jnp.concatenate(rows, axis=0).astype(BF16)


def s5_matrices(lam_re, lam_im, log_step, b_re, b_im, c_re, c_im):
    L, G, P, C = S5_CHUNK, S5_GROUPS, S5_STATE, S5_GROUP
    step = jnp.exp(log_step)[..., None]
    mag = jnp.exp(lam_re * step)
    ar, ai = mag * jnp.cos(lam_im * step), mag * jnp.sin(lam_im * step)
    den = lam_re * lam_re + lam_im * lam_im
    fr = ((ar - 1.0) * lam_re + ai * lam_im) / den
    fi = (ai * lam_re - (ar - 1.0) * lam_im) / den
    bbr = fr[..., None] * b_re - fi[..., None] * b_im
    bbi = fr[..., None] * b_im + fi[..., None] * b_re
    ks = jnp.arange(L + 1, dtype=F32)[None, None, None, :]
    pmag = jnp.exp(ks * (lam_re * step)[..., None])
    prt = pmag * jnp.cos(ks * (lam_im * step)[..., None])
    pit = pmag * jnp.sin(ks * (lam_im * step)[..., None])
    prow = jnp.swapaxes(prt, 2, 3)
    pirow = jnp.swapaxes(pit, 2, 3)
    bbrt, bbit = jnp.swapaxes(bbr, 2, 3), jnp.swapaxes(bbi, 2, 3)
    lanes = lambda a: jnp.pad(a, ((0, 0), (0, 0), (0, 0), (0, 128 - a.shape[-1])))
    args = [lanes(jnp.swapaxes(c_re, 2, 3)), lanes(jnp.swapaxes(c_im, 2, 3)), lanes(prt), lanes(pit),
            jnp.concatenate([bbrt, bbit], -1), jnp.concatenate([bbit, bbrt], -1),
            jnp.concatenate([prow, prow], -1), jnp.concatenate([-pirow, pirow], -1)]
    gg = S5_GEN_GROUPS
    spec = lambda a: pl.BlockSpec((2, gg) + a.shape[2:], lambda g: (0, g, 0, 0))
    wt, ws, wh = pl.pallas_call(
        _s5_gen_kernel,
        grid=(G // gg,),
        in_specs=[spec(a) for a in args],
        out_specs=[pl.BlockSpec((gg, S5_ROW, S5_ROW), lambda g: (g, 0, 0)),
                   pl.BlockSpec((2, gg, S5_ROW, 2 * P), lambda g: (0, g, 0, 0)),
                   pl.BlockSpec((2, gg, 2 * P, S5_ROW), lambda g: (0, g, 0, 0))],
        out_shape=[jax.ShapeDtypeStruct((G, S5_ROW, S5_ROW), BF16),
                   jax.ShapeDtypeStruct((2, G, S5_ROW, 2 * P), BF16),
                   jax.ShapeDtypeStruct((2, G, 2 * P, S5_ROW), BF16)],
        compiler_params=_cparams(("parallel",)),
        name="s5_gen",
    )(*args)
    a_l = jnp.stack([prt[..., L].reshape(2, 1, G * P), pit[..., L].reshape(2, 1, G * P)], 1)
    return wt, ws, wh, a_l


S5_GB = 8


def _s5_core_kernel(u_ref, ws_ref, wt_ref, wh_ref, a_ref, h0_ref, y_ref, hfin_ref, x_ref, s_ref, hin_ref,
                    *, nb, nchunk):
    r = nb * nchunk
    npair = S5_GB // 2
    us = [u_ref[pl.ds(i, r, stride=S5_CHUNK), :] for i in range(S5_CHUNK)]
    for g in range(S5_GB):
        xg = jnp.concatenate([u[:, S5_GROUP * g:S5_GROUP * (g + 1)] for u in us], axis=-1)
        x_ref[:, g * S5_ROW:(g + 1) * S5_ROW] = xg.astype(BF16)
    for p in range(npair):
        for d in range(2):
            sg = [jnp.dot(x_ref[:, (2 * p + k) * S5_ROW:(2 * p + k + 1) * S5_ROW], ws_ref[d, 2 * p + k],
                          preferred_element_type=F32) for k in range(2)]
            for comp in range(2):
                cs = slice(comp * S5_STATE, (comp + 1) * S5_STATE)
                s_ref[d, comp, p] = jnp.concatenate([sg[0][:, cs], sg[1][:, cs]], axis=-1)
    chains = [(d, p) for d in range(2) for p in range(npair)]
    h = {}
    for d, p in chains:
        cols = slice(p * 128, (p + 1) * 128)
        h[d, p] = (h0_ref[d, 0, :, cols], h0_ref[d, 1, :, cols], a_ref[d, 0, :, cols], a_ref[d, 1, :, cols])
    for step in range(nchunk):
        for d, p in chains:
            rows = pl.ds(step if d == 0 else nchunk - 1 - step, nb, stride=nchunk)
            hr, hi, ar, ai = h[d, p]
            hin_ref[d, 0, p, rows, :] = hr
            hin_ref[d, 1, p, rows, :] = hi
            sr, si = s_ref[d, 0, p, rows, :], s_ref[d, 1, p, rows, :]
            h[d, p] = (ar * hr - ai * hi + sr, ar * hi + ai * hr + si, ar, ai)
    for d, p in chains:
        cols = slice(p * 128, (p + 1) * 128)
        hfin_ref[d, 0, :, cols] = h[d, p][0]
        hfin_ref[d, 1, :, cols] = h[d, p][1]
    ys = []
    for g in range(S5_GB):
        y = jnp.dot(x_ref[:, g * S5_ROW:(g + 1) * S5_ROW], wt_ref[g], preferred_element_type=F32)
        cs = slice((g % 2) * S5_STATE, (g % 2 + 1) * S5_STATE)
        for d in range(2):
            hg = jnp.concatenate([hin_ref[d, 0, g // 2, :, cs], hin_ref[d, 1, g // 2, :, cs]], axis=-1)
            y = y + jnp.dot(hg.astype(BF16), wh_ref[d, g], preferred_element_type=F32)
        ys.append(y)
    for j in range(S5_CHUNK):
        y_ref[pl.ds(j, r, stride=S5_CHUNK), :] = jnp.concatenate(
            [y[:, S5_GROUP * j:S5_GROUP * (j + 1)] for y in ys], axis=-1)


def s5_core(proj, mats, h0, nb, nchunk):
    wt, ws, wh, a_l = mats
    r = nb * nchunk
    sw = S5_GB * S5_STATE
    return pl.pallas_call(
        functools.partial(_s5_core_kernel, nb=nb, nchunk=nchunk),
        grid=(S5_GROUPS // S5_GB,),
        in_specs=[pl.BlockSpec((r * S5_CHUNK, 128), lambda j: (0, COL_UA // 128 + j)),
                  pl.BlockSpec((2, S5_GB, S5_ROW, 2 * S5_STATE), lambda j: (0, j, 0, 0)),
                  pl.BlockSpec((S5_GB, S5_ROW, S5_ROW), lambda j: (j, 0, 0)),
                  pl.BlockSpec((2, S5_GB, 2 * S5_STATE, S5_ROW), lambda j: (0, j, 0, 0)),
                  pl.BlockSpec((2, 2, 1, sw), lambda j: (0, 0, 0, j)),
                  pl.BlockSpec((2, 2, nb, sw), lambda j: (0, 0, 0, j))],
        out_specs=[pl.BlockSpec((r * S5_CHUNK, 128), lambda j: (0, j)),
                   pl.BlockSpec((2, 2, nb, sw), lambda j: (0, 0, 0, j))],
        out_shape=[jax.ShapeDtypeStruct((r * S5_CHUNK, S5_WIDTH), F32),
                   jax.ShapeDtypeStruct((2, 2, nb, S5_GROUPS * S5_STATE), F32)],
        scratch_shapes=[pltpu.VMEM((r, S5_GB * S5_ROW), BF16),
                        pltpu.VMEM((2, 2, S5_GB // 2, r, 128), F32),
                        pltpu.VMEM((2, 2, S5_GB // 2, r, 128), F32)],
        compiler_params=_cparams(("parallel",)),
        name="s5_core",
    )(proj, ws, wt, wh, a_l, h0)


def _s5_epilogue_kernel(u_ref, y_ref, z_ref, d_ref, w_ref, o_ref):
    ya = jax.nn.gelu(d_ref[...] * u_ref[...] + y_ref[...])
    ya = ya * jax.nn.sigmoid(_mm(ya, w_ref[...]))
    o_ref[...] = (ya * _silu(z_ref[...])).astype(BF16)


def s5_epilogue(proj, y_s5, s5_d, w_glu):
    m = proj.shape[0]
    tm = 512
    return pl.pallas_call(
        _s5_epilogue_kernel,
        grid=(m // tm,),
        in_specs=[pl.BlockSpec((tm, S5_WIDTH), lambda i: (i, COL_UA // S5_WIDTH)),
                  pl.BlockSpec((tm, S5_WIDTH), lambda i: (i, 0)),
                  pl.BlockSpec((tm, S5_WIDTH), lambda i: (i, COL_ZA // S5_WIDTH)),
                  pl.BlockSpec((1, S5_WIDTH), lambda i: (0, 0)),
                  pl.BlockSpec((S5_WIDTH, S5_WIDTH), lambda i: (0, 0))],
        out_specs=pl.BlockSpec((tm, S5_WIDTH), lambda i: (i, 0)),
        out_shape=jax.ShapeDtypeStruct((m, S5_WIDTH), BF16),
        compiler_params=_cparams(("parallel",)),
        name="s5_epilogue",
    )(proj, y_s5, proj, s5_d.reshape(1, S5_WIDTH), w_glu)


def s5_branch(proj, nb, t, mats, s5_d, w_glu, h0):
    y, hfin = s5_core(proj, mats, h0, nb, t // S5_CHUNK)
    return s5_epilogue(proj, y, s5_d, w_glu), hfin


def _rope_tables(t):
    rows = t // GRID_W
    row = np.repeat(np.arange(rows), GRID_W).astype(np.float32)
    col = np.tile(np.arange(GRID_W), rows).astype(np.float32)
    nf = DA_HEAD // 4
    inv = (ROPE_BASE ** (-jnp.arange(nf, dtype=F32) / nf))

    def tab(pos):
        ang = jnp.asarray(pos)[:, None] * inv[None, :]
        c, s = jnp.cos(ang), jnp.sin(ang)
        return jnp.concatenate([c, c], -1), jnp.concatenate([-s, s], -1)

    cr, sr = tab(row)
    cc, sc = tab(col)
    cos = jnp.concatenate([cr, cc], -1)
    sin = jnp.concatenate([sr, sc], -1)
    return jnp.tile(cos, (1, 2)), jnp.tile(sin, (1, 2))


def _rope(x, cos, sin):
    lane = lax.broadcasted_iota(jnp.int32, x.shape, 1)
    swapped = jnp.where((lane % 32) < 16, pltpu.roll(x, 112, 1), pltpu.roll(x, 16, 1))
    return x * cos + swapped * sin


def _attn_kernel(*refs, lam_init, t, s_tot, tq, with_ctx):
    if with_ctx:
        (q_ref, k_ref, v_ref, z_ref, kc_ref, vc_ref, cq_ref, sq_ref, ck_ref, sk_ref,
         lam_ref, ng_ref, o_ref, kall_ref, vall_ref, s_ref, e_ref, l_ref) = refs
    else:
        q_ref, k_ref, v_ref, z_ref, lam_ref, ng_ref, o_ref, kall_ref, vall_ref, s_ref, e_ref, l_ref = refs

    @pl.when(pl.program_id(1) == 0)
    def _():
        if with_ctx:
            for h in range(DA_HEADS):
                hs = slice(h * 128, (h + 1) * 128)
                kall_ref[h, :, 0:t] = _rope(k_ref[0, :, hs], ck_ref[...], sk_ref[...]).T.astype(BF16)
                kall_ref[h, :, t:s_tot] = kc_ref[0, 0, h].astype(BF16)
            vall_ref[0:t, :] = v_ref[0].astype(BF16)
            vall_ref[t:s_tot, :] = vc_ref[0].astype(BF16)
        else:
            for h in range(DA_HEADS):
                kall_ref[h] = k_ref[0, :, h * 128:(h + 1) * 128].T.astype(BF16)
            vall_ref[...] = v_ref[0].astype(BF16)

    lp = lam_ref[...]
    lam = (jnp.exp(jnp.sum(lp[0:1] * lp[1:2], axis=-1, keepdims=True))
           - jnp.exp(jnp.sum(lp[2:3] * lp[3:4], axis=-1, keepdims=True)) + lam_init)
    lane = lax.broadcasted_iota(jnp.int32, (tq, 128), 1)
    def scores(h):
        hs = slice(h * 128, (h + 1) * 128)
        q = q_ref[0, :, hs]
        if with_ctx:
            q = _rope(q, cq_ref[...], sq_ref[...])
        q = q * (DA_HEAD ** -0.5 * LOG2E)
        qs = jnp.concatenate([jnp.where(lane < DA_HEAD, q, 0.0), jnp.where(lane >= DA_HEAD, q, 0.0)], axis=0)
        s_ref[h % 2] = _mm(qs, kall_ref[h])

    scores(0)
    for h in range(DA_HEADS):
        hs = slice(h * 128, (h + 1) * 128)
        b = h % 2
        if h + 1 < DA_HEADS:
            scores(h + 1)
        for r in range(0, 2 * tq, ATTN_ROWS):
            sc = s_ref[b, r:r + ATTN_ROWS, :]
            e = jnp.exp2(sc - jnp.max(sc, axis=-1, keepdims=True))
            e_ref[b, r:r + ATTN_ROWS, :] = e.astype(BF16)
            l_ref[b, r:r + ATTN_ROWS, :] = jnp.broadcast_to(jnp.sum(e, axis=-1, keepdims=True),
                                                            (ATTN_ROWS, 128))
        ov = jnp.dot(e_ref[b], vall_ref[:, hs], preferred_element_type=F32) / l_ref[b]
        o = ov[:tq] - lam * ov[tq:]
        o = o * lax.rsqrt(jnp.mean(o * o, axis=-1, keepdims=True) + EPS) * ng_ref[...]
        o = o * (1.0 - lam_init)
        o_ref[0, :, hs] = (o * _silu(z_ref[0, :, hs])).astype(BF16)


def diff_attention(proj3, layer, lam_init, da_lam, da_norm_g, ctx_kv):
    nb, t, _ = proj3.shape
    with_ctx = ctx_kv is not None
    s_tot = t + (PAST_LEN if with_ctx else 0)
    tq = 256
    wb = DA_WIDTH
    in_specs = [pl.BlockSpec((1, tq, wb), lambda b, i: (b, i, COL_QB // wb)),
                pl.BlockSpec((1, t, wb), lambda b, i: (b, 0, COL_KB // wb)),
                pl.BlockSpec((1, t, wb), lambda b, i: (b, 0, COL_VB // wb)),
                pl.BlockSpec((1, tq, wb), lambda b, i: (b, i, COL_ZB // wb))]
    args = [proj3, proj3, proj3, proj3]
    if with_ctx:
        kc, vc = ctx_kv
        cos, sin = _rope_tables(t)
        in_specs += [pl.BlockSpec((1, 1, DA_HEADS, 2 * DA_HEAD, PAST_LEN), lambda b, i: (b, layer, 0, 0, 0)),
                     pl.BlockSpec((1, PAST_LEN, wb), lambda b, i: (b, layer, 0)),
                     pl.BlockSpec((tq, 128), lambda b, i: (i, 0)),
                     pl.BlockSpec((tq, 128), lambda b, i: (i, 0)),
                     pl.BlockSpec((t, 128), lambda b, i: (0, 0)),
                     pl.BlockSpec((t, 128), lambda b, i: (0, 0))]
        args += [kc, vc, cos, sin, cos, sin]
    in_specs += [pl.BlockSpec((4, DA_HEAD), lambda b, i: (0, 0)),
                 pl.BlockSpec((1, DA_VDIM), lambda b, i: (0, 0))]
    args += [da_lam, da_norm_g.reshape(1, DA_VDIM)]
    return pl.pallas_call(
        functools.partial(_attn_kernel, lam_init=lam_init, t=t, s_tot=s_tot, tq=tq, with_ctx=with_ctx),
        grid=(nb, t // tq),
        in_specs=in_specs,
        out_specs=pl.BlockSpec((1, tq, wb), lambda b, i: (b, i, 0)),
        out_shape=jax.ShapeDtypeStruct((nb, t, wb), BF16),
        scratch_shapes=[pltpu.VMEM((DA_HEADS, 2 * DA_HEAD, s_tot), BF16), pltpu.VMEM((s_tot, wb), BF16),
                        pltpu.VMEM((2, 2 * tq, s_tot), F32), pltpu.VMEM((2, 2 * tq, s_tot), BF16),
                        pltpu.VMEM((2, 2 * tq, 128), F32)],
        compiler_params=_cparams(("parallel", "arbitrary")),
        name="diff_attention",
    )(*args)


DN_PAD = 8
DN_RT = 128
DN_GROUP = 128
DN_STEP_GROUPS = 2


def _dn_kernel(*refs, t, ngroup, with_s0):
    if with_s0:
        (qkv_ref, z_ref, ba_ref, cw_ref, alog_ref, dtb_ref, ng_ref, s0_ref,
         out_ref, sfin_ref, xp_ref, qkvn_ref, oacc_ref, st_ref) = refs
    else:
        (qkv_ref, z_ref, ba_ref, cw_ref, alog_ref, dtb_ref, ng_ref,
         out_ref, sfin_ref, xp_ref, qkvn_ref, oacc_ref, st_ref) = refs
    n = pl.program_id(1)
    cd = DN_CHUNK
    w3 = 3 * DN_WIDTH

    @pl.when(n == 0)
    def _init():
        xp_ref[0:DN_PAD, :] = jnp.zeros((DN_PAD, w3), F32)
        xp_ref[DN_PAD + t:2 * DN_PAD + t, :] = jnp.zeros((DN_PAD, w3), F32)
        xp_ref[DN_PAD:DN_PAD + t, :] = qkv_ref[0]
        half = DN_CONV // 2
        for r in range(t // DN_RT):
            for sec in range(3):
                for h in range(DN_HEADS):
                    cs = slice(sec * DN_WIDTH + h * DN_HEAD, sec * DN_WIDTH + (h + 1) * DN_HEAD)
                    acc = jnp.zeros((DN_RT, DN_HEAD), F32)
                    for j in range(DN_CONV):
                        r0 = DN_PAD + r * DN_RT + j - half
                        acc = acc + xp_ref[r0:r0 + DN_RT, cs] * cw_ref[j:j + 1, cs]
                    y = _silu(acc)
                    if sec < 2:
                        y = y * lax.rsqrt(jnp.sum(y * y, axis=-1, keepdims=True) + EPS)
                    if sec == 0:
                        y = y * (DN_HEAD ** -0.5)
                    qkvn_ref[r * DN_RT:(r + 1) * DN_RT, cs] = y
        oacc_ref[...] = jnp.zeros_like(oacc_ref)
        if with_s0:
            st_ref[...] = s0_ref[0, 0]
        else:
            st_ref[...] = jnp.zeros_like(st_ref)

    gb = DN_GROUP
    nsub = gb // cd
    ri = lax.broadcasted_iota(jnp.int32, (gb, gb), 0)
    ci = lax.broadcasted_iota(jnp.int32, (gb, gb), 1)
    same = (ri // cd) == (ci // cd)
    samef = same.astype(F32)
    eye = (ri == ci).astype(BF16)
    masked_out = -1e30

    chains = []
    for d, gi in [(d, gi) for d in range(2) for gi in range(DN_STEP_GROUPS)]:
        grp = n * DN_STEP_GROUPS + gi
        r0 = pl.multiple_of((grp if d == 0 else ngroup - 1 - grp) * gb, gb)
        incl = same & ((ci <= ri) if d == 0 else (ci >= ri))
        strict = same & ((ci < ri) if d == 0 else (ci > ri))
        inclf = incl.astype(F32)
        inclog = jnp.where(incl, 0.0, masked_out)
        nstrict = -strict.astype(BF16)
        ba = ba_ref[0, pl.ds(r0, gb), :]
        beta_all = jax.nn.sigmoid(ba)
        g_all = -jnp.exp(alog_ref[...]) * jax.nn.softplus(ba + dtb_ref[...])
        gc = jnp.dot(inclf, g_all, precision=HI, preferred_element_type=F32)
        gct = lax.dot_general(g_all.T, inclf, (((1,), (1,)), ((), ())), precision=HI,
                              preferred_element_type=F32)
        gtot = jnp.dot(samef, g_all, precision=HI, preferred_element_type=F32)
        for h in range(DN_HEADS):
            hs = slice(h * DN_HEAD, (h + 1) * DN_HEAD)
            col = 2 * DN_HEADS + d * DN_HEADS + h
            gcol = gc[:, col:col + 1]
            grow = gct[col:col + 1, :]
            gt = gtot[:, col:col + 1]
            beta = beta_all[:, d * DN_HEADS + h:d * DN_HEADS + h + 1]
            q = qkvn_ref[pl.ds(r0, gb), hs]
            k = qkvn_ref[pl.ds(r0, gb), DN_WIDTH + h * DN_HEAD:DN_WIDTH + (h + 1) * DN_HEAD]
            v = qkvn_ref[pl.ds(r0, gb), 2 * DN_WIDTH + h * DN_HEAD:2 * DN_WIDTH + (h + 1) * DN_HEAD]
            eg = jnp.exp(gcol)
            chains.append(dict(
                d=d, gi=gi, h=h, r0=r0, hs=hs, nstrict=nstrict, q=q, k=k, kb=k * beta,
                decay=jnp.exp((gcol - grow) + inclog),
                rhs=jnp.concatenate([v * beta, k * beta * eg], axis=-1),
                qe=q * eg, kdec=k * jnp.exp(gt - gcol), egt=jnp.exp(gt)))

    for c in chains:
        c["nmm"] = (_mm_nt(c["kb"], c["k"]) * c["decay"]).astype(BF16) * c["nstrict"]
    for c in chains:
        c["qk"] = _mm_nt(c["q"], c["k"]) * c["decay"]
    def level_mask(s):
        return (((ri // (2 * s)) == (ci // (2 * s))) & ((ri // s) != (ci // s))).astype(BF16)

    pair = level_mask(1)
    for c in chains:
        c["tinv"] = eye + c["nmm"] * pair
    s = 2
    while s < cd:
        offmask = level_mask(s)
        xs = [jnp.dot(c["nmm"] * offmask, c["tinv"], preferred_element_type=F32).astype(BF16) for c in chains]
        ys = [jnp.dot(c["tinv"], x, preferred_element_type=F32) for c, x in zip(chains, xs)]
        for c, y in zip(chains, ys):
            c["tinv"] = c["tinv"] + y.astype(BF16)
        s *= 2
    for c in chains:
        c["uw"] = _mm(c["tinv"], c["rhs"])

    st = {(d, h): st_ref[d, h] for d in range(2) for h in range(DN_HEADS)}
    outs = []
    for gi, step in [(gi, step) for gi in range(DN_STEP_GROUPS) for step in range(nsub)]:
        active = [c for c in chains if c["gi"] == gi]
        rs = []
        for c in active:
            sub = step if c["d"] == 0 else nsub - 1 - step
            c["rows"] = slice(sub * cd, (sub + 1) * cd)
            rs.append(_mm(jnp.concatenate([c["uw"][c["rows"], DN_HEAD:], c["qe"][c["rows"]]], axis=0),
                          st[c["d"], c["h"]]))
        for c, r in zip(active, rs):
            rows = c["rows"]
            v_new = c["uw"][rows, :DN_HEAD] - r[:cd]
            o = r[cd:] + _mm(c["qk"][rows, rows], v_new)
            st[c["d"], c["h"]] = (st[c["d"], c["h"]] * c["egt"][rows.start:rows.start + 1]
                                  + _mm(c["kdec"][rows].T, v_new))
            outs.append((c, rows.start, o))
    for (d, h), v in st.items():
        st_ref[d, h] = v
    for c, off, o in outs:
        oacc_ref[pl.ds(pl.multiple_of(c["r0"] + off, cd), cd), c["hs"]] += o

    @pl.when(n == ngroup // DN_STEP_GROUPS - 1)
    def _fin():
        for h in range(DN_HEADS):
            hs = slice(h * DN_HEAD, (h + 1) * DN_HEAD)
            o = oacc_ref[:, hs]
            o = o * lax.rsqrt(jnp.mean(o * o, axis=-1, keepdims=True) + EPS) * ng_ref[...]
            out_ref[0, :, hs] = (o * _silu(z_ref[0, :, hs])).astype(BF16)
        sfin_ref[0] = st_ref[...]


def deltanet(proj3, ba3, layer, conv_w, a_log, dt_bias, norm_g, s0):
    nb, t, _ = proj3.shape
    ngroup = t // DN_GROUP
    with_s0 = s0 is not None
    w3 = 3 * DN_WIDTH
    pad = jnp.zeros((2 * DN_HEADS,), F32)
    alog_row = jnp.concatenate([pad, a_log.reshape(-1), jnp.zeros((128 - 4 * DN_HEADS,), F32)]).reshape(1, 128)
    dtb_row = jnp.concatenate([pad, dt_bias.reshape(-1), jnp.zeros((128 - 4 * DN_HEADS,), F32)]).reshape(1, 128)
    in_specs = [pl.BlockSpec((1, t, w3), lambda b, n: (b, 0, COL_QC // w3)),
                pl.BlockSpec((1, t, DN_WIDTH), lambda b, n: (b, 0, COL_ZC // DN_WIDTH)),
                pl.BlockSpec((1, t, 128), lambda b, n: (b, 0, 0)),
                pl.BlockSpec((8, w3), lambda b, n: (0, 0)),
                pl.BlockSpec((1, 128), lambda b, n: (0, 0)),
                pl.BlockSpec((1, 128), lambda b, n: (0, 0)),
                pl.BlockSpec((1, DN_HEAD), lambda b, n: (0, 0))]
    args = [proj3, proj3, ba3, jnp.pad(conv_w, ((0, 8 - DN_CONV), (0, 0))), alog_row, dtb_row,
            norm_g.reshape(1, DN_HEAD)]
    if with_s0:
        in_specs.append(pl.BlockSpec((1, 1, 2, DN_HEADS, DN_HEAD, DN_HEAD), lambda b, n: (b, layer, 0, 0, 0, 0)))
        args.append(s0)
    return pl.pallas_call(
        functools.partial(_dn_kernel, t=t, ngroup=ngroup, with_s0=with_s0),
        grid=(nb, ngroup // DN_STEP_GROUPS),
        in_specs=in_specs,
        out_specs=[pl.BlockSpec((1, t, DN_WIDTH), lambda b, n: (b, 0, 0)),
                   pl.BlockSpec((1, 2, DN_HEADS, DN_HEAD, DN_HEAD), lambda b, n: (b, 0, 0, 0, 0))],
        out_shape=[jax.ShapeDtypeStruct((nb, t, DN_WIDTH), BF16),
                   jax.ShapeDtypeStruct((nb, 2, DN_HEADS, DN_HEAD, DN_HEAD), F32)],
        scratch_shapes=[pltpu.VMEM((t + 2 * DN_PAD, w3), F32),
                        pltpu.VMEM((t, w3), F32),
                        pltpu.VMEM((t, DN_WIDTH), F32),
                        pltpu.VMEM((2, DN_HEADS, DN_HEAD, DN_HEAD), F32)],
        compiler_params=_cparams(("parallel", "arbitrary")),
        name="deltanet",
    )(*args)


def _merge_kernel(oa_ref, ob_ref, oc_ref, hn_ref, wg_ref, wb_ref, wo_ref, x_ref, gate_ref, fg_ref,
                  *outs, final):
    tm = x_ref.shape[0]
    for rows in (slice(0, tm // 2), slice(tm // 2, tm)):
        acc = None
        hn = hn_ref[rows, :]
        for i, o_ref in enumerate((oa_ref, ob_ref, oc_ref)):
            pr = jnp.dot(o_ref[rows, :], wb_ref[i], preferred_element_type=F32)
            gt = jnp.dot(hn, wg_ref[:, i * D_MODEL:(i + 1) * D_MODEL], preferred_element_type=F32)
            term = jax.nn.sigmoid(gt) * pr
            acc = term if acc is None else acc + term
        y = jnp.dot(acc.astype(BF16), wo_ref[...], preferred_element_type=F32)
        xn = x_ref[rows, :] + gate_ref[0] * y
        outs[0][rows, :] = xn
        if final:
            yn = xn * lax.rsqrt(jnp.mean(xn * xn, axis=-1, keepdims=True) + EPS) * fg_ref[...]
            outs[1][rows, :] = yn


def merge(out_a, out_b, out_c, hn, w_gates, w_branch, w_out, x2, gate, final_g, rows_per_mod, final):
    m = x2.shape[0]
    tm = 512
    nmod = gate.shape[0]
    row = lambda i: (i, 0)
    out_specs = [pl.BlockSpec((tm, D_MODEL), row)]
    out_shape = [jax.ShapeDtypeStruct((m, D_MODEL), F32)]
    if final:
        out_specs.append(pl.BlockSpec((tm, D_MODEL), row))
        out_shape.append(jax.ShapeDtypeStruct((m, D_MODEL), F32))
    return pl.pallas_call(
        functools.partial(_merge_kernel, final=final),
        grid=(m // tm,),
        in_specs=[pl.BlockSpec((tm, BRANCH_WIDTH), row),
                  pl.BlockSpec((tm, BRANCH_WIDTH), row),
                  pl.BlockSpec((tm, BRANCH_WIDTH), row),
                  pl.BlockSpec((tm, D_MODEL), row),
                  pl.BlockSpec((D_MODEL, N_BRANCH * D_MODEL), lambda i: (0, 0)),
                  pl.BlockSpec((N_BRANCH, BRANCH_WIDTH, D_MODEL), lambda i: (0, 0, 0)),
                  pl.BlockSpec((D_MODEL, D_MODEL), lambda i: (0, 0)),
                  pl.BlockSpec((tm, D_MODEL), row),
                  pl.BlockSpec((1, 1, D_MODEL), lambda i: ((i * tm) // rows_per_mod, 0, 0)),
                  pl.BlockSpec((1, D_MODEL), lambda i: (0, 0))],
        out_specs=out_specs,
        out_shape=out_shape,
        compiler_params=_cparams(("parallel",)),
        name="merge",
    )(out_a, out_b, out_c, hn, w_gates, w_branch, w_out, x2, gate.reshape(nmod, 1, D_MODEL),
      final_g.reshape(1, D_MODEL))


def _run_pass(x, mod, wts, lam_inits, final_g, ctx):
    nb, t, _ = x.shape
    m = nb * t
    nmod = mod.shape[1]
    rows_per_mod = m if nmod == 1 else t
    x2 = x.reshape(m, D_MODEL)
    states = []
    y, caches = None, None
    for l in range(DEPTH):
        w = wts[l]
        shift, scale, gate = jnp.split(mod[l], 3, axis=-1)
        if ctx is None:
            proj, ba, hn, *caches = inproj(x2, w["norm_g"], scale, shift, w["w1"], w["w2"], rows_per_mod,
                                       kv=(l, t, caches))
            h0 = jnp.zeros((2, 2, nb, S5_GROUPS * S5_STATE), F32)
            ctx_kv, s0 = None, None
        else:
            proj, ba, hn = inproj(x2, w["norm_g"], scale, shift, w["w1"], w["w2"], rows_per_mod)
            cache_k, cache_v, st_re, st_im, st_dn = ctx
            h0 = jnp.stack([st_re[:, l], st_im[:, l]], 0)
            h0 = jnp.transpose(h0, (2, 0, 1, 3, 4)).reshape(2, 2, nb, S5_GROUPS * S5_STATE)
            ctx_kv, s0 = (cache_k, cache_v), st_dn
        proj3 = proj.reshape(nb, t, N_MAIN)
        out_a, hfin = s5_branch(proj, nb, t, w["s5_mats"], w["s5_d"], w["w_glu"], h0)
        out_b = diff_attention(proj3, l, lam_inits[l], w["da_lam"], w["da_norm_g"], ctx_kv)
        out_c, sfin = deltanet(proj3, ba.reshape(nb, t, 128), l, w["dn_conv"], w["dn_a_log"],
                               w["dn_dt_bias"], w["dn_norm_g"], s0)
        final = l == DEPTH - 1
        res = merge(out_a, out_b.reshape(m, DA_WIDTH), out_c.reshape(m, DN_WIDTH), hn, w["w_gates"],
                    w["w_branch"], w["w_out"], x2, gate, final_g, rows_per_mod, final)
        x2 = res[0]
        if final:
            y = res[1]
        if ctx is None:
            hf = hfin.reshape(2, 2, nb, S5_GROUPS, S5_STATE)
            states.append((jnp.transpose(hf[:, 0], (1, 0, 2, 3)), jnp.transpose(hf[:, 1], (1, 0, 2, 3)), sfin))
    return y.reshape(nb, t, D_MODEL), states, caches


def kernel(x_prompt, x_sample, cache_k, cache_v, state_s5_re, state_s5_im, state_dn, c, c_ctx,
           norm_g, w_ada, b_ada, w_in, s5_lam_re, s5_lam_im, s5_log_step, s5_b_re, s5_b_im,
           s5_c_re, s5_c_im, s5_d, s5_w_glu, da_lam, da_norm_g, dn_conv, dn_a_log, dn_dt_bias,
           dn_norm_g, w_branch, w_out, final_norm_g):
    nb_dec = x_sample.shape[0]
    cond8 = jnp.concatenate([c_ctx[None, :], c, jnp.zeros((8 - 1 - nb_dec, D_MODEL), F32)], 0)
    mod = ada_mod(cond8, w_ada, b_ada)
    w1, w_gates, w2 = cast_w_in(w_in)
    wts = []
    for l in range(DEPTH):
        wts.append(dict(
            norm_g=norm_g[l], w1=w1[l], w2=w2[l], w_gates=w_gates[l],
            s5_mats=s5_matrices(s5_lam_re[l], s5_lam_im[l], s5_log_step[l], s5_b_re[l], s5_b_im[l],
                                s5_c_re[l], s5_c_im[l]),
            s5_d=s5_d[l], w_glu=s5_w_glu[l].astype(BF16), da_lam=da_lam[l], da_norm_g=da_norm_g[l],
            dn_conv=dn_conv[l], dn_a_log=dn_a_log[l], dn_dt_bias=dn_dt_bias[l], dn_norm_g=dn_norm_g[l],
            w_branch=w_branch[l].astype(BF16), w_out=w_out[l].astype(BF16)))
    lam_inits = [0.8 - 0.6 * math.exp(-0.3 * l) for l in range(DEPTH)]

    y_prompt, states, (k_new, v_new) = _run_pass(x_prompt, mod[:, 0:1], wts, lam_inits, final_norm_g, None)
    cache_kt = jnp.transpose(cache_k, (0, 1, 3, 4, 5, 2)).reshape(nb_dec, DEPTH, DA_HEADS, 2 * DA_HEAD, PAST_LEN)
    ctx = (cache_kt, cache_v.reshape(nb_dec, DEPTH * PAST_LEN, DA_WIDTH), state_s5_re, state_s5_im, state_dn)
    y_sample, _, _ = _run_pass(x_sample, mod[:, 1:1 + nb_dec], wts, lam_inits, final_norm_g, ctx)

    nb, t = x_prompt.shape[:2]
    new_cache_k = k_new.reshape(nb, DEPTH, t, DA_HEADS, 2, DA_HEAD)
    new_cache_v = v_new.reshape(nb, DEPTH, t, DA_HEADS, DA_VDIM)
    new_s5_re = jnp.stack([s[0] for s in states], axis=1)
    new_s5_im = jnp.stack([s[1] for s in states], axis=1)
    new_dn = jnp.stack([s[2] for s in states], axis=1)
    return (y_prompt, y_sample, new_cache_k, new_cache_v, new_s5_re, new_s5_im, new_dn)
```

```python
import functools
import math

import numpy as np
import jax
import jax.numpy as jnp
from jax import lax
from jax.experimental import pallas as pl
from jax.experimental.pallas import tpu as pltpu

F32 = jnp.float32
BF16 = jnp.bfloat16

D_MODEL = 1024
DEPTH = 2
GRID_W = 64
EPS = 1e-6
S5_WIDTH = 512
S5_GROUP = 16
S5_GROUPS = 32
S5_STATE = 64
S5_CHUNK = 16
S5_PAIRS = S5_GROUPS // 2
S5_ROW = S5_CHUNK * S5_GROUP
S5_GEN_GROUPS = 4
DA_HEADS = 4
DA_HEAD = 64
DA_VDIM = 128
DA_WIDTH = 512
ROPE_BASE = 10000.0
DN_HEADS = 4
DN_HEAD = 128
DN_WIDTH = 512
DN_CONV = 5
DN_CHUNK = 64
N_BRANCH = 3
BRANCH_WIDTH = 512
PAST_LEN = 512

COL_UA, COL_ZA, COL_QB, COL_KB, COL_VB, COL_ZB = 0, 512, 1024, 1536, 2048, 2560
COL_QC, COL_ZC = 3072, 4608
N_MAIN = 5120
BA_OFF = 5120
GATES_OFF = 5136

VMEM_LIMIT = 56 * 1024 * 1024
HI = lax.Precision.HIGHEST
LOG2E = math.log2(math.e)
ATTN_ROWS = 16


def _cparams(sem):
    return pltpu.CompilerParams(dimension_semantics=sem, vmem_limit_bytes=VMEM_LIMIT)


def _mm(a, b):
    return jnp.dot(a.astype(BF16), b.astype(BF16), preferred_element_type=F32)


def _mm_nt(a, b):
    return lax.dot_general(a.astype(BF16), b.astype(BF16), (((1,), (1,)), ((), ())),
                           preferred_element_type=F32)


def _silu(x):
    return x * jax.nn.sigmoid(x)


def _ada_kernel(c_ref, w_ref, b_ref, o_ref):
    o_ref[0] = _mm(_silu(c_ref[...]), w_ref[0]) + b_ref[0]


def ada_mod(cond8, w_ada, b_ada):
    tn = 1024
    return pl.pallas_call(
        _ada_kernel,
        grid=(DEPTH, 3 * D_MODEL // tn),
        in_specs=[pl.BlockSpec((8, D_MODEL), lambda l, j: (0, 0)),
                  pl.BlockSpec((1, D_MODEL, tn), lambda l, j: (l, 0, j)),
                  pl.BlockSpec((1, 1, tn), lambda l, j: (l, 0, j))],
        out_specs=pl.BlockSpec((1, 8, tn), lambda l, j: (l, 0, j)),
        out_shape=jax.ShapeDtypeStruct((DEPTH, 8, 3 * D_MODEL), F32),
        compiler_params=_cparams(("parallel", "parallel")),
        name="ada_mod",
    )(cond8, w_ada, b_ada.reshape(DEPTH, 1, 3 * D_MODEL))


def _tcast_kernel(w_ref, o_ref, *, keep):
    x = w_ref[0].T
    if keep is not None:
        x = jnp.where(lax.broadcasted_iota(jnp.int32, x.shape, 1) < keep, x, 0.0)
    o_ref[...] = x.astype(BF16)


def _tcast_rows_kernel(w_hbm, o_ref, buf, sem, *, layer, row0):
    tn = buf.shape[0]
    start = pl.multiple_of(row0 + pl.program_id(0) * tn, 8)
    cp = pltpu.make_async_copy(w_hbm.at[layer, pl.ds(start, tn), :], buf, sem)
    cp.start()
    cp.wait()
    o_ref[...] = buf[...].T.astype(BF16)


def cast_w_in(w_in, layer):
    wt = jnp.swapaxes(w_in, 1, 2)
    tn = 1024
    w1 = pl.pallas_call(
        functools.partial(_tcast_kernel, keep=None),
        grid=(N_MAIN // tn,),
        in_specs=[pl.BlockSpec((1, tn, D_MODEL), lambda j: (layer, j, 0))],
        out_specs=pl.BlockSpec((D_MODEL, tn), lambda j: (0, j)),
        out_shape=jax.ShapeDtypeStruct((D_MODEL, N_MAIN), BF16),
        compiler_params=_cparams(("parallel",)),
        name="cast_w1",
    )(wt)
    gw = N_BRANCH * D_MODEL
    w_gates = pl.pallas_call(
        functools.partial(_tcast_rows_kernel, layer=layer, row0=GATES_OFF),
        grid=(gw // tn,),
        in_specs=[pl.BlockSpec(memory_space=pl.ANY)],
        out_specs=pl.BlockSpec((D_MODEL, tn), lambda j: (0, j)),
        out_shape=jax.ShapeDtypeStruct((D_MODEL, gw), BF16),
        scratch_shapes=[pltpu.VMEM((tn, D_MODEL), F32), pltpu.SemaphoreType.DMA(())],
        compiler_params=_cparams(("parallel",)),
        name="cast_w_gates",
    )(wt)
    w_ba = pl.pallas_call(
        functools.partial(_tcast_kernel, keep=GATES_OFF - BA_OFF),
        grid=(1,),
        in_specs=[pl.BlockSpec((1, 128, D_MODEL), lambda j: (layer, BA_OFF // 128, 0))],
        out_specs=pl.BlockSpec((D_MODEL, 128), lambda j: (0, 0)),
        out_shape=jax.ShapeDtypeStruct((D_MODEL, 128), BF16),
        compiler_params=_cparams(("parallel",)),
        name="cast_w_ba",
    )(wt)
    return w1, w_gates, w_ba


def _inproj_kernel(*refs, n_prev, with_kv):
    x_ref, g_ref, sc_ref, sh_ref, w1_ref, w2_ref = refs[:6]
    prev = refs[6:6 + n_prev]
    outs = refs[6 + n_prev:]
    proj_ref, ba_ref, hn_ref = outs[:3]
    j = pl.program_id(1)

    @pl.when(j == 0)
    def _():
        x = x_ref[...]
        y = x * lax.rsqrt(jnp.mean(x * x, axis=-1, keepdims=True) + EPS) * g_ref[...]
        hn = (y * (1.0 + sc_ref[0]) + sh_ref[0]).astype(BF16)
        hn_ref[...] = hn
        ba_ref[...] = jnp.dot(hn, w2_ref[...], preferred_element_type=F32)

    proj_ref[...] = jnp.dot(hn_ref[...], w1_ref[...], preferred_element_type=F32)

    if with_kv:
        tn = proj_ref.shape[1]
        for c, (ref, col) in enumerate(((outs[3], COL_KB), (outs[4], COL_VB))):
            @pl.when(j == col // tn)
            def _(c=c, ref=ref, col=col):
                nb, nl, t, w = ref.shape
                if n_prev:
                    ref[:, 0:nl - 1] = prev[c][...]
                ref[:, nl - 1:nl] = proj_ref[:, col % tn:col % tn + w].reshape(nb, 1, t, w)


def inproj(x2, norm_g, scale, shift, w1, w2, rows_per_mod, kv=None):
    m = x2.shape[0]
    tm, tn = 1024, 1024
    nmod = scale.shape[0]
    mod_idx = lambda i, j: ((i * tm) // rows_per_mod, 0, 0)
    in_specs = [pl.BlockSpec((tm, D_MODEL), lambda i, j: (i, 0)),
                pl.BlockSpec((1, D_MODEL), lambda i, j: (0, 0)),
                pl.BlockSpec((1, 1, D_MODEL), mod_idx),
                pl.BlockSpec((1, 1, D_MODEL), mod_idx),
                pl.BlockSpec((D_MODEL, tn), lambda i, j: (0, j)),
                pl.BlockSpec((D_MODEL, 128), lambda i, j: (0, 0))]
    args = [x2, norm_g.reshape(1, D_MODEL), scale.reshape(nmod, 1, D_MODEL),
            shift.reshape(nmod, 1, D_MODEL), w1, w2]
    out_specs = [pl.BlockSpec((tm, tn), lambda i, j: (i, j)),
                 pl.BlockSpec((tm, 128), lambda i, j: (i, 0)),
                 pl.BlockSpec((tm, D_MODEL), lambda i, j: (i, 0))]
    out_shape = [jax.ShapeDtypeStruct((m, N_MAIN), F32),
                 jax.ShapeDtypeStruct((m, 128), F32),
                 jax.ShapeDtypeStruct((m, D_MODEL), BF16)]
    n_prev = 0
    if kv is not None:
        layer, t, caches = kv
        cspec = lambda nl: pl.BlockSpec((tm // t, nl, t, DA_WIDTH), lambda i, j: (i, 0, 0, 0))
        out_specs += [cspec(layer + 1)] * 2
        out_shape += [jax.ShapeDtypeStruct((m // t, layer + 1, t, DA_WIDTH), F32)] * 2
        if caches is not None:
            n_prev = 2
            in_specs += [cspec(layer)] * 2
            args += list(caches)
    return pl.pallas_call(
        functools.partial(_inproj_kernel, n_prev=n_prev, with_kv=kv is not None),
        grid=(m // tm, N_MAIN // tn),
        in_specs=in_specs,
        out_specs=out_specs,
        out_shape=out_shape,
        compiler_params=_cparams(("parallel", "arbitrary")),
        name="inproj",
    )(*args)


def _s5_gen_kernel(crt_ref, cit_ref, prt_ref, pit_ref, bbt_ref, bbs_ref, prow_ref, pirow_ref,
                   wt_ref, ws_ref, wh_ref):
    L, C, P = S5_CHUNK, S5_GROUP, S5_STATE
    width = (L + 1) * C
    row = lax.broadcasted_iota(jnp.int32, (128, width), 0)
    lane = lax.broadcasted_iota(jnp.int32, (128, width), 1)
    tile_c = (lane % C == row).astype(F32)
    expand = lambda a, e: jnp.dot(a, e, precision=HI, preferred_element_type=F32)
    zeros = jnp.zeros((C, L * C), F32)
    for g in range(S5_GEN_GROUPS):
        strips = []
        for d in range(2):
            spread_k = ((lane // C if d == 0 else L - lane // C) == row).astype(F32)
            crx, cix = expand(crt_ref[d, g], tile_c), expand(cit_ref[d, g], tile_c)
            prx, pix = expand(prt_ref[d, g], spread_k), expand(pit_ref[d, g], spread_k)
            ca = jnp.concatenate([crx * prx - cix * pix, -(crx * pix + cix * prx)], axis=0)
            wh_ref[d, g] = (ca[:, C:] if d == 0 else ca[:, :L * C]).astype(BF16)
            bbt, bbs = bbt_ref[d, g], bbs_ref[d, g]
            strips.append(jnp.dot(bbt, ca[:, :L * C] if d == 0 else ca[:, C:], precision=HI,
                                  preferred_element_type=F32))
            rows = []
            for i in range(L):
                k = L - 1 - i if d == 0 else i
                rows.append(bbt * prow_ref[d, g, k:k + 1, :] + bbs * pirow_ref[d, g, k:k + 1, :])
            ws_ref[d, g] = jnp.concatenate(rows, axis=0).astype(BF16)
        fpad = jnp.concatenate([zeros, strips[0]], axis=-1)
        rpad = jnp.concatenate([strips[1], zeros], axis=-1)
        rows = []
        for i in range(L):
            rows.append(fpad[:, (L - i) * C:(2 * L - i) * C] + rpad[:, (L - 1 - i) * C:(2 * L - 1 - i) * C])
        wt_ref[g] = jnp.concatenate(rows, axis=0).astype(BF16)


def s5_matrices(lam_re, lam_im, log_step, b_re, b_im, c_re, c_im):
    L, G, P, C = S5_CHUNK, S5_GROUPS, S5_STATE, S5_GROUP
    step = jnp.exp(log_step)[..., None]
    mag = jnp.exp(lam_re * step)
    ar, ai = mag * jnp.cos(lam_im * step), mag * jnp.sin(lam_im * step)
    den = lam_re * lam_re + lam_im * lam_im
    fr = ((ar - 1.0) * lam_re + ai * lam_im) / den
    fi = (ai * lam_re - (ar - 1.0) * lam_im) / den
    bbr = fr[..., None] * b_re - fi[..., None] * b_im
    bbi = fr[..., None] * b_im + fi[..., None] * b_re
    ks = jnp.arange(L + 1, dtype=F32)[None, None, None, :]
    pmag = jnp.exp(ks * (lam_re * step)[..., None])
    prt = pmag * jnp.cos(ks * (lam_im * step)[..., None])
    pit = pmag * jnp.sin(ks * (lam_im * step)[..., None])
    prow = jnp.swapaxes(prt, 2, 3)
    pirow = jnp.swapaxes(pit, 2, 3)
    bbrt, bbit = jnp.swapaxes(bbr, 2, 3), jnp.swapaxes(bbi, 2, 3)
    lanes = lambda a: jnp.pad(a, ((0, 0), (0, 0), (0, 0), (0, 128 - a.shape[-1])))
    args = [lanes(jnp.swapaxes(c_re, 2, 3)), lanes(jnp.swapaxes(c_im, 2, 3)), lanes(prt), lanes(pit),
            jnp.concatenate([bbrt, bbit], -1), jnp.concatenate([bbit, bbrt], -1),
            jnp.concatenate([prow, prow], -1), jnp.concatenate([-pirow, pirow], -1)]
    gg = S5_GEN_GROUPS
    spec = lambda a: pl.BlockSpec((2, gg) + a.shape[2:], lambda g: (0, g, 0, 0))
    wt, ws, wh = pl.pallas_call(
        _s5_gen_kernel,
        grid=(G // gg,),
        in_specs=[spec(a) for a in args],
        out_specs=[pl.BlockSpec((gg, S5_ROW, S5_ROW), lambda g: (g, 0, 0)),
                   pl.BlockSpec((2, gg, S5_ROW, 2 * P), lambda g: (0, g, 0, 0)),
                   pl.BlockSpec((2, gg, 2 * P, S5_ROW), lambda g: (0, g, 0, 0))],
        out_shape=[jax.ShapeDtypeStruct((G, S5_ROW, S5_ROW), BF16),
                   jax.ShapeDtypeStruct((2, G, S5_ROW, 2 * P), BF16),
                   jax.ShapeDtypeStruct((2, G, 2 * P, S5_ROW), BF16)],
        compiler_params=_cparams(("parallel",)),
        name="s5_gen",
    )(*args)
    a_l = jnp.stack([prt[..., L].reshape(2, 1, G * P), pit[..., L].reshape(2, 1, G * P)], 1)
    return wt, ws, wh, a_l


S5_GB = 8


def _s5_core_kernel(u_ref, ws_ref, wt_ref, wh_ref, a_ref, h0_ref, y_ref, hfin_ref, x_ref, s_ref, hin_ref,
                    *, nb, nchunk):
    r = nb * nchunk
    npair = S5_GB // 2
    us = [u_ref[pl.ds(i, r, stride=S5_CHUNK), :] for i in range(S5_CHUNK)]
    for g in range(S5_GB):
        xg = jnp.concatenate([u[:, S5_GROUP * g:S5_GROUP * (g + 1)] for u in us], axis=-1)
        x_ref[:, g * S5_ROW:(g + 1) * S5_ROW] = xg.astype(BF16)
    for p in range(npair):
        for d in range(2):
            sg = [jnp.dot(x_ref[:, (2 * p + k) * S5_ROW:(2 * p + k + 1) * S5_ROW], ws_ref[d, 2 * p + k],
                          preferred_element_type=F32) for k in range(2)]
            for comp in range(2):
                cs = slice(comp * S5_STATE, (comp + 1) * S5_STATE)
                s_ref[d, comp, p] = jnp.concatenate([sg[0][:, cs], sg[1][:, cs]], axis=-1)
    chains = [(d, p) for d in range(2) for p in range(npair)]
    h = {}
    for d, p in chains:
        cols = slice(p * 128, (p + 1) * 128)
        h[d, p] = (h0_ref[d, 0, :, cols], h0_ref[d, 1, :, cols], a_ref[d, 0, :, cols], a_ref[d, 1, :, cols])
    for step in range(nchunk):
        for d, p in chains:
            rows = pl.ds(step if d == 0 else nchunk - 1 - step, nb, stride=nchunk)
            hr, hi, ar, ai = h[d, p]
            hin_ref[d, 0, p, rows, :] = hr
            hin_ref[d, 1, p, rows, :] = hi
            sr, si = s_ref[d, 0, p, rows, :], s_ref[d, 1, p, rows, :]
            h[d, p] = (ar * hr - ai * hi + sr, ar * hi + ai * hr + si, ar, ai)
    for d, p in chains:
        cols = slice(p * 128, (p + 1) * 128)
        hfin_ref[d, 0, :, cols] = h[d, p][0]
        hfin_ref[d, 1, :, cols] = h[d, p][1]
    ys = []
    for g in range(S5_GB):
        y = jnp.dot(x_ref[:, g * S5_ROW:(g + 1) * S5_ROW], wt_ref[g], preferred_element_type=F32)
        cs = slice((g % 2) * S5_STATE, (g % 2 + 1) * S5_STATE)
        for d in range(2):
            hg = jnp.concatenate([hin_ref[d, 0, g // 2, :, cs], hin_ref[d, 1, g // 2, :, cs]], axis=-1)
            y = y + jnp.dot(hg.astype(BF16), wh_ref[d, g], preferred_element_type=F32)
        ys.append(y)
    for j in range(S5_CHUNK):
        y_ref[pl.ds(j, r, stride=S5_CHUNK), :] = jnp.concatenate(
            [y[:, S5_GROUP * j:S5_GROUP * (j + 1)] for y in ys], axis=-1)


def s5_core(proj, mats, h0, nb, nchunk):
    wt, ws, wh, a_l = mats
    r = nb * nchunk
    sw = S5_GB * S5_STATE
    return pl.pallas_call(
        functools.partial(_s5_core_kernel, nb=nb, nchunk=nchunk),
        grid=(S5_GROUPS // S5_GB,),
        in_specs=[pl.BlockSpec((r * S5_CHUNK, 128), lambda j: (0, COL_UA // 128 + j)),
                  pl.BlockSpec((2, S5_GB, S5_ROW, 2 * S5_STATE), lambda j: (0, j, 0, 0)),
                  pl.BlockSpec((S5_GB, S5_ROW, S5_ROW), lambda j: (j, 0, 0)),
                  pl.BlockSpec((2, S5_GB, 2 * S5_STATE, S5_ROW), lambda j: (0, j, 0, 0)),
                  pl.BlockSpec((2, 2, 1, sw), lambda j: (0, 0, 0, j)),
                  pl.BlockSpec((2, 2, nb, sw), lambda j: (0, 0, 0, j))],
        out_specs=[pl.BlockSpec((r * S5_CHUNK, 128), lambda j: (0, j)),
                   pl.BlockSpec((2, 2, nb, sw), lambda j: (0, 0, 0, j))],
        out_shape=[jax.ShapeDtypeStruct((r * S5_CHUNK, S5_WIDTH), F32),
                   jax.ShapeDtypeStruct((2, 2, nb, S5_GROUPS * S5_STATE), F32)],
        scratch_shapes=[pltpu.VMEM((r, S5_GB * S5_ROW), BF16),
                        pltpu.VMEM((2, 2, S5_GB // 2, r, 128), F32),
                        pltpu.VMEM((2, 2, S5_GB // 2, r, 128), F32)],
        compiler_params=_cparams(("parallel",)),
        name="s5_core",
    )(proj, ws, wt, wh, a_l, h0)


def _rope_tables(t):
    rows = t // GRID_W
    row = np.repeat(np.arange(rows), GRID_W).astype(np.float32)
    col = np.tile(np.arange(GRID_W), rows).astype(np.float32)
    nf = DA_HEAD // 4
    inv = (ROPE_BASE ** (-jnp.arange(nf, dtype=F32) / nf))

    def tab(pos):
        ang = jnp.asarray(pos)[:, None] * inv[None, :]
        c, s = jnp.cos(ang), jnp.sin(ang)
        return jnp.concatenate([c, c], -1), jnp.concatenate([-s, s], -1)

    cr, sr = tab(row)
    cc, sc = tab(col)
    cos = jnp.concatenate([cr, cc], -1)
    sin = jnp.concatenate([sr, sc], -1)
    return jnp.tile(cos, (1, 2)), jnp.tile(sin, (1, 2))


def _rope(x, cos, sin):
    lane = lax.broadcasted_iota(jnp.int32, x.shape, 1)
    swapped = jnp.where((lane % 32) < 16, pltpu.roll(x, 112, 1), pltpu.roll(x, 16, 1))
    return x * cos + swapped * sin


def _attn_kernel(*refs, lam_init, t, s_tot, tq, with_ctx):
    if with_ctx:
        (q_ref, k_ref, v_ref, z_ref, kc_ref, vc_ref, cq_ref, sq_ref, ck_ref, sk_ref,
         lam_ref, ng_ref, o_ref, kall_ref, vall_ref, s_ref, e_ref, l_ref) = refs
    else:
        q_ref, k_ref, v_ref, z_ref, lam_ref, ng_ref, o_ref, kall_ref, vall_ref, s_ref, e_ref, l_ref = refs

    @pl.when(pl.program_id(1) == 0)
    def _():
        if with_ctx:
            for h in range(DA_HEADS):
                hs = slice(h * 128, (h + 1) * 128)
                kall_ref[h, :, 0:t] = _rope(k_ref[0, :, hs], ck_ref[...], sk_ref[...]).T.astype(BF16)
                kall_ref[h, :, t:s_tot] = kc_ref[0, 0, h].astype(BF16)
            vall_ref[0:t, :] = v_ref[0].astype(BF16)
            vall_ref[t:s_tot, :] = vc_ref[0].astype(BF16)
        else:
            for h in range(DA_HEADS):
                kall_ref[h] = k_ref[0, :, h * 128:(h + 1) * 128].T.astype(BF16)
            vall_ref[...] = v_ref[0].astype(BF16)

    lp = lam_ref[...]
    lam = (jnp.exp(jnp.sum(lp[0:1] * lp[1:2], axis=-1, keepdims=True))
           - jnp.exp(jnp.sum(lp[2:3] * lp[3:4], axis=-1, keepdims=True)) + lam_init)
    lane = lax.broadcasted_iota(jnp.int32, (tq, 128), 1)
    def scores(h):
        hs = slice(h * 128, (h + 1) * 128)
        q = q_ref[0, :, hs]
        if with_ctx:
            q = _rope(q, cq_ref[...], sq_ref[...])
        q = q * (DA_HEAD ** -0.5 * LOG2E)
        qs = jnp.concatenate([jnp.where(lane < DA_HEAD, q, 0.0), jnp.where(lane >= DA_HEAD, q, 0.0)], axis=0)
        s_ref[h % 2] = _mm(qs, kall_ref[h])

    scores(0)
    for h in range(DA_HEADS):
        hs = slice(h * 128, (h + 1) * 128)
        b = h % 2
        if h + 1 < DA_HEADS:
            scores(h + 1)
        for r in range(0, 2 * tq, ATTN_ROWS):
            sc = s_ref[b, r:r + ATTN_ROWS, :]
            e = jnp.exp2(sc - jnp.max(sc, axis=-1, keepdims=True))
            e_ref[b, r:r + ATTN_ROWS, :] = e.astype(BF16)
            l_ref[b, r:r + ATTN_ROWS, :] = jnp.broadcast_to(jnp.sum(e, axis=-1, keepdims=True),
                                                            (ATTN_ROWS, 128))
        ov = jnp.dot(e_ref[b], vall_ref[:, hs], preferred_element_type=F32) / l_ref[b]
        o = ov[:tq] - lam * ov[tq:]
        o = o * lax.rsqrt(jnp.mean(o * o, axis=-1, keepdims=True) + EPS) * ng_ref[...]
        o = o * (1.0 - lam_init)
        o_ref[0, :, hs] = (o * _silu(z_ref[0, :, hs])).astype(BF16)


def diff_attention(proj3, layer, lam_init, da_lam, da_norm_g, ctx_kv):
    nb, t, _ = proj3.shape
    with_ctx = ctx_kv is not None
    s_tot = t + (PAST_LEN if with_ctx else 0)
    tq = 256
    wb = DA_WIDTH
    in_specs = [pl.BlockSpec((1, tq, wb), lambda b, i: (b, i, COL_QB // wb)),
                pl.BlockSpec((1, t, wb), lambda b, i: (b, 0, COL_KB // wb)),
                pl.BlockSpec((1, t, wb), lambda b, i: (b, 0, COL_VB // wb)),
                pl.BlockSpec((1, tq, wb), lambda b, i: (b, i, COL_ZB // wb))]
    args = [proj3, proj3, proj3, proj3]
    if with_ctx:
        kc, vc = ctx_kv
        cos, sin = _rope_tables(t)
        in_specs += [pl.BlockSpec((1, 1, DA_HEADS, 2 * DA_HEAD, PAST_LEN), lambda b, i: (b, layer, 0, 0, 0)),
                     pl.BlockSpec((1, PAST_LEN, wb), lambda b, i: (b, layer, 0)),
                     pl.BlockSpec((tq, 128), lambda b, i: (i, 0)),
                     pl.BlockSpec((tq, 128), lambda b, i: (i, 0)),
                     pl.BlockSpec((t, 128), lambda b, i: (0, 0)),
                     pl.BlockSpec((t, 128), lambda b, i: (0, 0))]
        args += [kc, vc, cos, sin, cos, sin]
    in_specs += [pl.BlockSpec((4, DA_HEAD), lambda b, i: (0, 0)),
                 pl.BlockSpec((1, DA_VDIM), lambda b, i: (0, 0))]
    args += [da_lam, da_norm_g.reshape(1, DA_VDIM)]
    return pl.pallas_call(
        functools.partial(_attn_kernel, lam_init=lam_init, t=t, s_tot=s_tot, tq=tq, with_ctx=with_ctx),
        grid=(nb, t // tq),
        in_specs=in_specs,
        out_specs=pl.BlockSpec((1, tq, wb), lambda b, i: (b, i, 0)),
        out_shape=jax.ShapeDtypeStruct((nb, t, wb), BF16),
        scratch_shapes=[pltpu.VMEM((DA_HEADS, 2 * DA_HEAD, s_tot), BF16), pltpu.VMEM((s_tot, wb), BF16),
                        pltpu.VMEM((2, 2 * tq, s_tot), F32), pltpu.VMEM((2, 2 * tq, s_tot), BF16),
                        pltpu.VMEM((2, 2 * tq, 128), F32)],
        compiler_params=_cparams(("parallel", "arbitrary")),
        name="diff_attention",
    )(*args)


DN_PAD = 8
DN_RT = 128
DN_GROUP = 128
DN_STEP_GROUPS = 2


def _dn_kernel(*refs, t, ngroup, with_s0):
    if with_s0:
        (qkv_ref, z_ref, ba_ref, cw_ref, alog_ref, dtb_ref, ng_ref, s0_ref,
         out_ref, sfin_ref, xp_ref, qkvn_ref, oacc_ref, st_ref) = refs
    else:
        (qkv_ref, z_ref, ba_ref, cw_ref, alog_ref, dtb_ref, ng_ref,
         out_ref, sfin_ref, xp_ref, qkvn_ref, oacc_ref, st_ref) = refs
    n = pl.program_id(1)
    cd = DN_CHUNK
    w3 = 3 * DN_WIDTH

    @pl.when(n == 0)
    def _init():
        xp_ref[0:DN_PAD, :] = jnp.zeros((DN_PAD, w3), F32)
        xp_ref[DN_PAD + t:2 * DN_PAD + t, :] = jnp.zeros((DN_PAD, w3), F32)
        xp_ref[DN_PAD:DN_PAD + t, :] = qkv_ref[0]
        half = DN_CONV // 2
        for r in range(t // DN_RT):
            for sec in range(3):
                for h in range(DN_HEADS):
                    cs = slice(sec * DN_WIDTH + h * DN_HEAD, sec * DN_WIDTH + (h + 1) * DN_HEAD)
                    acc = jnp.zeros((DN_RT, DN_HEAD), F32)
                    for j in range(DN_CONV):
                        r0 = DN_PAD + r * DN_RT + j - half
                        acc = acc + xp_ref[r0:r0 + DN_RT, cs] * cw_ref[j:j + 1, cs]
                    y = _silu(acc)
                    if sec < 2:
                        y = y * lax.rsqrt(jnp.sum(y * y, axis=-1, keepdims=True) + EPS)
                    if sec == 0:
                        y = y * (DN_HEAD ** -0.5)
                    qkvn_ref[r * DN_RT:(r + 1) * DN_RT, cs] = y
        oacc_ref[...] = jnp.zeros_like(oacc_ref)
        if with_s0:
            st_ref[...] = s0_ref[0, 0]
        else:
            st_ref[...] = jnp.zeros_like(st_ref)

    gb = DN_GROUP
    nsub = gb // cd
    ri = lax.broadcasted_iota(jnp.int32, (gb, gb), 0)
    ci = lax.broadcasted_iota(jnp.int32, (gb, gb), 1)
    same = (ri // cd) == (ci // cd)
    samef = same.astype(F32)
    eye = (ri == ci).astype(BF16)
    masked_out = -1e30

    chains = []
    for d, gi in [(d, gi) for d in range(2) for gi in range(DN_STEP_GROUPS)]:
        grp = n * DN_STEP_GROUPS + gi
        r0 = pl.multiple_of((grp if d == 0 else ngroup - 1 - grp) * gb, gb)
        incl = same & ((ci <= ri) if d == 0 else (ci >= ri))
        strict = same & ((ci < ri) if d == 0 else (ci > ri))
        inclf = incl.astype(F32)
        inclog = jnp.where(incl, 0.0, masked_out)
        nstrict = -strict.astype(BF16)
        ba = ba_ref[0, pl.ds(r0, gb), :]
        beta_all = jax.nn.sigmoid(ba)
        g_all = -jnp.exp(alog_ref[...]) * jax.nn.softplus(ba + dtb_ref[...])
        gc = jnp.dot(inclf, g_all, precision=HI, preferred_element_type=F32)
        gct = lax.dot_general(g_all.T, inclf, (((1,), (1,)), ((), ())), precision=HI,
                              preferred_element_type=F32)
        gtot = jnp.dot(samef, g_all, precision=HI, preferred_element_type=F32)
        for h in range(DN_HEADS):
            hs = slice(h * DN_HEAD, (h + 1) * DN_HEAD)
            col = 2 * DN_HEADS + d * DN_HEADS + h
            gcol = gc[:, col:col + 1]
            grow = gct[col:col + 1, :]
            gt = gtot[:, col:col + 1]
            beta = beta_all[:, d * DN_HEADS + h:d * DN_HEADS + h + 1]
            q = qkvn_ref[pl.ds(r0, gb), hs]
            k = qkvn_ref[pl.ds(r0, gb), DN_WIDTH + h * DN_HEAD:DN_WIDTH + (h + 1) * DN_HEAD]
            v = qkvn_ref[pl.ds(r0, gb), 2 * DN_WIDTH + h * DN_HEAD:2 * DN_WIDTH + (h + 1) * DN_HEAD]
            eg = jnp.exp(gcol)
            chains.append(dict(
                d=d, gi=gi, h=h, r0=r0, hs=hs, nstrict=nstrict, q=q, k=k, kb=k * beta,
                decay=jnp.exp((gcol - grow) + inclog),
                rhs=jnp.concatenate([v * beta, k * beta * eg], axis=-1),
                qe=q * eg, kdec=k * jnp.exp(gt - gcol), egt=jnp.exp(gt)))

    for c in chains:
        c["nmm"] = (_mm_nt(c["kb"], c["k"]) * c["decay"]).astype(BF16) * c["nstrict"]
    for c in chains:
        c["qk"] = _mm_nt(c["q"], c["k"]) * c["decay"]
    def level_mask(s):
        return (((ri // (2 * s)) == (ci // (2 * s))) & ((ri // s) != (ci // s))).astype(BF16)

    pair = level_mask(1)
    for c in chains:
        c["tinv"] = eye + c["nmm"] * pair
    s = 2
    while s < cd:
        offmask = level_mask(s)
        xs = [jnp.dot(c["nmm"] * offmask, c["tinv"], preferred_element_type=F32).astype(BF16) for c in chains]
        ys = [jnp.dot(c["tinv"], x, preferred_element_type=F32) for c, x in zip(chains, xs)]
        for c, y in zip(chains, ys):
            c["tinv"] = c["tinv"] + y.astype(BF16)
        s *= 2
    for c in chains:
        c["uw"] = _mm(c["tinv"], c["rhs"])

    st = {(d, h): st_ref[d, h] for d in range(2) for h in range(DN_HEADS)}
    outs = []
    for gi, step in [(gi, step) for gi in range(DN_STEP_GROUPS) for step in range(nsub)]:
        active = [c for c in chains if c["gi"] == gi]
        rs = []
        for c in active:
            sub = step if c["d"] == 0 else nsub - 1 - step
            c["rows"] = slice(sub * cd, (sub + 1) * cd)
            rs.append(_mm(jnp.concatenate([c["uw"][c["rows"], DN_HEAD:], c["qe"][c["rows"]]], axis=0),
                          st[c["d"], c["h"]]))
        for c, r in zip(active, rs):
            rows = c["rows"]
            v_new = c["uw"][rows, :DN_HEAD] - r[:cd]
            o = r[cd:] + _mm(c["qk"][rows, rows], v_new)
            st[c["d"], c["h"]] = (st[c["d"], c["h"]] * c["egt"][rows.start:rows.start + 1]
                                  + _mm(c["kdec"][rows].T, v_new))
            outs.append((c, rows.start, o))
    for (d, h), v in st.items():
        st_ref[d, h] = v
    for c, off, o in outs:
        oacc_ref[pl.ds(pl.multiple_of(c["r0"] + off, cd), cd), c["hs"]] += o

    @pl.when(n == ngroup // DN_STEP_GROUPS - 1)
    def _fin():
        for h in range(DN_HEADS):
            hs = slice(h * DN_HEAD, (h + 1) * DN_HEAD)
            o = oacc_ref[:, hs]
            o = o * lax.rsqrt(jnp.mean(o * o, axis=-1, keepdims=True) + EPS) * ng_ref[...]
            out_ref[0, :, hs] = (o * _silu(z_ref[0, :, hs])).astype(BF16)
        sfin_ref[0] = st_ref[...]


def deltanet(proj3, ba3, layer, conv_w, a_log, dt_bias, norm_g, s0):
    nb, t, _ = proj3.shape
    ngroup = t // DN_GROUP
    with_s0 = s0 is not None
    w3 = 3 * DN_WIDTH
    pad = jnp.zeros((2 * DN_HEADS,), F32)
    alog_row = jnp.concatenate([pad, a_log.reshape(-1), jnp.zeros((128 - 4 * DN_HEADS,), F32)]).reshape(1, 128)
    dtb_row = jnp.concatenate([pad, dt_bias.reshape(-1), jnp.zeros((128 - 4 * DN_HEADS,), F32)]).reshape(1, 128)
    in_specs = [pl.BlockSpec((1, t, w3), lambda b, n: (b, 0, COL_QC // w3)),
                pl.BlockSpec((1, t, DN_WIDTH), lambda b, n: (b, 0, COL_ZC // DN_WIDTH)),
                pl.BlockSpec((1, t, 128), lambda b, n: (b, 0, 0)),
                pl.BlockSpec((8, w3), lambda b, n: (0, 0)),
                pl.BlockSpec((1, 128), lambda b, n: (0, 0)),
                pl.BlockSpec((1, 128), lambda b, n: (0, 0)),
                pl.BlockSpec((1, DN_HEAD), lambda b, n: (0, 0))]
    args = [proj3, proj3, ba3, jnp.pad(conv_w, ((0, 8 - DN_CONV), (0, 0))), alog_row, dtb_row,
            norm_g.reshape(1, DN_HEAD)]
    if with_s0:
        in_specs.append(pl.BlockSpec((1, 1, 2, DN_HEADS, DN_HEAD, DN_HEAD), lambda b, n: (b, layer, 0, 0, 0, 0)))
        args.append(s0)
    return pl.pallas_call(
        functools.partial(_dn_kernel, t=t, ngroup=ngroup, with_s0=with_s0),
        grid=(nb, ngroup // DN_STEP_GROUPS),
        in_specs=in_specs,
        out_specs=[pl.BlockSpec((1, t, DN_WIDTH), lambda b, n: (b, 0, 0)),
                   pl.BlockSpec((1, 2, DN_HEADS, DN_HEAD, DN_HEAD), lambda b, n: (b, 0, 0, 0, 0))],
        out_shape=[jax.ShapeDtypeStruct((nb, t, DN_WIDTH), BF16),
                   jax.ShapeDtypeStruct((nb, 2, DN_HEADS, DN_HEAD, DN_HEAD), F32)],
        scratch_shapes=[pltpu.VMEM((t + 2 * DN_PAD, w3), F32),
                        pltpu.VMEM((t, w3), F32),
                        pltpu.VMEM((t, DN_WIDTH), F32),
                        pltpu.VMEM((2, DN_HEADS, DN_HEAD, DN_HEAD), F32)],
        compiler_params=_cparams(("parallel", "arbitrary")),
        name="deltanet",
    )(*args)


def _merge_kernel(u_ref, ys_ref, za_ref, sd_ref, wglu_ref, ob_ref, oc_ref, hn_ref, wg_ref, wb_ref, wo_ref,
                  x_ref, gate_ref, fg_ref, *outs, final):
    tm = x_ref.shape[0]
    for rows in (slice(0, tm // 2), slice(tm // 2, tm)):
        ya = jax.nn.gelu(sd_ref[...] * u_ref[rows, :] + ys_ref[rows, :])
        ya = ya * jax.nn.sigmoid(_mm(ya, wglu_ref[...]))
        branches = ((ya * _silu(za_ref[rows, :])).astype(BF16), ob_ref[rows, :], oc_ref[rows, :])
        acc = None
        hn = hn_ref[rows, :]
        for i, o in enumerate(branches):
            pr = jnp.dot(o, wb_ref[i], preferred_element_type=F32)
            gt = jnp.dot(hn, wg_ref[:, i * D_MODEL:(i + 1) * D_MODEL], preferred_element_type=F32)
            term = jax.nn.sigmoid(gt) * pr
            acc = term if acc is None else acc + term
        y = jnp.dot(acc.astype(BF16), wo_ref[...], preferred_element_type=F32)
        xn = x_ref[rows, :] + gate_ref[0] * y
        outs[0][rows, :] = xn
        if final:
            yn = xn * lax.rsqrt(jnp.mean(xn * xn, axis=-1, keepdims=True) + EPS) * fg_ref[...]
            outs[1][rows, :] = yn


def merge(proj, y_s5, s5_d, w_glu, out_b, out_c, hn, w_gates, w_branch, w_out, x2, gate, final_g, rows_per_mod,
          final):
    m = x2.shape[0]
    tm = 512
    nmod = gate.shape[0]
    row = lambda i: (i, 0)
    out_specs = [pl.BlockSpec((tm, D_MODEL), row)]
    out_shape = [jax.ShapeDtypeStruct((m, D_MODEL), F32)]
    if final:
        out_specs.append(pl.BlockSpec((tm, D_MODEL), row))
        out_shape.append(jax.ShapeDtypeStruct((m, D_MODEL), F32))
    return pl.pallas_call(
        functools.partial(_merge_kernel, final=final),
        grid=(m // tm,),
        in_specs=[pl.BlockSpec((tm, S5_WIDTH), lambda i: (i, COL_UA // S5_WIDTH)),
                  pl.BlockSpec((tm, S5_WIDTH), row),
                  pl.BlockSpec((tm, S5_WIDTH), lambda i: (i, COL_ZA // S5_WIDTH)),
                  pl.BlockSpec((1, S5_WIDTH), lambda i: (0, 0)),
                  pl.BlockSpec((S5_WIDTH, S5_WIDTH), lambda i: (0, 0)),
                  pl.BlockSpec((tm, BRANCH_WIDTH), row),
                  pl.BlockSpec((tm, BRANCH_WIDTH), row),
                  pl.BlockSpec((tm, D_MODEL), row),
                  pl.BlockSpec((D_MODEL, N_BRANCH * D_MODEL), lambda i: (0, 0)),
                  pl.BlockSpec((N_BRANCH, BRANCH_WIDTH, D_MODEL), lambda i: (0, 0, 0)),
                  pl.BlockSpec((D_MODEL, D_MODEL), lambda i: (0, 0)),
                  pl.BlockSpec((tm, D_MODEL), row),
                  pl.BlockSpec((1, 1, D_MODEL), lambda i: ((i * tm) // rows_per_mod, 0, 0)),
                  pl.BlockSpec((1, D_MODEL), lambda i: (0, 0))],
        out_specs=out_specs,
        out_shape=out_shape,
        compiler_params=_cparams(("parallel",)),
        name="merge",
    )(proj, y_s5, proj, s5_d.reshape(1, S5_WIDTH), w_glu, out_b, out_c, hn, w_gates, w_branch, w_out, x2,
      gate.reshape(nmod, 1, D_MODEL),
      final_g.reshape(1, D_MODEL))


def _run_pass(x, mod, wts, lam_inits, final_g, ctx):
    nb, t, _ = x.shape
    m = nb * t
    nmod = mod.shape[1]
    rows_per_mod = m if nmod == 1 else t
    x2 = x.reshape(m, D_MODEL)
    states = []
    y, caches = None, None
    for l in range(DEPTH):
        w = wts[l]
        shift, scale, gate = jnp.split(mod[l], 3, axis=-1)
        if ctx is None:
            proj, ba, hn, *caches = inproj(x2, w["norm_g"], scale, shift, w["w1"], w["w2"], rows_per_mod,
                                       kv=(l, t, caches))
            h0 = jnp.zeros((2, 2, nb, S5_GROUPS * S5_STATE), F32)
            ctx_kv, s0 = None, None
        else:
            proj, ba, hn = inproj(x2, w["norm_g"], scale, shift, w["w1"], w["w2"], rows_per_mod)
            cache_k, cache_v, st_re, st_im, st_dn = ctx
            h0 = jnp.stack([st_re[:, l], st_im[:, l]], 0)
            h0 = jnp.transpose(h0, (2, 0, 1, 3, 4)).reshape(2, 2, nb, S5_GROUPS * S5_STATE)
            ctx_kv, s0 = (cache_k, cache_v), st_dn
        proj3 = proj.reshape(nb, t, N_MAIN)
        y_s5, hfin = s5_core(proj, w["s5_mats"], h0, nb, t // S5_CHUNK)
        out_b = diff_attention(proj3, l, lam_inits[l], w["da_lam"], w["da_norm_g"], ctx_kv)
        out_c, sfin = deltanet(proj3, ba.reshape(nb, t, 128), l, w["dn_conv"], w["dn_a_log"],
                               w["dn_dt_bias"], w["dn_norm_g"], s0)
        final = l == DEPTH - 1
        res = merge(proj, y_s5, w["s5_d"], w["w_glu"], out_b.reshape(m, DA_WIDTH), out_c.reshape(m, DN_WIDTH),
                    hn, w["w_gates"], w["w_branch"], w["w_out"], x2, gate, final_g, rows_per_mod, final)
        x2 = res[0]
        if final:
            y = res[1]
        if ctx is None:
            hf = hfin.reshape(2, 2, nb, S5_GROUPS, S5_STATE)
            states.append((jnp.transpose(hf[:, 0], (1, 0, 2, 3)), jnp.transpose(hf[:, 1], (1, 0, 2, 3)), sfin))
    return y.reshape(nb, t, D_MODEL), states, caches


def kernel(x_prompt, x_sample, cache_k, cache_v, state_s5_re, state_s5_im, state_dn, c, c_ctx,
           norm_g, w_ada, b_ada, w_in, s5_lam_re, s5_lam_im, s5_log_step, s5_b_re, s5_b_im,
           s5_c_re, s5_c_im, s5_d, s5_w_glu, da_lam, da_norm_g, dn_conv, dn_a_log, dn_dt_bias,
           dn_norm_g, w_branch, w_out, final_norm_g):
    nb_dec = x_sample.shape[0]
    cond8 = jnp.concatenate([c_ctx[None, :], c, jnp.zeros((8 - 1 - nb_dec, D_MODEL), F32)], 0)
    mod = ada_mod(cond8, w_ada, b_ada)
    wts = []
    for l in range(DEPTH):
        w1, w_gates, w2 = cast_w_in(w_in, l)
        wts.append(dict(
            norm_g=norm_g[l], w1=w1, w2=w2, w_gates=w_gates,
            s5_mats=s5_matrices(s5_lam_re[l], s5_lam_im[l], s5_log_step[l], s5_b_re[l], s5_b_im[l],
                                s5_c_re[l], s5_c_im[l]),
            s5_d=s5_d[l], w_glu=s5_w_glu[l].astype(BF16), da_lam=da_lam[l], da_norm_g=da_norm_g[l],
            dn_conv=dn_conv[l], dn_a_log=dn_a_log[l], dn_dt_bias=dn_dt_bias[l], dn_norm_g=dn_norm_g[l],
            w_branch=w_branch[l].astype(BF16), w_out=w_out[l].astype(BF16)))
    lam_inits = [0.8 - 0.6 * math.exp(-0.3 * l) for l in range(DEPTH)]

    y_prompt, states, (k_new, v_new) = _run_pass(x_prompt, mod[:, 0:1], wts, lam_inits, final_norm_g, None)
    cache_kt = jnp.transpose(cache_k, (0, 1, 3, 4, 5, 2)).reshape(nb_dec, DEPTH, DA_HEADS, 2 * DA_HEAD, PAST_LEN)
    ctx = (cache_kt, cache_v.reshape(nb_dec, DEPTH * PAST_LEN, DA_WIDTH), state_s5_re, state_s5_im, state_dn)
    y_sample, _, _ = _run_pass(x_sample, mod[:, 1:1 + nb_dec], wts, lam_inits, final_norm_g, ctx)

    nb, t = x_prompt.shape[:2]
    new_cache_k = k_new.reshape(nb, DEPTH, t, DA_HEADS, 2, DA_HEAD)
    new_cache_v = v_new.reshape(nb, DEPTH, t, DA_HEADS, DA_VDIM)
    new_s5_re = jnp.stack([s[0] for s in states], axis=1)
    new_s5_im = jnp.stack([s[1] for s in states], axis=1)
    new_dn = jnp.stack([s[2] for s in states], axis=1)
    return (y_prompt, y_sample, new_cache_k, new_cache_v, new_s5_re, new_s5_im, new_dn)
```

```python
import functools
import math

import numpy as np
import jax
import jax.numpy as jnp
from jax import lax
from jax.experimental import pallas as pl
from jax.experimental.pallas import tpu as pltpu

F32 = jnp.float32
BF16 = jnp.bfloat16

D_MODEL = 1024
DEPTH = 2
GRID_W = 64
EPS = 1e-6
S5_WIDTH = 512
S5_GROUP = 16
S5_GROUPS = 32
S5_STATE = 64
S5_CHUNK = 16
S5_PAIRS = S5_GROUPS // 2
S5_ROW = S5_CHUNK * S5_GROUP
S5_GEN_GROUPS = 4
DA_HEADS = 4
DA_HEAD = 64
DA_VDIM = 128
DA_WIDTH = 512
ROPE_BASE = 10000.0
DN_HEADS = 4
DN_HEAD = 128
DN_WIDTH = 512
DN_CONV = 5
DN_CHUNK = 64
N_BRANCH = 3
BRANCH_WIDTH = 512
PAST_LEN = 512

COL_UA, COL_ZA, COL_QB, COL_KB, COL_VB, COL_ZB = 0, 512, 1024, 1536, 2048, 2560
COL_QC, COL_ZC = 3072, 4608
N_MAIN = 5120
BA_OFF = 5120
GATES_OFF = 5136

VMEM_LIMIT = 56 * 1024 * 1024
HI = lax.Precision.HIGHEST
LOG2E = math.log2(math.e)
ATTN_ROWS = 16


def _cparams(sem):
    return pltpu.CompilerParams(dimension_semantics=sem, vmem_limit_bytes=VMEM_LIMIT)


def _mm(a, b):
    return jnp.dot(a.astype(BF16), b.astype(BF16), preferred_element_type=F32)


def _mm_nt(a, b):
    return lax.dot_general(a.astype(BF16), b.astype(BF16), (((1,), (1,)), ((), ())),
                           preferred_element_type=F32)


def _silu(x):
    return x * jax.nn.sigmoid(x)


def _ada_kernel(c_ref, w_ref, b_ref, o_ref):
    o_ref[0] = _mm(_silu(c_ref[...]), w_ref[0]) + b_ref[0]


def ada_mod(cond8, w_ada, b_ada):
    tn = 1024
    return pl.pallas_call(
        _ada_kernel,
        grid=(DEPTH, 3 * D_MODEL // tn),
        in_specs=[pl.BlockSpec((8, D_MODEL), lambda l, j: (0, 0)),
                  pl.BlockSpec((1, D_MODEL, tn), lambda l, j: (l, 0, j)),
                  pl.BlockSpec((1, 1, tn), lambda l, j: (l, 0, j))],
        out_specs=pl.BlockSpec((1, 8, tn), lambda l, j: (l, 0, j)),
        out_shape=jax.ShapeDtypeStruct((DEPTH, 8, 3 * D_MODEL), F32),
        compiler_params=_cparams(("parallel", "parallel")),
        name="ada_mod",
    )(cond8, w_ada, b_ada.reshape(DEPTH, 1, 3 * D_MODEL))


def _tcast_kernel(w_ref, o_ref, *, keep):
    x = w_ref[0].T
    if keep is not None:
        x = jnp.where(lax.broadcasted_iota(jnp.int32, x.shape, 1) < keep, x, 0.0)
    o_ref[...] = x.astype(BF16)


def _tcast_rows_kernel(w_hbm, o_ref, buf, sem, *, layer, row0):
    tn = buf.shape[0]
    start = pl.multiple_of(row0 + pl.program_id(0) * tn, 8)
    cp = pltpu.make_async_copy(w_hbm.at[layer, pl.ds(start, tn), :], buf, sem)
    cp.start()
    cp.wait()
    o_ref[...] = buf[...].T.astype(BF16)


def cast_w_in(w_in, layer):
    wt = jnp.swapaxes(w_in, 1, 2)
    tn = 1024
    w1 = pl.pallas_call(
        functools.partial(_tcast_kernel, keep=None),
        grid=(N_MAIN // tn,),
        in_specs=[pl.BlockSpec((1, tn, D_MODEL), lambda j: (layer, j, 0))],
        out_specs=pl.BlockSpec((D_MODEL, tn), lambda j: (0, j)),
        out_shape=jax.ShapeDtypeStruct((D_MODEL, N_MAIN), BF16),
        compiler_params=_cparams(("parallel",)),
        name="cast_w1",
    )(wt)
    gw = N_BRANCH * D_MODEL
    w_gates = pl.pallas_call(
        functools.partial(_tcast_rows_kernel, layer=layer, row0=GATES_OFF),
        grid=(gw // tn,),
        in_specs=[pl.BlockSpec(memory_space=pl.ANY)],
        out_specs=pl.BlockSpec((D_MODEL, tn), lambda j: (0, j)),
        out_shape=jax.ShapeDtypeStruct((D_MODEL, gw), BF16),
        scratch_shapes=[pltpu.VMEM((tn, D_MODEL), F32), pltpu.SemaphoreType.DMA(())],
        compiler_params=_cparams(("parallel",)),
        name="cast_w_gates",
    )(wt)
    w_ba = pl.pallas_call(
        functools.partial(_tcast_kernel, keep=GATES_OFF - BA_OFF),
        grid=(1,),
        in_specs=[pl.BlockSpec((1, 128, D_MODEL), lambda j: (layer, BA_OFF // 128, 0))],
        out_specs=pl.BlockSpec((D_MODEL, 128), lambda j: (0, 0)),
        out_shape=jax.ShapeDtypeStruct((D_MODEL, 128), BF16),
        compiler_params=_cparams(("parallel",)),
        name="cast_w_ba",
    )(wt)
    return w1, w_gates, w_ba


def _inproj_kernel(*refs, n_prev, with_kv):
    x_ref, g_ref, sc_ref, sh_ref, w1_ref, w2_ref = refs[:6]
    prev = refs[6:6 + n_prev]
    outs = refs[6 + n_prev:]
    proj_ref, ba_ref, hn_ref = outs[:3]
    j = pl.program_id(1)

    @pl.when(j == 0)
    def _():
        x = x_ref[...]
        y = x * lax.rsqrt(jnp.mean(x * x, axis=-1, keepdims=True) + EPS) * g_ref[...]
        hn = (y * (1.0 + sc_ref[0]) + sh_ref[0]).astype(BF16)
        hn_ref[...] = hn
        ba_ref[...] = jnp.dot(hn, w2_ref[...], preferred_element_type=F32)

    proj_ref[...] = jnp.dot(hn_ref[...], w1_ref[...], preferred_element_type=F32)

    if with_kv:
        tn = proj_ref.shape[1]
        for c, (ref, col) in enumerate(((outs[3], COL_KB), (outs[4], COL_VB))):
            @pl.when(j == col // tn)
            def _(c=c, ref=ref, col=col):
                nb, nl, t, w = ref.shape
                if n_prev:
                    ref[:, 0:nl - 1] = prev[c][...]
                ref[:, nl - 1:nl] = proj_ref[:, col % tn:col % tn + w].reshape(nb, 1, t, w)


def inproj(x2, norm_g, scale, shift, w1, w2, rows_per_mod, kv=None):
    m = x2.shape[0]
    tm, tn = 1024, 1024
    nmod = scale.shape[0]
    mod_idx = lambda i, j: ((i * tm) // rows_per_mod, 0, 0)
    in_specs = [pl.BlockSpec((tm, D_MODEL), lambda i, j: (i, 0)),
                pl.BlockSpec((1, D_MODEL), lambda i, j: (0, 0)),
                pl.BlockSpec((1, 1, D_MODEL), mod_idx),
                pl.BlockSpec((1, 1, D_MODEL), mod_idx),
                pl.BlockSpec((D_MODEL, tn), lambda i, j: (0, j)),
                pl.BlockSpec((D_MODEL, 128), lambda i, j: (0, 0))]
    args = [x2, norm_g.reshape(1, D_MODEL), scale.reshape(nmod, 1, D_MODEL),
            shift.reshape(nmod, 1, D_MODEL), w1, w2]
    out_specs = [pl.BlockSpec((tm, tn), lambda i, j: (i, j)),
                 pl.BlockSpec((tm, 128), lambda i, j: (i, 0)),
                 pl.BlockSpec((tm, D_MODEL), lambda i, j: (i, 0))]
    out_shape = [jax.ShapeDtypeStruct((m, N_MAIN), F32),
                 jax.ShapeDtypeStruct((m, 128), F32),
                 jax.ShapeDtypeStruct((m, D_MODEL), BF16)]
    n_prev = 0
    if kv is not None:
        layer, t, caches = kv
        cspec = lambda nl: pl.BlockSpec((tm // t, nl, t, DA_WIDTH), lambda i, j: (i, 0, 0, 0))
        out_specs += [cspec(layer + 1)] * 2
        out_shape += [jax.ShapeDtypeStruct((m // t, layer + 1, t, DA_WIDTH), F32)] * 2
        if caches is not None:
            n_prev = 2
            in_specs += [cspec(layer)] * 2
            args += list(caches)
    return pl.pallas_call(
        functools.partial(_inproj_kernel, n_prev=n_prev, with_kv=kv is not None),
        grid=(m // tm, N_MAIN // tn),
        in_specs=in_specs,
        out_specs=out_specs,
        out_shape=out_shape,
        compiler_params=_cparams(("parallel", "arbitrary")),
        name="inproj",
    )(*args)


def _s5_gen_kernel(crt_ref, cit_ref, prt_ref, pit_ref, bbt_ref, bbs_ref, prow_ref, pirow_ref,
                   wt_ref, ws_ref, wh_ref):
    L, C, P = S5_CHUNK, S5_GROUP, S5_STATE
    width = (L + 1) * C
    row = lax.broadcasted_iota(jnp.int32, (128, width), 0)
    lane = lax.broadcasted_iota(jnp.int32, (128, width), 1)
    tile_c = (lane % C == row).astype(F32)
    expand = lambda a, e: jnp.dot(a, e, precision=HI, preferred_element_type=F32)
    zeros = jnp.zeros((C, L * C), F32)
    for g in range(S5_GEN_GROUPS):
        strips = []
        for d in range(2):
            spread_k = ((lane // C if d == 0 else L - lane // C) == row).astype(F32)
            crx, cix = expand(crt_ref[d, g], tile_c), expand(cit_ref[d, g], tile_c)
            prx, pix = expand(prt_ref[d, g], spread_k), expand(pit_ref[d, g], spread_k)
            ca = jnp.concatenate([crx * prx - cix * pix, -(crx * pix + cix * prx)], axis=0)
            wh_ref[d, g] = (ca[:, C:] if d == 0 else ca[:, :L * C]).astype(BF16)
            bbt, bbs = bbt_ref[d, g], bbs_ref[d, g]
            strips.append(jnp.dot(bbt, ca[:, :L * C] if d == 0 else ca[:, C:], precision=HI,
                                  preferred_element_type=F32))
            rows = []
            for i in range(L):
                k = L - 1 - i if d == 0 else i
                rows.append(bbt * prow_ref[d, g, k:k + 1, :] + bbs * pirow_ref[d, g, k:k + 1, :])
            ws_ref[d, g] = jnp.concatenate(rows, axis=0).astype(BF16)
        fpad = jnp.concatenate([zeros, strips[0]], axis=-1)
        rpad = jnp.concatenate([strips[1], zeros], axis=-1)
        rows = []
        for i in range(L):
            rows.append(fpad[:, (L - i) * C:(2 * L - i) * C] + rpad[:, (L - 1 - i) * C:(2 * L - 1 - i) * C])
        wt_ref[g] = jnp.concatenate(rows, axis=0).astype(BF16)


def s5_matrices(lam_re, lam_im, log_step, b_re, b_im, c_re, c_im):
    L, G, P, C = S5_CHUNK, S5_GROUPS, S5_STATE, S5_GROUP
    step = jnp.exp(log_step)[..., None]
    mag = jnp.exp(lam_re * step)
    ar, ai = mag * jnp.cos(lam_im * step), mag * jnp.sin(lam_im * step)
    den = lam_re * lam_re + lam_im * lam_im
    fr = ((ar - 1.0) * lam_re + ai * lam_im) / den
    fi = (ai * lam_re - (ar - 1.0) * lam_im) / den
    bbr = fr[..., None] * b_re - fi[..., None] * b_im
    bbi = fr[..., None] * b_im + fi[..., None] * b_re
    ks = jnp.arange(L + 1, dtype=F32)[None, None, :, None]
    pmag = jnp.exp(ks * (lam_re * step)[:, :, None, :])
    prow = pmag * jnp.cos(ks * (lam_im * step)[:, :, None, :])
    pirow = pmag * jnp.sin(ks * (lam_im * step)[:, :, None, :])
    prow, pirow = lax.optimization_barrier((prow, pirow))
    prt, pit = jnp.swapaxes(prow, 2, 3), jnp.swapaxes(pirow, 2, 3)
    bbrt, bbit = jnp.swapaxes(bbr, 2, 3), jnp.swapaxes(bbi, 2, 3)
    lanes = lambda a: jnp.pad(a, ((0, 0), (0, 0), (0, 0), (0, 128 - a.shape[-1])))
    args = [lanes(jnp.swapaxes(c_re, 2, 3)), lanes(jnp.swapaxes(c_im, 2, 3)), lanes(prt), lanes(pit),
            jnp.concatenate([bbrt, bbit], -1), jnp.concatenate([bbit, bbrt], -1),
            jnp.concatenate([prow, prow], -1), jnp.concatenate([-pirow, pirow], -1)]
    gg = S5_GEN_GROUPS
    spec = lambda a: pl.BlockSpec((2, gg) + a.shape[2:], lambda g: (0, g, 0, 0))
    wt, ws, wh = pl.pallas_call(
        _s5_gen_kernel,
        grid=(G // gg,),
        in_specs=[spec(a) for a in args],
        out_specs=[pl.BlockSpec((gg, S5_ROW, S5_ROW), lambda g: (g, 0, 0)),
                   pl.BlockSpec((2, gg, S5_ROW, 2 * P), lambda g: (0, g, 0, 0)),
                   pl.BlockSpec((2, gg, 2 * P, S5_ROW), lambda g: (0, g, 0, 0))],
        out_shape=[jax.ShapeDtypeStruct((G, S5_ROW, S5_ROW), BF16),
                   jax.ShapeDtypeStruct((2, G, S5_ROW, 2 * P), BF16),
                   jax.ShapeDtypeStruct((2, G, 2 * P, S5_ROW), BF16)],
        compiler_params=_cparams(("parallel",)),
        name="s5_gen",
    )(*args)
    a_l = jnp.stack([prt[..., L].reshape(2, 1, G * P), pit[..., L].reshape(2, 1, G * P)], 1)
    return wt, ws, wh, a_l


S5_GB = 8


def _s5_core_kernel(u_ref, ws_ref, wt_ref, wh_ref, a_ref, h0_ref, y_ref, hfin_ref, x_ref, s_ref, hin_ref,
                    *, nb, nchunk):
    r = nb * nchunk
    npair = S5_GB // 2
    us = [u_ref[pl.ds(i, r, stride=S5_CHUNK), :] for i in range(S5_CHUNK)]
    for g in range(S5_GB):
        xg = jnp.concatenate([u[:, S5_GROUP * g:S5_GROUP * (g + 1)] for u in us], axis=-1)
        x_ref[:, g * S5_ROW:(g + 1) * S5_ROW] = xg.astype(BF16)
    for p in range(npair):
        for d in range(2):
            sg = [jnp.dot(x_ref[:, (2 * p + k) * S5_ROW:(2 * p + k + 1) * S5_ROW], ws_ref[d, 2 * p + k],
                          preferred_element_type=F32) for k in range(2)]
            for comp in range(2):
                cs = slice(comp * S5_STATE, (comp + 1) * S5_STATE)
                s_ref[d, comp, p] = jnp.concatenate([sg[0][:, cs], sg[1][:, cs]], axis=-1)
    chains = [(d, p) for d in range(2) for p in range(npair)]
    h = {}
    for d, p in chains:
        cols = slice(p * 128, (p + 1) * 128)
        h[d, p] = (h0_ref[d, 0, :, cols], h0_ref[d, 1, :, cols], a_ref[d, 0, :, cols], a_ref[d, 1, :, cols])
    for step in range(nchunk):
        for d, p in chains:
            rows = pl.ds(step if d == 0 else nchunk - 1 - step, nb, stride=nchunk)
            hr, hi, ar, ai = h[d, p]
            hin_ref[d, 0, p, rows, :] = hr
            hin_ref[d, 1, p, rows, :] = hi
            sr, si = s_ref[d, 0, p, rows, :], s_ref[d, 1, p, rows, :]
            h[d, p] = (ar * hr - ai * hi + sr, ar * hi + ai * hr + si, ar, ai)
    for d, p in chains:
        cols = slice(p * 128, (p + 1) * 128)
        hfin_ref[d, 0, :, cols] = h[d, p][0]
        hfin_ref[d, 1, :, cols] = h[d, p][1]
    ys = []
    for g in range(S5_GB):
        y = jnp.dot(x_ref[:, g * S5_ROW:(g + 1) * S5_ROW], wt_ref[g], preferred_element_type=F32)
        cs = slice((g % 2) * S5_STATE, (g % 2 + 1) * S5_STATE)
        for d in range(2):
            hg = jnp.concatenate([hin_ref[d, 0, g // 2, :, cs], hin_ref[d, 1, g // 2, :, cs]], axis=-1)
            y = y + jnp.dot(hg.astype(BF16), wh_ref[d, g], preferred_element_type=F32)
        ys.append(y)
    for j in range(S5_CHUNK):
        y_ref[pl.ds(j, r, stride=S5_CHUNK), :] = jnp.concatenate(
            [y[:, S5_GROUP * j:S5_GROUP * (j + 1)] for y in ys], axis=-1)


def s5_core(proj, mats, h0, nb, nchunk):
    wt, ws, wh, a_l = mats
    r = nb * nchunk
    sw = S5_GB * S5_STATE
    return pl.pallas_call(
        functools.partial(_s5_core_kernel, nb=nb, nchunk=nchunk),
        grid=(S5_GROUPS // S5_GB,),
        in_specs=[pl.BlockSpec((r * S5_CHUNK, 128), lambda j: (0, COL_UA // 128 + j)),
                  pl.BlockSpec((2, S5_GB, S5_ROW, 2 * S5_STATE), lambda j: (0, j, 0, 0)),
                  pl.BlockSpec((S5_GB, S5_ROW, S5_ROW), lambda j: (j, 0, 0)),
                  pl.BlockSpec((2, S5_GB, 2 * S5_STATE, S5_ROW), lambda j: (0, j, 0, 0)),
                  pl.BlockSpec((2, 2, 1, sw), lambda j: (0, 0, 0, j)),
                  pl.BlockSpec((2, 2, nb, sw), lambda j: (0, 0, 0, j))],
        out_specs=[pl.BlockSpec((r * S5_CHUNK, 128), lambda j: (0, j)),
                   pl.BlockSpec((2, 2, nb, sw), lambda j: (0, 0, 0, j))],
        out_shape=[jax.ShapeDtypeStruct((r * S5_CHUNK, S5_WIDTH), F32),
                   jax.ShapeDtypeStruct((2, 2, nb, S5_GROUPS * S5_STATE), F32)],
        scratch_shapes=[pltpu.VMEM((r, S5_GB * S5_ROW), BF16),
                        pltpu.VMEM((2, 2, S5_GB // 2, r, 128), F32),
                        pltpu.VMEM((2, 2, S5_GB // 2, r, 128), F32)],
        compiler_params=_cparams(("parallel",)),
        name="s5_core",
    )(proj, ws, wt, wh, a_l, h0)


def _rope_tables(t):
    rows = t // GRID_W
    row = np.repeat(np.arange(rows), GRID_W).astype(np.float32)
    col = np.tile(np.arange(GRID_W), rows).astype(np.float32)
    nf = DA_HEAD // 4
    inv = (ROPE_BASE ** (-jnp.arange(nf, dtype=F32) / nf))

    def tab(pos):
        ang = jnp.asarray(pos)[:, None] * inv[None, :]
        c, s = jnp.cos(ang), jnp.sin(ang)
        return jnp.concatenate([c, c], -1), jnp.concatenate([-s, s], -1)

    cr, sr = tab(row)
    cc, sc = tab(col)
    cos = jnp.concatenate([cr, cc], -1)
    sin = jnp.concatenate([sr, sc], -1)
    return jnp.tile(cos, (1, 2)), jnp.tile(sin, (1, 2))


def _rope(x, cos, sin):
    lane = lax.broadcasted_iota(jnp.int32, x.shape, 1)
    swapped = jnp.where((lane % 32) < 16, pltpu.roll(x, 112, 1), pltpu.roll(x, 16, 1))
    return x * cos + swapped * sin


def _attn_kernel(*refs, lam_init, t, s_tot, tq, with_ctx):
    if with_ctx:
        (q_ref, k_ref, v_ref, z_ref, kc_ref, vc_ref, cq_ref, sq_ref, ck_ref, sk_ref,
         lam_ref, ng_ref, o_ref, kall_ref, vall_ref, s_ref, e_ref, l_ref) = refs
    else:
        q_ref, k_ref, v_ref, z_ref, lam_ref, ng_ref, o_ref, kall_ref, vall_ref, s_ref, e_ref, l_ref = refs

    @pl.when(pl.program_id(1) == 0)
    def _():
        if with_ctx:
            for h in range(DA_HEADS):
                hs = slice(h * 128, (h + 1) * 128)
                kall_ref[h, :, 0:t] = _rope(k_ref[0, :, hs], ck_ref[...], sk_ref[...]).T.astype(BF16)
                kall_ref[h, :, t:s_tot] = kc_ref[0, 0, h].astype(BF16)
            vall_ref[0:t, :] = v_ref[0].astype(BF16)
            vall_ref[t:s_tot, :] = vc_ref[0].astype(BF16)
        else:
            for h in range(DA_HEADS):
                kall_ref[h] = k_ref[0, :, h * 128:(h + 1) * 128].T.astype(BF16)
            vall_ref[...] = v_ref[0].astype(BF16)

    lp = lam_ref[...]
    lam = (jnp.exp(jnp.sum(lp[0:1] * lp[1:2], axis=-1, keepdims=True))
           - jnp.exp(jnp.sum(lp[2:3] * lp[3:4], axis=-1, keepdims=True)) + lam_init)
    lane = lax.broadcasted_iota(jnp.int32, (tq, 128), 1)
    def scores(h):
        hs = slice(h * 128, (h + 1) * 128)
        q = q_ref[0, :, hs]
        if with_ctx:
            q = _rope(q, cq_ref[...], sq_ref[...])
        q = q * (DA_HEAD ** -0.5 * LOG2E)
        qs = jnp.concatenate([jnp.where(lane < DA_HEAD, q, 0.0), jnp.where(lane >= DA_HEAD, q, 0.0)], axis=0)
        s_ref[h % 2] = _mm(qs, kall_ref[h])

    scores(0)
    for h in range(DA_HEADS):
        hs = slice(h * 128, (h + 1) * 128)
        b = h % 2
        if h + 1 < DA_HEADS:
            scores(h + 1)
        for r in range(0, 2 * tq, ATTN_ROWS):
            sc = s_ref[b, r:r + ATTN_ROWS, :]
            e = jnp.exp2(sc - jnp.max(sc, axis=-1, keepdims=True))
            e_ref[b, r:r + ATTN_ROWS, :] = e.astype(BF16)
            l_ref[b, r:r + ATTN_ROWS, :] = jnp.broadcast_to(jnp.sum(e, axis=-1, keepdims=True),
                                                            (ATTN_ROWS, 128))
        ov = jnp.dot(e_ref[b], vall_ref[:, hs], preferred_element_type=F32) / l_ref[b]
        o = ov[:tq] - lam * ov[tq:]
        o = o * lax.rsqrt(jnp.mean(o * o, axis=-1, keepdims=True) + EPS) * ng_ref[...]
        o = o * (1.0 - lam_init)
        o_ref[0, :, hs] = (o * _silu(z_ref[0, :, hs])).astype(BF16)


def diff_attention(proj3, layer, lam_init, da_lam, da_norm_g, ctx_kv):
    nb, t, _ = proj3.shape
    with_ctx = ctx_kv is not None
    s_tot = t + (PAST_LEN if with_ctx else 0)
    tq = 256
    wb = DA_WIDTH
    in_specs = [pl.BlockSpec((1, tq, wb), lambda b, i: (b, i, COL_QB // wb)),
                pl.BlockSpec((1, t, wb), lambda b, i: (b, 0, COL_KB // wb)),
                pl.BlockSpec((1, t, wb), lambda b, i: (b, 0, COL_VB // wb)),
                pl.BlockSpec((1, tq, wb), lambda b, i: (b, i, COL_ZB // wb))]
    args = [proj3, proj3, proj3, proj3]
    if with_ctx:
        kc, vc = ctx_kv
        cos, sin = _rope_tables(t)
        in_specs += [pl.BlockSpec((1, 1, DA_HEADS, 2 * DA_HEAD, PAST_LEN), lambda b, i: (b, layer, 0, 0, 0)),
                     pl.BlockSpec((1, PAST_LEN, wb), lambda b, i: (b, layer, 0)),
                     pl.BlockSpec((tq, 128), lambda b, i: (i, 0)),
                     pl.BlockSpec((tq, 128), lambda b, i: (i, 0)),
                     pl.BlockSpec((t, 128), lambda b, i: (0, 0)),
                     pl.BlockSpec((t, 128), lambda b, i: (0, 0))]
        args += [kc, vc, cos, sin, cos, sin]
    in_specs += [pl.BlockSpec((4, DA_HEAD), lambda b, i: (0, 0)),
                 pl.BlockSpec((1, DA_VDIM), lambda b, i: (0, 0))]
    args += [da_lam, da_norm_g.reshape(1, DA_VDIM)]
    return pl.pallas_call(
        functools.partial(_attn_kernel, lam_init=lam_init, t=t, s_tot=s_tot, tq=tq, with_ctx=with_ctx),
        grid=(nb, t // tq),
        in_specs=in_specs,
        out_specs=pl.BlockSpec((1, tq, wb), lambda b, i: (b, i, 0)),
        out_shape=jax.ShapeDtypeStruct((nb, t, wb), BF16),
        scratch_shapes=[pltpu.VMEM((DA_HEADS, 2 * DA_HEAD, s_tot), BF16), pltpu.VMEM((s_tot, wb), BF16),
                        pltpu.VMEM((2, 2 * tq, s_tot), F32), pltpu.VMEM((2, 2 * tq, s_tot), BF16),
                        pltpu.VMEM((2, 2 * tq, 128), F32)],
        compiler_params=_cparams(("parallel", "arbitrary")),
        name="diff_attention",
    )(*args)


DN_PAD = 8
DN_RT = 128
DN_GROUP = 128
DN_STEP_GROUPS = 2


def _dn_kernel(*refs, t, ngroup, with_s0):
    if with_s0:
        (qkv_ref, z_ref, ba_ref, cw_ref, alog_ref, dtb_ref, ng_ref, s0_ref,
         out_ref, sfin_ref, xp_ref, qkvn_ref, oacc_ref, st_ref) = refs
    else:
        (qkv_ref, z_ref, ba_ref, cw_ref, alog_ref, dtb_ref, ng_ref,
         out_ref, sfin_ref, xp_ref, qkvn_ref, oacc_ref, st_ref) = refs
    n = pl.program_id(1)
    cd = DN_CHUNK
    w3 = 3 * DN_WIDTH

    @pl.when(n == 0)
    def _init():
        xp_ref[0:DN_PAD, :] = jnp.zeros((DN_PAD, w3), F32)
        xp_ref[DN_PAD + t:2 * DN_PAD + t, :] = jnp.zeros((DN_PAD, w3), F32)
        xp_ref[DN_PAD:DN_PAD + t, :] = qkv_ref[0]
        half = DN_CONV // 2
        for r in range(t // DN_RT):
            for sec in range(3):
                for h in range(DN_HEADS):
                    cs = slice(sec * DN_WIDTH + h * DN_HEAD, sec * DN_WIDTH + (h + 1) * DN_HEAD)
                    acc = jnp.zeros((DN_RT, DN_HEAD), F32)
                    for j in range(DN_CONV):
                        r0 = DN_PAD + r * DN_RT + j - half
                        acc = acc + xp_ref[r0:r0 + DN_RT, cs] * cw_ref[j:j + 1, cs]
                    y = _silu(acc)
                    if sec < 2:
                        y = y * lax.rsqrt(jnp.sum(y * y, axis=-1, keepdims=True) + EPS)
                    if sec == 0:
                        y = y * (DN_HEAD ** -0.5)
                    qkvn_ref[r * DN_RT:(r + 1) * DN_RT, cs] = y
        oacc_ref[...] = jnp.zeros_like(oacc_ref)
        if with_s0:
            st_ref[...] = s0_ref[0, 0]
        else:
            st_ref[...] = jnp.zeros_like(st_ref)

    gb = DN_GROUP
    nsub = gb // cd
    ri = lax.broadcasted_iota(jnp.int32, (gb, gb), 0)
    ci = lax.broadcasted_iota(jnp.int32, (gb, gb), 1)
    same = (ri // cd) == (ci // cd)
    samef = same.astype(F32)
    eye = (ri == ci).astype(BF16)
    masked_out = -1e30

    chains = []
    for d, gi in [(d, gi) for d in range(2) for gi in range(DN_STEP_GROUPS)]:
        grp = n * DN_STEP_GROUPS + gi
        r0 = pl.multiple_of((grp if d == 0 else ngroup - 1 - grp) * gb, gb)
        incl = same & ((ci <= ri) if d == 0 else (ci >= ri))
        strict = same & ((ci < ri) if d == 0 else (ci > ri))
        inclf = incl.astype(F32)
        inclog = jnp.where(incl, 0.0, masked_out)
        nstrict = -strict.astype(BF16)
        ba = ba_ref[0, pl.ds(r0, gb), :]
        beta_all = jax.nn.sigmoid(ba)
        g_all = -jnp.exp(alog_ref[...]) * jax.nn.softplus(ba + dtb_ref[...])
        gc = jnp.dot(inclf, g_all, precision=HI, preferred_element_type=F32)
        gct = lax.dot_general(g_all.T, inclf, (((1,), (1,)), ((), ())), precision=HI,
                              preferred_element_type=F32)
        gtot = jnp.dot(samef, g_all, precision=HI, preferred_element_type=F32)
        for h in range(DN_HEADS):
            hs = slice(h * DN_HEAD, (h + 1) * DN_HEAD)
            col = 2 * DN_HEADS + d * DN_HEADS + h
            gcol = gc[:, col:col + 1]
            grow = gct[col:col + 1, :]
            gt = gtot[:, col:col + 1]
            beta = beta_all[:, d * DN_HEADS + h:d * DN_HEADS + h + 1]
            q = qkvn_ref[pl.ds(r0, gb), hs]
            k = qkvn_ref[pl.ds(r0, gb), DN_WIDTH + h * DN_HEAD:DN_WIDTH + (h + 1) * DN_HEAD]
            v = qkvn_ref[pl.ds(r0, gb), 2 * DN_WIDTH + h * DN_HEAD:2 * DN_WIDTH + (h + 1) * DN_HEAD]
            eg = jnp.exp(gcol)
            chains.append(dict(
                d=d, gi=gi, h=h, r0=r0, hs=hs, nstrict=nstrict, q=q, k=k, kb=k * beta,
                decay=jnp.exp((gcol - grow) + inclog),
                rhs=jnp.concatenate([v * beta, k * beta * eg], axis=-1),
                qe=q * eg, kdec=k * jnp.exp(gt - gcol), egt=jnp.exp(gt)))

    for c in chains:
        c["nmm"] = (_mm_nt(c["kb"], c["k"]) * c["decay"]).astype(BF16) * c["nstrict"]
    for c in chains:
        c["qk"] = _mm_nt(c["q"], c["k"]) * c["decay"]
    def level_mask(s):
        return (((ri // (2 * s)) == (ci // (2 * s))) & ((ri // s) != (ci // s))).astype(BF16)

    pair = level_mask(1)
    for c in chains:
        c["tinv"] = eye + c["nmm"] * pair
    s = 2
    while s < cd:
        offmask = level_mask(s)
        xs = [jnp.dot(c["nmm"] * offmask, c["tinv"], preferred_element_type=F32).astype(BF16) for c in chains]
        ys = [jnp.dot(c["tinv"], x, preferred_element_type=F32) for c, x in zip(chains, xs)]
        for c, y in zip(chains, ys):
            c["tinv"] = c["tinv"] + y.astype(BF16)
        s *= 2
    for c in chains:
        c["uw"] = _mm(c["tinv"], c["rhs"])

    st = {(d, h): st_ref[d, h] for d in range(2) for h in range(DN_HEADS)}
    outs = []
    for gi, step in [(gi, step) for gi in range(DN_STEP_GROUPS) for step in range(nsub)]:
        active = [c for c in chains if c["gi"] == gi]
        rs = []
        for c in active:
            sub = step if c["d"] == 0 else nsub - 1 - step
            c["rows"] = slice(sub * cd, (sub + 1) * cd)
            rs.append(_mm(jnp.concatenate([c["uw"][c["rows"], DN_HEAD:], c["qe"][c["rows"]]], axis=0),
                          st[c["d"], c["h"]]))
        for c, r in zip(active, rs):
            rows = c["rows"]
            v_new = c["uw"][rows, :DN_HEAD] - r[:cd]
            o = r[cd:] + _mm(c["qk"][rows, rows], v_new)
            st[c["d"], c["h"]] = (st[c["d"], c["h"]] * c["egt"][rows.start:rows.start + 1]
                                  + _mm(c["kdec"][rows].T, v_new))
            outs.append((c, rows.start, o))
    for (d, h), v in st.items():
        st_ref[d, h] = v
    for c, off, o in outs:
        oacc_ref[pl.ds(pl.multiple_of(c["r0"] + off, cd), cd), c["hs"]] += o

    @pl.when(n == ngroup // DN_STEP_GROUPS - 1)
    def _fin():
        for h in range(DN_HEADS):
            hs = slice(h * DN_HEAD, (h + 1) * DN_HEAD)
            o = oacc_ref[:, hs]
            o = o * lax.rsqrt(jnp.mean(o * o, axis=-1, keepdims=True) + EPS) * ng_ref[...]
            out_ref[0, :, hs] = (o * _silu(z_ref[0, :, hs])).astype(BF16)
        sfin_ref[0] = st_ref[...]


def deltanet(proj3, ba3, layer, conv_w, a_log, dt_bias, norm_g, s0):
    nb, t, _ = proj3.shape
    ngroup = t // DN_GROUP
    with_s0 = s0 is not None
    w3 = 3 * DN_WIDTH
    pad = jnp.zeros((2 * DN_HEADS,), F32)
    alog_row = jnp.concatenate([pad, a_log.reshape(-1), jnp.zeros((128 - 4 * DN_HEADS,), F32)]).reshape(1, 128)
    dtb_row = jnp.concatenate([pad, dt_bias.reshape(-1), jnp.zeros((128 - 4 * DN_HEADS,), F32)]).reshape(1, 128)
    in_specs = [pl.BlockSpec((1, t, w3), lambda b, n: (b, 0, COL_QC // w3)),
                pl.BlockSpec((1, t, DN_WIDTH), lambda b, n: (b, 0, COL_ZC // DN_WIDTH)),
                pl.BlockSpec((1, t, 128), lambda b, n: (b, 0, 0)),
                pl.BlockSpec((8, w3), lambda b, n: (0, 0)),
                pl.BlockSpec((1, 128), lambda b, n: (0, 0)),
                pl.BlockSpec((1, 128), lambda b, n: (0, 0)),
                pl.BlockSpec((1, DN_HEAD), lambda b, n: (0, 0))]
    args = [proj3, proj3, ba3, jnp.pad(conv_w, ((0, 8 - DN_CONV), (0, 0))), alog_row, dtb_row,
            norm_g.reshape(1, DN_HEAD)]
    if with_s0:
        in_specs.append(pl.BlockSpec((1, 1, 2, DN_HEADS, DN_HEAD, DN_HEAD), lambda b, n: (b, layer, 0, 0, 0, 0)))
        args.append(s0)
    return pl.pallas_call(
        functools.partial(_dn_kernel, t=t, ngroup=ngroup, with_s0=with_s0),
        grid=(nb, ngroup // DN_STEP_GROUPS),
        in_specs=in_specs,
        out_specs=[pl.BlockSpec((1, t, DN_WIDTH), lambda b, n: (b, 0, 0)),
                   pl.BlockSpec((1, 2, DN_HEADS, DN_HEAD, DN_HEAD), lambda b, n: (b, 0, 0, 0, 0))],
        out_shape=[jax.ShapeDtypeStruct((nb, t, DN_WIDTH), BF16),
                   jax.ShapeDtypeStruct((nb, 2, DN_HEADS, DN_HEAD, DN_HEAD), F32)],
        scratch_shapes=[pltpu.VMEM((t + 2 * DN_PAD, w3), F32),
                        pltpu.VMEM((t, w3), F32),
                        pltpu.VMEM((t, DN_WIDTH), F32),
                        pltpu.VMEM((2, DN_HEADS, DN_HEAD, DN_HEAD), F32)],
        compiler_params=_cparams(("parallel", "arbitrary")),
        name="deltanet",
    )(*args)


def _merge_kernel(u_ref, ys_ref, za_ref, sd_ref, wglu_ref, ob_ref, oc_ref, hn_ref, wg_ref, wb_ref, wo_ref,
                  x_ref, gate_ref, fg_ref, *outs, final):
    tm = x_ref.shape[0]
    for rows in (slice(0, tm // 2), slice(tm // 2, tm)):
        ya = jax.nn.gelu(sd_ref[...] * u_ref[rows, :] + ys_ref[rows, :])
        ya = ya * jax.nn.sigmoid(_mm(ya, wglu_ref[...]))
        branches = ((ya * _silu(za_ref[rows, :])).astype(BF16), ob_ref[rows, :], oc_ref[rows, :])
        acc = None
        hn = hn_ref[rows, :]
        for i, o in enumerate(branches):
            pr = jnp.dot(o, wb_ref[i], preferred_element_type=F32)
            gt = jnp.dot(hn, wg_ref[:, i * D_MODEL:(i + 1) * D_MODEL], preferred_element_type=F32)
            term = jax.nn.sigmoid(gt) * pr
            acc = term if acc is None else acc + term
        y = jnp.dot(acc.astype(BF16), wo_ref[...], preferred_element_type=F32)
        xn = x_ref[rows, :] + gate_ref[0] * y
        outs[0][rows, :] = xn
        if final:
            yn = xn * lax.rsqrt(jnp.mean(xn * xn, axis=-1, keepdims=True) + EPS) * fg_ref[...]
            outs[1][rows, :] = yn


def merge(proj, y_s5, s5_d, w_glu, out_b, out_c, hn, w_gates, w_branch, w_out, x2, gate, final_g, rows_per_mod,
          final):
    m = x2.shape[0]
    tm = 512
    nmod = gate.shape[0]
    row = lambda i: (i, 0)
    out_specs = [pl.BlockSpec((tm, D_MODEL), row)]
    out_shape = [jax.ShapeDtypeStruct((m, D_MODEL), F32)]
    if final:
        out_specs.append(pl.BlockSpec((tm, D_MODEL), row))
        out_shape.append(jax.ShapeDtypeStruct((m, D_MODEL), F32))
    return pl.pallas_call(
        functools.partial(_merge_kernel, final=final),
        grid=(m // tm,),
        in_specs=[pl.BlockSpec((tm, S5_WIDTH), lambda i: (i, COL_UA // S5_WIDTH)),
                  pl.BlockSpec((tm, S5_WIDTH), row),
                  pl.BlockSpec((tm, S5_WIDTH), lambda i: (i, COL_ZA // S5_WIDTH)),
                  pl.BlockSpec((1, S5_WIDTH), lambda i: (0, 0)),
                  pl.BlockSpec((S5_WIDTH, S5_WIDTH), lambda i: (0, 0)),
                  pl.BlockSpec((tm, BRANCH_WIDTH), row),
                  pl.BlockSpec((tm, BRANCH_WIDTH), row),
                  pl.BlockSpec((tm, D_MODEL), row),
                  pl.BlockSpec((D_MODEL, N_BRANCH * D_MODEL), lambda i: (0, 0)),
                  pl.BlockSpec((N_BRANCH, BRANCH_WIDTH, D_MODEL), lambda i: (0, 0, 0)),
                  pl.BlockSpec((D_MODEL, D_MODEL), lambda i: (0, 0)),
                  pl.BlockSpec((tm, D_MODEL), row),
                  pl.BlockSpec((1, 1, D_MODEL), lambda i: ((i * tm) // rows_per_mod, 0, 0)),
                  pl.BlockSpec((1, D_MODEL), lambda i: (0, 0))],
        out_specs=out_specs,
        out_shape=out_shape,
        compiler_params=_cparams(("parallel",)),
        name="merge",
    )(proj, y_s5, proj, s5_d.reshape(1, S5_WIDTH), w_glu, out_b, out_c, hn, w_gates, w_branch, w_out, x2,
      gate.reshape(nmod, 1, D_MODEL),
      final_g.reshape(1, D_MODEL))


def _run_pass(x, mod, wts, lam_inits, final_g, ctx):
    nb, t, _ = x.shape
    m = nb * t
    nmod = mod.shape[1]
    rows_per_mod = m if nmod == 1 else t
    x2 = x.reshape(m, D_MODEL)
    states = []
    y, caches = None, None
    for l in range(DEPTH):
        w = wts[l]
        shift, scale, gate = jnp.split(mod[l], 3, axis=-1)
        if ctx is None:
            proj, ba, hn, *caches = inproj(x2, w["norm_g"], scale, shift, w["w1"], w["w2"], rows_per_mod,
                                       kv=(l, t, caches))
            h0 = jnp.zeros((2, 2, nb, S5_GROUPS * S5_STATE), F32)
            ctx_kv, s0 = None, None
        else:
            proj, ba, hn = inproj(x2, w["norm_g"], scale, shift, w["w1"], w["w2"], rows_per_mod)
            cache_k, cache_v, st_re, st_im, st_dn = ctx
            h0 = jnp.stack([st_re[:, l], st_im[:, l]], 0)
            h0 = jnp.transpose(h0, (2, 0, 1, 3, 4)).reshape(2, 2, nb, S5_GROUPS * S5_STATE)
            ctx_kv, s0 = (cache_k, cache_v), st_dn
        proj3 = proj.reshape(nb, t, N_MAIN)
        y_s5, hfin = s5_core(proj, w["s5_mats"], h0, nb, t // S5_CHUNK)
        out_b = diff_attention(proj3, l, lam_inits[l], w["da_lam"], w["da_norm_g"], ctx_kv)
        out_c, sfin = deltanet(proj3, ba.reshape(nb, t, 128), l, w["dn_conv"], w["dn_a_log"],
                               w["dn_dt_bias"], w["dn_norm_g"], s0)
        final = l == DEPTH - 1
        res = merge(proj, y_s5, w["s5_d"], w["w_glu"], out_b.reshape(m, DA_WIDTH), out_c.reshape(m, DN_WIDTH),
                    hn, w["w_gates"], w["w_branch"], w["w_out"], x2, gate, final_g, rows_per_mod, final)
        x2 = res[0]
        if final:
            y = res[1]
        if ctx is None:
            hf = hfin.reshape(2, 2, nb, S5_GROUPS, S5_STATE)
            states.append((jnp.transpose(hf[:, 0], (1, 0, 2, 3)), jnp.transpose(hf[:, 1], (1, 0, 2, 3)), sfin))
    return y.reshape(nb, t, D_MODEL), states, caches


def kernel(x_prompt, x_sample, cache_k, cache_v, state_s5_re, state_s5_im, state_dn, c, c_ctx,
           norm_g, w_ada, b_ada, w_in, s5_lam_re, s5_lam_im, s5_log_step, s5_b_re, s5_b_im,
           s5_c_re, s5_c_im, s5_d, s5_w_glu, da_lam, da_norm_g, dn_conv, dn_a_log, dn_dt_bias,
           dn_norm_g, w_branch, w_out, final_norm_g):
    nb_dec = x_sample.shape[0]
    cond8 = jnp.concatenate([c_ctx[None, :], c, jnp.zeros((8 - 1 - nb_dec, D_MODEL), F32)], 0)
    mod = ada_mod(cond8, w_ada, b_ada)
    wts = []
    for l in range(DEPTH):
        w1, w_gates, w2 = cast_w_in(w_in, l)
        wts.append(dict(
            norm_g=norm_g[l], w1=w1, w2=w2, w_gates=w_gates,
            s5_mats=s5_matrices(s5_lam_re[l], s5_lam_im[l], s5_log_step[l], s5_b_re[l], s5_b_im[l],
                                s5_c_re[l], s5_c_im[l]),
            s5_d=s5_d[l], w_glu=s5_w_glu[l].astype(BF16), da_lam=da_lam[l], da_norm_g=da_norm_g[l],
            dn_conv=dn_conv[l], dn_a_log=dn_a_log[l], dn_dt_bias=dn_dt_bias[l], dn_norm_g=dn_norm_g[l],
            w_branch=w_branch[l].astype(BF16), w_out=w_out[l].astype(BF16)))
    lam_inits = [0.8 - 0.6 * math.exp(-0.3 * l) for l in range(DEPTH)]

    y_prompt, states, (k_new, v_new) = _run_pass(x_prompt, mod[:, 0:1], wts, lam_inits, final_norm_g, None)
    cache_kt = jnp.transpose(cache_k, (0, 1, 3, 4, 5, 2)).reshape(nb_dec, DEPTH, DA_HEADS, 2 * DA_HEAD, PAST_LEN)
    ctx = (cache_kt, cache_v.reshape(nb_dec, DEPTH * PAST_LEN, DA_WIDTH), state_s5_re, state_s5_im, state_dn)
    y_sample, _, _ = _run_pass(x_sample, mod[:, 1:1 + nb_dec], wts, lam_inits, final_norm_g, ctx)

    nb, t = x_prompt.shape[:2]
    new_cache_k = k_new.reshape(nb, DEPTH, t, DA_HEADS, 2, DA_HEAD)
    new_cache_v = v_new.reshape(nb, DEPTH, t, DA_HEADS, DA_VDIM)
    new_s5_re = jnp.stack([s[0] for s in states], axis=1)
    new_s5_im = jnp.stack([s[1] for s in states], axis=1)
    new_dn = jnp.stack([s[2] for s in states], axis=1)
    return (y_prompt, y_sample, new_cache_k, new_cache_v, new_s5_re, new_s5_im, new_dn)
```

```python
import functools
import math

import numpy as np
import jax
import jax.numpy as jnp
from jax import lax
from jax.experimental import pallas as pl
from jax.experimental.pallas import tpu as pltpu

F32 = jnp.float32
BF16 = jnp.bfloat16

D_MODEL = 1024
DEPTH = 2
GRID_W = 64
EPS = 1e-6
S5_WIDTH = 512
S5_GROUP = 16
S5_GROUPS = 32
S5_STATE = 64
S5_CHUNK = 16
S5_PAIRS = S5_GROUPS // 2
S5_ROW = S5_CHUNK * S5_GROUP
S5_GEN_GROUPS = 4
DA_HEADS = 4
DA_HEAD = 64
DA_VDIM = 128
DA_WIDTH = 512
ROPE_BASE = 10000.0
DN_HEADS = 4
DN_HEAD = 128
DN_WIDTH = 512
DN_CONV = 5
DN_CHUNK = 64
N_BRANCH = 3
BRANCH_WIDTH = 512
PAST_LEN = 512

FCOL_QC, FCOL_UA, N_F32 = 0, 1536, 2048
HCOL_ZA, HCOL_QB, HCOL_KB, HCOL_VB, HCOL_ZB, HCOL_ZC, N_BF16 = 0, 512, 1024, 1536, 2048, 2560, 3072
N_MAIN = N_F32 + N_BF16
W_BLOCK_ORDER = (6, 7, 8, 0, 1, 2, 3, 4, 5, 9)
KV_TILE = (N_F32 + HCOL_KB) // 1024
BA_OFF = 5120
GATES_OFF = 5136

VMEM_LIMIT = 56 * 1024 * 1024
HI = lax.Precision.HIGHEST
LOG2E = math.log2(math.e)
ATTN_ROWS = 16


def _cparams(sem):
    return pltpu.CompilerParams(dimension_semantics=sem, vmem_limit_bytes=VMEM_LIMIT)


def _mm(a, b):
    return jnp.dot(a.astype(BF16), b.astype(BF16), preferred_element_type=F32)


def _mm_nt(a, b):
    return lax.dot_general(a.astype(BF16), b.astype(BF16), (((1,), (1,)), ((), ())),
                           preferred_element_type=F32)


def _silu(x):
    return x * jax.nn.sigmoid(x)


def _ada_kernel(c_ref, w_ref, b_ref, o_ref):
    o_ref[0] = _mm(_silu(c_ref[...]), w_ref[0]) + b_ref[0]


def ada_mod(cond8, w_ada, b_ada):
    tn = 1024
    return pl.pallas_call(
        _ada_kernel,
        grid=(DEPTH, 3 * D_MODEL // tn),
        in_specs=[pl.BlockSpec((8, D_MODEL), lambda l, j: (0, 0)),
                  pl.BlockSpec((1, D_MODEL, tn), lambda l, j: (l, 0, j)),
                  pl.BlockSpec((1, 1, tn), lambda l, j: (l, 0, j))],
        out_specs=pl.BlockSpec((1, 8, tn), lambda l, j: (l, 0, j)),
        out_shape=jax.ShapeDtypeStruct((DEPTH, 8, 3 * D_MODEL), F32),
        compiler_params=_cparams(("parallel", "parallel")),
        name="ada_mod",
    )(cond8, w_ada, b_ada.reshape(DEPTH, 1, 3 * D_MODEL))


def _tcast_kernel(w_ref, o_ref, *, keep):
    x = w_ref[0].T
    if keep is not None:
        x = jnp.where(lax.broadcasted_iota(jnp.int32, x.shape, 1) < keep, x, 0.0)
    o_ref[...] = x.astype(BF16)


def _tcast_perm_kernel(perm_ref, w_ref, o_ref):
    del perm_ref
    o_ref[...] = w_ref[0].T.astype(BF16)


def _tcast_rows_kernel(w_hbm, o_ref, buf, sem, *, layer, row0):
    tn = buf.shape[0]
    start = pl.multiple_of(row0 + pl.program_id(0) * tn, 8)
    cp = pltpu.make_async_copy(w_hbm.at[layer, pl.ds(start, tn), :], buf, sem)
    cp.start()
    cp.wait()
    o_ref[...] = buf[...].T.astype(BF16)


def cast_w_in(w_in, layer):
    wt = jnp.swapaxes(w_in, 1, 2)
    tn = 1024
    wb = 512
    w1 = pl.pallas_call(
        _tcast_perm_kernel,
        grid_spec=pltpu.PrefetchScalarGridSpec(
            num_scalar_prefetch=1, grid=(N_MAIN // wb,),
            in_specs=[pl.BlockSpec((1, wb, D_MODEL), lambda j, perm: (layer, perm[j], 0))],
            out_specs=pl.BlockSpec((D_MODEL, wb), lambda j, perm: (0, j))),
        out_shape=jax.ShapeDtypeStruct((D_MODEL, N_MAIN), BF16),
        compiler_params=_cparams(("arbitrary",)),
        name="cast_w1",
    )(jnp.asarray(W_BLOCK_ORDER, jnp.int32), wt)
    gw = N_BRANCH * D_MODEL
    w_gates = pl.pallas_call(
        functools.partial(_tcast_rows_kernel, layer=layer, row0=GATES_OFF),
        grid=(gw // tn,),
        in_specs=[pl.BlockSpec(memory_space=pl.ANY)],
        out_specs=pl.BlockSpec((D_MODEL, tn), lambda j: (0, j)),
        out_shape=jax.ShapeDtypeStruct((D_MODEL, gw), BF16),
        scratch_shapes=[pltpu.VMEM((tn, D_MODEL), F32), pltpu.SemaphoreType.DMA(())],
        compiler_params=_cparams(("parallel",)),
        name="cast_w_gates",
    )(wt)
    w_ba = pl.pallas_call(
        functools.partial(_tcast_kernel, keep=GATES_OFF - BA_OFF),
        grid=(1,),
        in_specs=[pl.BlockSpec((1, 128, D_MODEL), lambda j: (layer, BA_OFF // 128, 0))],
        out_specs=pl.BlockSpec((D_MODEL, 128), lambda j: (0, 0)),
        out_shape=jax.ShapeDtypeStruct((D_MODEL, 128), BF16),
        compiler_params=_cparams(("parallel",)),
        name="cast_w_ba",
    )(wt)
    return w1, w_gates, w_ba


def _inproj_kernel(*refs, n_prev, with_kv):
    x_ref, g_ref, sc_ref, sh_ref, w1_ref, w2_ref = refs[:6]
    prev = refs[6:6 + n_prev]
    outs = refs[6 + n_prev:]
    projf_ref, projh_ref, ba_ref, hn_ref = outs[:4]
    j = pl.program_id(1)
    tn = projf_ref.shape[1]

    @pl.when(j == 0)
    def _():
        x = x_ref[...]
        y = x * lax.rsqrt(jnp.mean(x * x, axis=-1, keepdims=True) + EPS) * g_ref[...]
        hn = (y * (1.0 + sc_ref[0]) + sh_ref[0]).astype(BF16)
        hn_ref[...] = hn
        ba_ref[...] = jnp.dot(hn, w2_ref[...], preferred_element_type=F32)

    res = jnp.dot(hn_ref[...], w1_ref[...], preferred_element_type=F32)

    @pl.when(j < N_F32 // tn)
    def _():
        projf_ref[...] = res

    @pl.when(j >= N_F32 // tn)
    def _():
        projh_ref[...] = res.astype(BF16)

    if with_kv:
        @pl.when(j == KV_TILE)
        def _():
            for c, ref in enumerate(outs[4:6]):
                nb, nl, t, w = ref.shape
                if n_prev:
                    ref[:, 0:nl - 1] = prev[c][...]
                ref[:, nl - 1:nl] = res[:, c * w:(c + 1) * w].reshape(nb, 1, t, w)


def inproj(x2, norm_g, scale, shift, w1, w2, rows_per_mod, kv=None):
    m = x2.shape[0]
    tm, tn = 1024, 1024
    nmod = scale.shape[0]
    mod_idx = lambda i, j: ((i * tm) // rows_per_mod, 0, 0)
    in_specs = [pl.BlockSpec((tm, D_MODEL), lambda i, j: (i, 0)),
                pl.BlockSpec((1, D_MODEL), lambda i, j: (0, 0)),
                pl.BlockSpec((1, 1, D_MODEL), mod_idx),
                pl.BlockSpec((1, 1, D_MODEL), mod_idx),
                pl.BlockSpec((D_MODEL, tn), lambda i, j: (0, j)),
                pl.BlockSpec((D_MODEL, 128), lambda i, j: (0, 0))]
    args = [x2, norm_g.reshape(1, D_MODEL), scale.reshape(nmod, 1, D_MODEL),
            shift.reshape(nmod, 1, D_MODEL), w1, w2]
    nf = N_F32 // tn
    out_specs = [pl.BlockSpec((tm, tn), lambda i, j: (i, jnp.minimum(j, nf - 1))),
                 pl.BlockSpec((tm, tn), lambda i, j: (i, jnp.maximum(j - nf, 0))),
                 pl.BlockSpec((tm, 128), lambda i, j: (i, 0)),
                 pl.BlockSpec((tm, D_MODEL), lambda i, j: (i, 0))]
    out_shape = [jax.ShapeDtypeStruct((m, N_F32), F32),
                 jax.ShapeDtypeStruct((m, N_BF16), BF16),
                 jax.ShapeDtypeStruct((m, 128), F32),
                 jax.ShapeDtypeStruct((m, D_MODEL), BF16)]
    n_prev = 0
    if kv is not None:
        layer, t, caches = kv
        cspec = lambda nl: pl.BlockSpec((tm // t, nl, t, DA_WIDTH), lambda i, j: (i, 0, 0, 0))
        out_specs += [cspec(layer + 1)] * 2
        out_shape += [jax.ShapeDtypeStruct((m // t, layer + 1, t, DA_WIDTH), F32)] * 2
        if caches is not None:
            n_prev = 2
            once = pl.Buffered(1)
            in_specs[0] = pl.BlockSpec((tm, D_MODEL), lambda i, j: (i, 0), pipeline_mode=once)
            in_specs += [pl.BlockSpec((tm // t, layer, t, DA_WIDTH), lambda i, j: (i, 0, 0, 0),
                                      pipeline_mode=once)] * 2
            args += list(caches)
    return pl.pallas_call(
        functools.partial(_inproj_kernel, n_prev=n_prev, with_kv=kv is not None),
        grid=(m // tm, N_MAIN // tn),
        in_specs=in_specs,
        out_specs=out_specs,
        out_shape=out_shape,
        compiler_params=_cparams(("parallel", "arbitrary")),
        name="inproj",
    )(*args)


def _s5_gen_kernel(crt_ref, cit_ref, prt_ref, pit_ref, bbt_ref, bbs_ref, prow_ref, pirow_ref,
                   wt_ref, ws_ref, wh_ref):
    L, C, P = S5_CHUNK, S5_GROUP, S5_STATE
    width = (L + 1) * C
    row = lax.broadcasted_iota(jnp.int32, (128, width), 0)
    lane = lax.broadcasted_iota(jnp.int32, (128, width), 1)
    tile_c = (lane % C == row).astype(F32)
    expand = lambda a, e: jnp.dot(a, e, precision=HI, preferred_element_type=F32)
    zeros = jnp.zeros((C, L * C), F32)
    for g in range(S5_GEN_GROUPS):
        strips = []
        for d in range(2):
            spread_k = ((lane // C if d == 0 else L - lane // C) == row).astype(F32)
            crx, cix = expand(crt_ref[d, g], tile_c), expand(cit_ref[d, g], tile_c)
            prx, pix = expand(prt_ref[d, g], spread_k), expand(pit_ref[d, g], spread_k)
            ca = jnp.concatenate([crx * prx - cix * pix, -(crx * pix + cix * prx)], axis=0)
            wh_ref[d, g] = (ca[:, C:] if d == 0 else ca[:, :L * C]).astype(BF16)
            bbt, bbs = bbt_ref[d, g], bbs_ref[d, g]
            strips.append(jnp.dot(bbt, ca[:, :L * C] if d == 0 else ca[:, C:], precision=HI,
                                  preferred_element_type=F32))
            rows = []
            for i in range(L):
                k = L - 1 - i if d == 0 else i
                rows.append(bbt * prow_ref[d, g, k:k + 1, :] + bbs * pirow_ref[d, g, k:k + 1, :])
            ws_ref[d, g] = jnp.concatenate(rows, axis=0).astype(BF16)
        fpad = jnp.concatenate([zeros, strips[0]], axis=-1)
        rpad = jnp.concatenate([strips[1], zeros], axis=-1)
        rows = []
        for i in range(L):
            rows.append(fpad[:, (L - i) * C:(2 * L - i) * C] + rpad[:, (L - 1 - i) * C:(2 * L - 1 - i) * C])
        wt_ref[g] = jnp.concatenate(rows, axis=0).astype(BF16)


def s5_matrices(lam_re, lam_im, log_step, b_re, b_im, c_re, c_im):
    L, G, P, C = S5_CHUNK, S5_GROUPS, S5_STATE, S5_GROUP
    step = jnp.exp(log_step)[..., None]
    mag = jnp.exp(lam_re * step)
    ar, ai = mag * jnp.cos(lam_im * step), mag * jnp.sin(lam_im * step)
    den = lam_re * lam_re + lam_im * lam_im
    fr = ((ar - 1.0) * lam_re + ai * lam_im) / den
    fi = (ai * lam_re - (ar - 1.0) * lam_im) / den
    bbr = fr[..., None] * b_re - fi[..., None] * b_im
    bbi = fr[..., None] * b_im + fi[..., None] * b_re
    ks = jnp.arange(L + 1, dtype=F32)[None, None, :, None]
    pmag = jnp.exp(ks * (lam_re * step)[:, :, None, :])
    prow = pmag * jnp.cos(ks * (lam_im * step)[:, :, None, :])
    pirow = pmag * jnp.sin(ks * (lam_im * step)[:, :, None, :])
    prow, pirow = lax.optimization_barrier((prow, pirow))
    prt, pit = jnp.swapaxes(prow, 2, 3), jnp.swapaxes(pirow, 2, 3)
    bbrt, bbit = jnp.swapaxes(bbr, 2, 3), jnp.swapaxes(bbi, 2, 3)
    lanes = lambda a: jnp.pad(a, ((0, 0), (0, 0), (0, 0), (0, 128 - a.shape[-1])))
    args = [lanes(jnp.swapaxes(c_re, 2, 3)), lanes(jnp.swapaxes(c_im, 2, 3)), lanes(prt), lanes(pit),
            jnp.concatenate([bbrt, bbit], -1), jnp.concatenate([bbit, bbrt], -1),
            jnp.concatenate([prow, prow], -1), jnp.concatenate([-pirow, pirow], -1)]
    gg = S5_GEN_GROUPS
    spec = lambda a: pl.BlockSpec((2, gg) + a.shape[2:], lambda g: (0, g, 0, 0))
    wt, ws, wh = pl.pallas_call(
        _s5_gen_kernel,
        grid=(G // gg,),
        in_specs=[spec(a) for a in args],
        out_specs=[pl.BlockSpec((gg, S5_ROW, S5_ROW), lambda g: (g, 0, 0)),
                   pl.BlockSpec((2, gg, S5_ROW, 2 * P), lambda g: (0, g, 0, 0)),
                   pl.BlockSpec((2, gg, 2 * P, S5_ROW), lambda g: (0, g, 0, 0))],
        out_shape=[jax.ShapeDtypeStruct((G, S5_ROW, S5_ROW), BF16),
                   jax.ShapeDtypeStruct((2, G, S5_ROW, 2 * P), BF16),
                   jax.ShapeDtypeStruct((2, G, 2 * P, S5_ROW), BF16)],
        compiler_params=_cparams(("parallel",)),
        name="s5_gen",
    )(*args)
    a_l = jnp.stack([prt[..., L].reshape(2, 1, G * P), pit[..., L].reshape(2, 1, G * P)], 1)
    return wt, ws, wh, a_l


S5_GB = 8


def _s5_core_kernel(u_ref, ws_ref, wt_ref, wh_ref, a_ref, h0_ref, y_ref, hfin_ref, x_ref, s_ref, hin_ref,
                    *, nb, nchunk):
    r = nb * nchunk
    npair = S5_GB // 2
    us = [u_ref[pl.ds(i, r, stride=S5_CHUNK), :] for i in range(S5_CHUNK)]
    for g in range(S5_GB):
        xg = jnp.concatenate([u[:, S5_GROUP * g:S5_GROUP * (g + 1)] for u in us], axis=-1)
        x_ref[:, g * S5_ROW:(g + 1) * S5_ROW] = xg.astype(BF16)
    for p in range(npair):
        for d in range(2):
            sg = [jnp.dot(x_ref[:, (2 * p + k) * S5_ROW:(2 * p + k + 1) * S5_ROW], ws_ref[d, 2 * p + k],
                          preferred_element_type=F32) for k in range(2)]
            for comp in range(2):
                cs = slice(comp * S5_STATE, (comp + 1) * S5_STATE)
                s_ref[d, comp, p] = jnp.concatenate([sg[0][:, cs], sg[1][:, cs]], axis=-1)
    chains = [(d, p) for d in range(2) for p in range(npair)]
    h = {}
    for d, p in chains:
        cols = slice(p * 128, (p + 1) * 128)
        h[d, p] = (h0_ref[d, 0, :, cols], h0_ref[d, 1, :, cols], a_ref[d, 0, :, cols], a_ref[d, 1, :, cols])
    for step in range(nchunk):
        for d, p in chains:
            rows = pl.ds(step if d == 0 else nchunk - 1 - step, nb, stride=nchunk)
            hr, hi, ar, ai = h[d, p]
            hin_ref[d, 0, p, rows, :] = hr
            hin_ref[d, 1, p, rows, :] = hi
            sr, si = s_ref[d, 0, p, rows, :], s_ref[d, 1, p, rows, :]
            h[d, p] = (ar * hr - ai * hi + sr, ar * hi + ai * hr + si, ar, ai)
    for d, p in chains:
        cols = slice(p * 128, (p + 1) * 128)
        hfin_ref[d, 0, :, cols] = h[d, p][0]
        hfin_ref[d, 1, :, cols] = h[d, p][1]
    ys = []
    for g in range(S5_GB):
        y = jnp.dot(x_ref[:, g * S5_ROW:(g + 1) * S5_ROW], wt_ref[g], preferred_element_type=F32)
        cs = slice((g % 2) * S5_STATE, (g % 2 + 1) * S5_STATE)
        for d in range(2):
            hg = jnp.concatenate([hin_ref[d, 0, g // 2, :, cs], hin_ref[d, 1, g // 2, :, cs]], axis=-1)
            y = y + jnp.dot(hg.astype(BF16), wh_ref[d, g], preferred_element_type=F32)
        ys.append(y)
    for j in range(S5_CHUNK):
        y_ref[pl.ds(j, r, stride=S5_CHUNK), :] = jnp.concatenate(
            [y[:, S5_GROUP * j:S5_GROUP * (j + 1)] for y in ys], axis=-1)


def s5_core(proj, mats, h0, nb, nchunk):
    wt, ws, wh, a_l = mats
    r = nb * nchunk
    sw = S5_GB * S5_STATE
    return pl.pallas_call(
        functools.partial(_s5_core_kernel, nb=nb, nchunk=nchunk),
        grid=(S5_GROUPS // S5_GB,),
        in_specs=[pl.BlockSpec((r * S5_CHUNK, 128), lambda j: (0, FCOL_UA // 128 + j)),
                  pl.BlockSpec((2, S5_GB, S5_ROW, 2 * S5_STATE), lambda j: (0, j, 0, 0)),
                  pl.BlockSpec((S5_GB, S5_ROW, S5_ROW), lambda j: (j, 0, 0)),
                  pl.BlockSpec((2, S5_GB, 2 * S5_STATE, S5_ROW), lambda j: (0, j, 0, 0)),
                  pl.BlockSpec((2, 2, 1, sw), lambda j: (0, 0, 0, j)),
                  pl.BlockSpec((2, 2, nb, sw), lambda j: (0, 0, 0, j))],
        out_specs=[pl.BlockSpec((r * S5_CHUNK, 128), lambda j: (0, j)),
                   pl.BlockSpec((2, 2, nb, sw), lambda j: (0, 0, 0, j))],
        out_shape=[jax.ShapeDtypeStruct((r * S5_CHUNK, S5_WIDTH), F32),
                   jax.ShapeDtypeStruct((2, 2, nb, S5_GROUPS * S5_STATE), F32)],
        scratch_shapes=[pltpu.VMEM((r, S5_GB * S5_ROW), BF16),
                        pltpu.VMEM((2, 2, S5_GB // 2, r, 128), F32),
                        pltpu.VMEM((2, 2, S5_GB // 2, r, 128), F32)],
        compiler_params=_cparams(("parallel",)),
        name="s5_core",
    )(proj, ws, wt, wh, a_l, h0)


def _rope_tables(t):
    rows = t // GRID_W
    row = np.repeat(np.arange(rows), GRID_W).astype(np.float32)
    col = np.tile(np.arange(GRID_W), rows).astype(np.float32)
    nf = DA_HEAD // 4
    inv = (ROPE_BASE ** (-jnp.arange(nf, dtype=F32) / nf))

    def tab(pos):
        ang = jnp.asarray(pos)[:, None] * inv[None, :]
        c, s = jnp.cos(ang), jnp.sin(ang)
        return jnp.concatenate([c, c], -1), jnp.concatenate([-s, s], -1)

    cr, sr = tab(row)
    cc, sc = tab(col)
    cos = jnp.concatenate([cr, cc], -1)
    sin = jnp.concatenate([sr, sc], -1)
    return jnp.tile(cos, (1, 2)), jnp.tile(sin, (1, 2))


def _rope(x, cos, sin):
    lane = lax.broadcasted_iota(jnp.int32, x.shape, 1)
    swapped = jnp.where((lane % 32) < 16, pltpu.roll(x, 112, 1), pltpu.roll(x, 16, 1))
    return x * cos + swapped * sin


def _attn_kernel(*refs, lam_init, t, s_tot, tq, with_ctx):
    if with_ctx:
        (q_ref, k_ref, v_ref, z_ref, kc_ref, vc_ref, cq_ref, sq_ref, ck_ref, sk_ref,
         lam_ref, ng_ref, o_ref, kall_ref, vall_ref, s_ref, e_ref, l_ref) = refs
    else:
        q_ref, k_ref, v_ref, z_ref, lam_ref, ng_ref, o_ref, kall_ref, vall_ref, s_ref, e_ref, l_ref = refs

    @pl.when(pl.program_id(1) == 0)
    def _():
        if with_ctx:
            for h in range(DA_HEADS):
                hs = slice(h * 128, (h + 1) * 128)
                kall_ref[h, :, 0:t] = _rope(k_ref[0, :, hs].astype(F32), ck_ref[...], sk_ref[...]).T.astype(BF16)
                kall_ref[h, :, t:s_tot] = kc_ref[0, 0, h].astype(BF16)
            vall_ref[0:t, :] = v_ref[0].astype(BF16)
            vall_ref[t:s_tot, :] = vc_ref[0].astype(BF16)
        else:
            for h in range(DA_HEADS):
                kall_ref[h] = k_ref[0, :, h * 128:(h + 1) * 128].astype(F32).T.astype(BF16)
            vall_ref[...] = v_ref[0].astype(BF16)

    lp = lam_ref[...]
    lam = (jnp.exp(jnp.sum(lp[0:1] * lp[1:2], axis=-1, keepdims=True))
           - jnp.exp(jnp.sum(lp[2:3] * lp[3:4], axis=-1, keepdims=True)) + lam_init)
    lane = lax.broadcasted_iota(jnp.int32, (tq, 128), 1)
    def scores(h):
        hs = slice(h * 128, (h + 1) * 128)
        q = q_ref[0, :, hs].astype(F32)
        if with_ctx:
            q = _rope(q, cq_ref[...], sq_ref[...])
        q = q * (DA_HEAD ** -0.5 * LOG2E)
        qs = jnp.concatenate([jnp.where(lane < DA_HEAD, q, 0.0), jnp.where(lane >= DA_HEAD, q, 0.0)], axis=0)
        s_ref[h % 2] = _mm(qs, kall_ref[h])

    scores(0)
    for h in range(DA_HEADS):
        hs = slice(h * 128, (h + 1) * 128)
        b = h % 2
        if h + 1 < DA_HEADS:
            scores(h + 1)
        for r in range(0, 2 * tq, ATTN_ROWS):
            sc = s_ref[b, r:r + ATTN_ROWS, :]
            e = jnp.exp2(sc - jnp.max(sc, axis=-1, keepdims=True))
            e_ref[b, r:r + ATTN_ROWS, :] = e.astype(BF16)
            l_ref[b, r:r + ATTN_ROWS, :] = jnp.broadcast_to(jnp.sum(e, axis=-1, keepdims=True),
                                                            (ATTN_ROWS, 128))
        ov = jnp.dot(e_ref[b], vall_ref[:, hs], preferred_element_type=F32) / l_ref[b]
        o = ov[:tq] - lam * ov[tq:]
        o = o * lax.rsqrt(jnp.mean(o * o, axis=-1, keepdims=True) + EPS) * ng_ref[...]
        o = o * (1.0 - lam_init)
        o_ref[0, :, hs] = (o * _silu(z_ref[0, :, hs].astype(F32))).astype(BF16)


def diff_attention(proj3, layer, lam_init, da_lam, da_norm_g, ctx_kv):
    nb, t, _ = proj3.shape
    with_ctx = ctx_kv is not None
    s_tot = t + (PAST_LEN if with_ctx else 0)
    tq = 256
    wb = DA_WIDTH
    in_specs = [pl.BlockSpec((1, tq, wb), lambda b, i: (b, i, HCOL_QB // wb)),
                pl.BlockSpec((1, t, wb), lambda b, i: (b, 0, HCOL_KB // wb)),
                pl.BlockSpec((1, t, wb), lambda b, i: (b, 0, HCOL_VB // wb)),
                pl.BlockSpec((1, tq, wb), lambda b, i: (b, i, HCOL_ZB // wb))]
    args = [proj3, proj3, proj3, proj3]
    if with_ctx:
        kc, vc = ctx_kv
        cos, sin = _rope_tables(t)
        in_specs += [pl.BlockSpec((1, 1, DA_HEADS, 2 * DA_HEAD, PAST_LEN), lambda b, i: (b, layer, 0, 0, 0)),
                     pl.BlockSpec((1, PAST_LEN, wb), lambda b, i: (b, layer, 0)),
                     pl.BlockSpec((tq, 128), lambda b, i: (i, 0)),
                     pl.BlockSpec((tq, 128), lambda b, i: (i, 0)),
                     pl.BlockSpec((t, 128), lambda b, i: (0, 0)),
                     pl.BlockSpec((t, 128), lambda b, i: (0, 0))]
        args += [kc, vc, cos, sin, cos, sin]
    in_specs += [pl.BlockSpec((4, DA_HEAD), lambda b, i: (0, 0)),
                 pl.BlockSpec((1, DA_VDIM), lambda b, i: (0, 0))]
    args += [da_lam, da_norm_g.reshape(1, DA_VDIM)]
    return pl.pallas_call(
        functools.partial(_attn_kernel, lam_init=lam_init, t=t, s_tot=s_tot, tq=tq, with_ctx=with_ctx),
        grid=(nb, t // tq),
        in_specs=in_specs,
        out_specs=pl.BlockSpec((1, tq, wb), lambda b, i: (b, i, 0)),
        out_shape=jax.ShapeDtypeStruct((nb, t, wb), BF16),
        scratch_shapes=[pltpu.VMEM((DA_HEADS, 2 * DA_HEAD, s_tot), BF16), pltpu.VMEM((s_tot, wb), BF16),
                        pltpu.VMEM((2, 2 * tq, s_tot), F32), pltpu.VMEM((2, 2 * tq, s_tot), BF16),
                        pltpu.VMEM((2, 2 * tq, 128), F32)],
        compiler_params=_cparams(("parallel", "arbitrary")),
        name="diff_attention",
    )(*args)


DN_PAD = 8
DN_RT = 128
DN_GROUP = 128
DN_STEP_GROUPS = 2


def _dn_kernel(*refs, t, ngroup, with_s0):
    if with_s0:
        (qkv_ref, z_ref, ba_ref, cw_ref, alog_ref, dtb_ref, ng_ref, s0_ref,
         out_ref, sfin_ref, xp_ref, qkvn_ref, oacc_ref, st_ref) = refs
    else:
        (qkv_ref, z_ref, ba_ref, cw_ref, alog_ref, dtb_ref, ng_ref,
         out_ref, sfin_ref, xp_ref, qkvn_ref, oacc_ref, st_ref) = refs
    n = pl.program_id(1)
    cd = DN_CHUNK
    w3 = 3 * DN_WIDTH

    @pl.when(n == 0)
    def _init():
        xp_ref[0:DN_PAD, :] = jnp.zeros((DN_PAD, w3), F32)
        xp_ref[DN_PAD + t:2 * DN_PAD + t, :] = jnp.zeros((DN_PAD, w3), F32)
        xp_ref[DN_PAD:DN_PAD + t, :] = qkv_ref[0]
        half = DN_CONV // 2
        for r in range(t // DN_RT):
            for sec in range(3):
                for h in range(DN_HEADS):
                    cs = slice(sec * DN_WIDTH + h * DN_HEAD, sec * DN_WIDTH + (h + 1) * DN_HEAD)
                    acc = jnp.zeros((DN_RT, DN_HEAD), F32)
                    for j in range(DN_CONV):
                        r0 = DN_PAD + r * DN_RT + j - half
                        acc = acc + xp_ref[r0:r0 + DN_RT, cs] * cw_ref[j:j + 1, cs]
                    y = _silu(acc)
                    if sec < 2:
                        y = y * lax.rsqrt(jnp.sum(y * y, axis=-1, keepdims=True) + EPS)
                    if sec == 0:
                        y = y * (DN_HEAD ** -0.5)
                    qkvn_ref[r * DN_RT:(r + 1) * DN_RT, cs] = y
        oacc_ref[...] = jnp.zeros_like(oacc_ref)
        if with_s0:
            st_ref[...] = s0_ref[0, 0]
        else:
            st_ref[...] = jnp.zeros_like(st_ref)

    gb = DN_GROUP
    nsub = gb // cd
    ri = lax.broadcasted_iota(jnp.int32, (gb, gb), 0)
    ci = lax.broadcasted_iota(jnp.int32, (gb, gb), 1)
    same = (ri // cd) == (ci // cd)
    samef = same.astype(F32)
    eye = (ri == ci).astype(BF16)
    masked_out = -1e30

    chains = []
    for d, gi in [(d, gi) for d in range(2) for gi in range(DN_STEP_GROUPS)]:
        grp = n * DN_STEP_GROUPS + gi
        r0 = pl.multiple_of((grp if d == 0 else ngroup - 1 - grp) * gb, gb)
        incl = same & ((ci <= ri) if d == 0 else (ci >= ri))
        strict = same & ((ci < ri) if d == 0 else (ci > ri))
        inclf = incl.astype(F32)
        inclog = jnp.where(incl, 0.0, masked_out)
        nstrict = -strict.astype(BF16)
        ba = ba_ref[0, pl.ds(r0, gb), :]
        beta_all = jax.nn.sigmoid(ba)
        g_all = -jnp.exp(alog_ref[...]) * jax.nn.softplus(ba + dtb_ref[...])
        gc = jnp.dot(inclf, g_all, precision=HI, preferred_element_type=F32)
        gct = lax.dot_general(g_all.T, inclf, (((1,), (1,)), ((), ())), precision=HI,
                              preferred_element_type=F32)
        gtot = jnp.dot(samef, g_all, precision=HI, preferred_element_type=F32)
        for h in range(DN_HEADS):
            hs = slice(h * DN_HEAD, (h + 1) * DN_HEAD)
            col = 2 * DN_HEADS + d * DN_HEADS + h
            gcol = gc[:, col:col + 1]
            grow = gct[col:col + 1, :]
            gt = gtot[:, col:col + 1]
            beta = beta_all[:, d * DN_HEADS + h:d * DN_HEADS + h + 1]
            q = qkvn_ref[pl.ds(r0, gb), hs]
            k = qkvn_ref[pl.ds(r0, gb), DN_WIDTH + h * DN_HEAD:DN_WIDTH + (h + 1) * DN_HEAD]
            v = qkvn_ref[pl.ds(r0, gb), 2 * DN_WIDTH + h * DN_HEAD:2 * DN_WIDTH + (h + 1) * DN_HEAD]
            eg = jnp.exp(gcol)
            chains.append(dict(
                d=d, gi=gi, h=h, r0=r0, hs=hs, nstrict=nstrict, q=q, k=k, kb=k * beta,
                decay=jnp.exp((gcol - grow) + inclog),
                rhs=jnp.concatenate([v * beta, k * beta * eg], axis=-1),
                qe=q * eg, kdec=k * jnp.exp(gt - gcol), egt=jnp.exp(gt)))

    for c in chains:
        c["nmm"] = (_mm_nt(c["kb"], c["k"]) * c["decay"]).astype(BF16) * c["nstrict"]
    for c in chains:
        c["qk"] = _mm_nt(c["q"], c["k"]) * c["decay"]
    def level_mask(s):
        return (((ri // (2 * s)) == (ci // (2 * s))) & ((ri // s) != (ci // s))).astype(BF16)

    pair = level_mask(1)
    for c in chains:
        c["tinv"] = eye + c["nmm"] * pair
    s = 2
    while s < cd:
        offmask = level_mask(s)
        xs = [jnp.dot(c["nmm"] * offmask, c["tinv"], preferred_element_type=F32).astype(BF16) for c in chains]
        ys = [jnp.dot(c["tinv"], x, preferred_element_type=F32) for c, x in zip(chains, xs)]
        for c, y in zip(chains, ys):
            c["tinv"] = c["tinv"] + y.astype(BF16)
        s *= 2
    for c in chains:
        c["uw"] = _mm(c["tinv"], c["rhs"])

    st = {(d, h): st_ref[d, h] for d in range(2) for h in range(DN_HEADS)}
    outs = []
    for gi, step in [(gi, step) for gi in range(DN_STEP_GROUPS) for step in range(nsub)]:
        active = [c for c in chains if c["gi"] == gi]
        rs = []
        for c in active:
            sub = step if c["d"] == 0 else nsub - 1 - step
            c["rows"] = slice(sub * cd, (sub + 1) * cd)
            rs.append(_mm(jnp.concatenate([c["uw"][c["rows"], DN_HEAD:], c["qe"][c["rows"]]], axis=0),
                          st[c["d"], c["h"]]))
        for c, r in zip(active, rs):
            rows = c["rows"]
            v_new = c["uw"][rows, :DN_HEAD] - r[:cd]
            o = r[cd:] + _mm(c["qk"][rows, rows], v_new)
            st[c["d"], c["h"]] = (st[c["d"], c["h"]] * c["egt"][rows.start:rows.start + 1]
                                  + _mm(c["kdec"][rows].T, v_new))
            outs.append((c, rows.start, o))
    for (d, h), v in st.items():
        st_ref[d, h] = v
    for c, off, o in outs:
        oacc_ref[pl.ds(pl.multiple_of(c["r0"] + off, cd), cd), c["hs"]] += o

    @pl.when(n == ngroup // DN_STEP_GROUPS - 1)
    def _fin():
        for h in range(DN_HEADS):
            hs = slice(h * DN_HEAD, (h + 1) * DN_HEAD)
            o = oacc_ref[:, hs]
            o = o * lax.rsqrt(jnp.mean(o * o, axis=-1, keepdims=True) + EPS) * ng_ref[...]
            out_ref[0, :, hs] = (o * _silu(z_ref[0, :, hs].astype(F32))).astype(BF16)
        sfin_ref[0] = st_ref[...]


def deltanet(projf3, projh3, ba3, layer, conv_w, a_log, dt_bias, norm_g, s0):
    nb, t, _ = projf3.shape
    ngroup = t // DN_GROUP
    with_s0 = s0 is not None
    w3 = 3 * DN_WIDTH
    pad = jnp.zeros((2 * DN_HEADS,), F32)
    alog_row = jnp.concatenate([pad, a_log.reshape(-1), jnp.zeros((128 - 4 * DN_HEADS,), F32)]).reshape(1, 128)
    dtb_row = jnp.concatenate([pad, dt_bias.reshape(-1), jnp.zeros((128 - 4 * DN_HEADS,), F32)]).reshape(1, 128)
    in_specs = [pl.BlockSpec((1, t, w3), lambda b, n: (b, 0, FCOL_QC // w3)),
                pl.BlockSpec((1, t, DN_WIDTH), lambda b, n: (b, 0, HCOL_ZC // DN_WIDTH)),
                pl.BlockSpec((1, t, 128), lambda b, n: (b, 0, 0)),
                pl.BlockSpec((8, w3), lambda b, n: (0, 0)),
                pl.BlockSpec((1, 128), lambda b, n: (0, 0)),
                pl.BlockSpec((1, 128), lambda b, n: (0, 0)),
                pl.BlockSpec((1, DN_HEAD), lambda b, n: (0, 0))]
    args = [projf3, projh3, ba3, jnp.pad(conv_w, ((0, 8 - DN_CONV), (0, 0))), alog_row, dtb_row,
            norm_g.reshape(1, DN_HEAD)]
    if with_s0:
        in_specs.append(pl.BlockSpec((1, 1, 2, DN_HEADS, DN_HEAD, DN_HEAD), lambda b, n: (b, layer, 0, 0, 0, 0)))
        args.append(s0)
    return pl.pallas_call(
        functools.partial(_dn_kernel, t=t, ngroup=ngroup, with_s0=with_s0),
        grid=(nb, ngroup // DN_STEP_GROUPS),
        in_specs=in_specs,
        out_specs=[pl.BlockSpec((1, t, DN_WIDTH), lambda b, n: (b, 0, 0)),
                   pl.BlockSpec((1, 2, DN_HEADS, DN_HEAD, DN_HEAD), lambda b, n: (b, 0, 0, 0, 0))],
        out_shape=[jax.ShapeDtypeStruct((nb, t, DN_WIDTH), BF16),
                   jax.ShapeDtypeStruct((nb, 2, DN_HEADS, DN_HEAD, DN_HEAD), F32)],
        scratch_shapes=[pltpu.VMEM((t + 2 * DN_PAD, w3), F32),
                        pltpu.VMEM((t, w3), F32),
                        pltpu.VMEM((t, DN_WIDTH), F32),
                        pltpu.VMEM((2, DN_HEADS, DN_HEAD, DN_HEAD), F32)],
        compiler_params=_cparams(("parallel", "arbitrary")),
        name="deltanet",
    )(*args)


def _merge_kernel(u_ref, ys_ref, za_ref, sd_ref, wglu_ref, ob_ref, oc_ref, hn_ref, wg_ref, wb_ref, wo_ref,
                  x_ref, gate_ref, fg_ref, *outs, final):
    tm = x_ref.shape[0]
    for rows in (slice(0, tm // 2), slice(tm // 2, tm)):
        ya = jax.nn.gelu(sd_ref[...] * u_ref[rows, :] + ys_ref[rows, :])
        ya = ya * jax.nn.sigmoid(_mm(ya, wglu_ref[...]))
        branches = ((ya * _silu(za_ref[rows, :].astype(F32))).astype(BF16), ob_ref[rows, :], oc_ref[rows, :])
        acc = None
        hn = hn_ref[rows, :]
        for i, o in enumerate(branches):
            pr = jnp.dot(o, wb_ref[i], preferred_element_type=F32)
            gt = jnp.dot(hn, wg_ref[:, i * D_MODEL:(i + 1) * D_MODEL], preferred_element_type=F32)
            term = jax.nn.sigmoid(gt) * pr
            acc = term if acc is None else acc + term
        y = jnp.dot(acc.astype(BF16), wo_ref[...], preferred_element_type=F32)
        xn = x_ref[rows, :] + gate_ref[0] * y
        outs[0][rows, :] = xn
        if final:
            yn = xn * lax.rsqrt(jnp.mean(xn * xn, axis=-1, keepdims=True) + EPS) * fg_ref[...]
            outs[1][rows, :] = yn


def merge(projf, projh, y_s5, s5_d, w_glu, out_b, out_c, hn, w_gates, w_branch, w_out, x2, gate, final_g, rows_per_mod,
          final):
    m = x2.shape[0]
    tm = 512
    nmod = gate.shape[0]
    row = lambda i: (i, 0)
    out_specs = [pl.BlockSpec((tm, D_MODEL), row)]
    out_shape = [jax.ShapeDtypeStruct((m, D_MODEL), F32)]
    if final:
        out_specs.append(pl.BlockSpec((tm, D_MODEL), row))
        out_shape.append(jax.ShapeDtypeStruct((m, D_MODEL), F32))
    return pl.pallas_call(
        functools.partial(_merge_kernel, final=final),
        grid=(m // tm,),
        in_specs=[pl.BlockSpec((tm, S5_WIDTH), lambda i: (i, FCOL_UA // S5_WIDTH)),
                  pl.BlockSpec((tm, S5_WIDTH), row),
                  pl.BlockSpec((tm, S5_WIDTH), lambda i: (i, HCOL_ZA // S5_WIDTH)),
                  pl.BlockSpec((1, S5_WIDTH), lambda i: (0, 0)),
                  pl.BlockSpec((S5_WIDTH, S5_WIDTH), lambda i: (0, 0)),
                  pl.BlockSpec((tm, BRANCH_WIDTH), row),
                  pl.BlockSpec((tm, BRANCH_WIDTH), row),
                  pl.BlockSpec((tm, D_MODEL), row),
                  pl.BlockSpec((D_MODEL, N_BRANCH * D_MODEL), lambda i: (0, 0)),
                  pl.BlockSpec((N_BRANCH, BRANCH_WIDTH, D_MODEL), lambda i: (0, 0, 0)),
                  pl.BlockSpec((D_MODEL, D_MODEL), lambda i: (0, 0)),
                  pl.BlockSpec((tm, D_MODEL), row),
                  pl.BlockSpec((1, 1, D_MODEL), lambda i: ((i * tm) // rows_per_mod, 0, 0)),
                  pl.BlockSpec((1, D_MODEL), lambda i: (0, 0))],
        out_specs=out_specs,
        out_shape=out_shape,
        compiler_params=_cparams(("parallel",)),
        name="merge",
    )(projf, y_s5, projh, s5_d.reshape(1, S5_WIDTH), w_glu, out_b, out_c, hn, w_gates, w_branch, w_out, x2,
      gate.reshape(nmod, 1, D_MODEL),
      final_g.reshape(1, D_MODEL))


def _run_pass(x, mod, wts, lam_inits, final_g, ctx):
    nb, t, _ = x.shape
    m = nb * t
    nmod = mod.shape[1]
    rows_per_mod = m if nmod == 1 else t
    x2 = x.reshape(m, D_MODEL)
    states = []
    y, caches = None, None
    for l in range(DEPTH):
        w = wts[l]
        shift, scale, gate = jnp.split(mod[l], 3, axis=-1)
        if ctx is None:
            projf, projh, ba, hn, *caches = inproj(x2, w["norm_g"], scale, shift, w["w1"], w["w2"], rows_per_mod,
                                       kv=(l, t, caches))
            h0 = jnp.zeros((2, 2, nb, S5_GROUPS * S5_STATE), F32)
            ctx_kv, s0 = None, None
        else:
            projf, projh, ba, hn = inproj(x2, w["norm_g"], scale, shift, w["w1"], w["w2"], rows_per_mod)
            cache_k, cache_v, st_re, st_im, st_dn = ctx
            h0 = jnp.stack([st_re[:, l], st_im[:, l]], 0)
            h0 = jnp.transpose(h0, (2, 0, 1, 3, 4)).reshape(2, 2, nb, S5_GROUPS * S5_STATE)
            ctx_kv, s0 = (cache_k, cache_v), st_dn
        projf3, projh3 = projf.reshape(nb, t, N_F32), projh.reshape(nb, t, N_BF16)
        y_s5, hfin = s5_core(projf, w["s5_mats"], h0, nb, t // S5_CHUNK)
        out_b = diff_attention(projh3, l, lam_inits[l], w["da_lam"], w["da_norm_g"], ctx_kv)
        out_c, sfin = deltanet(projf3, projh3, ba.reshape(nb, t, 128), l, w["dn_conv"], w["dn_a_log"],
                               w["dn_dt_bias"], w["dn_norm_g"], s0)
        final = l == DEPTH - 1
        res = merge(projf, projh, y_s5, w["s5_d"], w["w_glu"], out_b.reshape(m, DA_WIDTH), out_c.reshape(m, DN_WIDTH),
                    hn, w["w_gates"], w["w_branch"], w["w_out"], x2, gate, final_g, rows_per_mod, final)
        x2 = res[0]
        if final:
            y = res[1]
        if ctx is None:
            hf = hfin.reshape(2, 2, nb, S5_GROUPS, S5_STATE)
            states.append((jnp.transpose(hf[:, 0], (1, 0, 2, 3)), jnp.transpose(hf[:, 1], (1, 0, 2, 3)), sfin))
    return y.reshape(nb, t, D_MODEL), states, caches


def kernel(x_prompt, x_sample, cache_k, cache_v, state_s5_re, state_s5_im, state_dn, c, c_ctx,
           norm_g, w_ada, b_ada, w_in, s5_lam_re, s5_lam_im, s5_log_step, s5_b_re, s5_b_im,
           s5_c_re, s5_c_im, s5_d, s5_w_glu, da_lam, da_norm_g, dn_conv, dn_a_log, dn_dt_bias,
           dn_norm_g, w_branch, w_out, final_norm_g):
    nb_dec = x_sample.shape[0]
    cond8 = jnp.concatenate([c_ctx[None, :], c, jnp.zeros((8 - 1 - nb_dec, D_MODEL), F32)], 0)
    mod = ada_mod(cond8, w_ada, b_ada)
    wts = []
    for l in range(DEPTH):
        w1, w_gates, w2 = cast_w_in(w_in, l)
        wts.append(dict(
            norm_g=norm_g[l], w1=w1, w2=w2, w_gates=w_gates,
            s5_mats=s5_matrices(s5_lam_re[l], s5_lam_im[l], s5_log_step[l], s5_b_re[l], s5_b_im[l],
                                s5_c_re[l], s5_c_im[l]),
            s5_d=s5_d[l], w_glu=s5_w_glu[l].astype(BF16), da_lam=da_lam[l], da_norm_g=da_norm_g[l],
            dn_conv=dn_conv[l], dn_a_log=dn_a_log[l], dn_dt_bias=dn_dt_bias[l], dn_norm_g=dn_norm_g[l],
            w_branch=w_branch[l].astype(BF16), w_out=w_out[l].astype(BF16)))
    lam_inits = [0.8 - 0.6 * math.exp(-0.3 * l) for l in range(DEPTH)]

    y_prompt, states, (k_new, v_new) = _run_pass(x_prompt, mod[:, 0:1], wts, lam_inits, final_norm_g, None)
    cache_kt = jnp.transpose(cache_k, (0, 1, 3, 4, 5, 2)).reshape(nb_dec, DEPTH, DA_HEADS, 2 * DA_HEAD, PAST_LEN)
    ctx = (cache_kt, cache_v.reshape(nb_dec, DEPTH * PAST_LEN, DA_WIDTH), state_s5_re, state_s5_im, state_dn)
    y_sample, _, _ = _run_pass(x_sample, mod[:, 1:1 + nb_dec], wts, lam_inits, final_norm_g, ctx)

    nb, t = x_prompt.shape[:2]
    new_cache_k = k_new.reshape(nb, DEPTH, t, DA_HEADS, 2, DA_HEAD)
    new_cache_v = v_new.reshape(nb, DEPTH, t, DA_HEADS, DA_VDIM)
    new_s5_re = jnp.stack([s[0] for s in states], axis=1)
    new_s5_im = jnp.stack([s[1] for s in states], axis=1)
    new_dn = jnp.stack([s[2] for s in states], axis=1)
    return (y_prompt, y_sample, new_cache_k, new_cache_v, new_s5_re, new_s5_im, new_dn)
```

```python
import functools
import math

import numpy as np
import jax
import jax.numpy as jnp
from jax import lax
from jax.experimental import pallas as pl
from jax.experimental.pallas import tpu as pltpu

F32 = jnp.float32
BF16 = jnp.bfloat16

D_MODEL = 1024
DEPTH = 2
GRID_W = 64
EPS = 1e-6
S5_WIDTH = 512
S5_GROUP = 16
S5_GROUPS = 32
S5_STATE = 64
S5_CHUNK = 16
S5_PAIRS = S5_GROUPS // 2
S5_ROW = S5_CHUNK * S5_GROUP
S5_GEN_GROUPS = 4
DA_HEADS = 4
DA_HEAD = 64
DA_VDIM = 128
DA_WIDTH = 512
ROPE_BASE = 10000.0
DN_HEADS = 4
DN_HEAD = 128
DN_WIDTH = 512
DN_CONV = 5
DN_CHUNK = 64
N_BRANCH = 3
BRANCH_WIDTH = 512
PAST_LEN = 512

COL_UA, COL_ZA, COL_QB, COL_KB, COL_VB, COL_ZB = 0, 512, 1024, 1536, 2048, 2560
COL_QC, COL_ZC = 3072, 4608
N_MAIN = 5120
BA_OFF = 5120
GATES_OFF = 5136

VMEM_LIMIT = 56 * 1024 * 1024
HI = lax.Precision.HIGHEST
LOG2E = math.log2(math.e)
ATTN_ROWS = 16


def _cparams(sem):
    return pltpu.CompilerParams(dimension_semantics=sem, vmem_limit_bytes=VMEM_LIMIT)


def _mm(a, b):
    return jnp.dot(a.astype(BF16), b.astype(BF16), preferred_element_type=F32)


def _mm_nt(a, b):
    return lax.dot_general(a.astype(BF16), b.astype(BF16), (((1,), (1,)), ((), ())),
                           preferred_element_type=F32)


def _silu(x):
    return x * jax.nn.sigmoid(x)


def _ada_kernel(c_ref, w_ref, b_ref, o_ref):
    o_ref[0] = _mm(_silu(c_ref[...]), w_ref[0]) + b_ref[0]


def ada_mod(cond8, w_ada, b_ada):
    tn = 1024
    return pl.pallas_call(
        _ada_kernel,
        grid=(DEPTH, 3 * D_MODEL // tn),
        in_specs=[pl.BlockSpec((8, D_MODEL), lambda l, j: (0, 0)),
                  pl.BlockSpec((1, D_MODEL, tn), lambda l, j: (l, 0, j)),
                  pl.BlockSpec((1, 1, tn), lambda l, j: (l, 0, j))],
        out_specs=pl.BlockSpec((1, 8, tn), lambda l, j: (l, 0, j)),
        out_shape=jax.ShapeDtypeStruct((DEPTH, 8, 3 * D_MODEL), F32),
        compiler_params=_cparams(("parallel", "parallel")),
        name="ada_mod",
    )(cond8, w_ada, b_ada.reshape(DEPTH, 1, 3 * D_MODEL))


def _tcast_kernel(w_ref, o_ref, *, keep):
    x = w_ref[0].T
    if keep is not None:
        x = jnp.where(lax.broadcasted_iota(jnp.int32, x.shape, 1) < keep, x, 0.0)
    o_ref[...] = x.astype(BF16)


def _tcast_rows_kernel(w_hbm, o_ref, buf, sem, *, layer, row0):
    tn = buf.shape[0]
    start = pl.multiple_of(row0 + pl.program_id(0) * tn, 8)
    cp = pltpu.make_async_copy(w_hbm.at[layer, pl.ds(start, tn), :], buf, sem)
    cp.start()
    cp.wait()
    o_ref[...] = buf[...].T.astype(BF16)


def cast_w_in(w_in, layer):
    wt = jnp.swapaxes(w_in, 1, 2)
    tn = 1024
    w1 = pl.pallas_call(
        functools.partial(_tcast_kernel, keep=None),
        grid=(N_MAIN // tn,),
        in_specs=[pl.BlockSpec((1, tn, D_MODEL), lambda j: (layer, j, 0))],
        out_specs=pl.BlockSpec((D_MODEL, tn), lambda j: (0, j)),
        out_shape=jax.ShapeDtypeStruct((D_MODEL, N_MAIN), BF16),
        compiler_params=_cparams(("parallel",)),
        name="cast_w1",
    )(wt)
    gw = N_BRANCH * D_MODEL
    w_gates = pl.pallas_call(
        functools.partial(_tcast_rows_kernel, layer=layer, row0=GATES_OFF),
        grid=(gw // tn,),
        in_specs=[pl.BlockSpec(memory_space=pl.ANY)],
        out_specs=pl.BlockSpec((D_MODEL, tn), lambda j: (0, j)),
        out_shape=jax.ShapeDtypeStruct((D_MODEL, gw), BF16),
        scratch_shapes=[pltpu.VMEM((tn, D_MODEL), F32), pltpu.SemaphoreType.DMA(())],
        compiler_params=_cparams(("parallel",)),
        name="cast_w_gates",
    )(wt)
    w_ba = pl.pallas_call(
        functools.partial(_tcast_kernel, keep=GATES_OFF - BA_OFF),
        grid=(1,),
        in_specs=[pl.BlockSpec((1, 128, D_MODEL), lambda j: (layer, BA_OFF // 128, 0))],
        out_specs=pl.BlockSpec((D_MODEL, 128), lambda j: (0, 0)),
        out_shape=jax.ShapeDtypeStruct((D_MODEL, 128), BF16),
        compiler_params=_cparams(("parallel",)),
        name="cast_w_ba",
    )(wt)
    return w1, w_gates, w_ba


def _inproj_kernel(*refs, n_prev, with_kv):
    x_ref, g_ref, sc_ref, sh_ref, w1_ref, w2_ref = refs[:6]
    prev = refs[6:6 + n_prev]
    outs = refs[6 + n_prev:]
    proj_ref, ba_ref, hn_ref = outs[:3]
    j = pl.program_id(1)

    @pl.when(j == 0)
    def _():
        x = x_ref[...]
        y = x * lax.rsqrt(jnp.mean(x * x, axis=-1, keepdims=True) + EPS) * g_ref[...]
        hn = (y * (1.0 + sc_ref[0]) + sh_ref[0]).astype(BF16)
        hn_ref[...] = hn
        ba_ref[...] = jnp.dot(hn, w2_ref[...], preferred_element_type=F32)

    proj_ref[...] = jnp.dot(hn_ref[...], w1_ref[...], preferred_element_type=F32)

    if with_kv:
        tn = proj_ref.shape[1]
        for c, (ref, col) in enumerate(((outs[3], COL_KB), (outs[4], COL_VB))):
            @pl.when(j == col // tn)
            def _(c=c, ref=ref, col=col):
                nb, nl, t, w = ref.shape
                if n_prev:
                    ref[:, 0:nl - 1] = prev[c][...]
                ref[:, nl - 1:nl] = proj_ref[:, col % tn:col % tn + w].reshape(nb, 1, t, w)


def inproj(x2, norm_g, scale, shift, w1, w2, rows_per_mod, kv=None):
    m = x2.shape[0]
    tm = 1024
    tn = 1024 if (kv is not None and kv[2] is not None) else N_MAIN // 2
    nmod = scale.shape[0]
    mod_idx = lambda i, j: ((i * tm) // rows_per_mod, 0, 0)
    in_specs = [pl.BlockSpec((tm, D_MODEL), lambda i, j: (i, 0)),
                pl.BlockSpec((1, D_MODEL), lambda i, j: (0, 0)),
                pl.BlockSpec((1, 1, D_MODEL), mod_idx),
                pl.BlockSpec((1, 1, D_MODEL), mod_idx),
                pl.BlockSpec((D_MODEL, tn), lambda i, j: (0, j)),
                pl.BlockSpec((D_MODEL, 128), lambda i, j: (0, 0))]
    args = [x2, norm_g.reshape(1, D_MODEL), scale.reshape(nmod, 1, D_MODEL),
            shift.reshape(nmod, 1, D_MODEL), w1, w2]
    out_specs = [pl.BlockSpec((tm, tn), lambda i, j: (i, j)),
                 pl.BlockSpec((tm, 128), lambda i, j: (i, 0)),
                 pl.BlockSpec((tm, D_MODEL), lambda i, j: (i, 0))]
    out_shape = [jax.ShapeDtypeStruct((m, N_MAIN), F32),
                 jax.ShapeDtypeStruct((m, 128), F32),
                 jax.ShapeDtypeStruct((m, D_MODEL), BF16)]
    n_prev = 0
    if kv is not None:
        layer, t, caches = kv
        cspec = lambda nl: pl.BlockSpec((tm // t, nl, t, DA_WIDTH), lambda i, j: (i, 0, 0, 0))
        out_specs += [cspec(layer + 1)] * 2
        out_shape += [jax.ShapeDtypeStruct((m // t, layer + 1, t, DA_WIDTH), F32)] * 2
        if caches is not None:
            n_prev = 2
            in_specs += [cspec(layer)] * 2
            args += list(caches)
    return pl.pallas_call(
        functools.partial(_inproj_kernel, n_prev=n_prev, with_kv=kv is not None),
        grid=(m // tm, N_MAIN // tn),
        in_specs=in_specs,
        out_specs=out_specs,
        out_shape=out_shape,
        compiler_params=_cparams(("parallel", "arbitrary")),
        name="inproj",
    )(*args)


def _s5_gen_kernel(crt_ref, cit_ref, prt_ref, pit_ref, bbt_ref, bbs_ref, prow_ref, pirow_ref,
                   wt_ref, ws_ref, wh_ref):
    L, C, P = S5_CHUNK, S5_GROUP, S5_STATE
    width = (L + 1) * C
    row = lax.broadcasted_iota(jnp.int32, (128, width), 0)
    lane = lax.broadcasted_iota(jnp.int32, (128, width), 1)
    tile_c = (lane % C == row).astype(F32)
    expand = lambda a, e: jnp.dot(a, e, precision=HI, preferred_element_type=F32)
    zeros = jnp.zeros((C, L * C), F32)
    for g in range(S5_GEN_GROUPS):
        strips = []
        for d in range(2):
            spread_k = ((lane // C if d == 0 else L - lane // C) == row).astype(F32)
            crx, cix = expand(crt_ref[d, g], tile_c), expand(cit_ref[d, g], tile_c)
            prx, pix = expand(prt_ref[d, g], spread_k), expand(pit_ref[d, g], spread_k)
            ca = jnp.concatenate([crx * prx - cix * pix, -(crx * pix + cix * prx)], axis=0)
            wh_ref[d, g] = (ca[:, C:] if d == 0 else ca[:, :L * C]).astype(BF16)
            bbt, bbs = bbt_ref[d, g], bbs_ref[d, g]
            strips.append(jnp.dot(bbt, ca[:, :L * C] if d == 0 else ca[:, C:], precision=HI,
                                  preferred_element_type=F32))
            rows = []
            for i in range(L):
                k = L - 1 - i if d == 0 else i
                rows.append(bbt * prow_ref[d, g, k:k + 1, :] + bbs * pirow_ref[d, g, k:k + 1, :])
            ws_ref[d, g] = jnp.concatenate(rows, axis=0).astype(BF16)
        fpad = jnp.concatenate([zeros, strips[0]], axis=-1)
        rpad = jnp.concatenate([strips[1], zeros], axis=-1)
        rows = []
        for i in range(L):
            rows.append(fpad[:, (L - i) * C:(2 * L - i) * C] + rpad[:, (L - 1 - i) * C:(2 * L - 1 - i) * C])
        wt_ref[g] = jnp.concatenate(rows, axis=0).astype(BF16)


def s5_matrices(lam_re, lam_im, log_step, b_re, b_im, c_re, c_im):
    L, G, P, C = S5_CHUNK, S5_GROUPS, S5_STATE, S5_GROUP
    step = jnp.exp(log_step)[..., None]
    mag = jnp.exp(lam_re * step)
    ar, ai = mag * jnp.cos(lam_im * step), mag * jnp.sin(lam_im * step)
    den = lam_re * lam_re + lam_im * lam_im
    fr = ((ar - 1.0) * lam_re + ai * lam_im) / den
    fi = (ai * lam_re - (ar - 1.0) * lam_im) / den
    bbr = fr[..., None] * b_re - fi[..., None] * b_im
    bbi = fr[..., None] * b_im + fi[..., None] * b_re
    ks = jnp.arange(L + 1, dtype=F32)[None, None, :, None]
    pmag = jnp.exp(ks * (lam_re * step)[:, :, None, :])
    prow = pmag * jnp.cos(ks * (lam_im * step)[:, :, None, :])
    pirow = pmag * jnp.sin(ks * (lam_im * step)[:, :, None, :])
    prow, pirow = lax.optimization_barrier((prow, pirow))
    prt, pit = jnp.swapaxes(prow, 2, 3), jnp.swapaxes(pirow, 2, 3)
    bbrt, bbit = jnp.swapaxes(bbr, 2, 3), jnp.swapaxes(bbi, 2, 3)
    lanes = lambda a: jnp.pad(a, ((0, 0), (0, 0), (0, 0), (0, 128 - a.shape[-1])))
    args = [lanes(jnp.swapaxes(c_re, 2, 3)), lanes(jnp.swapaxes(c_im, 2, 3)), lanes(prt), lanes(pit),
            jnp.concatenate([bbrt, bbit], -1), jnp.concatenate([bbit, bbrt], -1),
            jnp.concatenate([prow, prow], -1), jnp.concatenate([-pirow, pirow], -1)]
    gg = S5_GEN_GROUPS
    spec = lambda a: pl.BlockSpec((2, gg) + a.shape[2:], lambda g: (0, g, 0, 0))
    wt, ws, wh = pl.pallas_call(
        _s5_gen_kernel,
        grid=(G // gg,),
        in_specs=[spec(a) for a in args],
        out_specs=[pl.BlockSpec((gg, S5_ROW, S5_ROW), lambda g: (g, 0, 0)),
                   pl.BlockSpec((2, gg, S5_ROW, 2 * P), lambda g: (0, g, 0, 0)),
                   pl.BlockSpec((2, gg, 2 * P, S5_ROW), lambda g: (0, g, 0, 0))],
        out_shape=[jax.ShapeDtypeStruct((G, S5_ROW, S5_ROW), BF16),
                   jax.ShapeDtypeStruct((2, G, S5_ROW, 2 * P), BF16),
                   jax.ShapeDtypeStruct((2, G, 2 * P, S5_ROW), BF16)],
        compiler_params=_cparams(("parallel",)),
        name="s5_gen",
    )(*args)
    a_l = jnp.stack([prt[..., L].reshape(2, 1, G * P), pit[..., L].reshape(2, 1, G * P)], 1)
    return wt, ws, wh, a_l


S5_GB = 8


def _s5_core_kernel(u_ref, ws_ref, wt_ref, wh_ref, a_ref, h0_ref, y_ref, hfin_ref, x_ref, s_ref, hin_ref,
                    *, nb, nchunk):
    r = nb * nchunk
    npair = S5_GB // 2
    us = [u_ref[pl.ds(i, r, stride=S5_CHUNK), :] for i in range(S5_CHUNK)]
    for g in range(S5_GB):
        xg = jnp.concatenate([u[:, S5_GROUP * g:S5_GROUP * (g + 1)] for u in us], axis=-1)
        x_ref[:, g * S5_ROW:(g + 1) * S5_ROW] = xg.astype(BF16)
    for p in range(npair):
        for d in range(2):
            sg = [jnp.dot(x_ref[:, (2 * p + k) * S5_ROW:(2 * p + k + 1) * S5_ROW], ws_ref[d, 2 * p + k],
                          preferred_element_type=F32) for k in range(2)]
            for comp in range(2):
                cs = slice(comp * S5_STATE, (comp + 1) * S5_STATE)
                s_ref[d, comp, p] = jnp.concatenate([sg[0][:, cs], sg[1][:, cs]], axis=-1)
    chains = [(d, p) for d in range(2) for p in range(npair)]
    h = {}
    for d, p in chains:
        cols = slice(p * 128, (p + 1) * 128)
        h[d, p] = (h0_ref[d, 0, :, cols], h0_ref[d, 1, :, cols], a_ref[d, 0, :, cols], a_ref[d, 1, :, cols])
    for step in range(nchunk):
        for d, p in chains:
            rows = pl.ds(step if d == 0 else nchunk - 1 - step, nb, stride=nchunk)
            hr, hi, ar, ai = h[d, p]
            hin_ref[d, 0, p, rows, :] = hr
            hin_ref[d, 1, p, rows, :] = hi
            sr, si = s_ref[d, 0, p, rows, :], s_ref[d, 1, p, rows, :]
            h[d, p] = (ar * hr - ai * hi + sr, ar * hi + ai * hr + si, ar, ai)
    for d, p in chains:
        cols = slice(p * 128, (p + 1) * 128)
        hfin_ref[d, 0, :, cols] = h[d, p][0]
        hfin_ref[d, 1, :, cols] = h[d, p][1]
    ys = []
    for g in range(S5_GB):
        y = jnp.dot(x_ref[:, g * S5_ROW:(g + 1) * S5_ROW], wt_ref[g], preferred_element_type=F32)
        cs = slice((g % 2) * S5_STATE, (g % 2 + 1) * S5_STATE)
        for d in range(2):
            hg = jnp.concatenate([hin_ref[d, 0, g // 2, :, cs], hin_ref[d, 1, g // 2, :, cs]], axis=-1)
            y = y + jnp.dot(hg.astype(BF16), wh_ref[d, g], preferred_element_type=F32)
        ys.append(y)
    for j in range(S5_CHUNK):
        y_ref[pl.ds(j, r, stride=S5_CHUNK), :] = jnp.concatenate(
            [y[:, S5_GROUP * j:S5_GROUP * (j + 1)] for y in ys], axis=-1)


def s5_core(proj, mats, h0, nb, nchunk):
    wt, ws, wh, a_l = mats
    r = nb * nchunk
    sw = S5_GB * S5_STATE
    return pl.pallas_call(
        functools.partial(_s5_core_kernel, nb=nb, nchunk=nchunk),
        grid=(S5_GROUPS // S5_GB,),
        in_specs=[pl.BlockSpec((r * S5_CHUNK, 128), lambda j: (0, COL_UA // 128 + j)),
                  pl.BlockSpec((2, S5_GB, S5_ROW, 2 * S5_STATE), lambda j: (0, j, 0, 0)),
                  pl.BlockSpec((S5_GB, S5_ROW, S5_ROW), lambda j: (j, 0, 0)),
                  pl.BlockSpec((2, S5_GB, 2 * S5_STATE, S5_ROW), lambda j: (0, j, 0, 0)),
                  pl.BlockSpec((2, 2, 1, sw), lambda j: (0, 0, 0, j)),
                  pl.BlockSpec((2, 2, nb, sw), lambda j: (0, 0, 0, j))],
        out_specs=[pl.BlockSpec((r * S5_CHUNK, 128), lambda j: (0, j)),
                   pl.BlockSpec((2, 2, nb, sw), lambda j: (0, 0, 0, j))],
        out_shape=[jax.ShapeDtypeStruct((r * S5_CHUNK, S5_WIDTH), F32),
                   jax.ShapeDtypeStruct((2, 2, nb, S5_GROUPS * S5_STATE), F32)],
        scratch_shapes=[pltpu.VMEM((r, S5_GB * S5_ROW), BF16),
                        pltpu.VMEM((2, 2, S5_GB // 2, r, 128), F32),
                        pltpu.VMEM((2, 2, S5_GB // 2, r, 128), F32)],
        compiler_params=_cparams(("parallel",)),
        name="s5_core",
    )(proj, ws, wt, wh, a_l, h0)


def _rope_tables(t):
    rows = t // GRID_W
    row = np.repeat(np.arange(rows), GRID_W).astype(np.float32)
    col = np.tile(np.arange(GRID_W), rows).astype(np.float32)
    nf = DA_HEAD // 4
    inv = (ROPE_BASE ** (-jnp.arange(nf, dtype=F32) / nf))

    def tab(pos):
        ang = jnp.asarray(pos)[:, None] * inv[None, :]
        c, s = jnp.cos(ang), jnp.sin(ang)
        return jnp.concatenate([c, c], -1), jnp.concatenate([-s, s], -1)

    cr, sr = tab(row)
    cc, sc = tab(col)
    cos = jnp.concatenate([cr, cc], -1)
    sin = jnp.concatenate([sr, sc], -1)
    return jnp.tile(cos, (1, 2)), jnp.tile(sin, (1, 2))


def _rope(x, cos, sin):
    lane = lax.broadcasted_iota(jnp.int32, x.shape, 1)
    swapped = jnp.where((lane % 32) < 16, pltpu.roll(x, 112, 1), pltpu.roll(x, 16, 1))
    return x * cos + swapped * sin


def _attn_kernel(*refs, lam_init, t, s_tot, tq, with_ctx):
    if with_ctx:
        (q_ref, k_ref, v_ref, z_ref, kc_ref, vc_ref, cq_ref, sq_ref, ck_ref, sk_ref,
         lam_ref, ng_ref, o_ref, kall_ref, vall_ref, s_ref, e_ref, l_ref) = refs
    else:
        q_ref, k_ref, v_ref, z_ref, lam_ref, ng_ref, o_ref, kall_ref, vall_ref, s_ref, e_ref, l_ref = refs

    @pl.when(pl.program_id(1) == 0)
    def _():
        if with_ctx:
            for h in range(DA_HEADS):
                hs = slice(h * 128, (h + 1) * 128)
                kall_ref[h, :, 0:t] = _rope(k_ref[0, :, hs], ck_ref[...], sk_ref[...]).T.astype(BF16)
                kall_ref[h, :, t:s_tot] = kc_ref[0, 0, h].astype(BF16)
            vall_ref[0:t, :] = v_ref[0].astype(BF16)
            vall_ref[t:s_tot, :] = vc_ref[0].astype(BF16)
        else:
            for h in range(DA_HEADS):
                kall_ref[h] = k_ref[0, :, h * 128:(h + 1) * 128].T.astype(BF16)
            vall_ref[...] = v_ref[0].astype(BF16)

    lp = lam_ref[...]
    lam = (jnp.exp(jnp.sum(lp[0:1] * lp[1:2], axis=-1, keepdims=True))
           - jnp.exp(jnp.sum(lp[2:3] * lp[3:4], axis=-1, keepdims=True)) + lam_init)
    lane = lax.broadcasted_iota(jnp.int32, (tq, 128), 1)
    def scores(h):
        hs = slice(h * 128, (h + 1) * 128)
        q = q_ref[0, :, hs]
        if with_ctx:
            q = _rope(q, cq_ref[...], sq_ref[...])
        q = q * (DA_HEAD ** -0.5 * LOG2E)
        qs = jnp.concatenate([jnp.where(lane < DA_HEAD, q, 0.0), jnp.where(lane >= DA_HEAD, q, 0.0)], axis=0)
        s_ref[h % 2] = _mm(qs, kall_ref[h])

    scores(0)
    for h in range(DA_HEADS):
        hs = slice(h * 128, (h + 1) * 128)
        b = h % 2
        if h + 1 < DA_HEADS:
            scores(h + 1)
        for r in range(0, 2 * tq, ATTN_ROWS):
            sc = s_ref[b, r:r + ATTN_ROWS, :]
            e = jnp.exp2(sc - jnp.max(sc, axis=-1, keepdims=True))
            e_ref[b, r:r + ATTN_ROWS, :] = e.astype(BF16)
            l_ref[b, r:r + ATTN_ROWS, :] = jnp.broadcast_to(jnp.sum(e, axis=-1, keepdims=True),
                                                            (ATTN_ROWS, 128))
        ov = jnp.dot(e_ref[b], vall_ref[:, hs], preferred_element_type=F32) / l_ref[b]
        o = ov[:tq] - lam * ov[tq:]
        o = o * lax.rsqrt(jnp.mean(o * o, axis=-1, keepdims=True) + EPS) * ng_ref[...]
        o = o * (1.0 - lam_init)
        o_ref[0, :, hs] = (o * _silu(z_ref[0, :, hs])).astype(BF16)


def diff_attention(proj3, layer, lam_init, da_lam, da_norm_g, ctx_kv):
    nb, t, _ = proj3.shape
    with_ctx = ctx_kv is not None
    s_tot = t + (PAST_LEN if with_ctx else 0)
    tq = 256
    wb = DA_WIDTH
    in_specs = [pl.BlockSpec((1, tq, wb), lambda b, i: (b, i, COL_QB // wb)),
                pl.BlockSpec((1, t, wb), lambda b, i: (b, 0, COL_KB // wb)),
                pl.BlockSpec((1, t, wb), lambda b, i: (b, 0, COL_VB // wb)),
                pl.BlockSpec((1, tq, wb), lambda b, i: (b, i, COL_ZB // wb))]
    args = [proj3, proj3, proj3, proj3]
    if with_ctx:
        kc, vc = ctx_kv
        cos, sin = _rope_tables(t)
        in_specs += [pl.BlockSpec((1, 1, DA_HEADS, 2 * DA_HEAD, PAST_LEN), lambda b, i: (b, layer, 0, 0, 0)),
                     pl.BlockSpec((1, PAST_LEN, wb), lambda b, i: (b, layer, 0)),
                     pl.BlockSpec((tq, 128), lambda b, i: (i, 0)),
                     pl.BlockSpec((tq, 128), lambda b, i: (i, 0)),
                     pl.BlockSpec((t, 128), lambda b, i: (0, 0)),
                     pl.BlockSpec((t, 128), lambda b, i: (0, 0))]
        args += [kc, vc, cos, sin, cos, sin]
    in_specs += [pl.BlockSpec((4, DA_HEAD), lambda b, i: (0, 0)),
                 pl.BlockSpec((1, DA_VDIM), lambda b, i: (0, 0))]
    args += [da_lam, da_norm_g.reshape(1, DA_VDIM)]
    return pl.pallas_call(
        functools.partial(_attn_kernel, lam_init=lam_init, t=t, s_tot=s_tot, tq=tq, with_ctx=with_ctx),
        grid=(nb, t // tq),
        in_specs=in_specs,
        out_specs=pl.BlockSpec((1, tq, wb), lambda b, i: (b, i, 0)),
        out_shape=jax.ShapeDtypeStruct((nb, t, wb), BF16),
        scratch_shapes=[pltpu.VMEM((DA_HEADS, 2 * DA_HEAD, s_tot), BF16), pltpu.VMEM((s_tot, wb), BF16),
                        pltpu.VMEM((2, 2 * tq, s_tot), F32), pltpu.VMEM((2, 2 * tq, s_tot), BF16),
                        pltpu.VMEM((2, 2 * tq, 128), F32)],
        compiler_params=_cparams(("parallel", "arbitrary")),
        name="diff_attention",
    )(*args)


DN_PAD = 8
DN_RT = 128
DN_GROUP = 128
DN_STEP_GROUPS = 2


def _dn_kernel(*refs, t, ngroup, with_s0):
    if with_s0:
        (qkv_ref, z_ref, ba_ref, cw_ref, alog_ref, dtb_ref, ng_ref, s0_ref,
         out_ref, sfin_ref, xp_ref, qkvn_ref, oacc_ref, st_ref) = refs
    else:
        (qkv_ref, z_ref, ba_ref, cw_ref, alog_ref, dtb_ref, ng_ref,
         out_ref, sfin_ref, xp_ref, qkvn_ref, oacc_ref, st_ref) = refs
    n = pl.program_id(1)
    cd = DN_CHUNK
    w3 = 3 * DN_WIDTH

    @pl.when(n == 0)
    def _init():
        xp_ref[0:DN_PAD, :] = jnp.zeros((DN_PAD, w3), F32)
        xp_ref[DN_PAD + t:2 * DN_PAD + t, :] = jnp.zeros((DN_PAD, w3), F32)
        xp_ref[DN_PAD:DN_PAD + t, :] = qkv_ref[0]
        half = DN_CONV // 2
        for r in range(t // DN_RT):
            for sec in range(3):
                for h in range(DN_HEADS):
                    cs = slice(sec * DN_WIDTH + h * DN_HEAD, sec * DN_WIDTH + (h + 1) * DN_HEAD)
                    acc = jnp.zeros((DN_RT, DN_HEAD), F32)
                    for j in range(DN_CONV):
                        r0 = DN_PAD + r * DN_RT + j - half
                        acc = acc + xp_ref[r0:r0 + DN_RT, cs] * cw_ref[j:j + 1, cs]
                    y = _silu(acc)
                    if sec < 2:
                        y = y * lax.rsqrt(jnp.sum(y * y, axis=-1, keepdims=True) + EPS)
                    if sec == 0:
                        y = y * (DN_HEAD ** -0.5)
                    qkvn_ref[r * DN_RT:(r + 1) * DN_RT, cs] = y
        oacc_ref[...] = jnp.zeros_like(oacc_ref)
        if with_s0:
            st_ref[...] = s0_ref[0, 0]
        else:
            st_ref[...] = jnp.zeros_like(st_ref)

    gb = DN_GROUP
    nsub = gb // cd
    ri = lax.broadcasted_iota(jnp.int32, (gb, gb), 0)
    ci = lax.broadcasted_iota(jnp.int32, (gb, gb), 1)
    same = (ri // cd) == (ci // cd)
    samef = same.astype(F32)
    eye = (ri == ci).astype(BF16)
    masked_out = -1e30

    chains = []
    for d, gi in [(d, gi) for d in range(2) for gi in range(DN_STEP_GROUPS)]:
        grp = n * DN_STEP_GROUPS + gi
        r0 = pl.multiple_of((grp if d == 0 else ngroup - 1 - grp) * gb, gb)
        incl = same & ((ci <= ri) if d == 0 else (ci >= ri))
        strict = same & ((ci < ri) if d == 0 else (ci > ri))
        inclf = incl.astype(F32)
        inclog = jnp.where(incl, 0.0, masked_out)
        nstrict = -strict.astype(BF16)
        ba = ba_ref[0, pl.ds(r0, gb), :]
        beta_all = jax.nn.sigmoid(ba)
        g_all = -jnp.exp(alog_ref[...]) * jax.nn.softplus(ba + dtb_ref[...])
        gc = jnp.dot(inclf, g_all, precision=HI, preferred_element_type=F32)
        gct = lax.dot_general(g_all.T, inclf, (((1,), (1,)), ((), ())), precision=HI,
                              preferred_element_type=F32)
        gtot = jnp.dot(samef, g_all, precision=HI, preferred_element_type=F32)
        for h in range(DN_HEADS):
            hs = slice(h * DN_HEAD, (h + 1) * DN_HEAD)
            col = 2 * DN_HEADS + d * DN_HEADS + h
            gcol = gc[:, col:col + 1]
            grow = gct[col:col + 1, :]
            gt = gtot[:, col:col + 1]
            beta = beta_all[:, d * DN_HEADS + h:d * DN_HEADS + h + 1]
            q = qkvn_ref[pl.ds(r0, gb), hs]
            k = qkvn_ref[pl.ds(r0, gb), DN_WIDTH + h * DN_HEAD:DN_WIDTH + (h + 1) * DN_HEAD]
            v = qkvn_ref[pl.ds(r0, gb), 2 * DN_WIDTH + h * DN_HEAD:2 * DN_WIDTH + (h + 1) * DN_HEAD]
            eg = jnp.exp(gcol)
            chains.append(dict(
                d=d, gi=gi, h=h, r0=r0, hs=hs, nstrict=nstrict, q=q, k=k, kb=k * beta,
                decay=jnp.exp((gcol - grow) + inclog),
                rhs=jnp.concatenate([v * beta, k * beta * eg], axis=-1),
                qe=q * eg, kdec=k * jnp.exp(gt - gcol), egt=jnp.exp(gt)))

    for c in chains:
        c["nmm"] = (_mm_nt(c["kb"], c["k"]) * c["decay"]).astype(BF16) * c["nstrict"]
    for c in chains:
        c["qk"] = _mm_nt(c["q"], c["k"]) * c["decay"]
    def level_mask(s):
        return (((ri // (2 * s)) == (ci // (2 * s))) & ((ri // s) != (ci // s))).astype(BF16)

    pair = level_mask(1)
    for c in chains:
        c["tinv"] = eye + c["nmm"] * pair
    s = 2
    while s < cd:
        offmask = level_mask(s)
        xs = [jnp.dot(c["nmm"] * offmask, c["tinv"], preferred_element_type=F32).astype(BF16) for c in chains]
        ys = [jnp.dot(c["tinv"], x, preferred_element_type=F32) for c, x in zip(chains, xs)]
        for c, y in zip(chains, ys):
            c["tinv"] = c["tinv"] + y.astype(BF16)
        s *= 2
    for c in chains:
        c["uw"] = _mm(c["tinv"], c["rhs"])

    st = {(d, h): st_ref[d, h] for d in range(2) for h in range(DN_HEADS)}
    outs = []
    for gi, step in [(gi, step) for gi in range(DN_STEP_GROUPS) for step in range(nsub)]:
        active = [c for c in chains if c["gi"] == gi]
        rs = []
        for c in active:
            sub = step if c["d"] == 0 else nsub - 1 - step
            c["rows"] = slice(sub * cd, (sub + 1) * cd)
            rs.append(_mm(jnp.concatenate([c["uw"][c["rows"], DN_HEAD:], c["qe"][c["rows"]]], axis=0),
                          st[c["d"], c["h"]]))
        for c, r in zip(active, rs):
            rows = c["rows"]
            v_new = c["uw"][rows, :DN_HEAD] - r[:cd]
            o = r[cd:] + _mm(c["qk"][rows, rows], v_new)
            st[c["d"], c["h"]] = (st[c["d"], c["h"]] * c["egt"][rows.start:rows.start + 1]
                                  + _mm(c["kdec"][rows].T, v_new))
            outs.append((c, rows.start, o))
    for (d, h), v in st.items():
        st_ref[d, h] = v
    for c, off, o in outs:
        oacc_ref[pl.ds(pl.multiple_of(c["r0"] + off, cd), cd), c["hs"]] += o

    @pl.when(n == ngroup // DN_STEP_GROUPS - 1)
    def _fin():
        for h in range(DN_HEADS):
            hs = slice(h * DN_HEAD, (h + 1) * DN_HEAD)
            o = oacc_ref[:, hs]
            o = o * lax.rsqrt(jnp.mean(o * o, axis=-1, keepdims=True) + EPS) * ng_ref[...]
            out_ref[0, :, hs] = (o * _silu(z_ref[0, :, hs])).astype(BF16)
        sfin_ref[0] = st_ref[...]


def deltanet(proj3, ba3, layer, conv_w, a_log, dt_bias, norm_g, s0):
    nb, t, _ = proj3.shape
    ngroup = t // DN_GROUP
    with_s0 = s0 is not None
    w3 = 3 * DN_WIDTH
    pad = jnp.zeros((2 * DN_HEADS,), F32)
    alog_row = jnp.concatenate([pad, a_log.reshape(-1), jnp.zeros((128 - 4 * DN_HEADS,), F32)]).reshape(1, 128)
    dtb_row = jnp.concatenate([pad, dt_bias.reshape(-1), jnp.zeros((128 - 4 * DN_HEADS,), F32)]).reshape(1, 128)
    in_specs = [pl.BlockSpec((1, t, w3), lambda b, n: (b, 0, COL_QC // w3)),
                pl.BlockSpec((1, t, DN_WIDTH), lambda b, n: (b, 0, COL_ZC // DN_WIDTH)),
                pl.BlockSpec((1, t, 128), lambda b, n: (b, 0, 0)),
                pl.BlockSpec((8, w3), lambda b, n: (0, 0)),
                pl.BlockSpec((1, 128), lambda b, n: (0, 0)),
                pl.BlockSpec((1, 128), lambda b, n: (0, 0)),
                pl.BlockSpec((1, DN_HEAD), lambda b, n: (0, 0))]
    args = [proj3, proj3, ba3, jnp.pad(conv_w, ((0, 8 - DN_CONV), (0, 0))), alog_row, dtb_row,
            norm_g.reshape(1, DN_HEAD)]
    if with_s0:
        in_specs.append(pl.BlockSpec((1, 1, 2, DN_HEADS, DN_HEAD, DN_HEAD), lambda b, n: (b, layer, 0, 0, 0, 0)))
        args.append(s0)
    return pl.pallas_call(
        functools.partial(_dn_kernel, t=t, ngroup=ngroup, with_s0=with_s0),
        grid=(nb, ngroup // DN_STEP_GROUPS),
        in_specs=in_specs,
        out_specs=[pl.BlockSpec((1, t, DN_WIDTH), lambda b, n: (b, 0, 0)),
                   pl.BlockSpec((1, 2, DN_HEADS, DN_HEAD, DN_HEAD), lambda b, n: (b, 0, 0, 0, 0))],
        out_shape=[jax.ShapeDtypeStruct((nb, t, DN_WIDTH), BF16),
                   jax.ShapeDtypeStruct((nb, 2, DN_HEADS, DN_HEAD, DN_HEAD), F32)],
        scratch_shapes=[pltpu.VMEM((t + 2 * DN_PAD, w3), F32),
                        pltpu.VMEM((t, w3), F32),
                        pltpu.VMEM((t, DN_WIDTH), F32),
                        pltpu.VMEM((2, DN_HEADS, DN_HEAD, DN_HEAD), F32)],
        compiler_params=_cparams(("parallel", "arbitrary")),
        name="deltanet",
    )(*args)


def _merge_kernel(u_ref, ys_ref, za_ref, sd_ref, wglu_ref, ob_ref, oc_ref, hn_ref, wg_ref, wb_ref, wo_ref,
                  x_ref, gate_ref, fg_ref, *outs, final):
    tm = x_ref.shape[0]
    for rows in (slice(0, tm // 2), slice(tm // 2, tm)):
        ya = jax.nn.gelu(sd_ref[...] * u_ref[rows, :] + ys_ref[rows, :])
        ya = ya * jax.nn.sigmoid(_mm(ya, wglu_ref[...]))
        branches = ((ya * _silu(za_ref[rows, :])).astype(BF16), ob_ref[rows, :], oc_ref[rows, :])
        acc = None
        hn = hn_ref[rows, :]
        for i, o in enumerate(branches):
            pr = jnp.dot(o, wb_ref[i], preferred_element_type=F32)
            gt = jnp.dot(hn, wg_ref[:, i * D_MODEL:(i + 1) * D_MODEL], preferred_element_type=F32)
            term = jax.nn.sigmoid(gt) * pr
            acc = term if acc is None else acc + term
        y = jnp.dot(acc.astype(BF16), wo_ref[...], preferred_element_type=F32)
        xn = x_ref[rows, :] + gate_ref[0] * y
        outs[0][rows, :] = xn
        if final:
            yn = xn * lax.rsqrt(jnp.mean(xn * xn, axis=-1, keepdims=True) + EPS) * fg_ref[...]
            outs[1][rows, :] = yn


def merge(proj, y_s5, s5_d, w_glu, out_b, out_c, hn, w_gates, w_branch, w_out, x2, gate, final_g, rows_per_mod,
          final):
    m = x2.shape[0]
    tm = 512
    nmod = gate.shape[0]
    row = lambda i: (i, 0)
    out_specs = [pl.BlockSpec((tm, D_MODEL), row)]
    out_shape = [jax.ShapeDtypeStruct((m, D_MODEL), F32)]
    if final:
        out_specs.append(pl.BlockSpec((tm, D_MODEL), row))
        out_shape.append(jax.ShapeDtypeStruct((m, D_MODEL), F32))
    return pl.pallas_call(
        functools.partial(_merge_kernel, final=final),
        grid=(m // tm,),
        in_specs=[pl.BlockSpec((tm, S5_WIDTH), lambda i: (i, COL_UA // S5_WIDTH)),
                  pl.BlockSpec((tm, S5_WIDTH), row),
                  pl.BlockSpec((tm, S5_WIDTH), lambda i: (i, COL_ZA // S5_WIDTH)),
                  pl.BlockSpec((1, S5_WIDTH), lambda i: (0, 0)),
                  pl.BlockSpec((S5_WIDTH, S5_WIDTH), lambda i: (0, 0)),
                  pl.BlockSpec((tm, BRANCH_WIDTH), row),
                  pl.BlockSpec((tm, BRANCH_WIDTH), row),
                  pl.BlockSpec((tm, D_MODEL), row),
                  pl.BlockSpec((D_MODEL, N_BRANCH * D_MODEL), lambda i: (0, 0)),
                  pl.BlockSpec((N_BRANCH, BRANCH_WIDTH, D_MODEL), lambda i: (0, 0, 0)),
                  pl.BlockSpec((D_MODEL, D_MODEL), lambda i: (0, 0)),
                  pl.BlockSpec((tm, D_MODEL), row),
                  pl.BlockSpec((1, 1, D_MODEL), lambda i: ((i * tm) // rows_per_mod, 0, 0)),
                  pl.BlockSpec((1, D_MODEL), lambda i: (0, 0))],
        out_specs=out_specs,
        out_shape=out_shape,
        compiler_params=_cparams(("parallel",)),
        name="merge",
    )(proj, y_s5, proj, s5_d.reshape(1, S5_WIDTH), w_glu, out_b, out_c, hn, w_gates, w_branch, w_out, x2,
      gate.reshape(nmod, 1, D_MODEL),
      final_g.reshape(1, D_MODEL))


def _run_pass(x, mod, wts, lam_inits, final_g, ctx):
    nb, t, _ = x.shape
    m = nb * t
    nmod = mod.shape[1]
    rows_per_mod = m if nmod == 1 else t
    x2 = x.reshape(m, D_MODEL)
    states = []
    y, caches = None, None
    for l in range(DEPTH):
        w = wts[l]
        shift, scale, gate = jnp.split(mod[l], 3, axis=-1)
        if ctx is None:
            proj, ba, hn, *caches = inproj(x2, w["norm_g"], scale, shift, w["w1"], w["w2"], rows_per_mod,
                                       kv=(l, t, caches))
            h0 = jnp.zeros((2, 2, nb, S5_GROUPS * S5_STATE), F32)
            ctx_kv, s0 = None, None
        else:
            proj, ba, hn = inproj(x2, w["norm_g"], scale, shift, w["w1"], w["w2"], rows_per_mod)
            cache_k, cache_v, st_re, st_im, st_dn = ctx
            h0 = jnp.stack([st_re[:, l], st_im[:, l]], 0)
            h0 = jnp.transpose(h0, (2, 0, 1, 3, 4)).reshape(2, 2, nb, S5_GROUPS * S5_STATE)
            ctx_kv, s0 = (cache_k, cache_v), st_dn
        proj3 = proj.reshape(nb, t, N_MAIN)
        y_s5, hfin = s5_core(proj, w["s5_mats"], h0, nb, t // S5_CHUNK)
        out_b = diff_attention(proj3, l, lam_inits[l], w["da_lam"], w["da_norm_g"], ctx_kv)
        out_c, sfin = deltanet(proj3, ba.reshape(nb, t, 128), l, w["dn_conv"], w["dn_a_log"],
                               w["dn_dt_bias"], w["dn_norm_g"], s0)
        final = l == DEPTH - 1
        res = merge(proj, y_s5, w["s5_d"], w["w_glu"], out_b.reshape(m, DA_WIDTH), out_c.reshape(m, DN_WIDTH),
                    hn, w["w_gates"], w["w_branch"], w["w_out"], x2, gate, final_g, rows_per_mod, final)
        x2 = res[0]
        if final:
            y = res[1]
        if ctx is None:
            hf = hfin.reshape(2, 2, nb, S5_GROUPS, S5_STATE)
            states.append((jnp.transpose(hf[:, 0], (1, 0, 2, 3)), jnp.transpose(hf[:, 1], (1, 0, 2, 3)), sfin))
    return y.reshape(nb, t, D_MODEL), states, caches


def kernel(x_prompt, x_sample, cache_k, cache_v, state_s5_re, state_s5_im, state_dn, c, c_ctx,
           norm_g, w_ada, b_ada, w_in, s5_lam_re, s5_lam_im, s5_log_step, s5_b_re, s5_b_im,
           s5_c_re, s5_c_im, s5_d, s5_w_glu, da_lam, da_norm_g, dn_conv, dn_a_log, dn_dt_bias,
           dn_norm_g, w_branch, w_out, final_norm_g):
    nb_dec = x_sample.shape[0]
    cond8 = jnp.concatenate([c_ctx[None, :], c, jnp.zeros((8 - 1 - nb_dec, D_MODEL), F32)], 0)
    mod = ada_mod(cond8, w_ada, b_ada)
    wts = []
    for l in range(DEPTH):
        w1, w_gates, w2 = cast_w_in(w_in, l)
        wts.append(dict(
            norm_g=norm_g[l], w1=w1, w2=w2, w_gates=w_gates,
            s5_mats=s5_matrices(s5_lam_re[l], s5_lam_im[l], s5_log_step[l], s5_b_re[l], s5_b_im[l],
                                s5_c_re[l], s5_c_im[l]),
            s5_d=s5_d[l], w_glu=s5_w_glu[l].astype(BF16), da_lam=da_lam[l], da_norm_g=da_norm_g[l],
            dn_conv=dn_conv[l], dn_a_log=dn_a_log[l], dn_dt_bias=dn_dt_bias[l], dn_norm_g=dn_norm_g[l],
            w_branch=w_branch[l].astype(BF16), w_out=w_out[l].astype(BF16)))
    lam_inits = [0.8 - 0.6 * math.exp(-0.3 * l) for l in range(DEPTH)]

    y_prompt, states, (k_new, v_new) = _run_pass(x_prompt, mod[:, 0:1], wts, lam_inits, final_norm_g, None)
    cache_kt = jnp.transpose(cache_k, (0, 1, 3, 4, 5, 2)).reshape(nb_dec, DEPTH, DA_HEADS, 2 * DA_HEAD, PAST_LEN)
    ctx = (cache_kt, cache_v.reshape(nb_dec, DEPTH * PAST_LEN, DA_WIDTH), state_s5_re, state_s5_im, state_dn)
    y_sample, _, _ = _run_pass(x_sample, mod[:, 1:1 + nb_dec], wts, lam_inits, final_norm_g, ctx)

    nb, t = x_prompt.shape[:2]
    new_cache_k = k_new.reshape(nb, DEPTH, t, DA_HEADS, 2, DA_HEAD)
    new_cache_v = v_new.reshape(nb, DEPTH, t, DA_HEADS, DA_VDIM)
    new_s5_re = jnp.stack([s[0] for s in states], axis=1)
    new_s5_im = jnp.stack([s[1] for s in states], axis=1)
    new_dn = jnp.stack([s[2] for s in states], axis=1)
    return (y_prompt, y_sample, new_cache_k, new_cache_v, new_s5_re, new_s5_im, new_dn)
```

```python
import functools
import math

import numpy as np
import jax
import jax.numpy as jnp
from jax import lax
from jax.experimental import pallas as pl
from jax.experimental.pallas import tpu as pltpu

F32 = jnp.float32
BF16 = jnp.bfloat16

D_MODEL = 1024
DEPTH = 2
GRID_W = 64
EPS = 1e-6
S5_WIDTH = 512
S5_GROUP = 16
S5_GROUPS = 32
S5_STATE = 64
S5_CHUNK = 16
S5_PAIRS = S5_GROUPS // 2
S5_ROW = S5_CHUNK * S5_GROUP
S5_GEN_GROUPS = 4
DA_HEADS = 4
DA_HEAD = 64
DA_VDIM = 128
DA_WIDTH = 512
ROPE_BASE = 10000.0
DN_HEADS = 4
DN_HEAD = 128
DN_WIDTH = 512
DN_CONV = 5
DN_CHUNK = 64
N_BRANCH = 3
BRANCH_WIDTH = 512
PAST_LEN = 512

COL_UA, COL_ZA, COL_QB, COL_KB, COL_VB, COL_ZB = 0, 512, 1024, 1536, 2048, 2560
COL_QC, COL_ZC = 3072, 4608
N_MAIN = 5120
BA_OFF = 5120
GATES_OFF = 5136

VMEM_LIMIT = 56 * 1024 * 1024
HI = lax.Precision.HIGHEST
LOG2E = math.log2(math.e)
ATTN_ROWS = 16


def _cparams(sem):
    return pltpu.CompilerParams(dimension_semantics=sem, vmem_limit_bytes=VMEM_LIMIT)


def _mm(a, b):
    return jnp.dot(a.astype(BF16), b.astype(BF16), preferred_element_type=F32)


def _mm_nt(a, b):
    return lax.dot_general(a.astype(BF16), b.astype(BF16), (((1,), (1,)), ((), ())),
                           preferred_element_type=F32)


def _silu(x):
    return x * jax.nn.sigmoid(x)


def _ada_kernel(c_ref, w_ref, b_ref, o_ref):
    o_ref[0] = _mm(_silu(c_ref[...]), w_ref[0]) + b_ref[0]


def ada_mod(cond8, w_ada, b_ada):
    tn = 1024
    return pl.pallas_call(
        _ada_kernel,
        grid=(DEPTH, 3 * D_MODEL // tn),
        in_specs=[pl.BlockSpec((8, D_MODEL), lambda l, j: (0, 0)),
                  pl.BlockSpec((1, D_MODEL, tn), lambda l, j: (l, 0, j)),
                  pl.BlockSpec((1, 1, tn), lambda l, j: (l, 0, j))],
        out_specs=pl.BlockSpec((1, 8, tn), lambda l, j: (l, 0, j)),
        out_shape=jax.ShapeDtypeStruct((DEPTH, 8, 3 * D_MODEL), F32),
        compiler_params=_cparams(("parallel", "parallel")),
        name="ada_mod",
    )(cond8, w_ada, b_ada.reshape(DEPTH, 1, 3 * D_MODEL))


def _tcast_kernel(w_ref, o_ref, *, keep):
    x = w_ref[0].T
    if keep is not None:
        x = jnp.where(lax.broadcasted_iota(jnp.int32, x.shape, 1) < keep, x, 0.0)
    o_ref[...] = x.astype(BF16)


def _tcast_rows_kernel(w_hbm, o_ref, buf, sem, *, layer, row0):
    tn = buf.shape[0]
    start = pl.multiple_of(row0 + pl.program_id(0) * tn, 8)
    cp = pltpu.make_async_copy(w_hbm.at[layer, pl.ds(start, tn), :], buf, sem)
    cp.start()
    cp.wait()
    o_ref[...] = buf[...].T.astype(BF16)


def cast_w_in(w_in, layer):
    wt = jnp.swapaxes(w_in, 1, 2)
    tn = 1024
    w1 = pl.pallas_call(
        functools.partial(_tcast_kernel, keep=None),
        grid=(N_MAIN // tn,),
        in_specs=[pl.BlockSpec((1, tn, D_MODEL), lambda j: (layer, j, 0))],
        out_specs=pl.BlockSpec((D_MODEL, tn), lambda j: (0, j)),
        out_shape=jax.ShapeDtypeStruct((D_MODEL, N_MAIN), BF16),
        compiler_params=_cparams(("parallel",)),
        name="cast_w1",
    )(wt)
    gw = N_BRANCH * D_MODEL
    w_gates = pl.pallas_call(
        functools.partial(_tcast_rows_kernel, layer=layer, row0=GATES_OFF),
        grid=(gw // tn,),
        in_specs=[pl.BlockSpec(memory_space=pl.ANY)],
        out_specs=pl.BlockSpec((D_MODEL, tn), lambda j: (0, j)),
        out_shape=jax.ShapeDtypeStruct((D_MODEL, gw), BF16),
        scratch_shapes=[pltpu.VMEM((tn, D_MODEL), F32), pltpu.SemaphoreType.DMA(())],
        compiler_params=_cparams(("parallel",)),
        name="cast_w_gates",
    )(wt)
    w_ba = pl.pallas_call(
        functools.partial(_tcast_kernel, keep=GATES_OFF - BA_OFF),
        grid=(1,),
        in_specs=[pl.BlockSpec((1, 128, D_MODEL), lambda j: (layer, BA_OFF // 128, 0))],
        out_specs=pl.BlockSpec((D_MODEL, 128), lambda j: (0, 0)),
        out_shape=jax.ShapeDtypeStruct((D_MODEL, 128), BF16),
        compiler_params=_cparams(("parallel",)),
        name="cast_w_ba",
    )(wt)
    return w1, w_gates, w_ba


def _inproj_kernel(*refs, n_prev, with_kv):
    x_ref, g_ref, sc_ref, sh_ref, w1_ref, w2_ref = refs[:6]
    prev = refs[6:6 + n_prev]
    outs = refs[6 + n_prev:]
    proj_ref, ba_ref, hn_ref = outs[:3]
    j = pl.program_id(1)

    @pl.when(j == 0)
    def _():
        x = x_ref[...]
        y = x * lax.rsqrt(jnp.mean(x * x, axis=-1, keepdims=True) + EPS) * g_ref[...]
        hn = (y * (1.0 + sc_ref[0]) + sh_ref[0]).astype(BF16)
        hn_ref[...] = hn
        ba_ref[...] = jnp.dot(hn, w2_ref[...], preferred_element_type=F32)

    proj_ref[...] = jnp.dot(hn_ref[...], w1_ref[...], preferred_element_type=F32)

    if with_kv:
        tn = proj_ref.shape[1]
        for c, (ref, col) in enumerate(((outs[3], COL_KB), (outs[4], COL_VB))):
            @pl.when(j == col // tn)
            def _(c=c, ref=ref, col=col):
                nb, nl, t, w = ref.shape
                if n_prev:
                    ref[:, 0:nl - 1] = prev[c][...]
                ref[:, nl - 1:nl] = proj_ref[:, col % tn:col % tn + w].reshape(nb, 1, t, w)


def inproj(x2, norm_g, scale, shift, w1, w2, rows_per_mod, kv=None):
    m = x2.shape[0]
    tm, tn = (1024, 1024) if (kv is not None and kv[2] is not None) else (512, N_MAIN)
    nmod = scale.shape[0]
    mod_idx = lambda i, j: ((i * tm) // rows_per_mod, 0, 0)
    in_specs = [pl.BlockSpec((tm, D_MODEL), lambda i, j: (i, 0)),
                pl.BlockSpec((1, D_MODEL), lambda i, j: (0, 0)),
                pl.BlockSpec((1, 1, D_MODEL), mod_idx),
                pl.BlockSpec((1, 1, D_MODEL), mod_idx),
                pl.BlockSpec((D_MODEL, tn), lambda i, j: (0, j)),
                pl.BlockSpec((D_MODEL, 128), lambda i, j: (0, 0))]
    args = [x2, norm_g.reshape(1, D_MODEL), scale.reshape(nmod, 1, D_MODEL),
            shift.reshape(nmod, 1, D_MODEL), w1, w2]
    out_specs = [pl.BlockSpec((tm, tn), lambda i, j: (i, j)),
                 pl.BlockSpec((tm, 128), lambda i, j: (i, 0)),
                 pl.BlockSpec((tm, D_MODEL), lambda i, j: (i, 0))]
    out_shape = [jax.ShapeDtypeStruct((m, N_MAIN), F32),
                 jax.ShapeDtypeStruct((m, 128), F32),
                 jax.ShapeDtypeStruct((m, D_MODEL), BF16)]
    n_prev = 0
    if kv is not None:
        layer, t, caches = kv
        cspec = lambda nl: pl.BlockSpec((tm // t, nl, t, DA_WIDTH), lambda i, j: (i, 0, 0, 0))
        out_specs += [cspec(layer + 1)] * 2
        out_shape += [jax.ShapeDtypeStruct((m // t, layer + 1, t, DA_WIDTH), F32)] * 2
        if caches is not None:
            n_prev = 2
            in_specs += [cspec(layer)] * 2
            args += list(caches)
    return pl.pallas_call(
        functools.partial(_inproj_kernel, n_prev=n_prev, with_kv=kv is not None),
        grid=(m // tm, N_MAIN // tn),
        in_specs=in_specs,
        out_specs=out_specs,
        out_shape=out_shape,
        compiler_params=_cparams(("parallel", "arbitrary")),
        name="inproj",
    )(*args)


def _s5_gen_kernel(crt_ref, cit_ref, prt_ref, pit_ref, bbt_ref, bbs_ref, prow_ref, pirow_ref,
                   wt_ref, ws_ref, wh_ref):
    L, C, P = S5_CHUNK, S5_GROUP, S5_STATE
    width = (L + 1) * C
    row = lax.broadcasted_iota(jnp.int32, (128, width), 0)
    lane = lax.broadcasted_iota(jnp.int32, (128, width), 1)
    tile_c = (lane % C == row).astype(F32)
    expand = lambda a, e: jnp.dot(a, e, precision=HI, preferred_element_type=F32)
    zeros = jnp.zeros((C, L * C), F32)
    for g in range(S5_GEN_GROUPS):
        strips = []
        for d in range(2):
            spread_k = ((lane // C if d == 0 else L - lane // C) == row).astype(F32)
            crx, cix = expand(crt_ref[d, g], tile_c), expand(cit_ref[d, g], tile_c)
            prx, pix = expand(prt_ref[d, g], spread_k), expand(pit_ref[d, g], spread_k)
            ca = jnp.concatenate([crx * prx - cix * pix, -(crx * pix + cix * prx)], axis=0)
            wh_ref[d, g] = (ca[:, C:] if d == 0 else ca[:, :L * C]).astype(BF16)
            bbt, bbs = bbt_ref[d, g], bbs_ref[d, g]
            strips.append(jnp.dot(bbt, ca[:, :L * C] if d == 0 else ca[:, C:], precision=HI,
                                  preferred_element_type=F32))
            rows = []
            for i in range(L):
                k = L - 1 - i if d == 0 else i
                rows.append(bbt * prow_ref[d, g, k:k + 1, :] + bbs * pirow_ref[d, g, k:k + 1, :])
            ws_ref[d, g] = jnp.concatenate(rows, axis=0).astype(BF16)
        fpad = jnp.concatenate([zeros, strips[0]], axis=-1)
        rpad = jnp.concatenate([strips[1], zeros], axis=-1)
        rows = []
        for i in range(L):
            rows.append(fpad[:, (L - i) * C:(2 * L - i) * C] + rpad[:, (L - 1 - i) * C:(2 * L - 1 - i) * C])
        wt_ref[g] = jnp.concatenate(rows, axis=0).astype(BF16)


def s5_matrices(lam_re, lam_im, log_step, b_re, b_im, c_re, c_im):
    L, G, P, C = S5_CHUNK, S5_GROUPS, S5_STATE, S5_GROUP
    step = jnp.exp(log_step)[..., None]
    mag = jnp.exp(lam_re * step)
    ar, ai = mag * jnp.cos(lam_im * step), mag * jnp.sin(lam_im * step)
    den = lam_re * lam_re + lam_im * lam_im
    fr = ((ar - 1.0) * lam_re + ai * lam_im) / den
    fi = (ai * lam_re - (ar - 1.0) * lam_im) / den
    bbr = fr[..., None] * b_re - fi[..., None] * b_im
    bbi = fr[..., None] * b_im + fi[..., None] * b_re
    ks = jnp.arange(L + 1, dtype=F32)[None, None, :, None]
    pmag = jnp.exp(ks * (lam_re * step)[:, :, None, :])
    prow = pmag * jnp.cos(ks * (lam_im * step)[:, :, None, :])
    pirow = pmag * jnp.sin(ks * (lam_im * step)[:, :, None, :])
    prow, pirow = lax.optimization_barrier((prow, pirow))
    prt, pit = jnp.swapaxes(prow, 2, 3), jnp.swapaxes(pirow, 2, 3)
    bbrt, bbit = jnp.swapaxes(bbr, 2, 3), jnp.swapaxes(bbi, 2, 3)
    lanes = lambda a: jnp.pad(a, ((0, 0), (0, 0), (0, 0), (0, 128 - a.shape[-1])))
    args = [lanes(jnp.swapaxes(c_re, 2, 3)), lanes(jnp.swapaxes(c_im, 2, 3)), lanes(prt), lanes(pit),
            jnp.concatenate([bbrt, bbit], -1), jnp.concatenate([bbit, bbrt], -1),
            jnp.concatenate([prow, prow], -1), jnp.concatenate([-pirow, pirow], -1)]
    gg = S5_GEN_GROUPS
    spec = lambda a: pl.BlockSpec((2, gg) + a.shape[2:], lambda g: (0, g, 0, 0))
    wt, ws, wh = pl.pallas_call(
        _s5_gen_kernel,
        grid=(G // gg,),
        in_specs=[spec(a) for a in args],
        out_specs=[pl.BlockSpec((gg, S5_ROW, S5_ROW), lambda g: (g, 0, 0)),
                   pl.BlockSpec((2, gg, S5_ROW, 2 * P), lambda g: (0, g, 0, 0)),
                   pl.BlockSpec((2, gg, 2 * P, S5_ROW), lambda g: (0, g, 0, 0))],
        out_shape=[jax.ShapeDtypeStruct((G, S5_ROW, S5_ROW), BF16),
                   jax.ShapeDtypeStruct((2, G, S5_ROW, 2 * P), BF16),
                   jax.ShapeDtypeStruct((2, G, 2 * P, S5_ROW), BF16)],
        compiler_params=_cparams(("parallel",)),
        name="s5_gen",
    )(*args)
    a_l = jnp.stack([prt[..., L].reshape(2, 1, G * P), pit[..., L].reshape(2, 1, G * P)], 1)
    return wt, ws, wh, a_l


S5_GB = 8


def _s5_core_kernel(u_ref, ws_ref, wt_ref, wh_ref, a_ref, h0_ref, y_ref, hfin_ref, x_ref, s_ref, hin_ref,
                    *, nb, nchunk):
    r = nb * nchunk
    npair = S5_GB // 2
    us = [u_ref[pl.ds(i, r, stride=S5_CHUNK), :] for i in range(S5_CHUNK)]
    for g in range(S5_GB):
        xg = jnp.concatenate([u[:, S5_GROUP * g:S5_GROUP * (g + 1)] for u in us], axis=-1)
        x_ref[:, g * S5_ROW:(g + 1) * S5_ROW] = xg.astype(BF16)
    for p in range(npair):
        for d in range(2):
            sg = [jnp.dot(x_ref[:, (2 * p + k) * S5_ROW:(2 * p + k + 1) * S5_ROW], ws_ref[d, 2 * p + k],
                          preferred_element_type=F32) for k in range(2)]
            for comp in range(2):
                cs = slice(comp * S5_STATE, (comp + 1) * S5_STATE)
                s_ref[d, comp, p] = jnp.concatenate([sg[0][:, cs], sg[1][:, cs]], axis=-1)
    chains = [(d, p) for d in range(2) for p in range(npair)]
    h = {}
    for d, p in chains:
        cols = slice(p * 128, (p + 1) * 128)
        h[d, p] = (h0_ref[d, 0, :, cols], h0_ref[d, 1, :, cols], a_ref[d, 0, :, cols], a_ref[d, 1, :, cols])
    for step in range(nchunk):
        for d, p in chains:
            rows = pl.ds(step if d == 0 else nchunk - 1 - step, nb, stride=nchunk)
            hr, hi, ar, ai = h[d, p]
            hin_ref[d, 0, p, rows, :] = hr
            hin_ref[d, 1, p, rows, :] = hi
            sr, si = s_ref[d, 0, p, rows, :], s_ref[d, 1, p, rows, :]
            h[d, p] = (ar * hr - ai * hi + sr, ar * hi + ai * hr + si, ar, ai)
    for d, p in chains:
        cols = slice(p * 128, (p + 1) * 128)
        hfin_ref[d, 0, :, cols] = h[d, p][0]
        hfin_ref[d, 1, :, cols] = h[d, p][1]
    ys = []
    for g in range(S5_GB):
        y = jnp.dot(x_ref[:, g * S5_ROW:(g + 1) * S5_ROW], wt_ref[g], preferred_element_type=F32)
        cs = slice((g % 2) * S5_STATE, (g % 2 + 1) * S5_STATE)
        for d in range(2):
            hg = jnp.concatenate([hin_ref[d, 0, g // 2, :, cs], hin_ref[d, 1, g // 2, :, cs]], axis=-1)
            y = y + jnp.dot(hg.astype(BF16), wh_ref[d, g], preferred_element_type=F32)
        ys.append(y)
    for j in range(S5_CHUNK):
        y_ref[pl.ds(j, r, stride=S5_CHUNK), :] = jnp.concatenate(
            [y[:, S5_GROUP * j:S5_GROUP * (j + 1)] for y in ys], axis=-1)


def s5_core(proj, mats, h0, nb, nchunk):
    wt, ws, wh, a_l = mats
    r = nb * nchunk
    sw = S5_GB * S5_STATE
    return pl.pallas_call(
        functools.partial(_s5_core_kernel, nb=nb, nchunk=nchunk),
        grid=(S5_GROUPS // S5_GB,),
        in_specs=[pl.BlockSpec((r * S5_CHUNK, 128), lambda j: (0, COL_UA // 128 + j)),
                  pl.BlockSpec((2, S5_GB, S5_ROW, 2 * S5_STATE), lambda j: (0, j, 0, 0)),
                  pl.BlockSpec((S5_GB, S5_ROW, S5_ROW), lambda j: (j, 0, 0)),
                  pl.BlockSpec((2, S5_GB, 2 * S5_STATE, S5_ROW), lambda j: (0, j, 0, 0)),
                  pl.BlockSpec((2, 2, 1, sw), lambda j: (0, 0, 0, j)),
                  pl.BlockSpec((2, 2, nb, sw), lambda j: (0, 0, 0, j))],
        out_specs=[pl.BlockSpec((r * S5_CHUNK, 128), lambda j: (0, j)),
                   pl.BlockSpec((2, 2, nb, sw), lambda j: (0, 0, 0, j))],
        out_shape=[jax.ShapeDtypeStruct((r * S5_CHUNK, S5_WIDTH), F32),
                   jax.ShapeDtypeStruct((2, 2, nb, S5_GROUPS * S5_STATE), F32)],
        scratch_shapes=[pltpu.VMEM((r, S5_GB * S5_ROW), BF16),
                        pltpu.VMEM((2, 2, S5_GB // 2, r, 128), F32),
                        pltpu.VMEM((2, 2, S5_GB // 2, r, 128), F32)],
        compiler_params=_cparams(("parallel",)),
        name="s5_core",
    )(proj, ws, wt, wh, a_l, h0)


def _rope_tables(t):
    rows = t // GRID_W
    row = np.repeat(np.arange(rows), GRID_W).astype(np.float32)
    col = np.tile(np.arange(GRID_W), rows).astype(np.float32)
    nf = DA_HEAD // 4
    inv = (ROPE_BASE ** (-jnp.arange(nf, dtype=F32) / nf))

    def tab(pos):
        ang = jnp.asarray(pos)[:, None] * inv[None, :]
        c, s = jnp.cos(ang), jnp.sin(ang)
        return jnp.concatenate([c, c], -1), jnp.concatenate([-s, s], -1)

    cr, sr = tab(row)
    cc, sc = tab(col)
    cos = jnp.concatenate([cr, cc], -1)
    sin = jnp.concatenate([sr, sc], -1)
    return jnp.tile(cos, (1, 2)), jnp.tile(sin, (1, 2))


def _rope(x, cos, sin):
    lane = lax.broadcasted_iota(jnp.int32, x.shape, 1)
    swapped = jnp.where((lane % 32) < 16, pltpu.roll(x, 112, 1), pltpu.roll(x, 16, 1))
    return x * cos + swapped * sin


def _attn_kernel(*refs, lam_init, t, s_tot, tq, with_ctx):
    if with_ctx:
        (q_ref, k_ref, v_ref, z_ref, kc_ref, vc_ref, cq_ref, sq_ref, ck_ref, sk_ref,
         lam_ref, ng_ref, o_ref, kall_ref, vall_ref, s_ref, e_ref, l_ref) = refs
    else:
        q_ref, k_ref, v_ref, z_ref, lam_ref, ng_ref, o_ref, kall_ref, vall_ref, s_ref, e_ref, l_ref = refs

    @pl.when(pl.program_id(1) == 0)
    def _():
        if with_ctx:
            for h in range(DA_HEADS):
                hs = slice(h * 128, (h + 1) * 128)
                kall_ref[h, :, 0:t] = _rope(k_ref[0, :, hs], ck_ref[...], sk_ref[...]).T.astype(BF16)
                kall_ref[h, :, t:s_tot] = kc_ref[0, 0, h].astype(BF16)
            vall_ref[0:t, :] = v_ref[0].astype(BF16)
            vall_ref[t:s_tot, :] = vc_ref[0].astype(BF16)
        else:
            for h in range(DA_HEADS):
                kall_ref[h] = k_ref[0, :, h * 128:(h + 1) * 128].T.astype(BF16)
            vall_ref[...] = v_ref[0].astype(BF16)

    lp = lam_ref[...]
    lam = (jnp.exp(jnp.sum(lp[0:1] * lp[1:2], axis=-1, keepdims=True))
           - jnp.exp(jnp.sum(lp[2:3] * lp[3:4], axis=-1, keepdims=True)) + lam_init)
    lane = lax.broadcasted_iota(jnp.int32, (tq, 128), 1)
    def scores(h):
        hs = slice(h * 128, (h + 1) * 128)
        q = q_ref[0, :, hs]
        if with_ctx:
            q = _rope(q, cq_ref[...], sq_ref[...])
        q = q * (DA_HEAD ** -0.5 * LOG2E)
        qs = jnp.concatenate([jnp.where(lane < DA_HEAD, q, 0.0), jnp.where(lane >= DA_HEAD, q, 0.0)], axis=0)
        s_ref[h % 2] = _mm(qs, kall_ref[h])

    scores(0)
    for h in range(DA_HEADS):
        hs = slice(h * 128, (h + 1) * 128)
        b = h % 2
        if h + 1 < DA_HEADS:
            scores(h + 1)
        for r in range(0, 2 * tq, ATTN_ROWS):
            sc = s_ref[b, r:r + ATTN_ROWS, :]
            e = jnp.exp2(sc - jnp.max(sc, axis=-1, keepdims=True))
            e_ref[b, r:r + ATTN_ROWS, :] = e.astype(BF16)
            l_ref[b, r:r + ATTN_ROWS, :] = jnp.broadcast_to(jnp.sum(e, axis=-1, keepdims=True),
                                                            (ATTN_ROWS, 128))
        ov = jnp.dot(e_ref[b], vall_ref[:, hs], preferred_element_type=F32) / l_ref[b]
        o = ov[:tq] - lam * ov[tq:]
        o = o * lax.rsqrt(jnp.mean(o * o, axis=-1, keepdims=True) + EPS) * ng_ref[...]
        o = o * (1.0 - lam_init)
        o_ref[0, :, hs] = (o * _silu(z_ref[0, :, hs])).astype(BF16)


def diff_attention(proj3, layer, lam_init, da_lam, da_norm_g, ctx_kv):
    nb, t, _ = proj3.shape
    with_ctx = ctx_kv is not None
    s_tot = t + (PAST_LEN if with_ctx else 0)
    tq = 256
    wb = DA_WIDTH
    in_specs = [pl.BlockSpec((1, tq, wb), lambda b, i: (b, i, COL_QB // wb)),
                pl.BlockSpec((1, t, wb), lambda b, i: (b, 0, COL_KB // wb)),
                pl.BlockSpec((1, t, wb), lambda b, i: (b, 0, COL_VB // wb)),
                pl.BlockSpec((1, tq, wb), lambda b, i: (b, i, COL_ZB // wb))]
    args = [proj3, proj3, proj3, proj3]
    if with_ctx:
        kc, vc = ctx_kv
        cos, sin = _rope_tables(t)
        in_specs += [pl.BlockSpec((1, 1, DA_HEADS, 2 * DA_HEAD, PAST_LEN), lambda b, i: (b, layer, 0, 0, 0)),
                     pl.BlockSpec((1, PAST_LEN, wb), lambda b, i: (b, layer, 0)),
                     pl.BlockSpec((tq, 128), lambda b, i: (i, 0)),
                     pl.BlockSpec((tq, 128), lambda b, i: (i, 0)),
                     pl.BlockSpec((t, 128), lambda b, i: (0, 0)),
                     pl.BlockSpec((t, 128), lambda b, i: (0, 0))]
        args += [kc, vc, cos, sin, cos, sin]
    in_specs += [pl.BlockSpec((4, DA_HEAD), lambda b, i: (0, 0)),
                 pl.BlockSpec((1, DA_VDIM), lambda b, i: (0, 0))]
    args += [da_lam, da_norm_g.reshape(1, DA_VDIM)]
    return pl.pallas_call(
        functools.partial(_attn_kernel, lam_init=lam_init, t=t, s_tot=s_tot, tq=tq, with_ctx=with_ctx),
        grid=(nb, t // tq),
        in_specs=in_specs,
        out_specs=pl.BlockSpec((1, tq, wb), lambda b, i: (b, i, 0)),
        out_shape=jax.ShapeDtypeStruct((nb, t, wb), BF16),
        scratch_shapes=[pltpu.VMEM((DA_HEADS, 2 * DA_HEAD, s_tot), BF16), pltpu.VMEM((s_tot, wb), BF16),
                        pltpu.VMEM((2, 2 * tq, s_tot), F32), pltpu.VMEM((2, 2 * tq, s_tot), BF16),
                        pltpu.VMEM((2, 2 * tq, 128), F32)],
        compiler_params=_cparams(("parallel", "arbitrary")),
        name="diff_attention",
    )(*args)


DN_PAD = 8
DN_RT = 128
DN_GROUP = 128
DN_STEP_GROUPS = 2


def _dn_kernel(*refs, t, ngroup, with_s0):
    if with_s0:
        (qkv_ref, z_ref, ba_ref, cw_ref, alog_ref, dtb_ref, ng_ref, s0_ref,
         out_ref, sfin_ref, xp_ref, qkvn_ref, oacc_ref, st_ref) = refs
    else:
        (qkv_ref, z_ref, ba_ref, cw_ref, alog_ref, dtb_ref, ng_ref,
         out_ref, sfin_ref, xp_ref, qkvn_ref, oacc_ref, st_ref) = refs
    n = pl.program_id(1)
    cd = DN_CHUNK
    w3 = 3 * DN_WIDTH

    @pl.when(n == 0)
    def _init():
        xp_ref[0:DN_PAD, :] = jnp.zeros((DN_PAD, w3), F32)
        xp_ref[DN_PAD + t:2 * DN_PAD + t, :] = jnp.zeros((DN_PAD, w3), F32)
        xp_ref[DN_PAD:DN_PAD + t, :] = qkv_ref[0]
        half = DN_CONV // 2
        for r in range(t // DN_RT):
            for sec in range(3):
                for h in range(DN_HEADS):
                    cs = slice(sec * DN_WIDTH + h * DN_HEAD, sec * DN_WIDTH + (h + 1) * DN_HEAD)
                    acc = jnp.zeros((DN_RT, DN_HEAD), F32)
                    for j in range(DN_CONV):
                        r0 = DN_PAD + r * DN_RT + j - half
                        acc = acc + xp_ref[r0:r0 + DN_RT, cs] * cw_ref[j:j + 1, cs]
                    y = _silu(acc)
                    if sec < 2:
                        y = y * lax.rsqrt(jnp.sum(y * y, axis=-1, keepdims=True) + EPS)
                    if sec == 0:
                        y = y * (DN_HEAD ** -0.5)
                    qkvn_ref[r * DN_RT:(r + 1) * DN_RT, cs] = y
        oacc_ref[...] = jnp.zeros_like(oacc_ref)
        if with_s0:
            st_ref[...] = s0_ref[0, 0]
        else:
            st_ref[...] = jnp.zeros_like(st_ref)

    gb = DN_GROUP
    nsub = gb // cd
    ri = lax.broadcasted_iota(jnp.int32, (gb, gb), 0)
    ci = lax.broadcasted_iota(jnp.int32, (gb, gb), 1)
    same = (ri // cd) == (ci // cd)
    samef = same.astype(F32)
    eye = (ri == ci).astype(BF16)
    masked_out = -1e30

    chains = []
    for d, gi in [(d, gi) for d in range(2) for gi in range(DN_STEP_GROUPS)]:
        grp = n * DN_STEP_GROUPS + gi
        r0 = pl.multiple_of((grp if d == 0 else ngroup - 1 - grp) * gb, gb)
        incl = same & ((ci <= ri) if d == 0 else (ci >= ri))
        strict = same & ((ci < ri) if d == 0 else (ci > ri))
        inclf = incl.astype(F32)
        inclog = jnp.where(incl, 0.0, masked_out)
        nstrict = -strict.astype(BF16)
        ba = ba_ref[0, pl.ds(r0, gb), :]
        beta_all = jax.nn.sigmoid(ba)
        g_all = -jnp.exp(alog_ref[...]) * jax.nn.softplus(ba + dtb_ref[...])
        gc = jnp.dot(inclf, g_all, precision=HI, preferred_element_type=F32)
        gct = lax.dot_general(g_all.T, inclf, (((1,), (1,)), ((), ())), precision=HI,
                              preferred_element_type=F32)
        gtot = jnp.dot(samef, g_all, precision=HI, preferred_element_type=F32)
        for h in range(DN_HEADS):
            hs = slice(h * DN_HEAD, (h + 1) * DN_HEAD)
            col = 2 * DN_HEADS + d * DN_HEADS + h
            gcol = gc[:, col:col + 1]
            grow = gct[col:col + 1, :]
            gt = gtot[:, col:col + 1]
            beta = beta_all[:, d * DN_HEADS + h:d * DN_HEADS + h + 1]
            q = qkvn_ref[pl.ds(r0, gb), hs]
            k = qkvn_ref[pl.ds(r0, gb), DN_WIDTH + h * DN_HEAD:DN_WIDTH + (h + 1) * DN_HEAD]
            v = qkvn_ref[pl.ds(r0, gb), 2 * DN_WIDTH + h * DN_HEAD:2 * DN_WIDTH + (h + 1) * DN_HEAD]
            eg = jnp.exp(gcol)
            chains.append(dict(
                d=d, gi=gi, h=h, r0=r0, hs=hs, nstrict=nstrict, q=q, k=k, kb=k * beta,
                decay=jnp.exp((gcol - grow) + inclog),
                rhs=jnp.concatenate([v * beta, k * beta * eg], axis=-1),
                qe=q * eg, kdec=k * jnp.exp(gt - gcol), egt=jnp.exp(gt)))

    for c in chains:
        c["nmm"] = (_mm_nt(c["kb"], c["k"]) * c["decay"]).astype(BF16) * c["nstrict"]
    for c in chains:
        c["qk"] = _mm_nt(c["q"], c["k"]) * c["decay"]
    def level_mask(s):
        return (((ri // (2 * s)) == (ci // (2 * s))) & ((ri // s) != (ci // s))).astype(BF16)

    pair = level_mask(1)
    for c in chains:
        c["tinv"] = eye + c["nmm"] * pair
    s = 2
    while s < cd:
        offmask = level_mask(s)
        xs = [jnp.dot(c["nmm"] * offmask, c["tinv"], preferred_element_type=F32).astype(BF16) for c in chains]
        ys = [jnp.dot(c["tinv"], x, preferred_element_type=F32) for c, x in zip(chains, xs)]
        for c, y in zip(chains, ys):
            c["tinv"] = c["tinv"] + y.astype(BF16)
        s *= 2
    for c in chains:
        c["uw"] = _mm(c["tinv"], c["rhs"])

    st = {(d, h): st_ref[d, h] for d in range(2) for h in range(DN_HEADS)}
    outs = []
    for gi, step in [(gi, step) for gi in range(DN_STEP_GROUPS) for step in range(nsub)]:
        active = [c for c in chains if c["gi"] == gi]
        rs = []
        for c in active:
            sub = step if c["d"] == 0 else nsub - 1 - step
            c["rows"] = slice(sub * cd, (sub + 1) * cd)
            rs.append(_mm(jnp.concatenate([c["uw"][c["rows"], DN_HEAD:], c["qe"][c["rows"]]], axis=0),
                          st[c["d"], c["h"]]))
        for c, r in zip(active, rs):
            rows = c["rows"]
            v_new = c["uw"][rows, :DN_HEAD] - r[:cd]
            o = r[cd:] + _mm(c["qk"][rows, rows], v_new)
            st[c["d"], c["h"]] = (st[c["d"], c["h"]] * c["egt"][rows.start:rows.start + 1]
                                  + _mm(c["kdec"][rows].T, v_new))
            outs.append((c, rows.start, o))
    for (d, h), v in st.items():
        st_ref[d, h] = v
    for c, off, o in outs:
        oacc_ref[pl.ds(pl.multiple_of(c["r0"] + off, cd), cd), c["hs"]] += o

    @pl.when(n == ngroup // DN_STEP_GROUPS - 1)
    def _fin():
        for h in range(DN_HEADS):
            hs = slice(h * DN_HEAD, (h + 1) * DN_HEAD)
            o = oacc_ref[:, hs]
            o = o * lax.rsqrt(jnp.mean(o * o, axis=-1, keepdims=True) + EPS) * ng_ref[...]
            out_ref[0, :, hs] = (o * _silu(z_ref[0, :, hs])).astype(BF16)
        sfin_ref[0] = st_ref[...]


def deltanet(proj3, ba3, layer, conv_w, a_log, dt_bias, norm_g, s0):
    nb, t, _ = proj3.shape
    ngroup = t // DN_GROUP
    with_s0 = s0 is not None
    w3 = 3 * DN_WIDTH
    pad = jnp.zeros((2 * DN_HEADS,), F32)
    alog_row = jnp.concatenate([pad, a_log.reshape(-1), jnp.zeros((128 - 4 * DN_HEADS,), F32)]).reshape(1, 128)
    dtb_row = jnp.concatenate([pad, dt_bias.reshape(-1), jnp.zeros((128 - 4 * DN_HEADS,), F32)]).reshape(1, 128)
    in_specs = [pl.BlockSpec((1, t, w3), lambda b, n: (b, 0, COL_QC // w3)),
                pl.BlockSpec((1, t, DN_WIDTH), lambda b, n: (b, 0, COL_ZC // DN_WIDTH)),
                pl.BlockSpec((1, t, 128), lambda b, n: (b, 0, 0)),
                pl.BlockSpec((8, w3), lambda b, n: (0, 0)),
                pl.BlockSpec((1, 128), lambda b, n: (0, 0)),
                pl.BlockSpec((1, 128), lambda b, n: (0, 0)),
                pl.BlockSpec((1, DN_HEAD), lambda b, n: (0, 0))]
    args = [proj3, proj3, ba3, jnp.pad(conv_w, ((0, 8 - DN_CONV), (0, 0))), alog_row, dtb_row,
            norm_g.reshape(1, DN_HEAD)]
    if with_s0:
        in_specs.append(pl.BlockSpec((1, 1, 2, DN_HEADS, DN_HEAD, DN_HEAD), lambda b, n: (b, layer, 0, 0, 0, 0)))
        args.append(s0)
    return pl.pallas_call(
        functools.partial(_dn_kernel, t=t, ngroup=ngroup, with_s0=with_s0),
        grid=(nb, ngroup // DN_STEP_GROUPS),
        in_specs=in_specs,
        out_specs=[pl.BlockSpec((1, t, DN_WIDTH), lambda b, n: (b, 0, 0)),
                   pl.BlockSpec((1, 2, DN_HEADS, DN_HEAD, DN_HEAD), lambda b, n: (b, 0, 0, 0, 0))],
        out_shape=[jax.ShapeDtypeStruct((nb, t, DN_WIDTH), BF16),
                   jax.ShapeDtypeStruct((nb, 2, DN_HEADS, DN_HEAD, DN_HEAD), F32)],
        scratch_shapes=[pltpu.VMEM((t + 2 * DN_PAD, w3), F32),
                        pltpu.VMEM((t, w3), F32),
                        pltpu.VMEM((t, DN_WIDTH), F32),
                        pltpu.VMEM((2, DN_HEADS, DN_HEAD, DN_HEAD), F32)],
        compiler_params=_cparams(("parallel", "arbitrary")),
        name="deltanet",
    )(*args)


def _merge_kernel(u_ref, ys_ref, za_ref, sd_ref, wglu_ref, ob_ref, oc_ref, hn_ref, wg_ref, wb_ref, wo_ref,
                  x_ref, gate_ref, fg_ref, *outs, final):
    tm = x_ref.shape[0]
    for rows in (slice(0, tm // 2), slice(tm // 2, tm)):
        ya = jax.nn.gelu(sd_ref[...] * u_ref[rows, :] + ys_ref[rows, :])
        ya = ya * jax.nn.sigmoid(_mm(ya, wglu_ref[...]))
        branches = ((ya * _silu(za_ref[rows, :])).astype(BF16), ob_ref[rows, :], oc_ref[rows, :])
        acc = None
        hn = hn_ref[rows, :]
        for i, o in enumerate(branches):
            pr = jnp.dot(o, wb_ref[i], preferred_element_type=F32)
            gt = jnp.dot(hn, wg_ref[:, i * D_MODEL:(i + 1) * D_MODEL], preferred_element_type=F32)
            term = jax.nn.sigmoid(gt) * pr
            acc = term if acc is None else acc + term
        y = jnp.dot(acc.astype(BF16), wo_ref[...], preferred_element_type=F32)
        xn = x_ref[rows, :] + gate_ref[0] * y
        outs[0][rows, :] = xn
        if final:
            yn = xn * lax.rsqrt(jnp.mean(xn * xn, axis=-1, keepdims=True) + EPS) * fg_ref[...]
            outs[1][rows, :] = yn


def merge(proj, y_s5, s5_d, w_glu, out_b, out_c, hn, w_gates, w_branch, w_out, x2, gate, final_g, rows_per_mod,
          final):
    m = x2.shape[0]
    tm = 512
    nmod = gate.shape[0]
    row = lambda i: (i, 0)
    out_specs = [pl.BlockSpec((tm, D_MODEL), row)]
    out_shape = [jax.ShapeDtypeStruct((m, D_MODEL), F32)]
    if final:
        out_specs.append(pl.BlockSpec((tm, D_MODEL), row))
        out_shape.append(jax.ShapeDtypeStruct((m, D_MODEL), F32))
    return pl.pallas_call(
        functools.partial(_merge_kernel, final=final),
        grid=(m // tm,),
        in_specs=[pl.BlockSpec((tm, S5_WIDTH), lambda i: (i, COL_UA // S5_WIDTH)),
                  pl.BlockSpec((tm, S5_WIDTH), row),
                  pl.BlockSpec((tm, S5_WIDTH), lambda i: (i, COL_ZA // S5_WIDTH)),
                  pl.BlockSpec((1, S5_WIDTH), lambda i: (0, 0)),
                  pl.BlockSpec((S5_WIDTH, S5_WIDTH), lambda i: (0, 0)),
                  pl.BlockSpec((tm, BRANCH_WIDTH), row),
                  pl.BlockSpec((tm, BRANCH_WIDTH), row),
                  pl.BlockSpec((tm, D_MODEL), row),
                  pl.BlockSpec((D_MODEL, N_BRANCH * D_MODEL), lambda i: (0, 0)),
                  pl.BlockSpec((N_BRANCH, BRANCH_WIDTH, D_MODEL), lambda i: (0, 0, 0)),
                  pl.BlockSpec((D_MODEL, D_MODEL), lambda i: (0, 0)),
                  pl.BlockSpec((tm, D_MODEL), row),
                  pl.BlockSpec((1, 1, D_MODEL), lambda i: ((i * tm) // rows_per_mod, 0, 0)),
                  pl.BlockSpec((1, D_MODEL), lambda i: (0, 0))],
        out_specs=out_specs,
        out_shape=out_shape,
        compiler_params=_cparams(("parallel",)),
        name="merge",
    )(proj, y_s5, proj, s5_d.reshape(1, S5_WIDTH), w_glu, out_b, out_c, hn, w_gates, w_branch, w_out, x2,
      gate.reshape(nmod, 1, D_MODEL),
      final_g.reshape(1, D_MODEL))


def _run_pass(x, mod, wts, lam_inits, final_g, ctx):
    nb, t, _ = x.shape
    m = nb * t
    nmod = mod.shape[1]
    rows_per_mod = m if nmod == 1 else t
    x2 = x.reshape(m, D_MODEL)
    states = []
    y, caches = None, None
    for l in range(DEPTH):
        w = wts[l]
        shift, scale, gate = jnp.split(mod[l], 3, axis=-1)
        if ctx is None:
            proj, ba, hn, *caches = inproj(x2, w["norm_g"], scale, shift, w["w1"], w["w2"], rows_per_mod,
                                       kv=(l, t, caches))
            h0 = jnp.zeros((2, 2, nb, S5_GROUPS * S5_STATE), F32)
            ctx_kv, s0 = None, None
        else:
            proj, ba, hn = inproj(x2, w["norm_g"], scale, shift, w["w1"], w["w2"], rows_per_mod)
            cache_k, cache_v, st_re, st_im, st_dn = ctx
            h0 = jnp.stack([st_re[:, l], st_im[:, l]], 0)
            h0 = jnp.transpose(h0, (2, 0, 1, 3, 4)).reshape(2, 2, nb, S5_GROUPS * S5_STATE)
            ctx_kv, s0 = (cache_k, cache_v), st_dn
        proj3 = proj.reshape(nb, t, N_MAIN)
        y_s5, hfin = s5_core(proj, w["s5_mats"], h0, nb, t // S5_CHUNK)
        out_b = diff_attention(proj3, l, lam_inits[l], w["da_lam"], w["da_norm_g"], ctx_kv)
        out_c, sfin = deltanet(proj3, ba.reshape(nb, t, 128), l, w["dn_conv"], w["dn_a_log"],
                               w["dn_dt_bias"], w["dn_norm_g"], s0)
        final = l == DEPTH - 1
        res = merge(proj, y_s5, w["s5_d"], w["w_glu"], out_b.reshape(m, DA_WIDTH), out_c.reshape(m, DN_WIDTH),
                    hn, w["w_gates"], w["w_branch"], w["w_out"], x2, gate, final_g, rows_per_mod, final)
        x2 = res[0]
        if final:
            y = res[1]
        if ctx is None:
            hf = hfin.reshape(2, 2, nb, S5_GROUPS, S5_STATE)
            states.append((jnp.transpose(hf[:, 0], (1, 0, 2, 3)), jnp.transpose(hf[:, 1], (1, 0, 2, 3)), sfin))
    return y.reshape(nb, t, D_MODEL), states, caches


def kernel(x_prompt, x_sample, cache_k, cache_v, state_s5_re, state_s5_im, state_dn, c, c_ctx,
           norm_g, w_ada, b_ada, w_in, s5_lam_re, s5_lam_im, s5_log_step, s5_b_re, s5_b_im,
           s5_c_re, s5_c_im, s5_d, s5_w_glu, da_lam, da_norm_g, dn_conv, dn_a_log, dn_dt_bias,
           dn_norm_g, w_branch, w_out, final_norm_g):
    nb_dec = x_sample.shape[0]
    cond8 = jnp.concatenate([c_ctx[None, :], c, jnp.zeros((8 - 1 - nb_dec, D_MODEL), F32)], 0)
    mod = ada_mod(cond8, w_ada, b_ada)
    wts = []
    for l in range(DEPTH):
        w1, w_gates, w2 = cast_w_in(w_in, l)
        wts.append(dict(
            norm_g=norm_g[l], w1=w1, w2=w2, w_gates=w_gates,
            s5_mats=s5_matrices(s5_lam_re[l], s5_lam_im[l], s5_log_step[l], s5_b_re[l], s5_b_im[l],
                                s5_c_re[l], s5_c_im[l]),
            s5_d=s5_d[l], w_glu=s5_w_glu[l].astype(BF16), da_lam=da_lam[l], da_norm_g=da_norm_g[l],
            dn_conv=dn_conv[l], dn_a_log=dn_a_log[l], dn_dt_bias=dn_dt_bias[l], dn_norm_g=dn_norm_g[l],
            w_branch=w_branch[l].astype(BF16), w_out=w_out[l].astype(BF16)))
    lam_inits = [0.8 - 0.6 * math.exp(-0.3 * l) for l in range(DEPTH)]

    y_prompt, states, (k_new, v_new) = _run_pass(x_prompt, mod[:, 0:1], wts, lam_inits, final_norm_g, None)
    cache_kt = jnp.transpose(cache_k, (0, 1, 3, 4, 5, 2)).reshape(nb_dec, DEPTH, DA_HEADS, 2 * DA_HEAD, PAST_LEN)
    ctx = (cache_kt, cache_v.reshape(nb_dec, DEPTH * PAST_LEN, DA_WIDTH), state_s5_re, state_s5_im, state_dn)
    y_sample, _, _ = _run_pass(x_sample, mod[:, 1:1 + nb_dec], wts, lam_inits, final_norm_g, ctx)

    nb, t = x_prompt.shape[:2]
    new_cache_k = k_new.reshape(nb, DEPTH, t, DA_HEADS, 2, DA_HEAD)
    new_cache_v = v_new.reshape(nb, DEPTH, t, DA_HEADS, DA_VDIM)
    new_s5_re = jnp.stack([s[0] for s in states], axis=1)
    new_s5_im = jnp.stack([s[1] for s in states], axis=1)
    new_dn = jnp.stack([s[2] for s in states], axis=1)
    return (y_prompt, y_sample, new_cache_k, new_cache_v, new_s5_re, new_s5_im, new_dn)
```

```python
import functools
import math

import numpy as np
import jax
import jax.numpy as jnp
from jax import lax
from jax.experimental import pallas as pl
from jax.experimental.pallas import tpu as pltpu

F32 = jnp.float32
BF16 = jnp.bfloat16

D_MODEL = 1024
DEPTH = 2
GRID_W = 64
EPS = 1e-6
S5_WIDTH = 512
S5_GROUP = 16
S5_GROUPS = 32
S5_STATE = 64
S5_CHUNK = 16
S5_PAIRS = S5_GROUPS // 2
S5_ROW = S5_CHUNK * S5_GROUP
S5_GEN_GROUPS = 4
DA_HEADS = 4
DA_HEAD = 64
DA_VDIM = 128
DA_WIDTH = 512
ROPE_BASE = 10000.0
DN_HEADS = 4
DN_HEAD = 128
DN_WIDTH = 512
DN_CONV = 5
DN_CHUNK = 64
N_BRANCH = 3
BRANCH_WIDTH = 512
PAST_LEN = 512

COL_UA, COL_ZA, COL_QB, COL_KB, COL_VB, COL_ZB = 0, 512, 1024, 1536, 2048, 2560
COL_QC, COL_ZC = 3072, 4608
N_MAIN = 5120
BA_OFF = 5120
GATES_OFF = 5136

VMEM_LIMIT = 56 * 1024 * 1024
HI = lax.Precision.HIGHEST
LOG2E = math.log2(math.e)
ATTN_ROWS = 16


def _cparams(sem):
    return pltpu.CompilerParams(dimension_semantics=sem, vmem_limit_bytes=VMEM_LIMIT)


def _mm(a, b):
    return jnp.dot(a.astype(BF16), b.astype(BF16), preferred_element_type=F32)


def _mm_nt(a, b):
    return lax.dot_general(a.astype(BF16), b.astype(BF16), (((1,), (1,)), ((), ())),
                           preferred_element_type=F32)


def _silu(x):
    return x * jax.nn.sigmoid(x)


def _ada_kernel(c_ref, w_ref, b_ref, o_ref):
    o_ref[0] = _mm(_silu(c_ref[...]), w_ref[0]) + b_ref[0]


def ada_mod(cond8, w_ada, b_ada):
    tn = 1024
    return pl.pallas_call(
        _ada_kernel,
        grid=(DEPTH, 3 * D_MODEL // tn),
        in_specs=[pl.BlockSpec((8, D_MODEL), lambda l, j: (0, 0)),
                  pl.BlockSpec((1, D_MODEL, tn), lambda l, j: (l, 0, j)),
                  pl.BlockSpec((1, 1, tn), lambda l, j: (l, 0, j))],
        out_specs=pl.BlockSpec((1, 8, tn), lambda l, j: (l, 0, j)),
        out_shape=jax.ShapeDtypeStruct((DEPTH, 8, 3 * D_MODEL), F32),
        compiler_params=_cparams(("parallel", "parallel")),
        name="ada_mod",
    )(cond8, w_ada, b_ada.reshape(DEPTH, 1, 3 * D_MODEL))


def _tcast_kernel(w_ref, o_ref, *, keep):
    x = w_ref[0].T
    if keep is not None:
        x = jnp.where(lax.broadcasted_iota(jnp.int32, x.shape, 1) < keep, x, 0.0)
    o_ref[...] = x.astype(BF16)


def _tcast_rows_kernel(w_hbm, o_ref, buf, sem, *, layer, row0):
    tn = buf.shape[0]
    start = pl.multiple_of(row0 + pl.program_id(0) * tn, 8)
    cp = pltpu.make_async_copy(w_hbm.at[layer, pl.ds(start, tn), :], buf, sem)
    cp.start()
    cp.wait()
    o_ref[...] = buf[...].T.astype(BF16)


def cast_w_in(w_in, layer):
    wt = jnp.swapaxes(w_in, 1, 2)
    tn = 1024
    w1 = pl.pallas_call(
        functools.partial(_tcast_kernel, keep=None),
        grid=(N_MAIN // tn,),
        in_specs=[pl.BlockSpec((1, tn, D_MODEL), lambda j: (layer, j, 0))],
        out_specs=pl.BlockSpec((D_MODEL, tn), lambda j: (0, j)),
        out_shape=jax.ShapeDtypeStruct((D_MODEL, N_MAIN), BF16),
        compiler_params=_cparams(("parallel",)),
        name="cast_w1",
    )(wt)
    gw = N_BRANCH * D_MODEL
    w_gates = pl.pallas_call(
        functools.partial(_tcast_rows_kernel, layer=layer, row0=GATES_OFF),
        grid=(gw // tn,),
        in_specs=[pl.BlockSpec(memory_space=pl.ANY)],
        out_specs=pl.BlockSpec((D_MODEL, tn), lambda j: (0, j)),
        out_shape=jax.ShapeDtypeStruct((D_MODEL, gw), BF16),
        scratch_shapes=[pltpu.VMEM((tn, D_MODEL), F32), pltpu.SemaphoreType.DMA(())],
        compiler_params=_cparams(("parallel",)),
        name="cast_w_gates",
    )(wt)
    w_ba = pl.pallas_call(
        functools.partial(_tcast_kernel, keep=GATES_OFF - BA_OFF),
        grid=(1,),
        in_specs=[pl.BlockSpec((1, 128, D_MODEL), lambda j: (layer, BA_OFF // 128, 0))],
        out_specs=pl.BlockSpec((D_MODEL, 128), lambda j: (0, 0)),
        out_shape=jax.ShapeDtypeStruct((D_MODEL, 128), BF16),
        compiler_params=_cparams(("parallel",)),
        name="cast_w_ba",
    )(wt)
    return w1, w_gates, w_ba


def _inproj_kernel(*refs, n_prev, with_kv):
    x_ref, g_ref, sc_ref, sh_ref, w1_ref, w2_ref = refs[:6]
    prev = refs[6:6 + n_prev]
    outs = refs[6 + n_prev:]
    proj_ref, ba_ref, hn_ref = outs[:3]
    j = pl.program_id(1)

    @pl.when(j == 0)
    def _():
        x = x_ref[...]
        y = x * lax.rsqrt(jnp.mean(x * x, axis=-1, keepdims=True) + EPS) * g_ref[...]
        hn = (y * (1.0 + sc_ref[0]) + sh_ref[0]).astype(BF16)
        hn_ref[...] = hn
        ba_ref[...] = jnp.dot(hn, w2_ref[...], preferred_element_type=F32)

    proj_ref[...] = jnp.dot(hn_ref[...], w1_ref[...], preferred_element_type=F32)

    if with_kv:
        tn = proj_ref.shape[1]
        for c, (ref, col) in enumerate(((outs[3], COL_KB), (outs[4], COL_VB))):
            @pl.when(j == col // tn)
            def _(c=c, ref=ref, col=col):
                nb, nl, t, w = ref.shape
                if n_prev:
                    ref[:, 0:nl - 1] = prev[c][...]
                ref[:, nl - 1:nl] = proj_ref[:, col % tn:col % tn + w].reshape(nb, 1, t, w)


def inproj(x2, norm_g, scale, shift, w1, w2, rows_per_mod, kv=None):
    m = x2.shape[0]
    tm, tn = 512, N_MAIN
    once = pl.Buffered(1)
    nmod = scale.shape[0]
    mod_idx = lambda i, j: ((i * tm) // rows_per_mod, 0, 0)
    in_specs = [pl.BlockSpec((tm, D_MODEL), lambda i, j: (i, 0)),
                pl.BlockSpec((1, D_MODEL), lambda i, j: (0, 0)),
                pl.BlockSpec((1, 1, D_MODEL), mod_idx),
                pl.BlockSpec((1, 1, D_MODEL), mod_idx),
                pl.BlockSpec((D_MODEL, tn), lambda i, j: (0, j), pipeline_mode=once),
                pl.BlockSpec((D_MODEL, 128), lambda i, j: (0, 0))]
    args = [x2, norm_g.reshape(1, D_MODEL), scale.reshape(nmod, 1, D_MODEL),
            shift.reshape(nmod, 1, D_MODEL), w1, w2]
    out_specs = [pl.BlockSpec((tm, tn), lambda i, j: (i, j)),
                 pl.BlockSpec((tm, 128), lambda i, j: (i, 0)),
                 pl.BlockSpec((tm, D_MODEL), lambda i, j: (i, 0))]
    out_shape = [jax.ShapeDtypeStruct((m, N_MAIN), F32),
                 jax.ShapeDtypeStruct((m, 128), F32),
                 jax.ShapeDtypeStruct((m, D_MODEL), BF16)]
    n_prev = 0
    if kv is not None:
        layer, t, caches = kv
        cspec = lambda nl: pl.BlockSpec((tm // t, nl, t, DA_WIDTH), lambda i, j: (i, 0, 0, 0))
        out_specs += [cspec(layer + 1)] * 2
        out_shape += [jax.ShapeDtypeStruct((m // t, layer + 1, t, DA_WIDTH), F32)] * 2
        if caches is not None:
            n_prev = 2
            in_specs += [pl.BlockSpec((tm // t, layer, t, DA_WIDTH), lambda i, j: (i, 0, 0, 0),
                                      pipeline_mode=once)] * 2
            args += list(caches)
    return pl.pallas_call(
        functools.partial(_inproj_kernel, n_prev=n_prev, with_kv=kv is not None),
        grid=(m // tm, N_MAIN // tn),
        in_specs=in_specs,
        out_specs=out_specs,
        out_shape=out_shape,
        compiler_params=_cparams(("parallel", "arbitrary")),
        name="inproj",
    )(*args)


def _s5_gen_kernel(crt_ref, cit_ref, prt_ref, pit_ref, bbt_ref, bbs_ref, prow_ref, pirow_ref,
                   wt_ref, ws_ref, wh_ref):
    L, C, P = S5_CHUNK, S5_GROUP, S5_STATE
    width = (L + 1) * C
    row = lax.broadcasted_iota(jnp.int32, (128, width), 0)
    lane = lax.broadcasted_iota(jnp.int32, (128, width), 1)
    tile_c = (lane % C == row).astype(F32)
    expand = lambda a, e: jnp.dot(a, e, precision=HI, preferred_element_type=F32)
    zeros = jnp.zeros((C, L * C), F32)
    for g in range(S5_GEN_GROUPS):
        strips = []
        for d in range(2):
            spread_k = ((lane // C if d == 0 else L - lane // C) == row).astype(F32)
            crx, cix = expand(crt_ref[d, g], tile_c), expand(cit_ref[d, g], tile_c)
            prx, pix = expand(prt_ref[d, g], spread_k), expand(pit_ref[d, g], spread_k)
            ca = jnp.concatenate([crx * prx - cix * pix, -(crx * pix + cix * prx)], axis=0)
            wh_ref[d, g] = (ca[:, C:] if d == 0 else ca[:, :L * C]).astype(BF16)
            bbt, bbs = bbt_ref[d, g], bbs_ref[d, g]
            strips.append(jnp.dot(bbt, ca[:, :L * C] if d == 0 else ca[:, C:], precision=HI,
                                  preferred_element_type=F32))
            rows = []
            for i in range(L):
                k = L - 1 - i if d == 0 else i
                rows.append(bbt * prow_ref[d, g, k:k + 1, :] + bbs * pirow_ref[d, g, k:k + 1, :])
            ws_ref[d, g] = jnp.concatenate(rows, axis=0).astype(BF16)
        fpad = jnp.concatenate([zeros, strips[0]], axis=-1)
        rpad = jnp.concatenate([strips[1], zeros], axis=-1)
        rows = []
        for i in range(L):
            rows.append(fpad[:, (L - i) * C:(2 * L - i) * C] + rpad[:, (L - 1 - i) * C:(2 * L - 1 - i) * C])
        wt_ref[g] = jnp.concatenate(rows, axis=0).astype(BF16)


def s5_matrices(lam_re, lam_im, log_step, b_re, b_im, c_re, c_im):
    L, G, P, C = S5_CHUNK, S5_GROUPS, S5_STATE, S5_GROUP
    step = jnp.exp(log_step)[..., None]
    mag = jnp.exp(lam_re * step)
    ar, ai = mag * jnp.cos(lam_im * step), mag * jnp.sin(lam_im * step)
    den = lam_re * lam_re + lam_im * lam_im
    fr = ((ar - 1.0) * lam_re + ai * lam_im) / den
    fi = (ai * lam_re - (ar - 1.0) * lam_im) / den
    bbr = fr[..., None] * b_re - fi[..., None] * b_im
    bbi = fr[..., None] * b_im + fi[..., None] * b_re
    ks = jnp.arange(L + 1, dtype=F32)[None, None, :, None]
    pmag = jnp.exp(ks * (lam_re * step)[:, :, None, :])
    prow = pmag * jnp.cos(ks * (lam_im * step)[:, :, None, :])
    pirow = pmag * jnp.sin(ks * (lam_im * step)[:, :, None, :])
    prow, pirow = lax.optimization_barrier((prow, pirow))
    prt, pit = jnp.swapaxes(prow, 2, 3), jnp.swapaxes(pirow, 2, 3)
    bbrt, bbit = jnp.swapaxes(bbr, 2, 3), jnp.swapaxes(bbi, 2, 3)
    lanes = lambda a: jnp.pad(a, ((0, 0), (0, 0), (0, 0), (0, 128 - a.shape[-1])))
    args = [lanes(jnp.swapaxes(c_re, 2, 3)), lanes(jnp.swapaxes(c_im, 2, 3)), lanes(prt), lanes(pit),
            jnp.concatenate([bbrt, bbit], -1), jnp.concatenate([bbit, bbrt], -1),
            jnp.concatenate([prow, prow], -1), jnp.concatenate([-pirow, pirow], -1)]
    gg = S5_GEN_GROUPS
    spec = lambda a: pl.BlockSpec((2, gg) + a.shape[2:], lambda g: (0, g, 0, 0))
    wt, ws, wh = pl.pallas_call(
        _s5_gen_kernel,
        grid=(G // gg,),
        in_specs=[spec(a) for a in args],
        out_specs=[pl.BlockSpec((gg, S5_ROW, S5_ROW), lambda g: (g, 0, 0)),
                   pl.BlockSpec((2, gg, S5_ROW, 2 * P), lambda g: (0, g, 0, 0)),
                   pl.BlockSpec((2, gg, 2 * P, S5_ROW), lambda g: (0, g, 0, 0))],
        out_shape=[jax.ShapeDtypeStruct((G, S5_ROW, S5_ROW), BF16),
                   jax.ShapeDtypeStruct((2, G, S5_ROW, 2 * P), BF16),
                   jax.ShapeDtypeStruct((2, G, 2 * P, S5_ROW), BF16)],
        compiler_params=_cparams(("parallel",)),
        name="s5_gen",
    )(*args)
    a_l = jnp.stack([prt[..., L].reshape(2, 1, G * P), pit[..., L].reshape(2, 1, G * P)], 1)
    return wt, ws, wh, a_l


S5_GB = 8


def _s5_core_kernel(u_ref, ws_ref, wt_ref, wh_ref, a_ref, h0_ref, y_ref, hfin_ref, x_ref, s_ref, hin_ref,
                    *, nb, nchunk):
    r = nb * nchunk
    npair = S5_GB // 2
    us = [u_ref[pl.ds(i, r, stride=S5_CHUNK), :] for i in range(S5_CHUNK)]
    for g in range(S5_GB):
        xg = jnp.concatenate([u[:, S5_GROUP * g:S5_GROUP * (g + 1)] for u in us], axis=-1)
        x_ref[:, g * S5_ROW:(g + 1) * S5_ROW] = xg.astype(BF16)
    for p in range(npair):
        for d in range(2):
            sg = [jnp.dot(x_ref[:, (2 * p + k) * S5_ROW:(2 * p + k + 1) * S5_ROW], ws_ref[d, 2 * p + k],
                          preferred_element_type=F32) for k in range(2)]
            for comp in range(2):
                cs = slice(comp * S5_STATE, (comp + 1) * S5_STATE)
                s_ref[d, comp, p] = jnp.concatenate([sg[0][:, cs], sg[1][:, cs]], axis=-1)
    chains = [(d, p) for d in range(2) for p in range(npair)]
    h = {}
    for d, p in chains:
        cols = slice(p * 128, (p + 1) * 128)
        h[d, p] = (h0_ref[d, 0, :, cols], h0_ref[d, 1, :, cols], a_ref[d, 0, :, cols], a_ref[d, 1, :, cols])
    for step in range(nchunk):
        for d, p in chains:
            rows = pl.ds(step if d == 0 else nchunk - 1 - step, nb, stride=nchunk)
            hr, hi, ar, ai = h[d, p]
            hin_ref[d, 0, p, rows, :] = hr
            hin_ref[d, 1, p, rows, :] = hi
            sr, si = s_ref[d, 0, p, rows, :], s_ref[d, 1, p, rows, :]
            h[d, p] = (ar * hr - ai * hi + sr, ar * hi + ai * hr + si, ar, ai)
    for d, p in chains:
        cols = slice(p * 128, (p + 1) * 128)
        hfin_ref[d, 0, :, cols] = h[d, p][0]
        hfin_ref[d, 1, :, cols] = h[d, p][1]
    ys = []
    for g in range(S5_GB):
        y = jnp.dot(x_ref[:, g * S5_ROW:(g + 1) * S5_ROW], wt_ref[g], preferred_element_type=F32)
        cs = slice((g % 2) * S5_STATE, (g % 2 + 1) * S5_STATE)
        for d in range(2):
            hg = jnp.concatenate([hin_ref[d, 0, g // 2, :, cs], hin_ref[d, 1, g // 2, :, cs]], axis=-1)
            y = y + jnp.dot(hg.astype(BF16), wh_ref[d, g], preferred_element_type=F32)
        ys.append(y)
    for j in range(S5_CHUNK):
        y_ref[pl.ds(j, r, stride=S5_CHUNK), :] = jnp.concatenate(
            [y[:, S5_GROUP * j:S5_GROUP * (j + 1)] for y in ys], axis=-1)


def s5_core(proj, mats, h0, nb, nchunk):
    wt, ws, wh, a_l = mats
    r = nb * nchunk
    sw = S5_GB * S5_STATE
    return pl.pallas_call(
        functools.partial(_s5_core_kernel, nb=nb, nchunk=nchunk),
        grid=(S5_GROUPS // S5_GB,),
        in_specs=[pl.BlockSpec((r * S5_CHUNK, 128), lambda j: (0, COL_UA // 128 + j)),
                  pl.BlockSpec((2, S5_GB, S5_ROW, 2 * S5_STATE), lambda j: (0, j, 0, 0)),
                  pl.BlockSpec((S5_GB, S5_ROW, S5_ROW), lambda j: (j, 0, 0)),
                  pl.BlockSpec((2, S5_GB, 2 * S5_STATE, S5_ROW), lambda j: (0, j, 0, 0)),
                  pl.BlockSpec((2, 2, 1, sw), lambda j: (0, 0, 0, j)),
                  pl.BlockSpec((2, 2, nb, sw), lambda j: (0, 0, 0, j))],
        out_specs=[pl.BlockSpec((r * S5_CHUNK, 128), lambda j: (0, j)),
                   pl.BlockSpec((2, 2, nb, sw), lambda j: (0, 0, 0, j))],
        out_shape=[jax.ShapeDtypeStruct((r * S5_CHUNK, S5_WIDTH), F32),
                   jax.ShapeDtypeStruct((2, 2, nb, S5_GROUPS * S5_STATE), F32)],
        scratch_shapes=[pltpu.VMEM((r, S5_GB * S5_ROW), BF16),
                        pltpu.VMEM((2, 2, S5_GB // 2, r, 128), F32),
                        pltpu.VMEM((2, 2, S5_GB // 2, r, 128), F32)],
        compiler_params=_cparams(("parallel",)),
        name="s5_core",
    )(proj, ws, wt, wh, a_l, h0)


def _rope_tables(t):
    rows = t // GRID_W
    row = np.repeat(np.arange(rows), GRID_W).astype(np.float32)
    col = np.tile(np.arange(GRID_W), rows).astype(np.float32)
    nf = DA_HEAD // 4
    inv = (ROPE_BASE ** (-jnp.arange(nf, dtype=F32) / nf))

    def tab(pos):
        ang = jnp.asarray(pos)[:, None] * inv[None, :]
        c, s = jnp.cos(ang), jnp.sin(ang)
        return jnp.concatenate([c, c], -1), jnp.concatenate([-s, s], -1)

    cr, sr = tab(row)
    cc, sc = tab(col)
    cos = jnp.concatenate([cr, cc], -1)
    sin = jnp.concatenate([sr, sc], -1)
    return jnp.tile(cos, (1, 2)), jnp.tile(sin, (1, 2))


def _rope(x, cos, sin):
    lane = lax.broadcasted_iota(jnp.int32, x.shape, 1)
    swapped = jnp.where((lane % 32) < 16, pltpu.roll(x, 112, 1), pltpu.roll(x, 16, 1))
    return x * cos + swapped * sin


def _attn_kernel(*refs, lam_init, t, s_tot, tq, with_ctx):
    if with_ctx:
        (q_ref, k_ref, v_ref, z_ref, kc_ref, vc_ref, cq_ref, sq_ref, ck_ref, sk_ref,
         lam_ref, ng_ref, o_ref, kall_ref, vall_ref, s_ref, e_ref, l_ref) = refs
    else:
        q_ref, k_ref, v_ref, z_ref, lam_ref, ng_ref, o_ref, kall_ref, vall_ref, s_ref, e_ref, l_ref = refs

    @pl.when(pl.program_id(1) == 0)
    def _():
        if with_ctx:
            for h in range(DA_HEADS):
                hs = slice(h * 128, (h + 1) * 128)
                kall_ref[h, :, 0:t] = _rope(k_ref[0, :, hs], ck_ref[...], sk_ref[...]).T.astype(BF16)
                kall_ref[h, :, t:s_tot] = kc_ref[0, 0, h].astype(BF16)
            vall_ref[0:t, :] = v_ref[0].astype(BF16)
            vall_ref[t:s_tot, :] = vc_ref[0].astype(BF16)
        else:
            for h in range(DA_HEADS):
                kall_ref[h] = k_ref[0, :, h * 128:(h + 1) * 128].T.astype(BF16)
            vall_ref[...] = v_ref[0].astype(BF16)

    lp = lam_ref[...]
    lam = (jnp.exp(jnp.sum(lp[0:1] * lp[1:2], axis=-1, keepdims=True))
           - jnp.exp(jnp.sum(lp[2:3] * lp[3:4], axis=-1, keepdims=True)) + lam_init)
    lane = lax.broadcasted_iota(jnp.int32, (tq, 128), 1)
    def scores(h):
        hs = slice(h * 128, (h + 1) * 128)
        q = q_ref[0, :, hs]
        if with_ctx:
            q = _rope(q, cq_ref[...], sq_ref[...])
        q = q * (DA_HEAD ** -0.5 * LOG2E)
        qs = jnp.concatenate([jnp.where(lane < DA_HEAD, q, 0.0), jnp.where(lane >= DA_HEAD, q, 0.0)], axis=0)
        s_ref[h % 2] = _mm(qs, kall_ref[h])

    scores(0)
    for h in range(DA_HEADS):
        hs = slice(h * 128, (h + 1) * 128)
        b = h % 2
        if h + 1 < DA_HEADS:
            scores(h + 1)
        for r in range(0, 2 * tq, ATTN_ROWS):
            sc = s_ref[b, r:r + ATTN_ROWS, :]
            e = jnp.exp2(sc - jnp.max(sc, axis=-1, keepdims=True))
            e_ref[b, r:r + ATTN_ROWS, :] = e.astype(BF16)
            l_ref[b, r:r + ATTN_ROWS, :] = jnp.broadcast_to(jnp.sum(e, axis=-1, keepdims=True),
                                                            (ATTN_ROWS, 128))
        ov = jnp.dot(e_ref[b], vall_ref[:, hs], preferred_element_type=F32) / l_ref[b]
        o = ov[:tq] - lam * ov[tq:]
        o = o * lax.rsqrt(jnp.mean(o * o, axis=-1, keepdims=True) + EPS) * ng_ref[...]
        o = o * (1.0 - lam_init)
        o_ref[0, :, hs] = (o * _silu(z_ref[0, :, hs])).astype(BF16)


def diff_attention(proj3, layer, lam_init, da_lam, da_norm_g, ctx_kv):
    nb, t, _ = proj3.shape
    with_ctx = ctx_kv is not None
    s_tot = t + (PAST_LEN if with_ctx else 0)
    tq = 256
    wb = DA_WIDTH
    in_specs = [pl.BlockSpec((1, tq, wb), lambda b, i: (b, i, COL_QB // wb)),
                pl.BlockSpec((1, t, wb), lambda b, i: (b, 0, COL_KB // wb)),
                pl.BlockSpec((1, t, wb), lambda b, i: (b, 0, COL_VB // wb)),
                pl.BlockSpec((1, tq, wb), lambda b, i: (b, i, COL_ZB // wb))]
    args = [proj3, proj3, proj3, proj3]
    if with_ctx:
        kc, vc = ctx_kv
        cos, sin = _rope_tables(t)
        in_specs += [pl.BlockSpec((1, 1, DA_HEADS, 2 * DA_HEAD, PAST_LEN), lambda b, i: (b, layer, 0, 0, 0)),
                     pl.BlockSpec((1, PAST_LEN, wb), lambda b, i: (b, layer, 0)),
                     pl.BlockSpec((tq, 128), lambda b, i: (i, 0)),
                     pl.BlockSpec((tq, 128), lambda b, i: (i, 0)),
                     pl.BlockSpec((t, 128), lambda b, i: (0, 0)),
                     pl.BlockSpec((t, 128), lambda b, i: (0, 0))]
        args += [kc, vc, cos, sin, cos, sin]
    in_specs += [pl.BlockSpec((4, DA_HEAD), lambda b, i: (0, 0)),
                 pl.BlockSpec((1, DA_VDIM), lambda b, i: (0, 0))]
    args += [da_lam, da_norm_g.reshape(1, DA_VDIM)]
    return pl.pallas_call(
        functools.partial(_attn_kernel, lam_init=lam_init, t=t, s_tot=s_tot, tq=tq, with_ctx=with_ctx),
        grid=(nb, t // tq),
        in_specs=in_specs,
        out_specs=pl.BlockSpec((1, tq, wb), lambda b, i: (b, i, 0)),
        out_shape=jax.ShapeDtypeStruct((nb, t, wb), BF16),
        scratch_shapes=[pltpu.VMEM((DA_HEADS, 2 * DA_HEAD, s_tot), BF16), pltpu.VMEM((s_tot, wb), BF16),
                        pltpu.VMEM((2, 2 * tq, s_tot), F32), pltpu.VMEM((2, 2 * tq, s_tot), BF16),
                        pltpu.VMEM((2, 2 * tq, 128), F32)],
        compiler_params=_cparams(("parallel", "arbitrary")),
        name="diff_attention",
    )(*args)


DN_PAD = 8
DN_RT = 128
DN_GROUP = 128
DN_STEP_GROUPS = 2


def _dn_kernel(*refs, t, ngroup, with_s0):
    if with_s0:
        (qkv_ref, z_ref, ba_ref, cw_ref, alog_ref, dtb_ref, ng_ref, s0_ref,
         out_ref, sfin_ref, xp_ref, qkvn_ref, oacc_ref, st_ref) = refs
    else:
        (qkv_ref, z_ref, ba_ref, cw_ref, alog_ref, dtb_ref, ng_ref,
         out_ref, sfin_ref, xp_ref, qkvn_ref, oacc_ref, st_ref) = refs
    n = pl.program_id(1)
    cd = DN_CHUNK
    w3 = 3 * DN_WIDTH

    @pl.when(n == 0)
    def _init():
        xp_ref[0:DN_PAD, :] = jnp.zeros((DN_PAD, w3), F32)
        xp_ref[DN_PAD + t:2 * DN_PAD + t, :] = jnp.zeros((DN_PAD, w3), F32)
        xp_ref[DN_PAD:DN_PAD + t, :] = qkv_ref[0]
        half = DN_CONV // 2
        for r in range(t // DN_RT):
            for sec in range(3):
                for h in range(DN_HEADS):
                    cs = slice(sec * DN_WIDTH + h * DN_HEAD, sec * DN_WIDTH + (h + 1) * DN_HEAD)
                    acc = jnp.zeros((DN_RT, DN_HEAD), F32)
                    for j in range(DN_CONV):
                        r0 = DN_PAD + r * DN_RT + j - half
                        acc = acc + xp_ref[r0:r0 + DN_RT, cs] * cw_ref[j:j + 1, cs]
                    y = _silu(acc)
                    if sec < 2:
                        y = y * lax.rsqrt(jnp.sum(y * y, axis=-1, keepdims=True) + EPS)
                    if sec == 0:
                        y = y * (DN_HEAD ** -0.5)
                    qkvn_ref[r * DN_RT:(r + 1) * DN_RT, cs] = y
        oacc_ref[...] = jnp.zeros_like(oacc_ref)
        if with_s0:
            st_ref[...] = s0_ref[0, 0]
        else:
            st_ref[...] = jnp.zeros_like(st_ref)

    gb = DN_GROUP
    nsub = gb // cd
    ri = lax.broadcasted_iota(jnp.int32, (gb, gb), 0)
    ci = lax.broadcasted_iota(jnp.int32, (gb, gb), 1)
    same = (ri // cd) == (ci // cd)
    samef = same.astype(F32)
    eye = (ri == ci).astype(BF16)
    masked_out = -1e30

    chains = []
    for d, gi in [(d, gi) for d in range(2) for gi in range(DN_STEP_GROUPS)]:
        grp = n * DN_STEP_GROUPS + gi
        r0 = pl.multiple_of((grp if d == 0 else ngroup - 1 - grp) * gb, gb)
        incl = same & ((ci <= ri) if d == 0 else (ci >= ri))
        strict = same & ((ci < ri) if d == 0 else (ci > ri))
        inclf = incl.astype(F32)
        inclog = jnp.where(incl, 0.0, masked_out)
        nstrict = -strict.astype(BF16)
        ba = ba_ref[0, pl.ds(r0, gb), :]
        beta_all = jax.nn.sigmoid(ba)
        g_all = -jnp.exp(alog_ref[...]) * jax.nn.softplus(ba + dtb_ref[...])
        gc = jnp.dot(inclf, g_all, precision=HI, preferred_element_type=F32)
        gct = lax.dot_general(g_all.T, inclf, (((1,), (1,)), ((), ())), precision=HI,
                              preferred_element_type=F32)
        gtot = jnp.dot(samef, g_all, precision=HI, preferred_element_type=F32)
        for h in range(DN_HEADS):
            hs = slice(h * DN_HEAD, (h + 1) * DN_HEAD)
            col = 2 * DN_HEADS + d * DN_HEADS + h
            gcol = gc[:, col:col + 1]
            grow = gct[col:col + 1, :]
            gt = gtot[:, col:col + 1]
            beta = beta_all[:, d * DN_HEADS + h:d * DN_HEADS + h + 1]
            q = qkvn_ref[pl.ds(r0, gb), hs]
            k = qkvn_ref[pl.ds(r0, gb), DN_WIDTH + h * DN_HEAD:DN_WIDTH + (h + 1) * DN_HEAD]
            v = qkvn_ref[pl.ds(r0, gb), 2 * DN_WIDTH + h * DN_HEAD:2 * DN_WIDTH + (h + 1) * DN_HEAD]
            eg = jnp.exp(gcol)
            chains.append(dict(
                d=d, gi=gi, h=h, r0=r0, hs=hs, nstrict=nstrict, q=q, k=k, kb=k * beta,
                decay=jnp.exp((gcol - grow) + inclog),
                rhs=jnp.concatenate([v * beta, k * beta * eg], axis=-1),
                qe=q * eg, kdec=k * jnp.exp(gt - gcol), egt=jnp.exp(gt)))

    for c in chains:
        c["nmm"] = (_mm_nt(c["kb"], c["k"]) * c["decay"]).astype(BF16) * c["nstrict"]
    for c in chains:
        c["qk"] = _mm_nt(c["q"], c["k"]) * c["decay"]
    def level_mask(s):
        return (((ri // (2 * s)) == (ci // (2 * s))) & ((ri // s) != (ci // s))).astype(BF16)

    pair = level_mask(1)
    for c in chains:
        c["tinv"] = eye + c["nmm"] * pair
    s = 2
    while s < cd:
        offmask = level_mask(s)
        xs = [jnp.dot(c["nmm"] * offmask, c["tinv"], preferred_element_type=F32).astype(BF16) for c in chains]
        ys = [jnp.dot(c["tinv"], x, preferred_element_type=F32) for c, x in zip(chains, xs)]
        for c, y in zip(chains, ys):
            c["tinv"] = c["tinv"] + y.astype(BF16)
        s *= 2
    for c in chains:
        c["uw"] = _mm(c["tinv"], c["rhs"])

    st = {(d, h): st_ref[d, h] for d in range(2) for h in range(DN_HEADS)}
    outs = []
    for gi, step in [(gi, step) for gi in range(DN_STEP_GROUPS) for step in range(nsub)]:
        active = [c for c in chains if c["gi"] == gi]
        rs = []
        for c in active:
            sub = step if c["d"] == 0 else nsub - 1 - step
            c["rows"] = slice(sub * cd, (sub + 1) * cd)
            rs.append(_mm(jnp.concatenate([c["uw"][c["rows"], DN_HEAD:], c["qe"][c["rows"]]], axis=0),
                          st[c["d"], c["h"]]))
        for c, r in zip(active, rs):
            rows = c["rows"]
            v_new = c["uw"][rows, :DN_HEAD] - r[:cd]
            o = r[cd:] + _mm(c["qk"][rows, rows], v_new)
            st[c["d"], c["h"]] = (st[c["d"], c["h"]] * c["egt"][rows.start:rows.start + 1]
                                  + _mm(c["kdec"][rows].T, v_new))
            outs.append((c, rows.start, o))
    for (d, h), v in st.items():
        st_ref[d, h] = v
    for c, off, o in outs:
        oacc_ref[pl.ds(pl.multiple_of(c["r0"] + off, cd), cd), c["hs"]] += o

    @pl.when(n == ngroup // DN_STEP_GROUPS - 1)
    def _fin():
        for h in range(DN_HEADS):
            hs = slice(h * DN_HEAD, (h + 1) * DN_HEAD)
            o = oacc_ref[:, hs]
            o = o * lax.rsqrt(jnp.mean(o * o, axis=-1, keepdims=True) + EPS) * ng_ref[...]
            out_ref[0, :, hs] = (o * _silu(z_ref[0, :, hs])).astype(BF16)
        sfin_ref[0] = st_ref[...]


def deltanet(proj3, ba3, layer, conv_w, a_log, dt_bias, norm_g, s0):
    nb, t, _ = proj3.shape
    ngroup = t // DN_GROUP
    with_s0 = s0 is not None
    w3 = 3 * DN_WIDTH
    pad = jnp.zeros((2 * DN_HEADS,), F32)
    alog_row = jnp.concatenate([pad, a_log.reshape(-1), jnp.zeros((128 - 4 * DN_HEADS,), F32)]).reshape(1, 128)
    dtb_row = jnp.concatenate([pad, dt_bias.reshape(-1), jnp.zeros((128 - 4 * DN_HEADS,), F32)]).reshape(1, 128)
    in_specs = [pl.BlockSpec((1, t, w3), lambda b, n: (b, 0, COL_QC // w3)),
                pl.BlockSpec((1, t, DN_WIDTH), lambda b, n: (b, 0, COL_ZC // DN_WIDTH)),
                pl.BlockSpec((1, t, 128), lambda b, n: (b, 0, 0)),
                pl.BlockSpec((8, w3), lambda b, n: (0, 0)),
                pl.BlockSpec((1, 128), lambda b, n: (0, 0)),
                pl.BlockSpec((1, 128), lambda b, n: (0, 0)),
                pl.BlockSpec((1, DN_HEAD), lambda b, n: (0, 0))]
    args = [proj3, proj3, ba3, jnp.pad(conv_w, ((0, 8 - DN_CONV), (0, 0))), alog_row, dtb_row,
            norm_g.reshape(1, DN_HEAD)]
    if with_s0:
        in_specs.append(pl.BlockSpec((1, 1, 2, DN_HEADS, DN_HEAD, DN_HEAD), lambda b, n: (b, layer, 0, 0, 0, 0)))
        args.append(s0)
    return pl.pallas_call(
        functools.partial(_dn_kernel, t=t, ngroup=ngroup, with_s0=with_s0),
        grid=(nb, ngroup // DN_STEP_GROUPS),
        in_specs=in_specs,
        out_specs=[pl.BlockSpec((1, t, DN_WIDTH), lambda b, n: (b, 0, 0)),
                   pl.BlockSpec((1, 2, DN_HEADS, DN_HEAD, DN_HEAD), lambda b, n: (b, 0, 0, 0, 0))],
        out_shape=[jax.ShapeDtypeStruct((nb, t, DN_WIDTH), BF16),
                   jax.ShapeDtypeStruct((nb, 2, DN_HEADS, DN_HEAD, DN_HEAD), F32)],
        scratch_shapes=[pltpu.VMEM((t + 2 * DN_PAD, w3), F32),
                        pltpu.VMEM((t, w3), F32),
                        pltpu.VMEM((t, DN_WIDTH), F32),
                        pltpu.VMEM((2, DN_HEADS, DN_HEAD, DN_HEAD), F32)],
        compiler_params=_cparams(("parallel", "arbitrary")),
        name="deltanet",
    )(*args)


def _merge_kernel(u_ref, ys_ref, za_ref, sd_ref, wglu_ref, ob_ref, oc_ref, hn_ref, wg_ref, wb_ref, wo_ref,
                  x_ref, gate_ref, fg_ref, *outs, final):
    tm = x_ref.shape[0]
    for rows in (slice(0, tm // 2), slice(tm // 2, tm)):
        ya = jax.nn.gelu(sd_ref[...] * u_ref[rows, :] + ys_ref[rows, :])
        ya = ya * jax.nn.sigmoid(_mm(ya, wglu_ref[...]))
        branches = ((ya * _silu(za_ref[rows, :])).astype(BF16), ob_ref[rows, :], oc_ref[rows, :])
        acc = None
        hn = hn_ref[rows, :]
        for i, o in enumerate(branches):
            pr = jnp.dot(o, wb_ref[i], preferred_element_type=F32)
            gt = jnp.dot(hn, wg_ref[:, i * D_MODEL:(i + 1) * D_MODEL], preferred_element_type=F32)
            term = jax.nn.sigmoid(gt) * pr
            acc = term if acc is None else acc + term
        y = jnp.dot(acc.astype(BF16), wo_ref[...], preferred_element_type=F32)
        xn = x_ref[rows, :] + gate_ref[0] * y
        outs[0][rows, :] = xn
        if final:
            yn = xn * lax.rsqrt(jnp.mean(xn * xn, axis=-1, keepdims=True) + EPS) * fg_ref[...]
            outs[1][rows, :] = yn


def merge(proj, y_s5, s5_d, w_glu, out_b, out_c, hn, w_gates, w_branch, w_out, x2, gate, final_g, rows_per_mod,
          final):
    m = x2.shape[0]
    tm = 512
    nmod = gate.shape[0]
    row = lambda i: (i, 0)
    out_specs = [pl.BlockSpec((tm, D_MODEL), row)]
    out_shape = [jax.ShapeDtypeStruct((m, D_MODEL), F32)]
    if final:
        out_specs.append(pl.BlockSpec((tm, D_MODEL), row))
        out_shape.append(jax.ShapeDtypeStruct((m, D_MODEL), F32))
    return pl.pallas_call(
        functools.partial(_merge_kernel, final=final),
        grid=(m // tm,),
        in_specs=[pl.BlockSpec((tm, S5_WIDTH), lambda i: (i, COL_UA // S5_WIDTH)),
                  pl.BlockSpec((tm, S5_WIDTH), row),
                  pl.BlockSpec((tm, S5_WIDTH), lambda i: (i, COL_ZA // S5_WIDTH)),
                  pl.BlockSpec((1, S5_WIDTH), lambda i: (0, 0)),
                  pl.BlockSpec((S5_WIDTH, S5_WIDTH), lambda i: (0, 0)),
                  pl.BlockSpec((tm, BRANCH_WIDTH), row),
                  pl.BlockSpec((tm, BRANCH_WIDTH), row),
                  pl.BlockSpec((tm, D_MODEL), row),
                  pl.BlockSpec((D_MODEL, N_BRANCH * D_MODEL), lambda i: (0, 0)),
                  pl.BlockSpec((N_BRANCH, BRANCH_WIDTH, D_MODEL), lambda i: (0, 0, 0)),
                  pl.BlockSpec((D_MODEL, D_MODEL), lambda i: (0, 0)),
                  pl.BlockSpec((tm, D_MODEL), row),
                  pl.BlockSpec((1, 1, D_MODEL), lambda i: ((i * tm) // rows_per_mod, 0, 0)),
                  pl.BlockSpec((1, D_MODEL), lambda i: (0, 0))],
        out_specs=out_specs,
        out_shape=out_shape,
        compiler_params=_cparams(("parallel",)),
        name="merge",
    )(proj, y_s5, proj, s5_d.reshape(1, S5_WIDTH), w_glu, out_b, out_c, hn, w_gates, w_branch, w_out, x2,
      gate.reshape(nmod, 1, D_MODEL),
      final_g.reshape(1, D_MODEL))


def _run_pass(x, mod, wts, lam_inits, final_g, ctx):
    nb, t, _ = x.shape
    m = nb * t
    nmod = mod.shape[1]
    rows_per_mod = m if nmod == 1 else t
    x2 = x.reshape(m, D_MODEL)
    states = []
    y, caches = None, None
    for l in range(DEPTH):
        w = wts[l]
        shift, scale, gate = jnp.split(mod[l], 3, axis=-1)
        if ctx is None:
            proj, ba, hn, *caches = inproj(x2, w["norm_g"], scale, shift, w["w1"], w["w2"], rows_per_mod,
                                       kv=(l, t, caches))
            h0 = jnp.zeros((2, 2, nb, S5_GROUPS * S5_STATE), F32)
            ctx_kv, s0 = None, None
        else:
            proj, ba, hn = inproj(x2, w["norm_g"], scale, shift, w["w1"], w["w2"], rows_per_mod)
            cache_k, cache_v, st_re, st_im, st_dn = ctx
            h0 = jnp.stack([st_re[:, l], st_im[:, l]], 0)
            h0 = jnp.transpose(h0, (2, 0, 1, 3, 4)).reshape(2, 2, nb, S5_GROUPS * S5_STATE)
            ctx_kv, s0 = (cache_k, cache_v), st_dn
        proj3 = proj.reshape(nb, t, N_MAIN)
        y_s5, hfin = s5_core(proj, w["s5_mats"], h0, nb, t // S5_CHUNK)
        out_b = diff_attention(proj3, l, lam_inits[l], w["da_lam"], w["da_norm_g"], ctx_kv)
        out_c, sfin = deltanet(proj3, ba.reshape(nb, t, 128), l, w["dn_conv"], w["dn_a_log"],
                               w["dn_dt_bias"], w["dn_norm_g"], s0)
        final = l == DEPTH - 1
        res = merge(proj, y_s5, w["s5_d"], w["w_glu"], out_b.reshape(m, DA_WIDTH), out_c.reshape(m, DN_WIDTH),
                    hn, w["w_gates"], w["w_branch"], w["w_out"], x2, gate, final_g, rows_per_mod, final)
        x2 = res[0]
        if final:
            y = res[1]
        if ctx is None:
            hf = hfin.reshape(2, 2, nb, S5_GROUPS, S5_STATE)
            states.append((jnp.transpose(hf[:, 0], (1, 0, 2, 3)), jnp.transpose(hf[:, 1], (1, 0, 2, 3)), sfin))
    return y.reshape(nb, t, D_MODEL), states, caches


def kernel(x_prompt, x_sample, cache_k, cache_v, state_s5_re, state_s5_im, state_dn, c, c_ctx,
           norm_g, w_ada, b_ada, w_in, s5_lam_re, s5_lam_im, s5_log_step, s5_b_re, s5_b_im,
           s5_c_re, s5_c_im, s5_d, s5_w_glu, da_lam, da_norm_g, dn_conv, dn_a_log, dn_dt_bias,
           dn_norm_g, w_branch, w_out, final_norm_g):
    nb_dec = x_sample.shape[0]
    cond8 = jnp.concatenate([c_ctx[None, :], c, jnp.zeros((8 - 1 - nb_dec, D_MODEL), F32)], 0)
    mod = ada_mod(cond8, w_ada, b_ada)
    wts = []
    for l in range(DEPTH):
        w1, w_gates, w2 = cast_w_in(w_in, l)
        wts.append(dict(
            norm_g=norm_g[l], w1=w1, w2=w2, w_gates=w_gates,
            s5_mats=s5_matrices(s5_lam_re[l], s5_lam_im[l], s5_log_step[l], s5_b_re[l], s5_b_im[l],
                                s5_c_re[l], s5_c_im[l]),
            s5_d=s5_d[l], w_glu=s5_w_glu[l].astype(BF16), da_lam=da_lam[l], da_norm_g=da_norm_g[l],
            dn_conv=dn_conv[l], dn_a_log=dn_a_log[l], dn_dt_bias=dn_dt_bias[l], dn_norm_g=dn_norm_g[l],
            w_branch=w_branch[l].astype(BF16), w_out=w_out[l].astype(BF16)))
    lam_inits = [0.8 - 0.6 * math.exp(-0.3 * l) for l in range(DEPTH)]

    y_prompt, states, (k_new, v_new) = _run_pass(x_prompt, mod[:, 0:1], wts, lam_inits, final_norm_g, None)
    cache_kt = jnp.transpose(cache_k, (0, 1, 3, 4, 5, 2)).reshape(nb_dec, DEPTH, DA_HEADS, 2 * DA_HEAD, PAST_LEN)
    ctx = (cache_kt, cache_v.reshape(nb_dec, DEPTH * PAST_LEN, DA_WIDTH), state_s5_re, state_s5_im, state_dn)
    y_sample, _, _ = _run_pass(x_sample, mod[:, 1:1 + nb_dec], wts, lam_inits, final_norm_g, ctx)

    nb, t = x_prompt.shape[:2]
    new_cache_k = k_new.reshape(nb, DEPTH, t, DA_HEADS, 2, DA_HEAD)
    new_cache_v = v_new.reshape(nb, DEPTH, t, DA_HEADS, DA_VDIM)
    new_s5_re = jnp.stack([s[0] for s in states], axis=1)
    new_s5_im = jnp.stack([s[1] for s in states], axis=1)
    new_dn = jnp.stack([s[2] for s in states], axis=1)
    return (y_prompt, y_sample, new_cache_k, new_cache_v, new_s5_re, new_s5_im, new_dn)
```

```python
import functools
import math

import numpy as np
import jax
import jax.numpy as jnp
from jax import lax
from jax.experimental import pallas as pl
from jax.experimental.pallas import tpu as pltpu

F32 = jnp.float32
BF16 = jnp.bfloat16

D_MODEL = 1024
DEPTH = 2
GRID_W = 64
EPS = 1e-6
S5_WIDTH = 512
S5_GROUP = 16
S5_GROUPS = 32
S5_STATE = 64
S5_CHUNK = 16
S5_PAIRS = S5_GROUPS // 2
S5_ROW = S5_CHUNK * S5_GROUP
S5_GEN_GROUPS = 4
DA_HEADS = 4
DA_HEAD = 64
DA_VDIM = 128
DA_WIDTH = 512
ROPE_BASE = 10000.0
DN_HEADS = 4
DN_HEAD = 128
DN_WIDTH = 512
DN_CONV = 5
DN_CHUNK = 64
N_BRANCH = 3
BRANCH_WIDTH = 512
PAST_LEN = 512

COL_UA, COL_ZA, COL_QB, COL_KB, COL_VB, COL_ZB = 0, 512, 1024, 1536, 2048, 2560
COL_QC, COL_ZC = 3072, 4608
N_MAIN = 5120
BA_OFF = 5120
GATES_OFF = 5136

VMEM_LIMIT = 56 * 1024 * 1024
HI = lax.Precision.HIGHEST
LOG2E = math.log2(math.e)
ATTN_ROWS = 16


def _cparams(sem):
    return pltpu.CompilerParams(dimension_semantics=sem, vmem_limit_bytes=VMEM_LIMIT)


def _mm(a, b):
    return jnp.dot(a.astype(BF16), b.astype(BF16), preferred_element_type=F32)


def _mm_nt(a, b):
    return lax.dot_general(a.astype(BF16), b.astype(BF16), (((1,), (1,)), ((), ())),
                           preferred_element_type=F32)


def _silu(x):
    return x * jax.nn.sigmoid(x)


def _ada_kernel(c_ref, w_ref, b_ref, o_ref):
    o_ref[0] = _mm(_silu(c_ref[...]), w_ref[0]) + b_ref[0]


def ada_mod(cond8, w_ada, b_ada):
    tn = 1024
    return pl.pallas_call(
        _ada_kernel,
        grid=(DEPTH, 3 * D_MODEL // tn),
        in_specs=[pl.BlockSpec((8, D_MODEL), lambda l, j: (0, 0)),
                  pl.BlockSpec((1, D_MODEL, tn), lambda l, j: (l, 0, j)),
                  pl.BlockSpec((1, 1, tn), lambda l, j: (l, 0, j))],
        out_specs=pl.BlockSpec((1, 8, tn), lambda l, j: (l, 0, j)),
        out_shape=jax.ShapeDtypeStruct((DEPTH, 8, 3 * D_MODEL), F32),
        compiler_params=_cparams(("parallel", "parallel")),
        name="ada_mod",
    )(cond8, w_ada, b_ada.reshape(DEPTH, 1, 3 * D_MODEL))


def _tcast_kernel(w_ref, o_ref, *, keep):
    x = w_ref[0].T
    if keep is not None:
        x = jnp.where(lax.broadcasted_iota(jnp.int32, x.shape, 1) < keep, x, 0.0)
    o_ref[...] = x.astype(BF16)


def _tcast_rows_kernel(w_hbm, o_ref, buf, sem, *, layer, row0):
    tn = buf.shape[0]
    start = pl.multiple_of(row0 + pl.program_id(0) * tn, 8)
    cp = pltpu.make_async_copy(w_hbm.at[layer, pl.ds(start, tn), :], buf, sem)
    cp.start()
    cp.wait()
    o_ref[...] = buf[...].T.astype(BF16)


def cast_w_in(w_in, layer):
    wt = jnp.swapaxes(w_in, 1, 2)
    tn = 1024
    w1 = pl.pallas_call(
        functools.partial(_tcast_kernel, keep=None),
        grid=(N_MAIN // tn,),
        in_specs=[pl.BlockSpec((1, tn, D_MODEL), lambda j: (layer, j, 0))],
        out_specs=pl.BlockSpec((D_MODEL, tn), lambda j: (0, j)),
        out_shape=jax.ShapeDtypeStruct((D_MODEL, N_MAIN), BF16),
        compiler_params=_cparams(("parallel",)),
        name="cast_w1",
    )(wt)
    gw = N_BRANCH * D_MODEL
    w_gates = pl.pallas_call(
        functools.partial(_tcast_rows_kernel, layer=layer, row0=GATES_OFF),
        grid=(gw // tn,),
        in_specs=[pl.BlockSpec(memory_space=pl.ANY)],
        out_specs=pl.BlockSpec((D_MODEL, tn), lambda j: (0, j)),
        out_shape=jax.ShapeDtypeStruct((D_MODEL, gw), BF16),
        scratch_shapes=[pltpu.VMEM((tn, D_MODEL), F32), pltpu.SemaphoreType.DMA(())],
        compiler_params=_cparams(("parallel",)),
        name="cast_w_gates",
    )(wt)
    w_ba = pl.pallas_call(
        functools.partial(_tcast_kernel, keep=GATES_OFF - BA_OFF),
        grid=(1,),
        in_specs=[pl.BlockSpec((1, 128, D_MODEL), lambda j: (layer, BA_OFF // 128, 0))],
        out_specs=pl.BlockSpec((D_MODEL, 128), lambda j: (0, 0)),
        out_shape=jax.ShapeDtypeStruct((D_MODEL, 128), BF16),
        compiler_params=_cparams(("parallel",)),
        name="cast_w_ba",
    )(wt)
    return w1, w_gates, w_ba


def _inproj_kernel(*refs, n_prev, with_kv):
    x_ref, g_ref, sc_ref, sh_ref, w1_ref, w2_ref = refs[:6]
    prev = refs[6:6 + n_prev]
    outs = refs[6 + n_prev:]
    proj_ref, ba_ref, hn_ref = outs[:3]
    j = pl.program_id(1)

    @pl.when(j == 0)
    def _():
        x = x_ref[...]
        y = x * lax.rsqrt(jnp.mean(x * x, axis=-1, keepdims=True) + EPS) * g_ref[...]
        hn = (y * (1.0 + sc_ref[0]) + sh_ref[0]).astype(BF16)
        hn_ref[...] = hn
        ba_ref[...] = jnp.dot(hn, w2_ref[...], preferred_element_type=F32)

    proj_ref[...] = jnp.dot(hn_ref[...], w1_ref[...], preferred_element_type=F32)

    if with_kv:
        tn = proj_ref.shape[1]
        for c, (ref, col) in enumerate(((outs[3], COL_KB), (outs[4], COL_VB))):
            @pl.when(j == col // tn)
            def _(c=c, ref=ref, col=col):
                if n_prev:
                    ref[:, 0:ref.shape[1] - 1] = prev[c][...]
                if c == 0:
                    nb, nl, t, w = ref.shape
                    ref[:, nl - 1:nl] = proj_ref[:, col % tn:col % tn + w].reshape(nb, 1, t, w)
                else:
                    nb, nl, rows, w = ref.shape
                    t = rows // DA_HEADS
                    for b in range(nb):
                        for h in range(DA_HEADS):
                            ref[b, nl - 1, pl.ds(h, t, stride=DA_HEADS), :] = proj_ref[
                                b * t:(b + 1) * t, col % tn + h * w:col % tn + (h + 1) * w]


def inproj(x2, norm_g, scale, shift, w1, w2, rows_per_mod, kv=None):
    m = x2.shape[0]
    tm, tn = (1024, 1024) if (kv is not None and kv[2] is not None) else (512, N_MAIN)
    nmod = scale.shape[0]
    mod_idx = lambda i, j: ((i * tm) // rows_per_mod, 0, 0)
    in_specs = [pl.BlockSpec((tm, D_MODEL), lambda i, j: (i, 0)),
                pl.BlockSpec((1, D_MODEL), lambda i, j: (0, 0)),
                pl.BlockSpec((1, 1, D_MODEL), mod_idx),
                pl.BlockSpec((1, 1, D_MODEL), mod_idx),
                pl.BlockSpec((D_MODEL, tn), lambda i, j: (0, j)),
                pl.BlockSpec((D_MODEL, 128), lambda i, j: (0, 0))]
    args = [x2, norm_g.reshape(1, D_MODEL), scale.reshape(nmod, 1, D_MODEL),
            shift.reshape(nmod, 1, D_MODEL), w1, w2]
    out_specs = [pl.BlockSpec((tm, tn), lambda i, j: (i, j)),
                 pl.BlockSpec((tm, 128), lambda i, j: (i, 0)),
                 pl.BlockSpec((tm, D_MODEL), lambda i, j: (i, 0))]
    out_shape = [jax.ShapeDtypeStruct((m, N_MAIN), F32),
                 jax.ShapeDtypeStruct((m, 128), F32),
                 jax.ShapeDtypeStruct((m, D_MODEL), BF16)]
    n_prev = 0
    if kv is not None:
        layer, t, caches = kv
        shapes = ((t, DA_WIDTH), (t * DA_HEADS, DA_VDIM))
        cspecs = lambda nl: [pl.BlockSpec((tm // t, nl) + sh, lambda i, j: (i, 0, 0, 0)) for sh in shapes]
        out_specs += cspecs(layer + 1)
        out_shape += [jax.ShapeDtypeStruct((m // t, layer + 1) + sh, F32) for sh in shapes]
        if caches is not None:
            n_prev = 2
            in_specs += cspecs(layer)
            args += list(caches)
    return pl.pallas_call(
        functools.partial(_inproj_kernel, n_prev=n_prev, with_kv=kv is not None),
        grid=(m // tm, N_MAIN // tn),
        in_specs=in_specs,
        out_specs=out_specs,
        out_shape=out_shape,
        compiler_params=_cparams(("parallel", "arbitrary")),
        name="inproj",
    )(*args)


def _s5_gen_kernel(crt_ref, cit_ref, prt_ref, pit_ref, bbt_ref, bbs_ref, prow_ref, pirow_ref,
                   wt_ref, ws_ref, wh_ref):
    L, C, P = S5_CHUNK, S5_GROUP, S5_STATE
    width = (L + 1) * C
    row = lax.broadcasted_iota(jnp.int32, (128, width), 0)
    lane = lax.broadcasted_iota(jnp.int32, (128, width), 1)
    tile_c = (lane % C == row).astype(F32)
    expand = lambda a, e: jnp.dot(a, e, precision=HI, preferred_element_type=F32)
    zeros = jnp.zeros((C, L * C), F32)
    for g in range(S5_GEN_GROUPS):
        strips = []
        for d in range(2):
            spread_k = ((lane // C if d == 0 else L - lane // C) == row).astype(F32)
            crx, cix = expand(crt_ref[d, g], tile_c), expand(cit_ref[d, g], tile_c)
            prx, pix = expand(prt_ref[d, g], spread_k), expand(pit_ref[d, g], spread_k)
            ca = jnp.concatenate([crx * prx - cix * pix, -(crx * pix + cix * prx)], axis=0)
            wh_ref[d, g] = (ca[:, C:] if d == 0 else ca[:, :L * C]).astype(BF16)
            bbt, bbs = bbt_ref[d, g], bbs_ref[d, g]
            strips.append(jnp.dot(bbt, ca[:, :L * C] if d == 0 else ca[:, C:], precision=HI,
                                  preferred_element_type=F32))
            rows = []
            for i in range(L):
                k = L - 1 - i if d == 0 else i
                rows.append(bbt * prow_ref[d, g, k:k + 1, :] + bbs * pirow_ref[d, g, k:k + 1, :])
            ws_ref[d, g] = jnp.concatenate(rows, axis=0).astype(BF16)
        fpad = jnp.concatenate([zeros, strips[0]], axis=-1)
        rpad = jnp.concatenate([strips[1], zeros], axis=-1)
        rows = []
        for i in range(L):
            rows.append(fpad[:, (L - i) * C:(2 * L - i) * C] + rpad[:, (L - 1 - i) * C:(2 * L - 1 - i) * C])
        wt_ref[g] = jnp.concatenate(rows, axis=0).astype(BF16)


def s5_matrices(lam_re, lam_im, log_step, b_re, b_im, c_re, c_im):
    L, G, P, C = S5_CHUNK, S5_GROUPS, S5_STATE, S5_GROUP
    step = jnp.exp(log_step)[..., None]
    mag = jnp.exp(lam_re * step)
    ar, ai = mag * jnp.cos(lam_im * step), mag * jnp.sin(lam_im * step)
    den = lam_re * lam_re + lam_im * lam_im
    fr = ((ar - 1.0) * lam_re + ai * lam_im) / den
    fi = (ai * lam_re - (ar - 1.0) * lam_im) / den
    bbr = fr[..., None] * b_re - fi[..., None] * b_im
    bbi = fr[..., None] * b_im + fi[..., None] * b_re
    ks = jnp.arange(L + 1, dtype=F32)[None, None, :, None]
    pmag = jnp.exp(ks * (lam_re * step)[:, :, None, :])
    prow = pmag * jnp.cos(ks * (lam_im * step)[:, :, None, :])
    pirow = pmag * jnp.sin(ks * (lam_im * step)[:, :, None, :])
    prow, pirow = lax.optimization_barrier((prow, pirow))
    prt, pit = jnp.swapaxes(prow, 2, 3), jnp.swapaxes(pirow, 2, 3)
    bbrt, bbit = jnp.swapaxes(bbr, 2, 3), jnp.swapaxes(bbi, 2, 3)
    lanes = lambda a: jnp.pad(a, ((0, 0), (0, 0), (0, 0), (0, 128 - a.shape[-1])))
    args = [lanes(jnp.swapaxes(c_re, 2, 3)), lanes(jnp.swapaxes(c_im, 2, 3)), lanes(prt), lanes(pit),
            jnp.concatenate([bbrt, bbit], -1), jnp.concatenate([bbit, bbrt], -1),
            jnp.concatenate([prow, prow], -1), jnp.concatenate([-pirow, pirow], -1)]
    gg = S5_GEN_GROUPS
    spec = lambda a: pl.BlockSpec((2, gg) + a.shape[2:], lambda g: (0, g, 0, 0))
    wt, ws, wh = pl.pallas_call(
        _s5_gen_kernel,
        grid=(G // gg,),
        in_specs=[spec(a) for a in args],
        out_specs=[pl.BlockSpec((gg, S5_ROW, S5_ROW), lambda g: (g, 0, 0)),
                   pl.BlockSpec((2, gg, S5_ROW, 2 * P), lambda g: (0, g, 0, 0)),
                   pl.BlockSpec((2, gg, 2 * P, S5_ROW), lambda g: (0, g, 0, 0))],
        out_shape=[jax.ShapeDtypeStruct((G, S5_ROW, S5_ROW), BF16),
                   jax.ShapeDtypeStruct((2, G, S5_ROW, 2 * P), BF16),
                   jax.ShapeDtypeStruct((2, G, 2 * P, S5_ROW), BF16)],
        compiler_params=_cparams(("parallel",)),
        name="s5_gen",
    )(*args)
    a_l = jnp.stack([prt[..., L].reshape(2, 1, G * P), pit[..., L].reshape(2, 1, G * P)], 1)
    return wt, ws, wh, a_l


S5_GB = 8


def _s5_core_kernel(u_ref, ws_ref, wt_ref, wh_ref, a_ref, h0_ref, y_ref, hfin_ref, x_ref, s_ref, hin_ref,
                    *, nb, nchunk):
    r = nb * nchunk
    npair = S5_GB // 2
    us = [u_ref[pl.ds(i, r, stride=S5_CHUNK), :] for i in range(S5_CHUNK)]
    for g in range(S5_GB):
        xg = jnp.concatenate([u[:, S5_GROUP * g:S5_GROUP * (g + 1)] for u in us], axis=-1)
        x_ref[:, g * S5_ROW:(g + 1) * S5_ROW] = xg.astype(BF16)
    for p in range(npair):
        for d in range(2):
            sg = [jnp.dot(x_ref[:, (2 * p + k) * S5_ROW:(2 * p + k + 1) * S5_ROW], ws_ref[d, 2 * p + k],
                          preferred_element_type=F32) for k in range(2)]
            for comp in range(2):
                cs = slice(comp * S5_STATE, (comp + 1) * S5_STATE)
                s_ref[d, comp, p] = jnp.concatenate([sg[0][:, cs], sg[1][:, cs]], axis=-1)
    chains = [(d, p) for d in range(2) for p in range(npair)]
    h = {}
    for d, p in chains:
        cols = slice(p * 128, (p + 1) * 128)
        h[d, p] = (h0_ref[d, 0, :, cols], h0_ref[d, 1, :, cols], a_ref[d, 0, :, cols], a_ref[d, 1, :, cols])
    for step in range(nchunk):
        for d, p in chains:
            rows = pl.ds(step if d == 0 else nchunk - 1 - step, nb, stride=nchunk)
            hr, hi, ar, ai = h[d, p]
            hin_ref[d, 0, p, rows, :] = hr
            hin_ref[d, 1, p, rows, :] = hi
            sr, si = s_ref[d, 0, p, rows, :], s_ref[d, 1, p, rows, :]
            h[d, p] = (ar * hr - ai * hi + sr, ar * hi + ai * hr + si, ar, ai)
    for d, p in chains:
        cols = slice(p * 128, (p + 1) * 128)
        hfin_ref[d, 0, :, cols] = h[d, p][0]
        hfin_ref[d, 1, :, cols] = h[d, p][1]
    ys = []
    for g in range(S5_GB):
        y = jnp.dot(x_ref[:, g * S5_ROW:(g + 1) * S5_ROW], wt_ref[g], preferred_element_type=F32)
        cs = slice((g % 2) * S5_STATE, (g % 2 + 1) * S5_STATE)
        for d in range(2):
            hg = jnp.concatenate([hin_ref[d, 0, g // 2, :, cs], hin_ref[d, 1, g // 2, :, cs]], axis=-1)
            y = y + jnp.dot(hg.astype(BF16), wh_ref[d, g], preferred_element_type=F32)
        ys.append(y)
    for j in range(S5_CHUNK):
        y_ref[pl.ds(j, r, stride=S5_CHUNK), :] = jnp.concatenate(
            [y[:, S5_GROUP * j:S5_GROUP * (j + 1)] for y in ys], axis=-1)


def s5_core(proj, mats, h0, nb, nchunk):
    wt, ws, wh, a_l = mats
    r = nb * nchunk
    sw = S5_GB * S5_STATE
    return pl.pallas_call(
        functools.partial(_s5_core_kernel, nb=nb, nchunk=nchunk),
        grid=(S5_GROUPS // S5_GB,),
        in_specs=[pl.BlockSpec((r * S5_CHUNK, 128), lambda j: (0, COL_UA // 128 + j)),
                  pl.BlockSpec((2, S5_GB, S5_ROW, 2 * S5_STATE), lambda j: (0, j, 0, 0)),
                  pl.BlockSpec((S5_GB, S5_ROW, S5_ROW), lambda j: (j, 0, 0)),
                  pl.BlockSpec((2, S5_GB, 2 * S5_STATE, S5_ROW), lambda j: (0, j, 0, 0)),
                  pl.BlockSpec((2, 2, 1, sw), lambda j: (0, 0, 0, j)),
                  pl.BlockSpec((2, 2, nb, sw), lambda j: (0, 0, 0, j))],
        out_specs=[pl.BlockSpec((r * S5_CHUNK, 128), lambda j: (0, j)),
                   pl.BlockSpec((2, 2, nb, sw), lambda j: (0, 0, 0, j))],
        out_shape=[jax.ShapeDtypeStruct((r * S5_CHUNK, S5_WIDTH), F32),
                   jax.ShapeDtypeStruct((2, 2, nb, S5_GROUPS * S5_STATE), F32)],
        scratch_shapes=[pltpu.VMEM((r, S5_GB * S5_ROW), BF16),
                        pltpu.VMEM((2, 2, S5_GB // 2, r, 128), F32),
                        pltpu.VMEM((2, 2, S5_GB // 2, r, 128), F32)],
        compiler_params=_cparams(("parallel",)),
        name="s5_core",
    )(proj, ws, wt, wh, a_l, h0)


def _rope_tables(t):
    rows = t // GRID_W
    row = np.repeat(np.arange(rows), GRID_W).astype(np.float32)
    col = np.tile(np.arange(GRID_W), rows).astype(np.float32)
    nf = DA_HEAD // 4
    inv = (ROPE_BASE ** (-jnp.arange(nf, dtype=F32) / nf))

    def tab(pos):
        ang = jnp.asarray(pos)[:, None] * inv[None, :]
        c, s = jnp.cos(ang), jnp.sin(ang)
        return jnp.concatenate([c, c], -1), jnp.concatenate([-s, s], -1)

    cr, sr = tab(row)
    cc, sc = tab(col)
    cos = jnp.concatenate([cr, cc], -1)
    sin = jnp.concatenate([sr, sc], -1)
    return jnp.tile(cos, (1, 2)), jnp.tile(sin, (1, 2))


def _rope(x, cos, sin):
    lane = lax.broadcasted_iota(jnp.int32, x.shape, 1)
    swapped = jnp.where((lane % 32) < 16, pltpu.roll(x, 112, 1), pltpu.roll(x, 16, 1))
    return x * cos + swapped * sin


def _attn_kernel(*refs, lam_init, t, s_tot, tq, with_ctx):
    if with_ctx:
        (q_ref, k_ref, v_ref, z_ref, kc_ref, vc_ref, cq_ref, sq_ref, ck_ref, sk_ref,
         lam_ref, ng_ref, o_ref, kall_ref, vall_ref, s_ref, e_ref, l_ref) = refs
    else:
        q_ref, k_ref, v_ref, z_ref, lam_ref, ng_ref, o_ref, kall_ref, vall_ref, s_ref, e_ref, l_ref = refs

    @pl.when(pl.program_id(1) == 0)
    def _():
        if with_ctx:
            for h in range(DA_HEADS):
                hs = slice(h * 128, (h + 1) * 128)
                kall_ref[h, :, 0:t] = _rope(k_ref[0, :, hs], ck_ref[...], sk_ref[...]).T.astype(BF16)
                kall_ref[h, :, t:s_tot] = kc_ref[0, 0, h].astype(BF16)
            vall_ref[0:t, :] = v_ref[0].astype(BF16)
            for h in range(DA_HEADS):
                vall_ref[t:s_tot, h * 128:(h + 1) * 128] = vc_ref[
                    0, pl.ds(h, s_tot - t, stride=DA_HEADS), :].astype(BF16)
        else:
            for h in range(DA_HEADS):
                kall_ref[h] = k_ref[0, :, h * 128:(h + 1) * 128].T.astype(BF16)
            vall_ref[...] = v_ref[0].astype(BF16)

    lp = lam_ref[...]
    lam = (jnp.exp(jnp.sum(lp[0:1] * lp[1:2], axis=-1, keepdims=True))
           - jnp.exp(jnp.sum(lp[2:3] * lp[3:4], axis=-1, keepdims=True)) + lam_init)
    lane = lax.broadcasted_iota(jnp.int32, (tq, 128), 1)
    def scores(h):
        hs = slice(h * 128, (h + 1) * 128)
        q = q_ref[0, :, hs]
        if with_ctx:
            q = _rope(q, cq_ref[...], sq_ref[...])
        q = q * (DA_HEAD ** -0.5 * LOG2E)
        qs = jnp.concatenate([jnp.where(lane < DA_HEAD, q, 0.0), jnp.where(lane >= DA_HEAD, q, 0.0)], axis=0)
        s_ref[h % 2] = _mm(qs, kall_ref[h])

    scores(0)
    for h in range(DA_HEADS):
        hs = slice(h * 128, (h + 1) * 128)
        b = h % 2
        if h + 1 < DA_HEADS:
            scores(h + 1)
        for r in range(0, 2 * tq, ATTN_ROWS):
            sc = s_ref[b, r:r + ATTN_ROWS, :]
            e = jnp.exp2(sc - jnp.max(sc, axis=-1, keepdims=True))
            e_ref[b, r:r + ATTN_ROWS, :] = e.astype(BF16)
            l_ref[b, r:r + ATTN_ROWS, :] = jnp.broadcast_to(jnp.sum(e, axis=-1, keepdims=True),
                                                            (ATTN_ROWS, 128))
        ov = jnp.dot(e_ref[b], vall_ref[:, hs], preferred_element_type=F32) / l_ref[b]
        o = ov[:tq] - lam * ov[tq:]
        o = o * lax.rsqrt(jnp.mean(o * o, axis=-1, keepdims=True) + EPS) * ng_ref[...]
        o = o * (1.0 - lam_init)
        o_ref[0, :, hs] = (o * _silu(z_ref[0, :, hs])).astype(BF16)


def diff_attention(proj3, layer, lam_init, da_lam, da_norm_g, ctx_kv):
    nb, t, _ = proj3.shape
    with_ctx = ctx_kv is not None
    s_tot = t + (PAST_LEN if with_ctx else 0)
    tq = 256
    wb = DA_WIDTH
    in_specs = [pl.BlockSpec((1, tq, wb), lambda b, i: (b, i, COL_QB // wb)),
                pl.BlockSpec((1, t, wb), lambda b, i: (b, 0, COL_KB // wb)),
                pl.BlockSpec((1, t, wb), lambda b, i: (b, 0, COL_VB // wb)),
                pl.BlockSpec((1, tq, wb), lambda b, i: (b, i, COL_ZB // wb))]
    args = [proj3, proj3, proj3, proj3]
    if with_ctx:
        kc, vc = ctx_kv
        cos, sin = _rope_tables(t)
        in_specs += [pl.BlockSpec((1, 1, DA_HEADS, 2 * DA_HEAD, PAST_LEN), lambda b, i: (b, layer, 0, 0, 0)),
                     pl.BlockSpec((1, PAST_LEN * DA_HEADS, DA_VDIM), lambda b, i: (b, layer, 0)),
                     pl.BlockSpec((tq, 128), lambda b, i: (i, 0)),
                     pl.BlockSpec((tq, 128), lambda b, i: (i, 0)),
                     pl.BlockSpec((t, 128), lambda b, i: (0, 0)),
                     pl.BlockSpec((t, 128), lambda b, i: (0, 0))]
        args += [kc, vc, cos, sin, cos, sin]
    in_specs += [pl.BlockSpec((4, DA_HEAD), lambda b, i: (0, 0)),
                 pl.BlockSpec((1, DA_VDIM), lambda b, i: (0, 0))]
    args += [da_lam, da_norm_g.reshape(1, DA_VDIM)]
    return pl.pallas_call(
        functools.partial(_attn_kernel, lam_init=lam_init, t=t, s_tot=s_tot, tq=tq, with_ctx=with_ctx),
        grid=(nb, t // tq),
        in_specs=in_specs,
        out_specs=pl.BlockSpec((1, tq, wb), lambda b, i: (b, i, 0)),
        out_shape=jax.ShapeDtypeStruct((nb, t, wb), BF16),
        scratch_shapes=[pltpu.VMEM((DA_HEADS, 2 * DA_HEAD, s_tot), BF16), pltpu.VMEM((s_tot, wb), BF16),
                        pltpu.VMEM((2, 2 * tq, s_tot), F32), pltpu.VMEM((2, 2 * tq, s_tot), BF16),
                        pltpu.VMEM((2, 2 * tq, 128), F32)],
        compiler_params=_cparams(("parallel", "arbitrary")),
        name="diff_attention",
    )(*args)


DN_PAD = 8
DN_RT = 128
DN_GROUP = 128
DN_STEP_GROUPS = 2


def _dn_kernel(*refs, t, ngroup, with_s0):
    if with_s0:
        (qkv_ref, z_ref, ba_ref, cw_ref, alog_ref, dtb_ref, ng_ref, s0_ref,
         out_ref, sfin_ref, xp_ref, qkvn_ref, oacc_ref, st_ref) = refs
    else:
        (qkv_ref, z_ref, ba_ref, cw_ref, alog_ref, dtb_ref, ng_ref,
         out_ref, sfin_ref, xp_ref, qkvn_ref, oacc_ref, st_ref) = refs
    n = pl.program_id(1)
    cd = DN_CHUNK
    w3 = 3 * DN_WIDTH

    @pl.when(n == 0)
    def _init():
        xp_ref[0:DN_PAD, :] = jnp.zeros((DN_PAD, w3), F32)
        xp_ref[DN_PAD + t:2 * DN_PAD + t, :] = jnp.zeros((DN_PAD, w3), F32)
        xp_ref[DN_PAD:DN_PAD + t, :] = qkv_ref[0]
        half = DN_CONV // 2
        for r in range(t // DN_RT):
            for sec in range(3):
                for h in range(DN_HEADS):
                    cs = slice(sec * DN_WIDTH + h * DN_HEAD, sec * DN_WIDTH + (h + 1) * DN_HEAD)
                    acc = jnp.zeros((DN_RT, DN_HEAD), F32)
                    for j in range(DN_CONV):
                        r0 = DN_PAD + r * DN_RT + j - half
                        acc = acc + xp_ref[r0:r0 + DN_RT, cs] * cw_ref[j:j + 1, cs]
                    y = _silu(acc)
                    if sec < 2:
                        y = y * lax.rsqrt(jnp.sum(y * y, axis=-1, keepdims=True) + EPS)
                    if sec == 0:
                        y = y * (DN_HEAD ** -0.5)
                    qkvn_ref[r * DN_RT:(r + 1) * DN_RT, cs] = y
        oacc_ref[...] = jnp.zeros_like(oacc_ref)
        if with_s0:
            st_ref[...] = s0_ref[0, 0]
        else:
            st_ref[...] = jnp.zeros_like(st_ref)

    gb = DN_GROUP
    nsub = gb // cd
    ri = lax.broadcasted_iota(jnp.int32, (gb, gb), 0)
    ci = lax.broadcasted_iota(jnp.int32, (gb, gb), 1)
    same = (ri // cd) == (ci // cd)
    samef = same.astype(F32)
    eye = (ri == ci).astype(BF16)
    masked_out = -1e30

    chains = []
    for d, gi in [(d, gi) for d in range(2) for gi in range(DN_STEP_GROUPS)]:
        grp = n * DN_STEP_GROUPS + gi
        r0 = pl.multiple_of((grp if d == 0 else ngroup - 1 - grp) * gb, gb)
        incl = same & ((ci <= ri) if d == 0 else (ci >= ri))
        strict = same & ((ci < ri) if d == 0 else (ci > ri))
        inclf = incl.astype(F32)
        inclog = jnp.where(incl, 0.0, masked_out)
        nstrict = -strict.astype(BF16)
        ba = ba_ref[0, pl.ds(r0, gb), :]
        beta_all = jax.nn.sigmoid(ba)
        g_all = -jnp.exp(alog_ref[...]) * jax.nn.softplus(ba + dtb_ref[...])
        gc = jnp.dot(inclf, g_all, precision=HI, preferred_element_type=F32)
        gct = lax.dot_general(g_all.T, inclf, (((1,), (1,)), ((), ())), precision=HI,
                              preferred_element_type=F32)
        gtot = jnp.dot(samef, g_all, precision=HI, preferred_element_type=F32)
        for h in range(DN_HEADS):
            hs = slice(h * DN_HEAD, (h + 1) * DN_HEAD)
            col = 2 * DN_HEADS + d * DN_HEADS + h
            gcol = gc[:, col:col + 1]
            grow = gct[col:col + 1, :]
            gt = gtot[:, col:col + 1]
            beta = beta_all[:, d * DN_HEADS + h:d * DN_HEADS + h + 1]
            q = qkvn_ref[pl.ds(r0, gb), hs]
            k = qkvn_ref[pl.ds(r0, gb), DN_WIDTH + h * DN_HEAD:DN_WIDTH + (h + 1) * DN_HEAD]
            v = qkvn_ref[pl.ds(r0, gb), 2 * DN_WIDTH + h * DN_HEAD:2 * DN_WIDTH + (h + 1) * DN_HEAD]
            eg = jnp.exp(gcol)
            chains.append(dict(
                d=d, gi=gi, h=h, r0=r0, hs=hs, nstrict=nstrict, q=q, k=k, kb=k * beta,
                decay=jnp.exp((gcol - grow) + inclog),
                rhs=jnp.concatenate([v * beta, k * beta * eg], axis=-1),
                qe=q * eg, kdec=k * jnp.exp(gt - gcol), egt=jnp.exp(gt)))

    for c in chains:
        c["nmm"] = (_mm_nt(c["kb"], c["k"]) * c["decay"]).astype(BF16) * c["nstrict"]
    for c in chains:
        c["qk"] = _mm_nt(c["q"], c["k"]) * c["decay"]
    def level_mask(s):
        return (((ri // (2 * s)) == (ci // (2 * s))) & ((ri // s) != (ci // s))).astype(BF16)

    pair = level_mask(1)
    for c in chains:
        c["tinv"] = eye + c["nmm"] * pair
    s = 2
    while s < cd:
        offmask = level_mask(s)
        xs = [jnp.dot(c["nmm"] * offmask, c["tinv"], preferred_element_type=F32).astype(BF16) for c in chains]
        ys = [jnp.dot(c["tinv"], x, preferred_element_type=F32) for c, x in zip(chains, xs)]
        for c, y in zip(chains, ys):
            c["tinv"] = c["tinv"] + y.astype(BF16)
        s *= 2
    for c in chains:
        c["uw"] = _mm(c["tinv"], c["rhs"])

    st = {(d, h): st_ref[d, h] for d in range(2) for h in range(DN_HEADS)}
    outs = []
    for gi, step in [(gi, step) for gi in range(DN_STEP_GROUPS) for step in range(nsub)]:
        active = [c for c in chains if c["gi"] == gi]
        rs = []
        for c in active:
            sub = step if c["d"] == 0 else nsub - 1 - step
            c["rows"] = slice(sub * cd, (sub + 1) * cd)
            rs.append(_mm(jnp.concatenate([c["uw"][c["rows"], DN_HEAD:], c["qe"][c["rows"]]], axis=0),
                          st[c["d"], c["h"]]))
        for c, r in zip(active, rs):
            rows = c["rows"]
            v_new = c["uw"][rows, :DN_HEAD] - r[:cd]
            o = r[cd:] + _mm(c["qk"][rows, rows], v_new)
            st[c["d"], c["h"]] = (st[c["d"], c["h"]] * c["egt"][rows.start:rows.start + 1]
                                  + _mm(c["kdec"][rows].T, v_new))
            outs.append((c, rows.start, o))
    for (d, h), v in st.items():
        st_ref[d, h] = v
    for c, off, o in outs:
        oacc_ref[pl.ds(pl.multiple_of(c["r0"] + off, cd), cd), c["hs"]] += o

    @pl.when(n == ngroup // DN_STEP_GROUPS - 1)
    def _fin():
        for h in range(DN_HEADS):
            hs = slice(h * DN_HEAD, (h + 1) * DN_HEAD)
            o = oacc_ref[:, hs]
            o = o * lax.rsqrt(jnp.mean(o * o, axis=-1, keepdims=True) + EPS) * ng_ref[...]
            out_ref[0, :, hs] = (o * _silu(z_ref[0, :, hs])).astype(BF16)
        sfin_ref[0] = st_ref[...]


def deltanet(proj3, ba3, layer, conv_w, a_log, dt_bias, norm_g, s0):
    nb, t, _ = proj3.shape
    ngroup = t // DN_GROUP
    with_s0 = s0 is not None
    w3 = 3 * DN_WIDTH
    pad = jnp.zeros((2 * DN_HEADS,), F32)
    alog_row = jnp.concatenate([pad, a_log.reshape(-1), jnp.zeros((128 - 4 * DN_HEADS,), F32)]).reshape(1, 128)
    dtb_row = jnp.concatenate([pad, dt_bias.reshape(-1), jnp.zeros((128 - 4 * DN_HEADS,), F32)]).reshape(1, 128)
    in_specs = [pl.BlockSpec((1, t, w3), lambda b, n: (b, 0, COL_QC // w3)),
                pl.BlockSpec((1, t, DN_WIDTH), lambda b, n: (b, 0, COL_ZC // DN_WIDTH)),
                pl.BlockSpec((1, t, 128), lambda b, n: (b, 0, 0)),
                pl.BlockSpec((8, w3), lambda b, n: (0, 0)),
                pl.BlockSpec((1, 128), lambda b, n: (0, 0)),
                pl.BlockSpec((1, 128), lambda b, n: (0, 0)),
                pl.BlockSpec((1, DN_HEAD), lambda b, n: (0, 0))]
    args = [proj3, proj3, ba3, jnp.pad(conv_w, ((0, 8 - DN_CONV), (0, 0))), alog_row, dtb_row,
            norm_g.reshape(1, DN_HEAD)]
    if with_s0:
        in_specs.append(pl.BlockSpec((1, 1, 2, DN_HEADS, DN_HEAD, DN_HEAD), lambda b, n: (b, layer, 0, 0, 0, 0)))
        args.append(s0)
    return pl.pallas_call(
        functools.partial(_dn_kernel, t=t, ngroup=ngroup, with_s0=with_s0),
        grid=(nb, ngroup // DN_STEP_GROUPS),
        in_specs=in_specs,
        out_specs=[pl.BlockSpec((1, t, DN_WIDTH), lambda b, n: (b, 0, 0)),
                   pl.BlockSpec((1, 2, DN_HEADS, DN_HEAD, DN_HEAD), lambda b, n: (b, 0, 0, 0, 0))],
        out_shape=[jax.ShapeDtypeStruct((nb, t, DN_WIDTH), BF16),
                   jax.ShapeDtypeStruct((nb, 2, DN_HEADS, DN_HEAD, DN_HEAD), F32)],
        scratch_shapes=[pltpu.VMEM((t + 2 * DN_PAD, w3), F32),
                        pltpu.VMEM((t, w3), F32),
                        pltpu.VMEM((t, DN_WIDTH), F32),
                        pltpu.VMEM((2, DN_HEADS, DN_HEAD, DN_HEAD), F32)],
        compiler_params=_cparams(("parallel", "arbitrary")),
        name="deltanet",
    )(*args)


def _merge_kernel(u_ref, ys_ref, za_ref, sd_ref, wglu_ref, ob_ref, oc_ref, hn_ref, wg_ref, wb_ref, wo_ref,
                  x_ref, gate_ref, fg_ref, *outs, final):
    tm = x_ref.shape[0]
    for rows in (slice(0, tm // 2), slice(tm // 2, tm)):
        ya = jax.nn.gelu(sd_ref[...] * u_ref[rows, :] + ys_ref[rows, :])
        ya = ya * jax.nn.sigmoid(_mm(ya, wglu_ref[...]))
        branches = ((ya * _silu(za_ref[rows, :])).astype(BF16), ob_ref[rows, :], oc_ref[rows, :])
        acc = None
        hn = hn_ref[rows, :]
        for i, o in enumerate(branches):
            pr = jnp.dot(o, wb_ref[i], preferred_element_type=F32)
            gt = jnp.dot(hn, wg_ref[:, i * D_MODEL:(i + 1) * D_MODEL], preferred_element_type=F32)
            term = jax.nn.sigmoid(gt) * pr
            acc = term if acc is None else acc + term
        y = jnp.dot(acc.astype(BF16), wo_ref[...], preferred_element_type=F32)
        xn = x_ref[rows, :] + gate_ref[0] * y
        outs[0][rows, :] = xn
        if final:
            yn = xn * lax.rsqrt(jnp.mean(xn * xn, axis=-1, keepdims=True) + EPS) * fg_ref[...]
            outs[1][rows, :] = yn


def merge(proj, y_s5, s5_d, w_glu, out_b, out_c, hn, w_gates, w_branch, w_out, x2, gate, final_g, rows_per_mod,
          final):
    m = x2.shape[0]
    tm = 512
    nmod = gate.shape[0]
    row = lambda i: (i, 0)
    out_specs = [pl.BlockSpec((tm, D_MODEL), row)]
    out_shape = [jax.ShapeDtypeStruct((m, D_MODEL), F32)]
    if final:
        out_specs.append(pl.BlockSpec((tm, D_MODEL), row))
        out_shape.append(jax.ShapeDtypeStruct((m, D_MODEL), F32))
    return pl.pallas_call(
        functools.partial(_merge_kernel, final=final),
        grid=(m // tm,),
        in_specs=[pl.BlockSpec((tm, S5_WIDTH), lambda i: (i, COL_UA // S5_WIDTH)),
                  pl.BlockSpec((tm, S5_WIDTH), row),
                  pl.BlockSpec((tm, S5_WIDTH), lambda i: (i, COL_ZA // S5_WIDTH)),
                  pl.BlockSpec((1, S5_WIDTH), lambda i: (0, 0)),
                  pl.BlockSpec((S5_WIDTH, S5_WIDTH), lambda i: (0, 0)),
                  pl.BlockSpec((tm, BRANCH_WIDTH), row),
                  pl.BlockSpec((tm, BRANCH_WIDTH), row),
                  pl.BlockSpec((tm, D_MODEL), row),
                  pl.BlockSpec((D_MODEL, N_BRANCH * D_MODEL), lambda i: (0, 0)),
                  pl.BlockSpec((N_BRANCH, BRANCH_WIDTH, D_MODEL), lambda i: (0, 0, 0)),
                  pl.BlockSpec((D_MODEL, D_MODEL), lambda i: (0, 0)),
                  pl.BlockSpec((tm, D_MODEL), row),
                  pl.BlockSpec((1, 1, D_MODEL), lambda i: ((i * tm) // rows_per_mod, 0, 0)),
                  pl.BlockSpec((1, D_MODEL), lambda i: (0, 0))],
        out_specs=out_specs,
        out_shape=out_shape,
        compiler_params=_cparams(("parallel",)),
        name="merge",
    )(proj, y_s5, proj, s5_d.reshape(1, S5_WIDTH), w_glu, out_b, out_c, hn, w_gates, w_branch, w_out, x2,
      gate.reshape(nmod, 1, D_MODEL),
      final_g.reshape(1, D_MODEL))


def _run_pass(x, mod, wts, lam_inits, final_g, ctx):
    nb, t, _ = x.shape
    m = nb * t
    nmod = mod.shape[1]
    rows_per_mod = m if nmod == 1 else t
    x2 = x.reshape(m, D_MODEL)
    states = []
    y, caches = None, None
    for l in range(DEPTH):
        w = wts[l]
        shift, scale, gate = jnp.split(mod[l], 3, axis=-1)
        if ctx is None:
            proj, ba, hn, *caches = inproj(x2, w["norm_g"], scale, shift, w["w1"], w["w2"], rows_per_mod,
                                       kv=(l, t, caches))
            h0 = jnp.zeros((2, 2, nb, S5_GROUPS * S5_STATE), F32)
            ctx_kv, s0 = None, None
        else:
            proj, ba, hn = inproj(x2, w["norm_g"], scale, shift, w["w1"], w["w2"], rows_per_mod)
            cache_k, cache_v, st_re, st_im, st_dn = ctx
            h0 = jnp.stack([st_re[:, l], st_im[:, l]], 0)
            h0 = jnp.transpose(h0, (2, 0, 1, 3, 4)).reshape(2, 2, nb, S5_GROUPS * S5_STATE)
            ctx_kv, s0 = (cache_k, cache_v), st_dn
        proj3 = proj.reshape(nb, t, N_MAIN)
        y_s5, hfin = s5_core(proj, w["s5_mats"], h0, nb, t // S5_CHUNK)
        out_b = diff_attention(proj3, l, lam_inits[l], w["da_lam"], w["da_norm_g"], ctx_kv)
        out_c, sfin = deltanet(proj3, ba.reshape(nb, t, 128), l, w["dn_conv"], w["dn_a_log"],
                               w["dn_dt_bias"], w["dn_norm_g"], s0)
        final = l == DEPTH - 1
        res = merge(proj, y_s5, w["s5_d"], w["w_glu"], out_b.reshape(m, DA_WIDTH), out_c.reshape(m, DN_WIDTH),
                    hn, w["w_gates"], w["w_branch"], w["w_out"], x2, gate, final_g, rows_per_mod, final)
        x2 = res[0]
        if final:
            y = res[1]
        if ctx is None:
            hf = hfin.reshape(2, 2, nb, S5_GROUPS, S5_STATE)
            states.append((jnp.transpose(hf[:, 0], (1, 0, 2, 3)), jnp.transpose(hf[:, 1], (1, 0, 2, 3)), sfin))
    return y.reshape(nb, t, D_MODEL), states, caches


def kernel(x_prompt, x_sample, cache_k, cache_v, state_s5_re, state_s5_im, state_dn, c, c_ctx,
           norm_g, w_ada, b_ada, w_in, s5_lam_re, s5_lam_im, s5_log_step, s5_b_re, s5_b_im,
           s5_c_re, s5_c_im, s5_d, s5_w_glu, da_lam, da_norm_g, dn_conv, dn_a_log, dn_dt_bias,
           dn_norm_g, w_branch, w_out, final_norm_g):
    nb_dec = x_sample.shape[0]
    cond8 = jnp.concatenate([c_ctx[None, :], c, jnp.zeros((8 - 1 - nb_dec, D_MODEL), F32)], 0)
    mod = ada_mod(cond8, w_ada, b_ada)
    wts = []
    for l in range(DEPTH):
        w1, w_gates, w2 = cast_w_in(w_in, l)
        wts.append(dict(
            norm_g=norm_g[l], w1=w1, w2=w2, w_gates=w_gates,
            s5_mats=s5_matrices(s5_lam_re[l], s5_lam_im[l], s5_log_step[l], s5_b_re[l], s5_b_im[l],
                                s5_c_re[l], s5_c_im[l]),
            s5_d=s5_d[l], w_glu=s5_w_glu[l].astype(BF16), da_lam=da_lam[l], da_norm_g=da_norm_g[l],
            dn_conv=dn_conv[l], dn_a_log=dn_a_log[l], dn_dt_bias=dn_dt_bias[l], dn_norm_g=dn_norm_g[l],
            w_branch=w_branch[l].astype(BF16), w_out=w_out[l].astype(BF16)))
    lam_inits = [0.8 - 0.6 * math.exp(-0.3 * l) for l in range(DEPTH)]

    y_prompt, states, (k_new, v_new) = _run_pass(x_prompt, mod[:, 0:1], wts, lam_inits, final_norm_g, None)
    cache_kt = jnp.transpose(cache_k, (0, 1, 3, 4, 5, 2)).reshape(nb_dec, DEPTH, DA_HEADS, 2 * DA_HEAD, PAST_LEN)
    ctx = (cache_kt, cache_v.reshape(nb_dec, DEPTH * PAST_LEN * DA_HEADS, DA_VDIM), state_s5_re, state_s5_im,
           state_dn)
    y_sample, _, _ = _run_pass(x_sample, mod[:, 1:1 + nb_dec], wts, lam_inits, final_norm_g, ctx)

    nb, t = x_prompt.shape[:2]
    new_cache_k = k_new.reshape(nb, DEPTH, t, DA_HEADS, 2, DA_HEAD)
    new_cache_v = v_new.reshape(nb, DEPTH, t, DA_HEADS, DA_VDIM)
    new_s5_re = jnp.stack([s[0] for s in states], axis=1)
    new_s5_im = jnp.stack([s[1] for s in states], axis=1)
    new_dn = jnp.stack([s[2] for s in states], axis=1)
    return (y_prompt, y_sample, new_cache_k, new_cache_v, new_s5_re, new_s5_im, new_dn)
```

```python
import functools
import math

import numpy as np
import jax
import jax.numpy as jnp
from jax import lax
from jax.experimental import pallas as pl
from jax.experimental.pallas import tpu as pltpu

F32 = jnp.float32
BF16 = jnp.bfloat16

D_MODEL = 1024
DEPTH = 2
GRID_W = 64
EPS = 1e-6
S5_WIDTH = 512
S5_GROUP = 16
S5_GROUPS = 32
S5_STATE = 64
S5_CHUNK = 16
S5_PAIRS = S5_GROUPS // 2
S5_ROW = S5_CHUNK * S5_GROUP
S5_GEN_GROUPS = 4
DA_HEADS = 4
DA_HEAD = 64
DA_VDIM = 128
DA_WIDTH = 512
ROPE_BASE = 10000.0
DN_HEADS = 4
DN_HEAD = 128
DN_WIDTH = 512
DN_CONV = 5
DN_CHUNK = 64
N_BRANCH = 3
BRANCH_WIDTH = 512
PAST_LEN = 512

COL_UA, COL_ZA, COL_QB, COL_KB, COL_VB, COL_ZB = 0, 512, 1024, 1536, 2048, 2560
COL_QC, COL_ZC = 3072, 4608
N_MAIN = 5120
BA_OFF = 5120
GATES_OFF = 5136

VMEM_LIMIT = 56 * 1024 * 1024
HI = lax.Precision.HIGHEST
LOG2E = math.log2(math.e)
ATTN_ROWS = 16


def _cparams(sem):
    return pltpu.CompilerParams(dimension_semantics=sem, vmem_limit_bytes=VMEM_LIMIT)


def _mm(a, b):
    return jnp.dot(a.astype(BF16), b.astype(BF16), preferred_element_type=F32)


def _mm_nt(a, b):
    return lax.dot_general(a.astype(BF16), b.astype(BF16), (((1,), (1,)), ((), ())),
                           preferred_element_type=F32)


def _silu(x):
    return x * jax.nn.sigmoid(x)


def _ada_kernel(c_ref, w_ref, b_ref, o_ref):
    o_ref[0] = _mm(_silu(c_ref[...]), w_ref[0]) + b_ref[0]


def ada_mod(cond8, w_ada, b_ada):
    tn = 1024
    return pl.pallas_call(
        _ada_kernel,
        grid=(DEPTH, 3 * D_MODEL // tn),
        in_specs=[pl.BlockSpec((8, D_MODEL), lambda l, j: (0, 0)),
                  pl.BlockSpec((1, D_MODEL, tn), lambda l, j: (l, 0, j)),
                  pl.BlockSpec((1, 1, tn), lambda l, j: (l, 0, j))],
        out_specs=pl.BlockSpec((1, 8, tn), lambda l, j: (l, 0, j)),
        out_shape=jax.ShapeDtypeStruct((DEPTH, 8, 3 * D_MODEL), F32),
        compiler_params=_cparams(("parallel", "parallel")),
        name="ada_mod",
    )(cond8, w_ada, b_ada.reshape(DEPTH, 1, 3 * D_MODEL))


def _tcast_kernel(w_ref, o_ref, *, keep):
    x = w_ref[0].T
    if keep is not None:
        x = jnp.where(lax.broadcasted_iota(jnp.int32, x.shape, 1) < keep, x, 0.0)
    o_ref[...] = x.astype(BF16)


def _tcast_rows_kernel(w_hbm, o_ref, buf, sem, *, layer, row0):
    tn = buf.shape[0]
    start = pl.multiple_of(row0 + pl.program_id(0) * tn, 8)
    cp = pltpu.make_async_copy(w_hbm.at[layer, pl.ds(start, tn), :], buf, sem)
    cp.start()
    cp.wait()
    o_ref[...] = buf[...].T.astype(BF16)


def cast_w_in(w_in, layer):
    wt = jnp.swapaxes(w_in, 1, 2)
    tn = 1024
    w1 = pl.pallas_call(
        functools.partial(_tcast_kernel, keep=None),
        grid=(N_MAIN // tn,),
        in_specs=[pl.BlockSpec((1, tn, D_MODEL), lambda j: (layer, j, 0))],
        out_specs=pl.BlockSpec((D_MODEL, tn), lambda j: (0, j)),
        out_shape=jax.ShapeDtypeStruct((D_MODEL, N_MAIN), BF16),
        compiler_params=_cparams(("parallel",)),
        name="cast_w1",
    )(wt)
    gw = N_BRANCH * D_MODEL
    w_gates = pl.pallas_call(
        functools.partial(_tcast_rows_kernel, layer=layer, row0=GATES_OFF),
        grid=(gw // tn,),
        in_specs=[pl.BlockSpec(memory_space=pl.ANY)],
        out_specs=pl.BlockSpec((D_MODEL, tn), lambda j: (0, j)),
        out_shape=jax.ShapeDtypeStruct((D_MODEL, gw), BF16),
        scratch_shapes=[pltpu.VMEM((tn, D_MODEL), F32), pltpu.SemaphoreType.DMA(())],
        compiler_params=_cparams(("parallel",)),
        name="cast_w_gates",
    )(wt)
    w_ba = pl.pallas_call(
        functools.partial(_tcast_kernel, keep=GATES_OFF - BA_OFF),
        grid=(1,),
        in_specs=[pl.BlockSpec((1, 128, D_MODEL), lambda j: (layer, BA_OFF // 128, 0))],
        out_specs=pl.BlockSpec((D_MODEL, 128), lambda j: (0, 0)),
        out_shape=jax.ShapeDtypeStruct((D_MODEL, 128), BF16),
        compiler_params=_cparams(("parallel",)),
        name="cast_w_ba",
    )(wt)
    return w1, w_gates, w_ba


def _inproj_kernel(*refs, n_prev, with_kv):
    x_ref, g_ref, sc_ref, sh_ref, w1_ref, w2_ref = refs[:6]
    prev = refs[6:6 + n_prev]
    outs = refs[6 + n_prev:]
    proj_ref, ba_ref, hn_ref = outs[:3]
    j = pl.program_id(1)

    @pl.when(j == 0)
    def _():
        x = x_ref[...]
        y = x * lax.rsqrt(jnp.mean(x * x, axis=-1, keepdims=True) + EPS) * g_ref[...]
        hn = (y * (1.0 + sc_ref[0]) + sh_ref[0]).astype(BF16)
        hn_ref[...] = hn
        ba_ref[...] = jnp.dot(hn, w2_ref[...], preferred_element_type=F32)

    proj_ref[...] = jnp.dot(hn_ref[...], w1_ref[...], preferred_element_type=F32)

    if with_kv:
        tn = proj_ref.shape[1]
        for c, (ref, col) in enumerate(((outs[3], COL_KB), (outs[4], COL_VB))):
            @pl.when(j == col // tn)
            def _(c=c, ref=ref, col=col):
                if n_prev:
                    ref[:, 0:ref.shape[1] - 1] = prev[c][...]
                if c == 0:
                    nb, nl, t, w = ref.shape
                    ref[:, nl - 1:nl] = proj_ref[:, col % tn:col % tn + w].reshape(nb, 1, t, w)
                else:
                    nb, nl, rows, w = ref.shape
                    t = rows // DA_HEADS
                    for b in range(nb):
                        for h in range(DA_HEADS):
                            ref[b, nl - 1, pl.ds(h, t, stride=DA_HEADS), :] = proj_ref[
                                b * t:(b + 1) * t, col % tn + h * w:col % tn + (h + 1) * w]


def inproj(x2, norm_g, scale, shift, w1, w2, rows_per_mod, kv=None):
    m = x2.shape[0]
    tm, tn = (1024, 1024) if (kv is not None and kv[2] is not None) else (512, N_MAIN)
    nmod = scale.shape[0]
    mod_idx = lambda i, j: ((i * tm) // rows_per_mod, 0, 0)
    in_specs = [pl.BlockSpec((tm, D_MODEL), lambda i, j: (i, 0)),
                pl.BlockSpec((1, D_MODEL), lambda i, j: (0, 0)),
                pl.BlockSpec((1, 1, D_MODEL), mod_idx),
                pl.BlockSpec((1, 1, D_MODEL), mod_idx),
                pl.BlockSpec((D_MODEL, tn), lambda i, j: (0, j)),
                pl.BlockSpec((D_MODEL, 128), lambda i, j: (0, 0))]
    args = [x2, norm_g.reshape(1, D_MODEL), scale.reshape(nmod, 1, D_MODEL),
            shift.reshape(nmod, 1, D_MODEL), w1, w2]
    out_specs = [pl.BlockSpec((tm, tn), lambda i, j: (i, j)),
                 pl.BlockSpec((tm, 128), lambda i, j: (i, 0)),
                 pl.BlockSpec((tm, D_MODEL), lambda i, j: (i, 0))]
    out_shape = [jax.ShapeDtypeStruct((m, N_MAIN), F32),
                 jax.ShapeDtypeStruct((m, 128), F32),
                 jax.ShapeDtypeStruct((m, D_MODEL), BF16)]
    n_prev = 0
    if kv is not None:
        layer, t, caches = kv
        shapes = ((t, DA_WIDTH), (t * DA_HEADS, DA_VDIM))
        cspecs = lambda nl: [pl.BlockSpec((tm // t, nl) + sh, lambda i, j: (i, 0, 0, 0)) for sh in shapes]
        out_specs += cspecs(layer + 1)
        out_shape += [jax.ShapeDtypeStruct((m // t, layer + 1) + sh, F32) for sh in shapes]
        if caches is not None:
            n_prev = 2
            in_specs += cspecs(layer)
            args += list(caches)
    return pl.pallas_call(
        functools.partial(_inproj_kernel, n_prev=n_prev, with_kv=kv is not None),
        grid=(m // tm, N_MAIN // tn),
        in_specs=in_specs,
        out_specs=out_specs,
        out_shape=out_shape,
        compiler_params=_cparams(("parallel", "arbitrary")),
        name="inproj",
    )(*args)


def _s5_gen_kernel(crt_ref, cit_ref, prt_ref, pit_ref, bbt_ref, bbs_ref, prow_ref, pirow_ref,
                   wt_ref, ws_ref, wh_ref):
    L, C, P = S5_CHUNK, S5_GROUP, S5_STATE
    width = (L + 1) * C
    row = lax.broadcasted_iota(jnp.int32, (128, width), 0)
    lane = lax.broadcasted_iota(jnp.int32, (128, width), 1)
    tile_c = (lane % C == row).astype(F32)
    expand = lambda a, e: jnp.dot(a, e, precision=HI, preferred_element_type=F32)
    zeros = jnp.zeros((C, L * C), F32)
    for g in range(S5_GEN_GROUPS):
        strips = []
        for d in range(2):
            spread_k = ((lane // C if d == 0 else L - lane // C) == row).astype(F32)
            crx, cix = expand(crt_ref[d, g], tile_c), expand(cit_ref[d, g], tile_c)
            prx, pix = expand(prt_ref[d, g], spread_k), expand(pit_ref[d, g], spread_k)
            ca = jnp.concatenate([crx * prx - cix * pix, -(crx * pix + cix * prx)], axis=0)
            wh_ref[d, g] = (ca[:, C:] if d == 0 else ca[:, :L * C]).astype(BF16)
            bbt, bbs = bbt_ref[d, g], bbs_ref[d, g]
            strips.append(jnp.dot(bbt, ca[:, :L * C] if d == 0 else ca[:, C:], precision=HI,
                                  preferred_element_type=F32))
            rows = []
            for i in range(L):
                k = L - 1 - i if d == 0 else i
                rows.append(bbt * prow_ref[d, g, k:k + 1, :] + bbs * pirow_ref[d, g, k:k + 1, :])
            ws_ref[d, g] = jnp.concatenate(rows, axis=0).astype(BF16)
        fpad = jnp.concatenate([zeros, strips[0]], axis=-1)
        rpad = jnp.concatenate([strips[1], zeros], axis=-1)
        rows = []
        for i in range(L):
            rows.append(fpad[:, (L - i) * C:(2 * L - i) * C] + rpad[:, (L - 1 - i) * C:(2 * L - 1 - i) * C])
        wt_ref[g] = jnp.concatenate(rows, axis=0).astype(BF16)


def s5_matrices(lam_re, lam_im, log_step, b_re, b_im, c_re, c_im):
    L, G, P, C = S5_CHUNK, S5_GROUPS, S5_STATE, S5_GROUP
    step = jnp.exp(log_step)[..., None]
    mag = jnp.exp(lam_re * step)
    ar, ai = mag * jnp.cos(lam_im * step), mag * jnp.sin(lam_im * step)
    den = lam_re * lam_re + lam_im * lam_im
    fr = ((ar - 1.0) * lam_re + ai * lam_im) / den
    fi = (ai * lam_re - (ar - 1.0) * lam_im) / den
    bbr = fr[..., None] * b_re - fi[..., None] * b_im
    bbi = fr[..., None] * b_im + fi[..., None] * b_re
    ks = jnp.arange(L + 1, dtype=F32)[None, None, :, None]
    pmag = jnp.exp(ks * (lam_re * step)[:, :, None, :])
    prow = pmag * jnp.cos(ks * (lam_im * step)[:, :, None, :])
    pirow = pmag * jnp.sin(ks * (lam_im * step)[:, :, None, :])
    prow, pirow = lax.optimization_barrier((prow, pirow))
    prt, pit = jnp.swapaxes(prow, 2, 3), jnp.swapaxes(pirow, 2, 3)
    bbrt, bbit = jnp.swapaxes(bbr, 2, 3), jnp.swapaxes(bbi, 2, 3)
    lanes = lambda a: jnp.pad(a, ((0, 0), (0, 0), (0, 0), (0, 128 - a.shape[-1])))
    args = [lanes(jnp.swapaxes(c_re, 2, 3)), lanes(jnp.swapaxes(c_im, 2, 3)), lanes(prt), lanes(pit),
            jnp.concatenate([bbrt, bbit], -1), jnp.concatenate([bbit, bbrt], -1),
            jnp.concatenate([prow, prow], -1), jnp.concatenate([-pirow, pirow], -1)]
    gg = S5_GEN_GROUPS
    spec = lambda a: pl.BlockSpec((2, gg) + a.shape[2:], lambda g: (0, g, 0, 0))
    wt, ws, wh = pl.pallas_call(
        _s5_gen_kernel,
        grid=(G // gg,),
        in_specs=[spec(a) for a in args],
        out_specs=[pl.BlockSpec((gg, S5_ROW, S5_ROW), lambda g: (g, 0, 0)),
                   pl.BlockSpec((2, gg, S5_ROW, 2 * P), lambda g: (0, g, 0, 0)),
                   pl.BlockSpec((2, gg, 2 * P, S5_ROW), lambda g: (0, g, 0, 0))],
        out_shape=[jax.ShapeDtypeStruct((G, S5_ROW, S5_ROW), BF16),
                   jax.ShapeDtypeStruct((2, G, S5_ROW, 2 * P), BF16),
                   jax.ShapeDtypeStruct((2, G, 2 * P, S5_ROW), BF16)],
        compiler_params=_cparams(("parallel",)),
        name="s5_gen",
    )(*args)
    a_l = jnp.stack([prt[..., L].reshape(2, 1, G * P), pit[..., L].reshape(2, 1, G * P)], 1)
    return wt, ws, wh, a_l


S5_GB = 8


def _s5_core_kernel(u_ref, ws_ref, wt_ref, wh_ref, a_ref, h0_ref, y_ref, hfin_ref, x_ref, s_ref, hin_ref,
                    *, nb, nchunk):
    r = nb * nchunk
    npair = S5_GB // 2
    us = [u_ref[pl.ds(i, r, stride=S5_CHUNK), :] for i in range(S5_CHUNK)]
    for g in range(S5_GB):
        xg = jnp.concatenate([u[:, S5_GROUP * g:S5_GROUP * (g + 1)] for u in us], axis=-1)
        x_ref[:, g * S5_ROW:(g + 1) * S5_ROW] = xg.astype(BF16)
    for p in range(npair):
        for d in range(2):
            sg = [jnp.dot(x_ref[:, (2 * p + k) * S5_ROW:(2 * p + k + 1) * S5_ROW], ws_ref[d, 2 * p + k],
                          preferred_element_type=F32) for k in range(2)]
            for comp in range(2):
                cs = slice(comp * S5_STATE, (comp + 1) * S5_STATE)
                s_ref[d, comp, p] = jnp.concatenate([sg[0][:, cs], sg[1][:, cs]], axis=-1)
    chains = [(d, p) for d in range(2) for p in range(npair)]
    h = {}
    for d, p in chains:
        cols = slice(p * 128, (p + 1) * 128)
        h[d, p] = (h0_ref[d, 0, :, cols], h0_ref[d, 1, :, cols], a_ref[d, 0, :, cols], a_ref[d, 1, :, cols])
    for step in range(nchunk):
        for d, p in chains:
            rows = pl.ds(step if d == 0 else nchunk - 1 - step, nb, stride=nchunk)
            hr, hi, ar, ai = h[d, p]
            hin_ref[d, 0, p, rows, :] = hr
            hin_ref[d, 1, p, rows, :] = hi
            sr, si = s_ref[d, 0, p, rows, :], s_ref[d, 1, p, rows, :]
            h[d, p] = (ar * hr - ai * hi + sr, ar * hi + ai * hr + si, ar, ai)
    for d, p in chains:
        cols = slice(p * 128, (p + 1) * 128)
        hfin_ref[d, 0, :, cols] = h[d, p][0]
        hfin_ref[d, 1, :, cols] = h[d, p][1]
    ys = []
    for g in range(S5_GB):
        y = jnp.dot(x_ref[:, g * S5_ROW:(g + 1) * S5_ROW], wt_ref[g], preferred_element_type=F32)
        cs = slice((g % 2) * S5_STATE, (g % 2 + 1) * S5_STATE)
        for d in range(2):
            hg = jnp.concatenate([hin_ref[d, 0, g // 2, :, cs], hin_ref[d, 1, g // 2, :, cs]], axis=-1)
            y = y + jnp.dot(hg.astype(BF16), wh_ref[d, g], preferred_element_type=F32)
        ys.append(y)
    for j in range(S5_CHUNK):
        y_ref[pl.ds(j, r, stride=S5_CHUNK), :] = jnp.concatenate(
            [y[:, S5_GROUP * j:S5_GROUP * (j + 1)] for y in ys], axis=-1)


def s5_core(proj, mats, h0, nb, nchunk):
    wt, ws, wh, a_l = mats
    r = nb * nchunk
    sw = S5_GB * S5_STATE
    return pl.pallas_call(
        functools.partial(_s5_core_kernel, nb=nb, nchunk=nchunk),
        grid=(S5_GROUPS // S5_GB,),
        in_specs=[pl.BlockSpec((r * S5_CHUNK, 128), lambda j: (0, COL_UA // 128 + j)),
                  pl.BlockSpec((2, S5_GB, S5_ROW, 2 * S5_STATE), lambda j: (0, j, 0, 0)),
                  pl.BlockSpec((S5_GB, S5_ROW, S5_ROW), lambda j: (j, 0, 0)),
                  pl.BlockSpec((2, S5_GB, 2 * S5_STATE, S5_ROW), lambda j: (0, j, 0, 0)),
                  pl.BlockSpec((2, 2, 1, sw), lambda j: (0, 0, 0, j)),
                  pl.BlockSpec((2, 2, nb, sw), lambda j: (0, 0, 0, j))],
        out_specs=[pl.BlockSpec((r * S5_CHUNK, 128), lambda j: (0, j)),
                   pl.BlockSpec((2, 2, nb, sw), lambda j: (0, 0, 0, j))],
        out_shape=[jax.ShapeDtypeStruct((r * S5_CHUNK, S5_WIDTH), F32),
                   jax.ShapeDtypeStruct((2, 2, nb, S5_GROUPS * S5_STATE), F32)],
        scratch_shapes=[pltpu.VMEM((r, S5_GB * S5_ROW), BF16),
                        pltpu.VMEM((2, 2, S5_GB // 2, r, 128), F32),
                        pltpu.VMEM((2, 2, S5_GB // 2, r, 128), F32)],
        compiler_params=_cparams(("parallel",)),
        name="s5_core",
    )(proj, ws, wt, wh, a_l, h0)


def _rope_tables(t):
    rows = t // GRID_W
    row = np.repeat(np.arange(rows), GRID_W).astype(np.float32)
    col = np.tile(np.arange(GRID_W), rows).astype(np.float32)
    nf = DA_HEAD // 4
    inv = (ROPE_BASE ** (-jnp.arange(nf, dtype=F32) / nf))

    def tab(pos):
        ang = jnp.asarray(pos)[:, None] * inv[None, :]
        c, s = jnp.cos(ang), jnp.sin(ang)
        return jnp.concatenate([c, c], -1), jnp.concatenate([-s, s], -1)

    cr, sr = tab(row)
    cc, sc = tab(col)
    cos = jnp.concatenate([cr, cc], -1)
    sin = jnp.concatenate([sr, sc], -1)
    return jnp.tile(cos, (1, 2)), jnp.tile(sin, (1, 2))


def _rope(x, cos, sin):
    lane = lax.broadcasted_iota(jnp.int32, x.shape, 1)
    swapped = jnp.where((lane % 32) < 16, pltpu.roll(x, 112, 1), pltpu.roll(x, 16, 1))
    return x * cos + swapped * sin


def _attn_kernel(*refs, lam_init, t, s_tot, tq, with_ctx):
    if with_ctx:
        (q_ref, k_ref, v_ref, z_ref, kc_ref, vc_ref, cq_ref, sq_ref, ck_ref, sk_ref,
         lam_ref, ng_ref, o_ref, kall_ref, vall_ref, s_ref, e_ref, l_ref) = refs
    else:
        q_ref, k_ref, v_ref, z_ref, lam_ref, ng_ref, o_ref, kall_ref, vall_ref, s_ref, e_ref, l_ref = refs

    @pl.when(pl.program_id(1) == 0)
    def _():
        if with_ctx:
            for h in range(DA_HEADS):
                hs = slice(h * 128, (h + 1) * 128)
                kall_ref[h, :, 0:t] = _rope(k_ref[0, :, hs], ck_ref[...], sk_ref[...]).T.astype(BF16)
                kall_ref[h, :, t:s_tot] = kc_ref[0, 0, h].astype(BF16)
            vall_ref[0:t, :] = v_ref[0].astype(BF16)
            for h in range(DA_HEADS):
                vall_ref[t:s_tot, h * 128:(h + 1) * 128] = vc_ref[
                    0, pl.ds(h, s_tot - t, stride=DA_HEADS), :].astype(BF16)
        else:
            for h in range(DA_HEADS):
                kall_ref[h] = k_ref[0, :, h * 128:(h + 1) * 128].T.astype(BF16)
            vall_ref[...] = v_ref[0].astype(BF16)

    lp = lam_ref[...]
    lam = (jnp.exp(jnp.sum(lp[0:1] * lp[1:2], axis=-1, keepdims=True))
           - jnp.exp(jnp.sum(lp[2:3] * lp[3:4], axis=-1, keepdims=True)) + lam_init)
    lane = lax.broadcasted_iota(jnp.int32, (tq, 128), 1)
    def scores(h):
        hs = slice(h * 128, (h + 1) * 128)
        q = q_ref[0, :, hs]
        if with_ctx:
            q = _rope(q, cq_ref[...], sq_ref[...])
        q = q * (DA_HEAD ** -0.5 * LOG2E)
        qs = jnp.concatenate([jnp.where(lane < DA_HEAD, q, 0.0), jnp.where(lane >= DA_HEAD, q, 0.0)], axis=0)
        s_ref[h % 2] = _mm(qs, kall_ref[h])

    scores(0)
    for h in range(DA_HEADS):
        hs = slice(h * 128, (h + 1) * 128)
        b = h % 2
        if h + 1 < DA_HEADS:
            scores(h + 1)
        for r in range(0, 2 * tq, ATTN_ROWS):
            sc = s_ref[b, r:r + ATTN_ROWS, :]
            e = jnp.exp2(sc - jnp.max(sc, axis=-1, keepdims=True))
            e_ref[b, r:r + ATTN_ROWS, :] = e.astype(BF16)
            l_ref[b, r:r + ATTN_ROWS, :] = jnp.broadcast_to(jnp.sum(e, axis=-1, keepdims=True),
                                                            (ATTN_ROWS, 128))
        ov = jnp.dot(e_ref[b], vall_ref[:, hs], preferred_element_type=F32) / l_ref[b]
        o = ov[:tq] - lam * ov[tq:]
        o = o * lax.rsqrt(jnp.mean(o * o, axis=-1, keepdims=True) + EPS) * ng_ref[...]
        o = o * (1.0 - lam_init)
        o_ref[0, :, hs] = (o * _silu(z_ref[0, :, hs])).astype(BF16)


def diff_attention(proj3, layer, lam_init, da_lam, da_norm_g, ctx_kv):
    nb, t, _ = proj3.shape
    with_ctx = ctx_kv is not None
    s_tot = t + (PAST_LEN if with_ctx else 0)
    tq = 256
    wb = DA_WIDTH
    in_specs = [pl.BlockSpec((1, tq, wb), lambda b, i: (b, i, COL_QB // wb)),
                pl.BlockSpec((1, t, wb), lambda b, i: (b, 0, COL_KB // wb)),
                pl.BlockSpec((1, t, wb), lambda b, i: (b, 0, COL_VB // wb)),
                pl.BlockSpec((1, tq, wb), lambda b, i: (b, i, COL_ZB // wb))]
    args = [proj3, proj3, proj3, proj3]
    if with_ctx:
        kc, vc = ctx_kv
        cos, sin = _rope_tables(t)
        in_specs += [pl.BlockSpec((1, 1, DA_HEADS, 2 * DA_HEAD, PAST_LEN), lambda b, i: (b, layer, 0, 0, 0)),
                     pl.BlockSpec((1, PAST_LEN * DA_HEADS, DA_VDIM), lambda b, i: (b, layer, 0)),
                     pl.BlockSpec((tq, 128), lambda b, i: (i, 0)),
                     pl.BlockSpec((tq, 128), lambda b, i: (i, 0)),
                     pl.BlockSpec((t, 128), lambda b, i: (0, 0)),
                     pl.BlockSpec((t, 128), lambda b, i: (0, 0))]
        args += [kc, vc, cos, sin, cos, sin]
    in_specs += [pl.BlockSpec((4, DA_HEAD), lambda b, i: (0, 0)),
                 pl.BlockSpec((1, DA_VDIM), lambda b, i: (0, 0))]
    args += [da_lam, da_norm_g.reshape(1, DA_VDIM)]
    return pl.pallas_call(
        functools.partial(_attn_kernel, lam_init=lam_init, t=t, s_tot=s_tot, tq=tq, with_ctx=with_ctx),
        grid=(nb, t // tq),
        in_specs=in_specs,
        out_specs=pl.BlockSpec((1, tq, wb), lambda b, i: (b, i, 0)),
        out_shape=jax.ShapeDtypeStruct((nb, t, wb), BF16),
        scratch_shapes=[pltpu.VMEM((DA_HEADS, 2 * DA_HEAD, s_tot), BF16), pltpu.VMEM((s_tot, wb), BF16),
                        pltpu.VMEM((2, 2 * tq, s_tot), F32), pltpu.VMEM((2, 2 * tq, s_tot), BF16),
                        pltpu.VMEM((2, 2 * tq, 128), F32)],
        compiler_params=_cparams(("parallel", "arbitrary")),
        name="diff_attention",
    )(*args)


DN_PAD = 8
DN_RT = 128
DN_GROUP = 128
DN_STEP_GROUPS = 2


def _dn_kernel(*refs, t, ngroup, with_s0):
    if with_s0:
        (qkv_ref, z_ref, ba_ref, cw_ref, alog_ref, dtb_ref, ng_ref, s0_ref,
         out_ref, sfin_ref, xp_ref, qkvn_ref, oacc_ref, st_ref) = refs
    else:
        (qkv_ref, z_ref, ba_ref, cw_ref, alog_ref, dtb_ref, ng_ref,
         out_ref, sfin_ref, xp_ref, qkvn_ref, oacc_ref, st_ref) = refs
    n = pl.program_id(1)
    cd = DN_CHUNK
    w3 = 3 * DN_WIDTH

    @pl.when(n == 0)
    def _init():
        xp_ref[0:DN_PAD, :] = jnp.zeros((DN_PAD, w3), F32)
        xp_ref[DN_PAD + t:2 * DN_PAD + t, :] = jnp.zeros((DN_PAD, w3), F32)
        xp_ref[DN_PAD:DN_PAD + t, :] = qkv_ref[0]
        half = DN_CONV // 2
        for r in range(t // DN_RT):
            for sec in range(3):
                for h in range(DN_HEADS):
                    cs = slice(sec * DN_WIDTH + h * DN_HEAD, sec * DN_WIDTH + (h + 1) * DN_HEAD)
                    acc = jnp.zeros((DN_RT, DN_HEAD), F32)
                    for j in range(DN_CONV):
                        r0 = DN_PAD + r * DN_RT + j - half
                        acc = acc + xp_ref[r0:r0 + DN_RT, cs] * cw_ref[j:j + 1, cs]
                    y = _silu(acc)
                    if sec < 2:
                        y = y * lax.rsqrt(jnp.sum(y * y, axis=-1, keepdims=True) + EPS)
                    if sec == 0:
                        y = y * (DN_HEAD ** -0.5)
                    qkvn_ref[r * DN_RT:(r + 1) * DN_RT, cs] = y
        oacc_ref[...] = jnp.zeros_like(oacc_ref)
        if with_s0:
            st_ref[...] = s0_ref[0, 0]
        else:
            st_ref[...] = jnp.zeros_like(st_ref)

    gb = DN_GROUP
    nsub = gb // cd
    ri = lax.broadcasted_iota(jnp.int32, (gb, gb), 0)
    ci = lax.broadcasted_iota(jnp.int32, (gb, gb), 1)
    same = (ri // cd) == (ci // cd)
    samef = same.astype(F32)
    eye = (ri == ci).astype(BF16)
    masked_out = -1e30

    chains = []
    for d, gi in [(d, gi) for d in range(2) for gi in range(DN_STEP_GROUPS)]:
        grp = n * DN_STEP_GROUPS + gi
        r0 = pl.multiple_of((grp if d == 0 else ngroup - 1 - grp) * gb, gb)
        incl = same & ((ci <= ri) if d == 0 else (ci >= ri))
        strict = same & ((ci < ri) if d == 0 else (ci > ri))
        inclf = incl.astype(F32)
        inclog = jnp.where(incl, 0.0, masked_out)
        nstrict = -strict.astype(BF16)
        ba = ba_ref[0, pl.ds(r0, gb), :]
        beta_all = jax.nn.sigmoid(ba)
        g_all = -jnp.exp(alog_ref[...]) * jax.nn.softplus(ba + dtb_ref[...])
        gc = jnp.dot(inclf, g_all, precision=HI, preferred_element_type=F32)
        gct = lax.dot_general(g_all.T, inclf, (((1,), (1,)), ((), ())), precision=HI,
                              preferred_element_type=F32)
        gtot = jnp.dot(samef, g_all, precision=HI, preferred_element_type=F32)
        for h in range(DN_HEADS):
            hs = slice(h * DN_HEAD, (h + 1) * DN_HEAD)
            col = 2 * DN_HEADS + d * DN_HEADS + h
            gcol = gc[:, col:col + 1]
            grow = gct[col:col + 1, :]
            gt = gtot[:, col:col + 1]
            beta = beta_all[:, d * DN_HEADS + h:d * DN_HEADS + h + 1]
            q = qkvn_ref[pl.ds(r0, gb), hs]
            k = qkvn_ref[pl.ds(r0, gb), DN_WIDTH + h * DN_HEAD:DN_WIDTH + (h + 1) * DN_HEAD]
            v = qkvn_ref[pl.ds(r0, gb), 2 * DN_WIDTH + h * DN_HEAD:2 * DN_WIDTH + (h + 1) * DN_HEAD]
            eg = jnp.exp(gcol)
            chains.append(dict(
                d=d, gi=gi, h=h, r0=r0, hs=hs, nstrict=nstrict, q=q, k=k, kb=k * beta,
                decay=jnp.exp((gcol - grow) + inclog),
                rhs=jnp.concatenate([v * beta, k * beta * eg], axis=-1),
                qe=q * eg, kdec=k * jnp.exp(gt - gcol), egt=jnp.exp(gt)))

    for c in chains:
        c["nmm"] = (_mm_nt(c["kb"], c["k"]) * c["decay"]).astype(BF16) * c["nstrict"]
    for c in chains:
        c["qk"] = _mm_nt(c["q"], c["k"]) * c["decay"]
    def level_mask(s):
        return (((ri // (2 * s)) == (ci // (2 * s))) & ((ri // s) != (ci // s))).astype(BF16)

    pair = level_mask(1)
    for c in chains:
        c["tinv"] = eye + c["nmm"] * pair
    s = 2
    while s < cd:
        offmask = level_mask(s)
        xs = [jnp.dot(c["nmm"] * offmask, c["tinv"], preferred_element_type=F32).astype(BF16) for c in chains]
        ys = [jnp.dot(c["tinv"], x, preferred_element_type=F32) for c, x in zip(chains, xs)]
        for c, y in zip(chains, ys):
            c["tinv"] = c["tinv"] + y.astype(BF16)
        s *= 2
    for c in chains:
        c["uw"] = _mm(c["tinv"], c["rhs"])

    st = {(d, h): st_ref[d, h] for d in range(2) for h in range(DN_HEADS)}
    outs = []
    for gi, step in [(gi, step) for gi in range(DN_STEP_GROUPS) for step in range(nsub)]:
        active = [c for c in chains if c["gi"] == gi]
        rs = []
        for c in active:
            sub = step if c["d"] == 0 else nsub - 1 - step
            c["rows"] = slice(sub * cd, (sub + 1) * cd)
            rs.append(_mm(jnp.concatenate([c["uw"][c["rows"], DN_HEAD:], c["qe"][c["rows"]]], axis=0),
                          st[c["d"], c["h"]]))
        for c, r in zip(active, rs):
            rows = c["rows"]
            v_new = c["uw"][rows, :DN_HEAD] - r[:cd]
            o = r[cd:] + _mm(c["qk"][rows, rows], v_new)
            st[c["d"], c["h"]] = (st[c["d"], c["h"]] * c["egt"][rows.start:rows.start + 1]
                                  + _mm(c["kdec"][rows].T, v_new))
            outs.append((c, rows.start, o))
    for (d, h), v in st.items():
        st_ref[d, h] = v
    for c, off, o in outs:
        oacc_ref[pl.ds(pl.multiple_of(c["r0"] + off, cd), cd), c["hs"]] += o

    @pl.when(n == ngroup // DN_STEP_GROUPS - 1)
    def _fin():
        for h in range(DN_HEADS):
            hs = slice(h * DN_HEAD, (h + 1) * DN_HEAD)
            o = oacc_ref[:, hs]
            o = o * lax.rsqrt(jnp.mean(o * o, axis=-1, keepdims=True) + EPS) * ng_ref[...]
            out_ref[0, :, hs] = (o * _silu(z_ref[0, :, hs])).astype(BF16)
        sfin_ref[0] = st_ref[...]


def deltanet(proj3, ba3, layer, conv_w, a_log, dt_bias, norm_g, s0):
    nb, t, _ = proj3.shape
    ngroup = t // DN_GROUP
    with_s0 = s0 is not None
    w3 = 3 * DN_WIDTH
    pad = jnp.zeros((2 * DN_HEADS,), F32)
    alog_row = jnp.concatenate([pad, a_log.reshape(-1), jnp.zeros((128 - 4 * DN_HEADS,), F32)]).reshape(1, 128)
    dtb_row = jnp.concatenate([pad, dt_bias.reshape(-1), jnp.zeros((128 - 4 * DN_HEADS,), F32)]).reshape(1, 128)
    in_specs = [pl.BlockSpec((1, t, w3), lambda b, n: (b, 0, COL_QC // w3)),
                pl.BlockSpec((1, t, DN_WIDTH), lambda b, n: (b, 0, COL_ZC // DN_WIDTH)),
                pl.BlockSpec((1, t, 128), lambda b, n: (b, 0, 0)),
                pl.BlockSpec((8, w3), lambda b, n: (0, 0)),
                pl.BlockSpec((1, 128), lambda b, n: (0, 0)),
                pl.BlockSpec((1, 128), lambda b, n: (0, 0)),
                pl.BlockSpec((1, DN_HEAD), lambda b, n: (0, 0))]
    args = [proj3, proj3, ba3, jnp.pad(conv_w, ((0, 8 - DN_CONV), (0, 0))), alog_row, dtb_row,
            norm_g.reshape(1, DN_HEAD)]
    if with_s0:
        in_specs.append(pl.BlockSpec((1, 1, 2, DN_HEADS, DN_HEAD, DN_HEAD), lambda b, n: (b, layer, 0, 0, 0, 0)))
        args.append(s0)
    return pl.pallas_call(
        functools.partial(_dn_kernel, t=t, ngroup=ngroup, with_s0=with_s0),
        grid=(nb, ngroup // DN_STEP_GROUPS),
        in_specs=in_specs,
        out_specs=[pl.BlockSpec((1, t, DN_WIDTH), lambda b, n: (b, 0, 0)),
                   pl.BlockSpec((1, 2, DN_HEADS, DN_HEAD, DN_HEAD), lambda b, n: (b, 0, 0, 0, 0))],
        out_shape=[jax.ShapeDtypeStruct((nb, t, DN_WIDTH), BF16),
                   jax.ShapeDtypeStruct((nb, 2, DN_HEADS, DN_HEAD, DN_HEAD), F32)],
        scratch_shapes=[pltpu.VMEM((t + 2 * DN_PAD, w3), F32),
                        pltpu.VMEM((t, w3), F32),
                        pltpu.VMEM((t, DN_WIDTH), F32),
                        pltpu.VMEM((2, DN_HEADS, DN_HEAD, DN_HEAD), F32)],
        compiler_params=_cparams(("parallel", "arbitrary")),
        name="deltanet",
    )(*args)


def _merge_kernel(u_ref, ys_ref, za_ref, sd_ref, wglu_ref, ob_ref, oc_ref, hn_ref, wg_ref, wb_ref, wo_ref,
                  x_ref, gate_ref, fg_ref, *outs, final):
    tm = x_ref.shape[0]
    for rows in (slice(0, tm // 2), slice(tm // 2, tm)):
        ya = jax.nn.gelu(sd_ref[...] * u_ref[rows, :] + ys_ref[rows, :])
        ya = ya * jax.nn.sigmoid(_mm(ya, wglu_ref[...]))
        branches = ((ya * _silu(za_ref[rows, :])).astype(BF16), ob_ref[rows, :], oc_ref[rows, :])
        acc = None
        hn = hn_ref[rows, :]
        for i, o in enumerate(branches):
            pr = jnp.dot(o, wb_ref[i], preferred_element_type=F32)
            gt = jnp.dot(hn, wg_ref[:, i * D_MODEL:(i + 1) * D_MODEL], preferred_element_type=F32)
            term = jax.nn.sigmoid(gt) * pr
            acc = term if acc is None else acc + term
        y = jnp.dot(acc.astype(BF16), wo_ref[...], preferred_element_type=F32)
        xn = x_ref[rows, :] + gate_ref[0] * y
        outs[0][rows, :] = xn
        if final:
            yn = xn * lax.rsqrt(jnp.mean(xn * xn, axis=-1, keepdims=True) + EPS) * fg_ref[...]
            outs[1][rows, :] = yn


def merge(proj, y_s5, s5_d, w_glu, out_b, out_c, hn, w_gates, w_branch, w_out, x2, gate, final_g, rows_per_mod,
          final):
    m = x2.shape[0]
    tm = 512
    nmod = gate.shape[0]
    row = lambda i: (i, 0)
    out_specs = [pl.BlockSpec((tm, D_MODEL), row)]
    out_shape = [jax.ShapeDtypeStruct((m, D_MODEL), F32)]
    if final:
        out_specs.append(pl.BlockSpec((tm, D_MODEL), row))
        out_shape.append(jax.ShapeDtypeStruct((m, D_MODEL), F32))
    return pl.pallas_call(
        functools.partial(_merge_kernel, final=final),
        grid=(m // tm,),
        in_specs=[pl.BlockSpec((tm, S5_WIDTH), lambda i: (i, COL_UA // S5_WIDTH)),
                  pl.BlockSpec((tm, S5_WIDTH), row),
                  pl.BlockSpec((tm, S5_WIDTH), lambda i: (i, COL_ZA // S5_WIDTH)),
                  pl.BlockSpec((1, S5_WIDTH), lambda i: (0, 0)),
                  pl.BlockSpec((S5_WIDTH, S5_WIDTH), lambda i: (0, 0)),
                  pl.BlockSpec((tm, BRANCH_WIDTH), row),
                  pl.BlockSpec((tm, BRANCH_WIDTH), row),
                  pl.BlockSpec((tm, D_MODEL), row),
                  pl.BlockSpec((D_MODEL, N_BRANCH * D_MODEL), lambda i: (0, 0)),
                  pl.BlockSpec((N_BRANCH, BRANCH_WIDTH, D_MODEL), lambda i: (0, 0, 0)),
                  pl.BlockSpec((D_MODEL, D_MODEL), lambda i: (0, 0)),
                  pl.BlockSpec((tm, D_MODEL), row),
                  pl.BlockSpec((1, 1, D_MODEL), lambda i: ((i * tm) // rows_per_mod, 0, 0)),
                  pl.BlockSpec((1, D_MODEL), lambda i: (0, 0))],
        out_specs=out_specs,
        out_shape=out_shape,
        compiler_params=pltpu.CompilerParams(
            dimension_semantics=("parallel",), vmem_limit_bytes=VMEM_LIMIT,
            allow_input_fusion=[i in (4, 9, 10) for i in range(14)]),
        name="merge",
    )(proj, y_s5, proj, s5_d.reshape(1, S5_WIDTH), w_glu, out_b, out_c, hn, w_gates, w_branch, w_out, x2,
      gate.reshape(nmod, 1, D_MODEL),
      final_g.reshape(1, D_MODEL))


def _run_pass(x, mod, wts, lam_inits, final_g, ctx):
    nb, t, _ = x.shape
    m = nb * t
    nmod = mod.shape[1]
    rows_per_mod = m if nmod == 1 else t
    x2 = x.reshape(m, D_MODEL)
    states = []
    y, caches = None, None
    for l in range(DEPTH):
        w = wts[l]
        shift, scale, gate = jnp.split(mod[l], 3, axis=-1)
        if ctx is None:
            proj, ba, hn, *caches = inproj(x2, w["norm_g"], scale, shift, w["w1"], w["w2"], rows_per_mod,
                                       kv=(l, t, caches))
            h0 = jnp.zeros((2, 2, nb, S5_GROUPS * S5_STATE), F32)
            ctx_kv, s0 = None, None
        else:
            proj, ba, hn = inproj(x2, w["norm_g"], scale, shift, w["w1"], w["w2"], rows_per_mod)
            cache_k, cache_v, st_re, st_im, st_dn = ctx
            h0 = jnp.stack([st_re[:, l], st_im[:, l]], 0)
            h0 = jnp.transpose(h0, (2, 0, 1, 3, 4)).reshape(2, 2, nb, S5_GROUPS * S5_STATE)
            ctx_kv, s0 = (cache_k, cache_v), st_dn
        proj3 = proj.reshape(nb, t, N_MAIN)
        y_s5, hfin = s5_core(proj, w["s5_mats"], h0, nb, t // S5_CHUNK)
        out_b = diff_attention(proj3, l, lam_inits[l], w["da_lam"], w["da_norm_g"], ctx_kv)
        out_c, sfin = deltanet(proj3, ba.reshape(nb, t, 128), l, w["dn_conv"], w["dn_a_log"],
                               w["dn_dt_bias"], w["dn_norm_g"], s0)
        final = l == DEPTH - 1
        res = merge(proj, y_s5, w["s5_d"], w["w_glu"], out_b.reshape(m, DA_WIDTH), out_c.reshape(m, DN_WIDTH),
                    hn, w["w_gates"], w["w_branch"], w["w_out"], x2, gate, final_g, rows_per_mod, final)
        x2 = res[0]
        if final:
            y = res[1]
        if ctx is None:
            hf = hfin.reshape(2, 2, nb, S5_GROUPS, S5_STATE)
            states.append((jnp.transpose(hf[:, 0], (1, 0, 2, 3)), jnp.transpose(hf[:, 1], (1, 0, 2, 3)), sfin))
    return y.reshape(nb, t, D_MODEL), states, caches


def kernel(x_prompt, x_sample, cache_k, cache_v, state_s5_re, state_s5_im, state_dn, c, c_ctx,
           norm_g, w_ada, b_ada, w_in, s5_lam_re, s5_lam_im, s5_log_step, s5_b_re, s5_b_im,
           s5_c_re, s5_c_im, s5_d, s5_w_glu, da_lam, da_norm_g, dn_conv, dn_a_log, dn_dt_bias,
           dn_norm_g, w_branch, w_out, final_norm_g):
    nb_dec = x_sample.shape[0]
    cond8 = jnp.concatenate([c_ctx[None, :], c, jnp.zeros((8 - 1 - nb_dec, D_MODEL), F32)], 0)
    mod = ada_mod(cond8, w_ada, b_ada)
    wts = []
    for l in range(DEPTH):
        w1, w_gates, w2 = cast_w_in(w_in, l)
        wts.append(dict(
            norm_g=norm_g[l], w1=w1, w2=w2, w_gates=w_gates,
            s5_mats=s5_matrices(s5_lam_re[l], s5_lam_im[l], s5_log_step[l], s5_b_re[l], s5_b_im[l],
                                s5_c_re[l], s5_c_im[l]),
            s5_d=s5_d[l], w_glu=s5_w_glu[l].astype(BF16), da_lam=da_lam[l], da_norm_g=da_norm_g[l],
            dn_conv=dn_conv[l], dn_a_log=dn_a_log[l], dn_dt_bias=dn_dt_bias[l], dn_norm_g=dn_norm_g[l],
            w_branch=w_branch[l].astype(BF16), w_out=w_out[l].astype(BF16)))
    lam_inits = [0.8 - 0.6 * math.exp(-0.3 * l) for l in range(DEPTH)]

    y_prompt, states, (k_new, v_new) = _run_pass(x_prompt, mod[:, 0:1], wts, lam_inits, final_norm_g, None)
    cache_kt = jnp.transpose(cache_k, (0, 1, 3, 4, 5, 2)).reshape(nb_dec, DEPTH, DA_HEADS, 2 * DA_HEAD, PAST_LEN)
    ctx = (cache_kt, cache_v.reshape(nb_dec, DEPTH * PAST_LEN * DA_HEADS, DA_VDIM), state_s5_re, state_s5_im,
           state_dn)
    y_sample, _, _ = _run_pass(x_sample, mod[:, 1:1 + nb_dec], wts, lam_inits, final_norm_g, ctx)

    nb, t = x_prompt.shape[:2]
    new_cache_k = k_new.reshape(nb, DEPTH, t, DA_HEADS, 2, DA_HEAD)
    new_cache_v = v_new.reshape(nb, DEPTH, t, DA_HEADS, DA_VDIM)
    new_s5_re = jnp.stack([s[0] for s in states], axis=1)
    new_s5_im = jnp.stack([s[1] for s in states], axis=1)
    new_dn = jnp.stack([s[2] for s in states], axis=1)
    return (y_prompt, y_sample, new_cache_k, new_cache_v, new_s5_re, new_s5_im, new_dn)
```
